```python
import math
import jax
import jax.numpy as jnp
from jax import lax
import numpy as np

D_MODEL = 1024
BATCH = 2
SEQ = 8192
DEPTH = 4
DEC_BATCH = 128
DEC_SEQ = 8
PAST_LEN = 2048
PAGE_SIZE = 128

F32 = jnp.float32
N_AB = (DEPTH + 1) // 2
N_C = DEPTH // 2
MIX = D_MODEL
GDN_HEADS = 4
GDN_DK = 128
GDN_DV = 128
GDN_CONV = 4
GDN_CHUNK = 64
GDN_QKV = GDN_HEADS * (2 * GDN_DK + GDN_DV)
NSA_HEADS = 8
NSA_KV = 2
NSA_HD = 64
NSA_REP = NSA_HEADS // NSA_KV
CMP_STRIDE = 16
CMP_LEN = 2 * CMP_STRIDE
CMP_HIDDEN = 128
SEL_BLOCK = 64
SEL_TOPN = 16
WINDOW = 512
Q_BLOCK = 128
SEL_FORCED = 1.0e4
NEG = -1.0e30
S5_CH = 16
S5_GROUPS = MIX // S5_CH
S5_P = 64
S5_MAX_RE = -1.0e-4
N_MEM = 256
X_HEADS = 4
X_HD = D_MODEL // X_HEADS
D_FF = 2816
EPS = 1.0e-6
AB_SIZES = (GDN_QKV, GDN_HEADS, GDN_HEADS, GDN_HEADS * GDN_DV, NSA_HEADS * NSA_HD, 6 * NSA_KV * NSA_HD, 3 * NSA_HEADS)
AB_COLS = sum(AB_SIZES)
AB_SPLIT = tuple(sum(AB_SIZES[:i + 1]) for i in range(len(AB_SIZES) - 1))

kernel_name = 'hybrid_gdn_nsa_s5_decoder_step'


def rmsnorm(x, w):
    xf = x.astype(F32)
    y = xf * lax.rsqrt(jnp.mean(xf * xf, axis=-1, keepdims=True) + EPS)
    return (y * w.astype(F32)).astype(x.dtype)


def l2norm(x):
    return x * lax.rsqrt(jnp.sum(x * x, axis=-1, keepdims=True) + EPS)


def swiglu(h, wg, wu, wd):
    return (jax.nn.silu(h @ wg) * (h @ wu)) @ wd


def masked_softmax(s, mask):
    p = jax.nn.softmax(jnp.where(mask, s, NEG), axis=-1)
    return jnp.where(mask, p, 0.0)


def gated_delta_chunked(q, k, v, beta, g, s0):
    N, T, H, dk = q.shape
    dv = v.shape[-1]
    C = math.gcd(T, GDN_CHUNK)
    nc = T // C

    def chunks(a):
        a = a.reshape((N, nc, C) + a.shape[2:])
        return jnp.moveaxis(jnp.moveaxis(a, 3, 2), 0, 1)

    qc, kc, vc = chunks(q), chunks(k), chunks(v)
    bc, gc = chunks(beta), chunks(g)
    gcum = jnp.cumsum(gc, axis=-1)
    idx = jnp.arange(C)
    incl = idx[:, None] >= idx[None, :]
    strict = idx[:, None] > idx[None, :]
    decay = jnp.exp(jnp.where(incl, gcum[..., :, None] - gcum[..., None, :], -jnp.inf))
    kb = kc * bc[..., None]
    lmat = jnp.where(strict, jnp.einsum('...id,...jd->...ij', kb, kc) * decay, 0.0)
    amat = lmat + jnp.eye(C, dtype=F32)
    u = lax.linalg.triangular_solve(amat, vc * bc[..., None], left_side=True, lower=True)
    w = lax.linalg.triangular_solve(amat, kb * jnp.exp(gcum)[..., None], left_side=True, lower=True)
    qk = jnp.where(incl, jnp.einsum('...id,...jd->...ij', qc, kc) * decay, 0.0)
    qd = qc * jnp.exp(gcum)[..., None]
    kd = kc * jnp.exp(gcum[..., -1:] - gcum)[..., None]
    glast = jnp.exp(gcum[..., -1])

    def step(S, xs):
        u_i, w_i, qk_i, qd_i, kd_i, gl_i = xs
        v_new = u_i - jnp.einsum('nhcd,nhde->nhce', w_i, S)
        o = jnp.einsum('nhcd,nhde->nhce', qd_i, S) + jnp.einsum('nhcj,nhje->nhce', qk_i, v_new)
        S = S * gl_i[..., None, None] + jnp.einsum('nhcd,nhce->nhde', kd_i, v_new)
        return S, o

    S, o = lax.scan(step, s0, (u, w, qk, qd, kd, glast))
    o = jnp.moveaxis(jnp.moveaxis(o, 0, 1), 2, 3).reshape(N, T, H, dv)
    return o, S


def gdn_mix(qkv_pre, b_raw, a_raw, gate, p, conv_buf, s0):
    N, T, _ = qkv_pre.shape
    xcat = jnp.concatenate([conv_buf.astype(qkv_pre.dtype), qkv_pre], axis=1)
    conv = xcat[:, 0:T] * p['conv'][0]
    for j in range(1, GDN_CONV):
        conv = conv + xcat[:, j:j + T] * p['conv'][j]
    qkv = jax.nn.silu(conv.astype(F32))
    q, k, v = jnp.split(qkv, (GDN_HEADS * GDN_DK, 2 * GDN_HEADS * GDN_DK), axis=-1)
    q = l2norm(q.reshape(N, T, GDN_HEADS, GDN_DK)) * (GDN_DK ** -0.5)
    k = l2norm(k.reshape(N, T, GDN_HEADS, GDN_DK))
    v = v.reshape(N, T, GDN_HEADS, GDN_DV)
    beta = jax.nn.sigmoid(b_raw.astype(F32))
    g = -jnp.exp(p['a_log'].astype(F32)) * jax.nn.softplus(a_raw.astype(F32) + p['dt_bias'].astype(F32))
    o, s_new = gated_delta_chunked(q, k, v, beta, g, s0.astype(F32))
    o = rmsnorm(o, p['norm']) * jax.nn.silu(gate.astype(F32).reshape(N, T, GDN_HEADS, GDN_DV))
    return o.reshape(N, T, GDN_HEADS * GDN_DV), xcat[:, T:], s_new


def nsa_compress(k, pos_emb, w1, w2):
    N, L, G, hd = k.shape
    n_str = L // CMP_STRIDE
    s = k[:, :n_str * CMP_STRIDE].reshape(N, n_str, CMP_STRIDE, G, hd)
    blocks = jnp.concatenate([s[:, :-1], s[:, 1:]], axis=2) + pos_emb[:, None, :]
    flat = jnp.transpose(blocks, (0, 1, 3, 2, 4)).reshape(N, n_str - 1, G, CMP_LEN * hd)
    return jax.nn.silu(flat @ w1) @ w2


def sel_blocks(a):
    N, L, G, hd = a.shape
    n_slc = -(-L // SEL_BLOCK)
    a = jnp.pad(a, ((0, 0), (0, n_slc * SEL_BLOCK - L), (0, 0), (0, 0)))
    return jnp.transpose(a.reshape(N, n_slc, SEL_BLOCK, G, hd), (0, 3, 1, 2, 4))


def nsa_cmp_slc(q, q_pos, kcmp, vcmp, ks_blk, vs_blk):
    N, Tq = q.shape[:2]
    scale = NSA_HD ** -0.5
    qg = q.reshape(N, Tq, NSA_KV, NSA_REP, NSA_HD).astype(F32)
    n_cmp = kcmp.shape[1]
    cmp_end = jnp.arange(n_cmp) * CMP_STRIDE + (CMP_LEN - 1)
    s = jnp.einsum('ntgrd,nigd->ngrti', qg, kcmp.astype(F32)) * scale
    p = masked_softmax(s, cmp_end[None, :] <= q_pos[:, None])
    o_cmp = jnp.einsum('ngrti,nigd->ntgrd', p, vcmp.astype(F32))
    n_slc = ks_blk.shape[2]
    c_start = jnp.arange(n_cmp)[:, None] * CMP_STRIDE
    s_start = jnp.arange(n_slc)[None, :] * SEL_BLOCK
    overlap = jnp.clip(jnp.minimum(c_start + CMP_LEN, s_start + SEL_BLOCK) - jnp.maximum(c_start, s_start), 0, CMP_LEN)
    imp = jnp.einsum('ngrti,ij->ngtj', p, overlap.astype(F32) / CMP_LEN)
    blk = jnp.arange(n_slc)[None, :]
    cur = (q_pos // SEL_BLOCK)[:, None]
    valid = blk <= cur
    forced = valid & ((blk == 0) | (blk >= cur - 1))
    score = jnp.where(valid, jnp.where(forced, SEL_FORCED, imp), NEG)
    _, idx = lax.top_k(score, min(SEL_TOPN, n_slc))
    n_i = jnp.arange(N)[:, None, None, None]
    g_i = jnp.arange(NSA_KV)[None, :, None, None]
    kg = ks_blk[n_i, g_i, idx]
    vg = vs_blk[n_i, g_i, idx]
    n_sel = idx.shape[-1] * SEL_BLOCK
    kpos = (idx[..., None] * SEL_BLOCK + jnp.arange(SEL_BLOCK)).reshape(N, NSA_KV, Tq, n_sel)
    kg = kg.reshape(N, NSA_KV, Tq, n_sel, NSA_HD).astype(F32)
    vg = vg.reshape(N, NSA_KV, Tq, n_sel, NSA_HD).astype(F32)
    s2 = jnp.einsum('ntgrd,ngtsd->ngrts', qg, kg) * scale
    p2 = masked_softmax(s2, (kpos <= q_pos[None, None, :, None])[:, :, None])
    o_slc = jnp.einsum('ngrts,ngtsd->ntgrd', p2, vg)
    return o_cmp.reshape(N, Tq, NSA_HEADS, NSA_HD), o_slc.reshape(N, Tq, NSA_HEADS, NSA_HD)


def band_attend(qb, kb, vb, q_pos, k_pos):
    N, nb, Tq = qb.shape[:3]
    qg = qb.reshape(N, nb, Tq, NSA_KV, NSA_REP, NSA_HD).astype(F32)
    s = jnp.einsum('nbtgrd,nbsgd->nbgrts', qg, kb.astype(F32)) * (NSA_HD ** -0.5)
    d = q_pos[:, :, None] - k_pos[:, None, :]
    mask = (d >= 0) & (d <= WINDOW) & (k_pos[:, None, :] >= 0)
    p = masked_softmax(s, mask[None, :, None, None])
    o = jnp.einsum('nbgrts,nbsgd->nbtgrd', p, vb.astype(F32))
    return o.reshape(N, nb, Tq, NSA_HEADS, NSA_HD)


def window_prompt(q, kw, vw):
    N, T = q.shape[:2]
    nb = T // Q_BLOCK
    nband = WINDOW // Q_BLOCK

    def band(a):
        ap = jnp.concatenate([jnp.zeros((N, WINDOW) + a.shape[2:], a.dtype), a], axis=1)
        ap = ap.reshape((N, nb + nband, Q_BLOCK) + a.shape[2:])
        return jnp.concatenate([ap[:, j:j + nb] for j in range(nband + 1)], axis=2)

    q_pos = jnp.arange(nb)[:, None] * Q_BLOCK + jnp.arange(Q_BLOCK)[None, :]
    k_pos = jnp.arange(nb)[:, None] * Q_BLOCK - WINDOW + jnp.arange((nband + 1) * Q_BLOCK)[None, :]
    o = band_attend(q.reshape(N, nb, Q_BLOCK, NSA_HEADS, NSA_HD), band(kw), band(vw), q_pos, k_pos)
    return o.reshape(N, T, NSA_HEADS, NSA_HD)


def mixer_ab(h, p, conv_buf, s0, past, win_buf):
    N, T, _ = h.shape
    proj = h @ p['w_in']
    qkv_pre, b_raw, a_raw, gate, nq, nkv, ngate = jnp.split(proj, AB_SPLIT, axis=-1)
    o_a, conv_new, s_new = gdn_mix(qkv_pre, b_raw, a_raw, gate, p, conv_buf, s0)
    q = nq.reshape(N, T, NSA_HEADS, NSA_HD)
    kc, vc, ks, vs, kw, vw = [a.reshape(N, T, NSA_KV, NSA_HD) for a in jnp.split(nkv, 6, axis=-1)]
    if past is None:
        p_len = 0
        ck, cv, sk, sv = kc, vc, ks, vs
    else:
        p_len = past[0].shape[1]
        ck = jnp.concatenate([past[0].astype(kc.dtype), kc], axis=1)
        cv = jnp.concatenate([past[1].astype(vc.dtype), vc], axis=1)
        sk = jnp.concatenate([past[2].astype(ks.dtype), ks], axis=1)
        sv = jnp.concatenate([past[3].astype(vs.dtype), vs], axis=1)
    kcmp = nsa_compress(ck, p['cmp_pos_k'], p['cmp_w1_k'], p['cmp_w2_k'])
    vcmp = nsa_compress(cv, p['cmp_pos_v'], p['cmp_w1_v'], p['cmp_w2_v'])
    ks_blk, vs_blk = sel_blocks(sk), sel_blocks(sv)
    q_pos = p_len + jnp.arange(T)
    if win_buf is None:
        nqb = T // Q_BLOCK
        qb = jnp.moveaxis(q.reshape(N, nqb, Q_BLOCK, NSA_HEADS, NSA_HD), 1, 0)
        pb = q_pos.reshape(nqb, Q_BLOCK)
        o_cmp, o_slc = lax.map(lambda xs: nsa_cmp_slc(xs[0], xs[1], kcmp, vcmp, ks_blk, vs_blk), (qb, pb))
        o_cmp = jnp.moveaxis(o_cmp, 0, 1).reshape(N, T, NSA_HEADS, NSA_HD)
        o_slc = jnp.moveaxis(o_slc, 0, 1).reshape(N, T, NSA_HEADS, NSA_HD)
        o_win = window_prompt(q, kw, vw)
        wk_all, wv_all = kw, vw
    else:
        o_cmp, o_slc = nsa_cmp_slc(q, q_pos, kcmp, vcmp, ks_blk, vs_blk)
        wk_all = jnp.concatenate([win_buf[0].astype(kw.dtype), kw], axis=1)
        wv_all = jnp.concatenate([win_buf[1].astype(vw.dtype), vw], axis=1)
        wb = win_buf[0].shape[1]
        k_pos = p_len - wb + jnp.arange(wb + T)
        o_win = band_attend(q[:, None], wk_all[:, None], wv_all[:, None], q_pos[None], k_pos[None])[:, 0]
    gates = jax.nn.sigmoid(ngate.astype(F32)).reshape(N, T, NSA_HEADS, 3)
    o_b = gates[..., 0:1] * o_cmp + gates[..., 1:2] * o_slc + gates[..., 2:3] * o_win
    mixed = jnp.concatenate([o_a.astype(h.dtype), o_b.reshape(N, T, NSA_HEADS * NSA_HD).astype(h.dtype)], axis=-1)
    out = mixed @ p['w_out']
    keep = min(WINDOW, wk_all.shape[1])
    return out, (conv_new, s_new, kc, vc, ks, vs, wk_all[:, -keep:], wv_all[:, -keep:])


def mixer_c(h, p, x0r, x0i):
    N, T, _ = h.shape
    u = (h @ p['w_in']).astype(F32).reshape(N, T, S5_GROUPS, S5_CH)
    dt = jnp.exp(p['log_dt'].astype(F32))[:, None]
    lr = jnp.minimum(p['a_re'].astype(F32), S5_MAX_RE)
    li = p['a_im'].astype(F32)
    mag = jnp.exp(lr * dt)
    ar = mag * jnp.cos(li * dt)
    ai = mag * jnp.sin(li * dt)
    den = lr * lr + li * li
    fr = ((ar - 1.0) * lr + ai * li) / den
    fi = (ai * lr - (ar - 1.0) * li) / den
    b_re, b_im = p['b_re'].astype(F32), p['b_im'].astype(F32)
    bbr = fr[..., None] * b_re - fi[..., None] * b_im
    bbi = fr[..., None] * b_im + fi[..., None] * b_re
    br = jnp.einsum('ntgc,gpc->ntgp', u, bbr)
    bi = jnp.einsum('ntgc,gpc->ntgp', u, bbi)
    x0r = x0r.astype(F32)
    x0i = x0i.astype(F32)
    br = br.at[:, 0].add(ar * x0r - ai * x0i)
    bi = bi.at[:, 0].add(ar * x0i + ai * x0r)
    a_r = jnp.broadcast_to(ar, (1, T) + ar.shape)
    a_i = jnp.broadcast_to(ai, (1, T) + ai.shape)

    def comb(e1, e2):
        a1r, a1i, b1r, b1i = e1
        a2r, a2i, b2r, b2i = e2
        return (a2r * a1r - a2i * a1i, a2r * a1i + a2i * a1r,
                a2r * b1r - a2i * b1i + b2r, a2r * b1i + a2i * b1r + b2i)

    _, _, xr, xi = lax.associative_scan(comb, (a_r, a_i, br, bi), axis=1)
    y = (jnp.einsum('ntgp,gcp->ntgc', xr, p['c_re'].astype(F32))
         - jnp.einsum('ntgp,gcp->ntgc', xi, p['c_im'].astype(F32))
         + p['d'].astype(F32) * u)
    y = jax.nn.gelu(y.reshape(N, T, MIX)).astype(h.dtype)
    za, zb = jnp.split(y @ p['w_glu'], 2, axis=-1)
    out = (za * jax.nn.sigmoid(zb)) @ p['w_out']
    return out, xr[:, -1], xi[:, -1]


def mem_kv(mem, g, wk, wv):
    N = mem.shape[0]
    mh = rmsnorm(mem, g)
    return (mh @ wk).reshape(N, N_MEM, X_HEADS, X_HD), (mh @ wv).reshape(N, N_MEM, X_HEADS, X_HD)


def cross_attn(h, wq, wo, mk, mv):
    N, T, _ = h.shape
    q = (h @ wq).reshape(N, T, X_HEADS, X_HD).astype(F32)
    s = jnp.einsum('nthd,nmhd->nhtm', q, mk.astype(F32)) * (X_HD ** -0.5)
    p = jax.nn.softmax(s, axis=-1)
    o = jnp.einsum('nhtm,nmhd->nthd', p, mv.astype(F32)).reshape(N, T, D_MODEL).astype(h.dtype)
    return o @ wo


def gather_pages(pool, page_table):
    pages = pool[page_table]
    n, npg, ps = pages.shape[:3]
    return pages.reshape((n, npg * ps) + pages.shape[3:])


def stack_field(group, j):
    return jnp.stack([t[j] for t in group])


def setup_inputs(seed: int = 0) -> dict:
    key = jax.random.key(seed)
    keys = iter(list(jax.random.split(key, 72)))

    def nrm(shape, scale=1.0):
        return scale * jax.random.normal(next(keys), shape, F32)

    def gain(shape):
        return 1.0 + 0.02 * jax.random.normal(next(keys), shape, F32)

    def unif(shape, lo, hi):
        return jax.random.uniform(next(keys), shape, F32, lo, hi)

    n_pages = PAST_LEN // PAGE_SIZE
    n_used = DEC_BATCH * n_pages
    n_pool = n_used + (n_used + 3) // 4
    wbuf = min(WINDOW, PAST_LEN)
    page_table = jax.random.permutation(next(keys), n_pool)[:n_used].reshape(DEC_BATCH, n_pages).astype(jnp.int32)
    dt = jnp.exp(unif((N_AB, GDN_HEADS), math.log(1e-3), math.log(1e-1)))
    d_in = D_MODEL ** -0.5
    return {
        'x_prompt': nrm((BATCH, SEQ, D_MODEL)),
        'x_sample': nrm((DEC_BATCH, DEC_SEQ, D_MODEL)),
        'mem_prompt': nrm((BATCH, N_MEM, D_MODEL)),
        'cache_mem_k': nrm((DEPTH, DEC_BATCH, N_MEM, X_HEADS, X_HD)),
        'cache_mem_v': nrm((DEPTH, DEC_BATCH, N_MEM, X_HEADS, X_HD)),
        'state_gdn': nrm((N_AB, DEC_BATCH, GDN_HEADS, GDN_DK, GDN_DV), 0.1),
        'state_gdn_conv': nrm((N_AB, DEC_BATCH, GDN_CONV - 1, GDN_QKV)),
        'cache_cmp_k': nrm((N_AB, n_pool, PAGE_SIZE, NSA_KV, NSA_HD)),
        'cache_cmp_v': nrm((N_AB, n_pool, PAGE_SIZE, NSA_KV, NSA_HD)),
        'cache_slc_k': nrm((N_AB, n_pool, PAGE_SIZE, NSA_KV, NSA_HD)),
        'cache_slc_v': nrm((N_AB, n_pool, PAGE_SIZE, NSA_KV, NSA_HD)),
        'cache_win_k': nrm((N_AB, DEC_BATCH, wbuf, NSA_KV, NSA_HD)),
        'cache_win_v': nrm((N_AB, DEC_BATCH, wbuf, NSA_KV, NSA_HD)),
        'state_s5_re': nrm((N_C, DEC_BATCH, S5_GROUPS, S5_P), 0.1),
        'state_s5_im': nrm((N_C, DEC_BATCH, S5_GROUPS, S5_P), 0.1),
        'page_table': page_table,
        'norm_ffn1': gain((DEPTH, D_MODEL)),
        'w_ffn1_gate': nrm((DEPTH, D_MODEL, D_FF), d_in),
        'w_ffn1_up': nrm((DEPTH, D_MODEL, D_FF), d_in),
        'w_ffn1_down': nrm((DEPTH, D_FF, D_MODEL), D_FF ** -0.5),
        'norm_mix': gain((DEPTH, D_MODEL)),
        'norm_xq': gain((DEPTH, D_MODEL)),
        'norm_mem': gain((DEPTH, D_MODEL)),
        'w_xq': nrm((DEPTH, D_MODEL, D_MODEL), d_in),
        'w_xk': nrm((DEPTH, D_MODEL, D_MODEL), d_in),
        'w_xv': nrm((DEPTH, D_MODEL, D_MODEL), d_in),
        'w_xo': nrm((DEPTH, D_MODEL, D_MODEL), d_in),
        'norm_ffn2': gain((DEPTH, D_MODEL)),
        'w_ffn2_gate': nrm((DEPTH, D_MODEL, D_FF), d_in),
        'w_ffn2_up': nrm((DEPTH, D_MODEL, D_FF), d_in),
        'w_ffn2_down': nrm((DEPTH, D_FF, D_MODEL), D_FF ** -0.5),
        'norm_final': gain((D_MODEL,)),
        'w_in_ab': nrm((N_AB, D_MODEL, AB_COLS), d_in),
        'w_out_ab': nrm((N_AB, MIX, D_MODEL), MIX ** -0.5),
        'gdn_conv': nrm((N_AB, GDN_CONV, GDN_QKV), GDN_CONV ** -0.5),
        'gdn_a_log': jnp.log(unif((N_AB, GDN_HEADS), 1.0, 16.0)),
        'gdn_dt_bias': dt + jnp.log(-jnp.expm1(-dt)),
        'gdn_norm': gain((N_AB, GDN_DV)),
        'cmp_pos_k': nrm((N_AB, CMP_LEN, NSA_HD), 0.02),
        'cmp_w1_k': nrm((N_AB, CMP_LEN * NSA_HD, CMP_HIDDEN), (CMP_LEN * NSA_HD) ** -0.5),
        'cmp_w2_k': nrm((N_AB, CMP_HIDDEN, NSA_HD), CMP_HIDDEN ** -0.5),
        'cmp_pos_v': nrm((N_AB, CMP_LEN, NSA_HD), 0.02),
        'cmp_w1_v': nrm((N_AB, CMP_LEN * NSA_HD, CMP_HIDDEN), (CMP_LEN * NSA_HD) ** -0.5),
        'cmp_w2_v': nrm((N_AB, CMP_HIDDEN, NSA_HD), CMP_HIDDEN ** -0.5),
        'w_in_c': nrm((N_C, D_MODEL, MIX), d_in),
        's5_a_re': -0.5 + nrm((N_C, S5_GROUPS, S5_P), 0.01),
        's5_a_im': math.pi * jnp.arange(S5_P, dtype=F32)[None, None, :] + nrm((N_C, S5_GROUPS, S5_P), 0.01),
        's5_b_re': nrm((N_C, S5_GROUPS, S5_P, S5_CH), (2 * S5_CH) ** -0.5),
        's5_b_im': nrm((N_C, S5_GROUPS, S5_P, S5_CH), (2 * S5_CH) ** -0.5),
        's5_c_re': nrm((N_C, S5_GROUPS, S5_CH, S5_P), S5_P ** -0.5),
        's5_c_im': nrm((N_C, S5_GROUPS, S5_CH, S5_P), S5_P ** -0.5),
        's5_d': nrm((N_C, S5_GROUPS, S5_CH), 0.5),
        's5_log_dt': unif((N_C, S5_GROUPS), math.log(1e-3), math.log(1e-1)),
        'w_glu': nrm((N_C, MIX, 2 * MIX), MIX ** -0.5),
        'w_out_c': nrm((N_C, MIX, D_MODEL), MIX ** -0.5),
    }


def reference(x_prompt, x_sample, mem_prompt, cache_mem_k, cache_mem_v, state_gdn, state_gdn_conv,
              cache_cmp_k, cache_cmp_v, cache_slc_k, cache_slc_v, cache_win_k, cache_win_v,
              state_s5_re, state_s5_im, page_table,
              norm_ffn1, w_ffn1_gate, w_ffn1_up, w_ffn1_down, norm_mix, norm_xq, norm_mem,
              w_xq, w_xk, w_xv, w_xo, norm_ffn2, w_ffn2_gate, w_ffn2_up, w_ffn2_down, norm_final,
              w_in_ab, w_out_ab, gdn_conv, gdn_a_log, gdn_dt_bias, gdn_norm,
              cmp_pos_k, cmp_w1_k, cmp_w2_k, cmp_pos_v, cmp_w1_v, cmp_w2_v,
              w_in_c, s5_a_re, s5_a_im, s5_b_re, s5_b_im, s5_c_re, s5_c_im, s5_d, s5_log_dt,
              w_glu, w_out_c):
    xp, xs = x_prompt, x_sample
    nbp = x_prompt.shape[0]
    ab_prompt, ab_sample, c_prompt, c_sample, mem_k_new, mem_v_new = [], [], [], [], [], []
    for l in range(DEPTH):
        i = l // 2
        xp = xp + 0.5 * swiglu(rmsnorm(xp, norm_ffn1[l]), w_ffn1_gate[l], w_ffn1_up[l], w_ffn1_down[l])
        xs = xs + 0.5 * swiglu(rmsnorm(xs, norm_ffn1[l]), w_ffn1_gate[l], w_ffn1_up[l], w_ffn1_down[l])
        hp = rmsnorm(xp, norm_mix[l])
        hs = rmsnorm(xs, norm_mix[l])
        if l % 2 == 0:
            p = {'w_in': w_in_ab[i], 'w_out': w_out_ab[i], 'conv': gdn_conv[i], 'a_log': gdn_a_log[i],
                 'dt_bias': gdn_dt_bias[i], 'norm': gdn_norm[i],
                 'cmp_pos_k': cmp_pos_k[i], 'cmp_w1_k': cmp_w1_k[i], 'cmp_w2_k': cmp_w2_k[i],
                 'cmp_pos_v': cmp_pos_v[i], 'cmp_w1_v': cmp_w1_v[i], 'cmp_w2_v': cmp_w2_v[i]}
            conv0 = jnp.zeros((nbp, GDN_CONV - 1, GDN_QKV), xp.dtype)
            s00 = jnp.zeros((nbp, GDN_HEADS, GDN_DK, GDN_DV), F32)
            mp, newp = mixer_ab(hp, p, conv0, s00, None, None)
            past = (gather_pages(cache_cmp_k[i], page_table), gather_pages(cache_cmp_v[i], page_table),
                    gather_pages(cache_slc_k[i], page_table), gather_pages(cache_slc_v[i], page_table))
            ms, news = mixer_ab(hs, p, state_gdn_conv[i], state_gdn[i], past, (cache_win_k[i], cache_win_v[i]))
            ab_prompt.append(newp)
            ab_sample.append(news)
        else:
            p = {'w_in': w_in_c[i], 'a_re': s5_a_re[i], 'a_im': s5_a_im[i], 'b_re': s5_b_re[i], 'b_im': s5_b_im[i],
                 'c_re': s5_c_re[i], 'c_im': s5_c_im[i], 'd': s5_d[i], 'log_dt': s5_log_dt[i],
                 'w_glu': w_glu[i], 'w_out': w_out_c[i]}
            z0 = jnp.zeros((nbp, S5_GROUPS, S5_P), F32)
            mp, rp, ip = mixer_c(hp, p, z0, z0)
            ms, rs, is_ = mixer_c(hs, p, state_s5_re[i], state_s5_im[i])
            c_prompt.append((rp, ip))
            c_sample.append((rs, is_))
        xp = xp + mp.astype(xp.dtype)
        xs = xs + ms.astype(xs.dtype)
        mk, mv = mem_kv(mem_prompt, norm_mem[l], w_xk[l], w_xv[l])
        mem_k_new.append(mk)
        mem_v_new.append(mv)
        xp = xp + cross_attn(rmsnorm(xp, norm_xq[l]), w_xq[l], w_xo[l], mk, mv)
        xs = xs + cross_attn(rmsnorm(xs, norm_xq[l]), w_xq[l], w_xo[l], cache_mem_k[l], cache_mem_v[l])
        xp = xp + 0.5 * swiglu(rmsnorm(xp, norm_ffn2[l]), w_ffn2_gate[l], w_ffn2_up[l], w_ffn2_down[l])
        xs = xs + 0.5 * swiglu(rmsnorm(xs, norm_ffn2[l]), w_ffn2_gate[l], w_ffn2_up[l], w_ffn2_down[l])
    y_prompt = rmsnorm(xp, norm_final)
    y_sample = rmsnorm(xs, norm_final)
    mem_k_prompt = jnp.stack(mem_k_new)
    mem_v_prompt = jnp.stack(mem_v_new)
    return (y_prompt, y_sample,
            mem_k_prompt, mem_v_prompt,
            stack_field(ab_prompt, 1), stack_field(ab_sample, 1),
            stack_field(ab_prompt, 0), stack_field(ab_sample, 0),
            stack_field(ab_prompt, 2), stack_field(ab_prompt, 3), stack_field(ab_prompt, 4), stack_field(ab_prompt, 5),
            stack_field(ab_sample, 2), stack_field(ab_sample, 3), stack_field(ab_sample, 4), stack_field(ab_sample, 5),
            stack_field(ab_prompt, 6), stack_field(ab_prompt, 7), stack_field(ab_sample, 6), stack_field(ab_sample, 7),
            stack_field(c_prompt, 0), stack_field(c_prompt, 1), stack_field(c_sample, 0), stack_field(c_sample, 1))
```

```python
import functools
import math

import jax
import jax.numpy as jnp
from jax import lax
from jax.experimental import pallas as pl
from jax.experimental.pallas import tpu as pltpu

F32 = jnp.float32
BF16 = jnp.bfloat16
HIGHEST = lax.Precision.HIGHEST

D_MODEL = 1024
BATCH = 2
SEQ = 8192
DEPTH = 4
DEC_BATCH = 128
DEC_SEQ = 8
PAST_LEN = 2048
PAGE_SIZE = 128
N_PAGES = PAST_LEN // PAGE_SIZE
RP = BATCH * SEQ
RS = DEC_BATCH * DEC_SEQ
ROWS = RP + RS

GDN_HEADS = 4
GDN_DK = 128
GDN_QKV = 1536
GDN_CONV = 4
GDN_CHUNK = 64
NSA_HEADS = 8
NSA_KV = 2
NSA_HD = 64
NSA_REP = 4
CMP_STRIDE = 16
CMP_LEN = 32
CMP_HIDDEN = 128
SEL_BLOCK = 64
SEL_TOPN = 16
WINDOW = 512
Q_BLOCK = 128
SEL_FORCED = 1.0e4
NEG = -1.0e30
S5_CH = 16
S5_GROUPS = 64
S5_P = 64
S5_STATE = S5_GROUPS * S5_P
S5_MAX_RE = -1.0e-4
N_MEM = 256
X_HEADS = 4
X_HD = 256
D_FF = 2816
EPS = 1.0e-6

VMEM_LIMIT = 56 * 1024 * 1024


def _cp(*sem):
    return pltpu.CompilerParams(dimension_semantics=sem, vmem_limit_bytes=VMEM_LIMIT)


def _dot(a, b, precision=None):
    return jnp.dot(a, b, preferred_element_type=F32, precision=precision)


def _dot_nt(a, b, precision=None):
    return lax.dot_general(a, b, (((1,), (1,)), ((), ())), preferred_element_type=F32, precision=precision)


def _sigmoid(x):
    return 1.0 / (1.0 + jnp.exp(-x))


def _silu(x):
    return x * _sigmoid(x)


def _rms(x, g):
    return x * lax.rsqrt(jnp.mean(x * x, axis=-1, keepdims=True) + EPS) * g


def _norm_matmul_kernel(x_ref, g_ref, w_ref, *o_refs, norm, splits):
    x = x_ref[...]
    if norm:
        x = _rms(x, g_ref[...])
    h = x.astype(BF16)
    off = 0
    for o_ref, wd in zip(o_refs, splits):
        o_ref[...] = _dot(h, w_ref[:, off:off + wd])
        off += wd


def norm_matmul(x, g, w, splits, tm, norm=True):
    rows, k = x.shape
    n = w.shape[1]
    assert sum(splits) == n and rows % tm == 0
    outs = pl.pallas_call(
        functools.partial(_norm_matmul_kernel, norm=norm, splits=tuple(splits)),
        grid=(rows // tm,),
        in_specs=[pl.BlockSpec((tm, k), lambda i: (i, 0)),
                  pl.BlockSpec((1, k), lambda i: (0, 0)),
                  pl.BlockSpec((k, n), lambda i: (0, 0))],
        out_specs=[pl.BlockSpec((tm, wd), lambda i: (i, 0)) for wd in splits],
        out_shape=[jax.ShapeDtypeStruct((rows, wd), F32) for wd in splits],
        compiler_params=_cp("parallel"),
    )(x, g.reshape(1, k), w)
    return outs


def _mem_kv_kernel(x_ref, g_ref, w_ref, o_ref):
    h = _rms(x_ref[...], g_ref[0]).astype(BF16)
    o_ref[0] = _dot(h, w_ref[0])


def mem_kv_all(mem2d, g, w):
    m, k = mem2d.shape
    nl, _, n = w.shape
    return pl.pallas_call(
        _mem_kv_kernel,
        grid=(nl,),
        in_specs=[pl.BlockSpec((m, k), lambda l: (0, 0)),
                  pl.BlockSpec((1, 1, k), lambda l: (l, 0, 0)),
                  pl.BlockSpec((1, k, n), lambda l: (l, 0, 0))],
        out_specs=pl.BlockSpec((1, m, n), lambda l: (l, 0, 0)),
        out_shape=jax.ShapeDtypeStruct((nl, m, n), F32),
        compiler_params=_cp("parallel"),
    )(mem2d, g.reshape(nl, 1, k), w)


def _matmul_res_kernel(*refs, n_terms, n_p):
    r_ref, o_ref = refs[3 * n_terms], refs[3 * n_terms + 1]
    i = pl.program_id(0)

    def run(sel):
        acc = r_ref[...]
        for t in range(n_terms):
            acc = acc + _dot(refs[3 * t + sel][...].astype(BF16), refs[3 * t + 2][...])
        o_ref[...] = acc

    @pl.when(i < n_p)
    def _():
        run(0)

    @pl.when(i >= n_p)
    def _():
        run(1)


def matmul_residual(terms, res, tm):
    rows, n = res.shape
    n_p = RP // tm
    args, specs = [], []
    for a_p, a_s, w in terms:
        k = w.shape[0]
        args += [a_p, a_s, w]
        specs += [pl.BlockSpec((tm, k), lambda i: (jnp.minimum(i, n_p - 1), 0)),
                  pl.BlockSpec((tm, k), lambda i: (jnp.maximum(i - n_p, 0), 0)),
                  pl.BlockSpec((k, n), lambda i: (0, 0))]
    return pl.pallas_call(
        functools.partial(_matmul_res_kernel, n_terms=len(terms), n_p=n_p),
        grid=(rows // tm,),
        in_specs=specs + [pl.BlockSpec((tm, n), lambda i: (i, 0))],
        out_specs=pl.BlockSpec((tm, n), lambda i: (i, 0)),
        out_shape=jax.ShapeDtypeStruct((rows, n), F32),
        compiler_params=_cp("arbitrary"),
    )(*args, res)


def _glu_mlp_kernel(src_ref, g_ref, wa_ref, wb_ref, wd_ref, res_ref, o_ref, h_scr, acc_scr, *, norm, swiglu, scale):
    j = pl.program_id(1)

    @pl.when(j == 0)
    def _():
        x = src_ref[...]
        if norm:
            x = _rms(x, g_ref[...])
        h_scr[...] = x.astype(BF16)
        acc_scr[...] = jnp.zeros_like(acc_scr)

    h = h_scr[...]
    a = _dot(h, wa_ref[...])
    b = _dot(h, wb_ref[...])
    s = _silu(a) * b if swiglu else a * _sigmoid(b)
    acc_scr[...] += _dot(s.astype(BF16), wd_ref[...])

    @pl.when(j == pl.num_programs(1) - 1)
    def _():
        o_ref[...] = res_ref[...] + scale * acc_scr[...]


def glu_mlp(src, g, wa, wb, wd, res, tm, tf, norm, swiglu, scale):
    rows, k = src.shape
    ff = wa.shape[1]
    n = wd.shape[1]
    return pl.pallas_call(
        functools.partial(_glu_mlp_kernel, norm=norm, swiglu=swiglu, scale=scale),
        grid=(rows // tm, ff // tf),
        in_specs=[pl.BlockSpec((tm, k), lambda i, j: (i, 0)),
                  pl.BlockSpec((1, k), lambda i, j: (0, 0)),
                  pl.BlockSpec((k, tf), lambda i, j: (0, j)),
                  pl.BlockSpec((k, tf), lambda i, j: (0, j)),
                  pl.BlockSpec((tf, n), lambda i, j: (j, 0)),
                  pl.BlockSpec((tm, n), lambda i, j: (i, 0))],
        out_specs=pl.BlockSpec((tm, n), lambda i, j: (i, 0)),
        out_shape=jax.ShapeDtypeStruct((rows, n), F32),
        scratch_shapes=[pltpu.VMEM((tm, k), BF16), pltpu.VMEM((tm, n), F32)],
        compiler_params=_cp("parallel", "arbitrary"),
    )(src, g.reshape(1, k), wa, wb, wd, res)


def _rmsnorm_kernel(x_ref, g_ref, o_ref):
    o_ref[...] = _rms(x_ref[...], g_ref[...])


def rmsnorm_rows(x, g, tm):
    rows, k = x.shape
    return pl.pallas_call(
        _rmsnorm_kernel,
        grid=(rows // tm,),
        in_specs=[pl.BlockSpec((tm, k), lambda i: (i, 0)), pl.BlockSpec((1, k), lambda i: (0, 0))],
        out_specs=pl.BlockSpec((tm, k), lambda i: (i, 0)),
        out_shape=jax.ShapeDtypeStruct((rows, k), F32),
        compiler_params=_cp("parallel"),
    )(x, g.reshape(1, k))


def _xattn_kernel(q_ref, k_ref, v_ref, o_ref):
    q = q_ref[...]
    for h in range(X_HEADS):
        sl = slice(h * X_HD, (h + 1) * X_HD)
        qh = (q[:, sl] * (X_HD ** -0.5)).astype(BF16)
        s = _dot_nt(qh, k_ref[0, :, sl].astype(BF16))
        m = jnp.max(s, axis=-1, keepdims=True)
        p = jnp.exp(s - m)
        p = p / jnp.sum(p, axis=-1, keepdims=True)
        o_ref[:, sl] = _dot(p.astype(BF16), v_ref[0, :, sl].astype(BF16))


def cross_attention(q_all, mk, mv, row_off, n_seq, t_seq, tq):
    nb = t_seq // tq
    off = row_off // tq
    return pl.pallas_call(
        _xattn_kernel,
        grid=(n_seq, nb),
        in_specs=[pl.BlockSpec((tq, D_MODEL), lambda n, i: (off + n * nb + i, 0)),
                  pl.BlockSpec((1, N_MEM, D_MODEL), lambda n, i: (n, 0, 0)),
                  pl.BlockSpec((1, N_MEM, D_MODEL), lambda n, i: (n, 0, 0))],
        out_specs=pl.BlockSpec((tq, D_MODEL), lambda n, i: (n * nb + i, 0)),
        out_shape=jax.ShapeDtypeStruct((n_seq * t_seq, D_MODEL), F32),
        compiler_params=_cp("parallel", "parallel"),
    )(q_all, mk, mv)


def _softplus(x):
    return jnp.maximum(x, 0.0) + jnp.log(1.0 + jnp.exp(-jnp.abs(x)))


def _gdn_kernel(qkv_ref, sm_ref, gate_ref, cw_ref, gp_ref, nw_ref, cb_ref, s0_ref, o_ref, sout_ref,
                xbuf, s_scr, *, rows, chunk):
    i = pl.program_id(1)
    n_chunks = rows // chunk

    @pl.when(i == 0)
    def _():
        xbuf[0:8, :] = cb_ref[0]
        s_scr[...] = s0_ref[0]

    xbuf[8:8 + rows, :] = qkv_ref[...]
    conv = xbuf[5:5 + rows, :] * cw_ref[0:1, :]
    for j in range(1, GDN_CONV):
        conv = conv + xbuf[5 + j:5 + j + rows, :] * cw_ref[j:j + 1, :]
    tail = xbuf[8 + rows - 3:8 + rows, :]
    xbuf[5:8, :] = tail
    qkv = _silu(conv)

    sm = sm_ref[...]
    beta_all = _sigmoid(sm)
    g_all = -jnp.exp(gp_ref[0:1, :]) * _softplus(sm + gp_ref[1:2, :])
    gate = gate_ref[...]
    nw = nw_ref[...]

    ri = lax.broadcasted_iota(jnp.int32, (chunk, chunk), 0)
    ci = lax.broadcasted_iota(jnp.int32, (chunk, chunk), 1)
    incl = ri >= ci
    strict = ri > ci
    tri = incl.astype(F32)
    eye = (ri == ci).astype(F32)

    s_heads = [s_scr[h] for h in range(GDN_HEADS)]
    for c in range(n_chunks):
        rs = slice(c * chunk, (c + 1) * chunk)
        gcum = _dot(tri, g_all[rs], HIGHEST)
        gcum_t = gcum.T
        for h in range(GDN_HEADS):
            q = qkv[rs, h * GDN_DK:(h + 1) * GDN_DK]
            k = qkv[rs, 512 + h * GDN_DK:512 + (h + 1) * GDN_DK]
            v = qkv[rs, 1024 + h * GDN_DK:1024 + (h + 1) * GDN_DK]
            q = q * lax.rsqrt(jnp.sum(q * q, axis=-1, keepdims=True) + EPS) * (GDN_DK ** -0.5)
            k = k * lax.rsqrt(jnp.sum(k * k, axis=-1, keepdims=True) + EPS)
            beta = beta_all[rs, h:h + 1]
            gc_col = gcum[:, 4 + h:5 + h]
            gc_row = gcum_t[4 + h:5 + h, :]
            decay = jnp.where(incl, jnp.exp(jnp.minimum(gc_col - gc_row, 0.0)), 0.0)
            kb = k * beta
            lmat = jnp.where(strict, _dot_nt(kb, k, HIGHEST) * decay, 0.0)
            minv = eye
            w = 1
            while w < chunk:
                off_blk = ((ri // (2 * w)) == (ci // (2 * w))) & ((ri % (2 * w)) >= w) & ((ci % (2 * w)) < w)
                coff = jnp.where(off_blk, lmat, 0.0)
                minv = minv - _dot(_dot(minv, coff, HIGHEST), minv, HIGHEST)
                w *= 2
            eg = jnp.exp(gc_col)
            uw = _dot(minv, jnp.concatenate([v * beta, kb * eg], axis=1), HIGHEST)
            u, wm = uw[:, :GDN_DK], uw[:, GDN_DK:]
            qk = jnp.where(incl, _dot_nt(q, k, HIGHEST) * decay, 0.0)
            g_last = gcum[chunk - 1:chunk, 4 + h:5 + h]
            kd = k * jnp.exp(g_last - gc_col)
            s_h = s_heads[h]
            v_new = u - _dot(wm, s_h, HIGHEST)
            o = _dot(q * eg, s_h, HIGHEST) + _dot(qk, v_new, HIGHEST)
            s_heads[h] = s_h * jnp.exp(g_last) + _dot(kd.T, v_new, HIGHEST)
            o = _rms(o, nw) * _silu(gate[rs, h * GDN_DK:(h + 1) * GDN_DK])
            o_ref[rs, h * GDN_DK:(h + 1) * GDN_DK] = o
    for h in range(GDN_HEADS):
        s_scr[h] = s_heads[h]

    @pl.when(i == pl.num_programs(1) - 1)
    def _():
        sout_ref[0] = s_scr[...]


def gdn(qkv, small, gate, conv_w, gpar, norm_w, conv_buf8, s0, row_off, n_seq, t_seq, rows, chunk):
    nb = t_seq // rows
    off = row_off // rows
    rmap = lambda n, i: (off + n * nb + i, 0)
    return pl.pallas_call(
        functools.partial(_gdn_kernel, rows=rows, chunk=chunk),
        grid=(n_seq, nb),
        in_specs=[pl.BlockSpec((rows, GDN_QKV), rmap),
                  pl.BlockSpec((rows, 128), rmap),
                  pl.BlockSpec((rows, 512), rmap),
                  pl.BlockSpec((GDN_CONV, GDN_QKV), lambda n, i: (0, 0)),
                  pl.BlockSpec((8, 128), lambda n, i: (0, 0)),
                  pl.BlockSpec((1, GDN_DK), lambda n, i: (0, 0)),
                  pl.BlockSpec((1, 8, GDN_QKV), lambda n, i: (n, 0, 0)),
                  pl.BlockSpec((1, GDN_HEADS, GDN_DK, GDN_DK), lambda n, i: (n, 0, 0, 0))],
        out_specs=[pl.BlockSpec((rows, 512), lambda n, i: (n * nb + i, 0)),
                   pl.BlockSpec((1, GDN_HEADS, GDN_DK, GDN_DK), lambda n, i: (n, 0, 0, 0))],
        out_shape=[jax.ShapeDtypeStruct((n_seq * t_seq, 512), F32),
                   jax.ShapeDtypeStruct((n_seq, GDN_HEADS, GDN_DK, GDN_DK), F32)],
        scratch_shapes=[pltpu.VMEM((8 + rows, GDN_QKV), F32), pltpu.VMEM((GDN_HEADS, GDN_DK, GDN_DK), F32)],
        compiler_params=_cp("parallel", "arbitrary"),
    )(qkv, small, gate, conv_w, gpar, norm_w, conv_buf8, s0)


def _compress_core(xj, posab_ref, w1_ref, w2_ref, m):
    acc = jnp.zeros((m + 8, 4 * CMP_HIDDEN), F32)
    for j in range(CMP_STRIDE):
        lhs = jnp.concatenate([xj(j), posab_ref[j]], axis=0).astype(BF16)
        acc = acc + _dot(lhs, w1_ref[j])
    top = acc[0:m, 0:2 * CMP_HIDDEN]
    bot = acc[0:m, 2 * CMP_HIDDEN:]
    c = acc[m:m + 1, 0:2 * CMP_HIDDEN] + acc[m + 1:m + 2, 2 * CMP_HIDDEN:]
    h = top + pltpu.roll(bot, m - 1, 0) + c
    out = _dot(_silu(h).astype(BF16), w2_ref[...])
    row = lax.broadcasted_iota(jnp.int32, (m, 2 * NSA_HD), 0)
    return jnp.where(row < m - 1, out, 0.0)


def _compress_prompt_kernel(kc_ref, vc_ref, pk_ref, w1k_ref, w2k_ref, pv_ref, w1v_ref, w2v_ref, ok_ref, ov_ref, *, m):
    ok_ref[0] = _compress_core(lambda j: kc_ref[pl.ds(j, m, stride=CMP_STRIDE), :], pk_ref, w1k_ref, w2k_ref, m)
    ov_ref[0] = _compress_core(lambda j: vc_ref[pl.ds(j, m, stride=CMP_STRIDE), :], pv_ref, w1v_ref, w2v_ref, m)


def _cmp_weight_specs(nmap):
    return [pl.BlockSpec((CMP_STRIDE, 8, 128), nmap(3)),
            pl.BlockSpec((CMP_STRIDE, 128, 4 * CMP_HIDDEN), nmap(3)),
            pl.BlockSpec((2 * CMP_HIDDEN, 2 * NSA_HD), nmap(2))]


def compress_prompt(nkv, cw, n_seq, t_seq):
    m = t_seq // CMP_STRIDE
    zmap = lambda nd: (lambda n: (0,) * nd)
    out = jax.ShapeDtypeStruct((n_seq, m, 2 * NSA_HD), F32)
    return pl.pallas_call(
        functools.partial(_compress_prompt_kernel, m=m),
        grid=(n_seq,),
        in_specs=[pl.BlockSpec((t_seq, 128), lambda n: (n, 4)), pl.BlockSpec((t_seq, 128), lambda n: (n, 5))]
        + _cmp_weight_specs(zmap) + _cmp_weight_specs(zmap),
        out_specs=[pl.BlockSpec((1, m, 2 * NSA_HD), lambda n: (n, 0, 0))] * 2,
        out_shape=[out, out],
        compiler_params=_cp("parallel"),
    )(nkv, nkv, *cw)


def _compress_sample_kernel(pt_ref, *refs, m):
    del pt_ref
    kpages = refs[0:N_PAGES]
    vpages = refs[N_PAGES:2 * N_PAGES]
    pk_ref, w1k_ref, w2k_ref, pv_ref, w1v_ref, w2v_ref, ok_ref, ov_ref = refs[2 * N_PAGES:]
    per_page = PAGE_SIZE // CMP_STRIDE

    def rows_of(pages):
        return lambda j: jnp.concatenate([p[pl.ds(j, per_page, stride=CMP_STRIDE), :] for p in pages], axis=0)

    ok_ref[0] = _compress_core(rows_of(kpages), pk_ref, w1k_ref, w2k_ref, m)
    ov_ref[0] = _compress_core(rows_of(vpages), pv_ref, w1v_ref, w2v_ref, m)


def _page_specs(layer):
    return [pl.BlockSpec((None, None, PAGE_SIZE, 128), functools.partial(lambda n, pt, p: (layer, pt[n, p], 0, 0), p=p))
            for p in range(N_PAGES)]


def compress_sample(page_table, cache_k, cache_v, layer, cw):
    m = PAST_LEN // CMP_STRIDE
    zmap = lambda nd: (lambda n, pt: (0,) * nd)
    out = jax.ShapeDtypeStruct((DEC_BATCH, m, 2 * NSA_HD), F32)
    return pl.pallas_call(
        functools.partial(_compress_sample_kernel, m=m),
        grid_spec=pltpu.PrefetchScalarGridSpec(
            num_scalar_prefetch=1, grid=(DEC_BATCH,),
            in_specs=_page_specs(layer) + _page_specs(layer) + _cmp_weight_specs(zmap) + _cmp_weight_specs(zmap),
            out_specs=[pl.BlockSpec((1, m, 2 * NSA_HD), lambda n, pt: (n, 0, 0))] * 2),
        out_shape=[out, out],
        compiler_params=_cp("arbitrary"),
    )(page_table, *([cache_k] * N_PAGES), *([cache_v] * N_PAGES), *cw)


def _masked_softmax(s, mask):
    s = jnp.where(mask, s, NEG)
    m = jnp.max(s, axis=-1, keepdims=True)
    p = jnp.where(mask, jnp.exp(s - m), 0.0)
    l = jnp.sum(p, axis=-1, keepdims=True)
    return p / jnp.where(l > 0.0, l, 1.0)


def _stack_heads(nq, g, tq):
    parts = [nq[:, (g * NSA_REP + r) * NSA_HD:(g * NSA_REP + r + 1) * NSA_HD] for r in range(NSA_REP)]
    return (jnp.concatenate(parts, axis=0) * (NSA_HD ** -0.5)).astype(BF16)


def _rep(mask, tq):
    return jnp.concatenate([mask] * NSA_REP, axis=0)


def _cmp_branch(q4, kcmp, vcmp, qpos, tq):
    n_cmp = kcmp.shape[0]
    s = _dot_nt(q4, kcmp.astype(BF16))
    cmp_end = lax.broadcasted_iota(jnp.int32, (tq, n_cmp), 1) * CMP_STRIDE + (CMP_LEN - 1)
    p = _masked_softmax(s, _rep(cmp_end <= qpos, tq))
    o = _dot(p.astype(BF16), vcmp.astype(BF16))
    psum = p[0:tq] + p[tq:2 * tq] + p[2 * tq:3 * tq] + p[3 * tq:4 * tq]
    ci = lax.broadcasted_iota(jnp.int32, (n_cmp, 128), 0) * CMP_STRIDE
    sj = lax.broadcasted_iota(jnp.int32, (n_cmp, 128), 1) * SEL_BLOCK
    ov = jnp.clip(jnp.minimum(ci + CMP_LEN, sj + SEL_BLOCK) - jnp.maximum(ci, sj), 0, CMP_LEN).astype(F32) / CMP_LEN
    imp = _dot(psum, ov, HIGHEST)
    return o, imp


def _select_blocks(imp, qpos, tq):
    blk = lax.broadcasted_iota(jnp.int32, (tq, 128), 1)
    cur = qpos // SEL_BLOCK
    valid = blk <= cur
    forced = valid & ((blk == 0) | (blk >= cur - 1))
    work = jnp.where(valid, jnp.where(forced, SEL_FORCED, imp), NEG)
    sel = jnp.zeros((tq, 128), F32)
    for _ in range(SEL_TOPN):
        m = jnp.max(work, axis=-1, keepdims=True)
        idx = jnp.min(jnp.where(work == m, blk, 128), axis=-1, keepdims=True)
        pick = blk == idx
        sel = jnp.where(pick, 1.0, sel)
        work = jnp.where(pick, -jnp.inf, work)
    return sel


def _expand_sel(sel, first_blk, n_keys):
    bj = lax.broadcasted_iota(jnp.int32, (128, n_keys), 0)
    kb = lax.broadcasted_iota(jnp.int32, (128, n_keys), 1) // SEL_BLOCK + first_blk
    e = jnp.where(bj == kb, 1.0, 0.0).astype(BF16)
    return _dot(sel.astype(BF16), e) > 0.5


def _gate_mix(o_ref, gates, g, o_cmp, o_slc, o_win, tq):
    for r in range(NSA_REP):
        h = g * NSA_REP + r
        rs = slice(r * tq, (r + 1) * tq)
        c0 = 8 + 3 * h
        o = (gates[:, c0:c0 + 1] * o_cmp[rs] + gates[:, c0 + 1:c0 + 2] * o_slc[rs] + gates[:, c0 + 2:c0 + 3] * o_win[rs])
        o_ref[:, h * NSA_HD:(h + 1) * NSA_HD] = o


def _nsa_prompt_kernel(nq_ref, sm_ref, kv_ref, kcmp_ref, vcmp_ref, o_ref):
    tq = Q_BLOCK
    i = pl.program_id(1)
    qpos = i * tq + lax.broadcasted_iota(jnp.int32, (tq, 1), 0)
    nq = nq_ref[...]
    gates = _sigmoid(sm_ref[...])
    kc = 512
    for g in range(NSA_KV):
        gs = slice(g * NSA_HD, (g + 1) * NSA_HD)
        q4 = _stack_heads(nq, g, tq)
        o_cmp, imp = _cmp_branch(q4, kcmp_ref[0, :, gs], vcmp_ref[0, :, gs], qpos, tq)
        sel = _select_blocks(imp, qpos, tq)

        def slc_step(c, carry, q4=q4, sel=sel, g=g):
            m_i, l_i, acc = carry
            start = pl.multiple_of(c * kc, kc)
            k = kv_ref[pl.ds(start, kc), g * NSA_HD:(g + 1) * NSA_HD]
            v = kv_ref[pl.ds(start, kc), 128 + g * NSA_HD:128 + (g + 1) * NSA_HD]
            s = _dot_nt(q4, k)
            kpos = start + lax.broadcasted_iota(jnp.int32, (tq, kc), 1)
            msk = _rep(_expand_sel(sel, c * (kc // SEL_BLOCK), kc) & (kpos <= qpos), tq)
            s = jnp.where(msk, s, NEG)
            m_new = jnp.maximum(m_i, jnp.max(s, axis=-1, keepdims=True))
            alpha = jnp.exp(m_i - m_new)
            p = jnp.where(msk, jnp.exp(s - m_new), 0.0)
            l_new = alpha * l_i + jnp.sum(p, axis=-1, keepdims=True)
            acc = alpha * acc + _dot(p.astype(BF16), v)
            return m_new, l_new, acc

        init = (jnp.full((NSA_REP * tq, 1), NEG, F32), jnp.zeros((NSA_REP * tq, 1), F32),
                jnp.zeros((NSA_REP * tq, NSA_HD), F32))
        _, l_f, acc = lax.fori_loop(0, i // (kc // tq) + 1, slc_step, init)
        o_slc = acc / jnp.where(l_f > 0.0, l_f, 1.0)

        nband = WINDOW + tq
        wstart = pl.multiple_of(jnp.maximum(i - WINDOW // tq, 0) * tq, tq)
        kw = kv_ref[pl.ds(wstart, nband), 256 + g * NSA_HD:256 + (g + 1) * NSA_HD]
        vw = kv_ref[pl.ds(wstart, nband), 384 + g * NSA_HD:384 + (g + 1) * NSA_HD]
        d = qpos - (wstart + lax.broadcasted_iota(jnp.int32, (tq, nband), 1))
        pw = _masked_softmax(_dot_nt(q4, kw), _rep((d >= 0) & (d <= WINDOW), tq))
        o_win = _dot(pw.astype(BF16), vw)
        _gate_mix(o_ref, gates, g, o_cmp, o_slc, o_win, tq)


def nsa_prompt(nq, small, kv_bf, kcmp, vcmp, n_seq, t_seq):
    nb = t_seq // Q_BLOCK
    m = kcmp.shape[1]
    return pl.pallas_call(
        _nsa_prompt_kernel,
        grid=(n_seq, nb),
        in_specs=[pl.BlockSpec((Q_BLOCK, 512), lambda n, i: (n * nb + i, 0)),
                  pl.BlockSpec((Q_BLOCK, 128), lambda n, i: (n * nb + i, 0)),
                  pl.BlockSpec((t_seq, 512), lambda n, i: (n, 0)),
                  pl.BlockSpec((1, m, 128), lambda n, i: (n, 0, 0)),
                  pl.BlockSpec((1, m, 128), lambda n, i: (n, 0, 0))],
        out_specs=pl.BlockSpec((Q_BLOCK, 512), lambda n, i: (n * nb + i, 0)),
        out_shape=jax.ShapeDtypeStruct((n_seq * t_seq, 512), F32),
        compiler_params=_cp("parallel", "arbitrary"),
    )(nq, small, kv_bf, kcmp, vcmp)


def _nsa_sample_kernel(pt_ref, *refs):
    del pt_ref
    kpages = refs[0:N_PAGES]
    vpages = refs[N_PAGES:2 * N_PAGES]
    nq_ref, sm_ref, kvn_ref, kcmp_ref, vcmp_ref, wk_ref, wv_ref, o_ref = refs[2 * N_PAGES:]
    tq = DEC_SEQ
    qpos = PAST_LEN + lax.broadcasted_iota(jnp.int32, (tq, 1), 0)
    nq = nq_ref[...]
    gates = _sigmoid(sm_ref[...])
    kvn = kvn_ref[...]
    pad = jnp.zeros((PAGE_SIZE - tq, 128), F32)
    new = [jnp.concatenate([kvn[:, c * 128:(c + 1) * 128], pad], axis=0) for c in range(4)]
    n_chunks = N_PAGES + 1
    n_keys = n_chunks * PAGE_SIZE
    for g in range(NSA_KV):
        gs = slice(g * NSA_HD, (g + 1) * NSA_HD)
        q4 = _stack_heads(nq, g, tq)
        o_cmp, imp = _cmp_branch(q4, kcmp_ref[0, :, gs], vcmp_ref[0, :, gs], qpos, tq)
        sel = _select_blocks(imp, qpos, tq)
        kchunks = [p[:, gs].astype(BF16) for p in kpages] + [new[0][:, gs].astype(BF16)]
        vchunks = [p[:, gs].astype(BF16) for p in vpages] + [new[1][:, gs].astype(BF16)]
        s = jnp.concatenate([_dot_nt(q4, kch) for kch in kchunks], axis=1)
        kpos = lax.broadcasted_iota(jnp.int32, (tq, n_keys), 1)
        msk = _rep(_expand_sel(sel, 0, n_keys) & (kpos <= qpos), tq)
        p = _masked_softmax(s, msk).astype(BF16)
        o_slc = _dot(p[:, 0:PAGE_SIZE], vchunks[0])
        for c in range(1, n_chunks):
            o_slc = o_slc + _dot(p[:, c * PAGE_SIZE:(c + 1) * PAGE_SIZE], vchunks[c])

        kw = jnp.concatenate([wk_ref[0, :, gs], new[2][:, gs]], axis=0).astype(BF16)
        vw = jnp.concatenate([wv_ref[0, :, gs], new[3][:, gs]], axis=0).astype(BF16)
        d = qpos - (PAST_LEN - WINDOW + lax.broadcasted_iota(jnp.int32, (tq, WINDOW + PAGE_SIZE), 1))
        pw = _masked_softmax(_dot_nt(q4, kw), _rep((d >= 0) & (d <= WINDOW), tq))
        o_win = _dot(pw.astype(BF16), vw)
        _gate_mix(o_ref, gates, g, o_cmp, o_slc, o_win, tq)


def nsa_sample(page_table, cache_k, cache_v, layer, nq, small, nkv, kcmp, vcmp, win_k, win_v):
    off = RP // DEC_SEQ
    rmap = lambda n, pt: (off + n, 0)
    m = kcmp.shape[1]
    return pl.pallas_call(
        _nsa_sample_kernel,
        grid_spec=pltpu.PrefetchScalarGridSpec(
            num_scalar_prefetch=1, grid=(DEC_BATCH,),
            in_specs=_page_specs(layer) + _page_specs(layer) + [
                pl.BlockSpec((DEC_SEQ, 512), rmap),
                pl.BlockSpec((DEC_SEQ, 128), rmap),
                pl.BlockSpec((DEC_SEQ, 768), rmap),
                pl.BlockSpec((1, m, 128), lambda n, pt: (n, 0, 0)),
                pl.BlockSpec((1, m, 128), lambda n, pt: (n, 0, 0)),
                pl.BlockSpec((None, 1, WINDOW, 128), lambda n, pt: (layer, n, 0, 0)),
                pl.BlockSpec((None, 1, WINDOW, 128), lambda n, pt: (layer, n, 0, 0))],
            out_specs=pl.BlockSpec((DEC_SEQ, 512), lambda n, pt: (n, 0))),
        out_shape=jax.ShapeDtypeStruct((RS, 512), F32),
        compiler_params=_cp("arbitrary"),
    )(page_table, *([cache_k] * N_PAGES), *([cache_v] * N_PAGES), nq, small, nkv, kcmp, vcmp, win_k, win_v)


S5_TILE = 256
S5_LANES = 512
S5_BLK = 16


def _gelu_tanh(x):
    return 0.5 * x * (1.0 + jnp.tanh(math.sqrt(2.0 / math.pi) * (x + 0.044715 * (x * x * x))))


def _s5_kernel(u_ref, bdr_ref, bdi_ref, cr_ref, ci_ref, tab_ref, d_ref, x0r_ref, x0i_ref, *rest, chained, aliased):
    if aliased:
        rest = rest[1:]
    y_ref, fr_ref, fi_ref, xr_scr, xi_scr, car_scr = rest
    i = pl.program_id(1)
    nblk = S5_TILE // 8
    nb = S5_GROUPS // S5_BLK
    wi = S5_BLK * S5_CH
    ws = S5_BLK * S5_P
    u = u_ref[...]
    ub = u.astype(BF16)
    for b in range(nb):
        xr_scr[:, b * ws:(b + 1) * ws] = _dot(ub[:, b * wi:(b + 1) * wi], bdr_ref[b])
        xi_scr[:, b * ws:(b + 1) * ws] = _dot(ub[:, b * wi:(b + 1) * wi], bdi_ref[b])

    if chained:
        @pl.when(i == 0)
        def _():
            car_scr[0:1, :] = x0r_ref[0]
            car_scr[1:2, :] = x0i_ref[0]

    for c in range(S5_STATE // S5_LANES):
        ls = slice(c * S5_LANES, (c + 1) * S5_LANES)
        t1r, t1i, t2r, t2i, t4r, t4i, tpr, tpi = [tab_ref[k, :, ls] for k in range(8)]
        if chained:
            cr, ci = car_scr[0:1, ls], car_scr[1:2, ls]
        lasts = []
        for blk in range(nblk):
            rs = slice(blk * 8, (blk + 1) * 8)
            if not chained:
                cr, ci = x0r_ref[0, blk:blk + 1, ls], x0i_ref[0, blk:blk + 1, ls]
            xr, xi = xr_scr[rs, ls], xi_scr[rs, ls]
            for sh, (mr, mi) in ((1, (t1r, t1i)), (2, (t2r, t2i)), (4, (t4r, t4i))):
                sr, si = pltpu.roll(xr, sh, 0), pltpu.roll(xi, sh, 0)
                xr, xi = xr + mr * sr - mi * si, xi + mr * si + mi * sr
            xr, xi = xr + tpr * cr - tpi * ci, xi + tpr * ci + tpi * cr
            xr_scr[rs, ls] = xr
            xi_scr[rs, ls] = xi
            cr, ci = xr[7:8, :], xi[7:8, :]
            lasts.append((cr, ci))
        if chained:
            car_scr[0:1, ls] = cr
            car_scr[1:2, ls] = ci
        else:
            fr_ref[0, :, ls] = jnp.concatenate([p[0] for p in lasts], axis=0)
            fi_ref[0, :, ls] = jnp.concatenate([p[1] for p in lasts], axis=0)

    ys = []
    for b in range(nb):
        ys.append(_dot(xr_scr[:, b * ws:(b + 1) * ws].astype(BF16), cr_ref[b])
                  - _dot(xi_scr[:, b * ws:(b + 1) * ws].astype(BF16), ci_ref[b]))
    y_ref[...] = _gelu_tanh(jnp.concatenate(ys, axis=1) + d_ref[...] * u)

    if chained:
        @pl.when(i == pl.num_programs(1) - 1)
        def _():
            fr_ref[0] = car_scr[0:1, :]
            fi_ref[0] = car_scr[1:2, :]


def s5_scan(u, sp, x0r, x0i, row_off, n_seq, t_seq, chained, y_prev=None):
    if chained:
        grid = (n_seq, t_seq // S5_TILE)
        nb = grid[1]
        smap = lambda n, i: (n, 0, 0)
    else:
        grid = (1, n_seq * t_seq // S5_TILE)
        nb = grid[1]
        smap = lambda n, i: (i, 0, 0)
    off = row_off // S5_TILE
    rmap = lambda n, i: (off + n * nb + i, 0)
    sblk = (1,) + x0r.shape[1:]
    const = lambda nd: (lambda n, i: (0,) * nd)
    in_specs = [pl.BlockSpec((S5_TILE, D_MODEL), rmap),
                pl.BlockSpec(sp["bdr"].shape, const(3)), pl.BlockSpec(sp["bdi"].shape, const(3)),
                pl.BlockSpec(sp["cr"].shape, const(3)), pl.BlockSpec(sp["ci"].shape, const(3)),
                pl.BlockSpec(sp["tab"].shape, const(3)), pl.BlockSpec((1, D_MODEL), const(2)),
                pl.BlockSpec(sblk, smap), pl.BlockSpec(sblk, smap)]
    args = [u, sp["bdr"], sp["bdi"], sp["cr"], sp["ci"], sp["tab"], sp["d"], x0r, x0i]
    aliases = {}
    if y_prev is not None:
        in_specs.append(pl.BlockSpec(memory_space=pl.ANY))
        args.append(y_prev)
        aliases = {len(args) - 1: 0}
    return pl.pallas_call(
        functools.partial(_s5_kernel, chained=chained, aliased=y_prev is not None),
        grid=grid,
        in_specs=in_specs,
        out_specs=[pl.BlockSpec((S5_TILE, D_MODEL), rmap), pl.BlockSpec(sblk, smap), pl.BlockSpec(sblk, smap)],
        out_shape=[jax.ShapeDtypeStruct((ROWS, D_MODEL), F32), jax.ShapeDtypeStruct(x0r.shape, F32),
                   jax.ShapeDtypeStruct(x0r.shape, F32)],
        scratch_shapes=[pltpu.VMEM((S5_TILE, S5_STATE), F32), pltpu.VMEM((S5_TILE, S5_STATE), F32),
                        pltpu.VMEM((8, S5_STATE), F32)],
        input_output_aliases=aliases,
        compiler_params=_cp("arbitrary", "arbitrary"),
    )(*args)


def _prep_ab_in(w):
    qkv, b, a, gate, nq, nkv, ngate = jnp.split(w, (1536, 1540, 1544, 2056, 2568, 3336), axis=-1)
    kc, vc, ks, vs, kw, vw = jnp.split(nkv, 6, axis=-1)
    small = jnp.concatenate([b, a, ngate, jnp.zeros(w.shape[:-1] + (96,), w.dtype)], axis=-1)
    return jnp.concatenate([qkv, ks, vs, kw, vw, kc, vc, gate, nq, small], axis=-1).astype(BF16)


AB_SPLITS = (1536, 768, 512, 512, 128)


def _prep_cmp(pos, w1, w2):
    top = w1[:CMP_STRIDE * NSA_HD].reshape(CMP_STRIDE, NSA_HD, CMP_HIDDEN)
    bot = w1[CMP_STRIDE * NSA_HD:].reshape(CMP_STRIDE, NSA_HD, CMP_HIDDEN)
    w1bd = jnp.zeros((CMP_STRIDE, 2 * NSA_HD, 4 * CMP_HIDDEN), F32)
    w2bd = jnp.zeros((2 * CMP_HIDDEN, 2 * NSA_HD), F32)
    for g in range(NSA_KV):
        rs = slice(g * NSA_HD, (g + 1) * NSA_HD)
        w1bd = w1bd.at[:, rs, g * CMP_HIDDEN:(g + 1) * CMP_HIDDEN].set(top)
        w1bd = w1bd.at[:, rs, (2 + g) * CMP_HIDDEN:(3 + g) * CMP_HIDDEN].set(bot)
        w2bd = w2bd.at[g * CMP_HIDDEN:(g + 1) * CMP_HIDDEN, rs].set(w2)
    posab = jnp.zeros((CMP_STRIDE, 8, 2 * NSA_HD), F32)
    posab = posab.at[:, 0, :].set(jnp.tile(pos[:CMP_STRIDE], (1, NSA_KV)))
    posab = posab.at[:, 1, :].set(jnp.tile(pos[CMP_STRIDE:], (1, NSA_KV)))
    return posab, w1bd.astype(BF16), w2bd.astype(BF16)


def _cmul(ar, ai, br, bi):
    return ar * br - ai * bi, ar * bi + ai * br


def _prep_s5(a_re, a_im, b_re, b_im, c_re, c_im, d, log_dt):
    dt = jnp.exp(log_dt)[:, None]
    lr = jnp.minimum(a_re, S5_MAX_RE)
    li = a_im
    mag = jnp.exp(lr * dt)
    ar = mag * jnp.cos(li * dt)
    ai = mag * jnp.sin(li * dt)
    den = lr * lr + li * li
    fr = ((ar - 1.0) * lr + ai * li) / den
    fi = (ai * lr - (ar - 1.0) * li) / den
    bbr = fr[..., None] * b_re - fi[..., None] * b_im
    bbi = fr[..., None] * b_im + fi[..., None] * b_re
    nb = S5_GROUPS // S5_BLK
    eye = jnp.eye(S5_BLK, dtype=F32)

    def bd_in(m):
        m4 = jnp.swapaxes(m, 1, 2).reshape(nb, S5_BLK, S5_CH, S5_P)
        return jnp.einsum('bgcp,gh->bgchp', m4, eye).reshape(nb, S5_BLK * S5_CH, S5_BLK * S5_P).astype(BF16)

    def bd_out(m):
        m4 = jnp.swapaxes(m, 1, 2).reshape(nb, S5_BLK, S5_P, S5_CH)
        return jnp.einsum('bgpc,gh->bgphc', m4, eye).reshape(nb, S5_BLK * S5_P, S5_BLK * S5_CH).astype(BF16)

    a1 = (ar.reshape(1, S5_STATE), ai.reshape(1, S5_STATE))
    pw = [a1]
    for _ in range(7):
        pw.append(_cmul(*pw[-1], *a1))
    row = jnp.arange(8)[:, None]
    tabs = []
    for sh in (1, 2, 4):
        for part in pw[sh - 1]:
            tabs.append(jnp.where(row >= sh, part, 0.0))
    tabs.append(jnp.concatenate([p[0] for p in pw], axis=0))
    tabs.append(jnp.concatenate([p[1] for p in pw], axis=0))
    return {"bdr": bd_in(bbr), "bdi": bd_in(bbi), "cr": bd_out(c_re), "ci": bd_out(c_im),
            "tab": jnp.stack(tabs), "d": d.reshape(1, D_MODEL)}


def _heads(a, n, t):
    return a.reshape(n, t, NSA_KV, NSA_HD)


def kernel(x_prompt, x_sample, mem_prompt, cache_mem_k, cache_mem_v, state_gdn, state_gdn_conv, cache_cmp_k, cache_cmp_v, cache_slc_k, cache_slc_v, cache_win_k, cache_win_v, state_s5_re, state_s5_im, page_table, norm_ffn1, w_ffn1_gate, w_ffn1_up, w_ffn1_down, norm_mix, norm_xq, norm_mem, w_xq, w_xk, w_xv, w_xo, norm_ffn2, w_ffn2_gate, w_ffn2_up, w_ffn2_down, norm_final, w_in_ab, w_out_ab, gdn_conv, gdn_a_log, gdn_dt_bias, gdn_norm, cmp_pos_k, cmp_w1_k, cmp_w2_k, cmp_pos_v, cmp_w1_v, cmp_w2_v, w_in_c, s5_a_re, s5_a_im, s5_b_re, s5_b_im, s5_c_re, s5_c_im, s5_d, s5_log_dt, w_glu, w_out_c):
    bf = lambda w: w.astype(BF16)
    n_ab = w_in_ab.shape[0]
    n_pool = cache_cmp_k.shape[1]
    x = jnp.concatenate([x_prompt.reshape(RP, D_MODEL), x_sample.reshape(RS, D_MODEL)], axis=0)

    memkv = mem_kv_all(mem_prompt.reshape(BATCH * N_MEM, D_MODEL), norm_mem, bf(jnp.concatenate([w_xk, w_xv], axis=-1)))
    mem_k_prompt = memkv[:, :, :D_MODEL].reshape(DEPTH, BATCH, N_MEM, X_HEADS, X_HD)
    mem_v_prompt = memkv[:, :, D_MODEL:].reshape(DEPTH, BATCH, N_MEM, X_HEADS, X_HD)

    paged = lambda c: c.reshape(n_ab, n_pool, PAGE_SIZE, NSA_KV * NSA_HD)
    cck, ccv, csk, csv = paged(cache_cmp_k), paged(cache_cmp_v), paged(cache_slc_k), paged(cache_slc_v)
    cwk = cache_win_k.reshape(n_ab, DEC_BATCH, WINDOW, NSA_KV * NSA_HD)
    cwv = cache_win_v.reshape(n_ab, DEC_BATCH, WINDOW, NSA_KV * NSA_HD)

    ab_p, ab_s, c_p, c_s = [], [], [], []
    for l in range(DEPTH):
        i = l // 2
        x = glu_mlp(x, norm_ffn1[l], bf(w_ffn1_gate[l]), bf(w_ffn1_up[l]), bf(w_ffn1_down[l]), x, 1024, 256, True, True, 0.5)
        if l % 2 == 0:
            qkv, nkv, gate, nq, small = norm_matmul(x, norm_mix[l], _prep_ab_in(w_in_ab[i]), AB_SPLITS, 512)
            gpar = jnp.zeros((8, 128), F32).at[0, 4:8].set(gdn_a_log[i]).at[1, 4:8].set(gdn_dt_bias[i])
            nw = gdn_norm[i].reshape(1, GDN_DK)
            cb_s = jnp.zeros((DEC_BATCH, 8, GDN_QKV), F32).at[:, 5:8].set(state_gdn_conv[i])
            oa_p, st_p = gdn(qkv, small, gate, gdn_conv[i], gpar, nw, jnp.zeros((BATCH, 8, GDN_QKV), F32),
                             jnp.zeros((BATCH, GDN_HEADS, GDN_DK, GDN_DK), F32), 0, BATCH, SEQ, 256, GDN_CHUNK)
            oa_s, st_s = gdn(qkv, small, gate, gdn_conv[i], gpar, nw, cb_s, state_gdn[i], RP, DEC_BATCH, DEC_SEQ,
                             DEC_SEQ, math.gcd(DEC_SEQ, GDN_CHUNK))
            cw = _prep_cmp(cmp_pos_k[i], cmp_w1_k[i], cmp_w2_k[i]) + _prep_cmp(cmp_pos_v[i], cmp_w1_v[i], cmp_w2_v[i])
            kcmp_p, vcmp_p = compress_prompt(nkv, cw, BATCH, SEQ)
            ob_p = nsa_prompt(nq, small, bf(nkv[:RP, :512]), kcmp_p, vcmp_p, BATCH, SEQ)
            kcmp_s, vcmp_s = compress_sample(page_table, cck, ccv, i, cw)
            ob_s = nsa_sample(page_table, csk, csv, i, nq, small, nkv, kcmp_s, vcmp_s, cwk, cwv)
            w_out = bf(w_out_ab[i])
            x = matmul_residual([(oa_p, oa_s, w_out[:512]), (ob_p, ob_s, w_out[512:])], x, 512)
            qkv_p = qkv[:RP].reshape(BATCH, SEQ, GDN_QKV)
            qkv_s = qkv[RP:].reshape(DEC_BATCH, DEC_SEQ, GDN_QKV)
            col = lambda a, c: a[:, c * 128:(c + 1) * 128]
            nkv_p, nkv_s = nkv[:RP], nkv[RP:]
            hp = lambda c: _heads(col(nkv_p, c), BATCH, SEQ)
            hs = lambda c: _heads(col(nkv_s, c), DEC_BATCH, DEC_SEQ)
            ab_p.append((qkv_p[:, SEQ - 3:], st_p, hp(4), hp(5), hp(0), hp(1), hp(2)[:, SEQ - WINDOW:], hp(3)[:, SEQ - WINDOW:]))
            ab_s.append((qkv_s[:, DEC_SEQ - 3:], st_s, hs(4), hs(5), hs(0), hs(1),
                         jnp.concatenate([cache_win_k[i][:, DEC_SEQ:], hs(2)], axis=1),
                         jnp.concatenate([cache_win_v[i][:, DEC_SEQ:], hs(3)], axis=1)))
        else:
            (u,) = norm_matmul(x, norm_mix[l], bf(w_in_c[i]), (D_MODEL,), 512)
            sp = _prep_s5(s5_a_re[i], s5_a_im[i], s5_b_re[i], s5_b_im[i], s5_c_re[i], s5_c_im[i], s5_d[i], s5_log_dt[i])
            z0 = jnp.zeros((BATCH, 1, S5_STATE), F32)
            y, fr_p, fi_p = s5_scan(u, sp, z0, z0, 0, BATCH, SEQ, True)
            per_tile = S5_TILE // DEC_SEQ
            x0r = state_s5_re[i].reshape(RS // S5_TILE, per_tile, S5_STATE)
            x0i = state_s5_im[i].reshape(RS // S5_TILE, per_tile, S5_STATE)
            y, fr_s, fi_s = s5_scan(u, sp, x0r, x0i, RP, DEC_BATCH, DEC_SEQ, False, y_prev=y)
            wg = bf(w_glu[i])
            x = glu_mlp(y, norm_mix[l], wg[:, :D_MODEL], wg[:, D_MODEL:], bf(w_out_c[i]), x, 1024, 256, False, False, 1.0)
            c_p.append((fr_p.reshape(BATCH, S5_GROUPS, S5_P), fi_p.reshape(BATCH, S5_GROUPS, S5_P)))
            c_s.append((fr_s.reshape(DEC_BATCH, S5_GROUPS, S5_P), fi_s.reshape(DEC_BATCH, S5_GROUPS, S5_P)))
        (q,) = norm_matmul(x, norm_xq[l], bf(w_xq[l]), (D_MODEL,), 512)
        mk = memkv[l, :, :D_MODEL].reshape(BATCH, N_MEM, D_MODEL)
        mv = memkv[l, :, D_MODEL:].reshape(BATCH, N_MEM, D_MODEL)
        o_p = cross_attention(q, mk, mv, 0, BATCH, SEQ, 512)
        o_s = cross_attention(q, cache_mem_k[l].reshape(DEC_BATCH, N_MEM, D_MODEL),
                              cache_mem_v[l].reshape(DEC_BATCH, N_MEM, D_MODEL), RP, DEC_BATCH, DEC_SEQ, DEC_SEQ)
        x = matmul_residual([(o_p, o_s, bf(w_xo[l]))], x, 512)
        x = glu_mlp(x, norm_ffn2[l], bf(w_ffn2_gate[l]), bf(w_ffn2_up[l]), bf(w_ffn2_down[l]), x, 1024, 256, True, True, 0.5)

    y = rmsnorm_rows(x, norm_final, 512)
    st = lambda grp, j: jnp.stack([t[j] for t in grp])
    return (y[:RP].reshape(BATCH, SEQ, D_MODEL), y[RP:].reshape(DEC_BATCH, DEC_SEQ, D_MODEL),
            mem_k_prompt, mem_v_prompt,
            st(ab_p, 1), st(ab_s, 1), st(ab_p, 0), st(ab_s, 0),
            st(ab_p, 2), st(ab_p, 3), st(ab_p, 4), st(ab_p, 5),
            st(ab_s, 2), st(ab_s, 3), st(ab_s, 4), st(ab_s, 5),
            st(ab_p, 6), st(ab_p, 7), st(ab_s, 6), st(ab_s, 7),
            st(c_p, 0), st(c_p, 1), st(c_s, 0), st(c_s, 1))
```

```python
import functools
import math

import jax
import jax.numpy as jnp
from jax import lax
from jax.experimental import pallas as pl
from jax.experimental.pallas import tpu as pltpu

F32 = jnp.float32
BF16 = jnp.bfloat16
HIGHEST = lax.Precision.HIGHEST

D_MODEL = 1024
BATCH = 2
SEQ = 8192
DEPTH = 4
DEC_BATCH = 128
DEC_SEQ = 8
PAST_LEN = 2048
PAGE_SIZE = 128
N_PAGES = PAST_LEN // PAGE_SIZE
RP = BATCH * SEQ
RS = DEC_BATCH * DEC_SEQ
ROWS = RP + RS

GDN_HEADS = 4
GDN_DK = 128
GDN_QKV = 1536
GDN_CONV = 4
GDN_CHUNK = 64
NSA_HEADS = 8
NSA_KV = 2
NSA_HD = 64
NSA_REP = 4
CMP_STRIDE = 16
CMP_LEN = 32
CMP_HIDDEN = 128
SEL_BLOCK = 64
SEL_TOPN = 16
WINDOW = 512
Q_BLOCK = 128
SEL_FORCED = 1.0e4
NEG = -1.0e30
SEL_MASK = 2.0 ** 100
S5_CH = 16
S5_GROUPS = 64
S5_P = 64
S5_STATE = S5_GROUPS * S5_P
S5_MAX_RE = -1.0e-4
N_MEM = 256
X_HEADS = 4
X_HD = 256
D_FF = 2816
EPS = 1.0e-6

VMEM_LIMIT = 56 * 1024 * 1024


def _cp(*sem):
    return pltpu.CompilerParams(dimension_semantics=sem, vmem_limit_bytes=VMEM_LIMIT)


def _dot(a, b, precision=None):
    return jnp.dot(a, b, preferred_element_type=F32, precision=precision)


def _dot_nt(a, b, precision=None):
    return lax.dot_general(a, b, (((1,), (1,)), ((), ())), preferred_element_type=F32, precision=precision)


def _sigmoid(x):
    return 1.0 / (1.0 + jnp.exp(-x))


def _silu(x):
    return x * _sigmoid(x)


def _rms(x, g):
    return x * lax.rsqrt(jnp.mean(x * x, axis=-1, keepdims=True) + EPS) * g


def _norm_matmul_kernel(x_ref, g_ref, w_ref, *o_refs, norm, splits):
    x = x_ref[...]
    if norm:
        x = _rms(x, g_ref[...])
    h = x.astype(BF16)
    off = 0
    for o_ref, wd in zip(o_refs, splits):
        o_ref[...] = _dot(h, w_ref[:, off:off + wd])
        off += wd


def norm_matmul(x, g, w, splits, tm, norm=True):
    rows, k = x.shape
    n = w.shape[1]
    assert sum(splits) == n and rows % tm == 0
    outs = pl.pallas_call(
        functools.partial(_norm_matmul_kernel, norm=norm, splits=tuple(splits)),
        grid=(rows // tm,),
        in_specs=[pl.BlockSpec((tm, k), lambda i: (i, 0)),
                  pl.BlockSpec((1, k), lambda i: (0, 0)),
                  pl.BlockSpec((k, n), lambda i: (0, 0))],
        out_specs=[pl.BlockSpec((tm, wd), lambda i: (i, 0)) for wd in splits],
        out_shape=[jax.ShapeDtypeStruct((rows, wd), F32) for wd in splits],
        compiler_params=_cp("parallel"),
    )(x, g.reshape(1, k), w)
    return outs


def _mem_kv_kernel(x_ref, g_ref, w_ref, o_ref):
    h = _rms(x_ref[...], g_ref[0]).astype(BF16)
    o_ref[0] = _dot(h, w_ref[0])


def mem_kv_all(mem2d, g, w):
    m, k = mem2d.shape
    nl, _, n = w.shape
    return pl.pallas_call(
        _mem_kv_kernel,
        grid=(nl,),
        in_specs=[pl.BlockSpec((m, k), lambda l: (0, 0)),
                  pl.BlockSpec((1, 1, k), lambda l: (l, 0, 0)),
                  pl.BlockSpec((1, k, n), lambda l: (l, 0, 0))],
        out_specs=pl.BlockSpec((1, m, n), lambda l: (l, 0, 0)),
        out_shape=jax.ShapeDtypeStruct((nl, m, n), F32),
        compiler_params=_cp("parallel"),
    )(mem2d, g.reshape(nl, 1, k), w)


def _matmul_res_kernel(*refs, n_terms, n_p):
    r_ref, o_ref = refs[3 * n_terms], refs[3 * n_terms + 1]
    i = pl.program_id(0)

    def run(sel):
        acc = r_ref[...]
        for t in range(n_terms):
            acc = acc + _dot(refs[3 * t + sel][...].astype(BF16), refs[3 * t + 2][...])
        o_ref[...] = acc

    @pl.when(i < n_p)
    def _():
        run(0)

    @pl.when(i >= n_p)
    def _():
        run(1)


def matmul_residual(terms, res, tm):
    rows, n = res.shape
    n_p = RP // tm
    args, specs = [], []
    for a_p, a_s, w in terms:
        k = w.shape[0]
        args += [a_p, a_s, w]
        specs += [pl.BlockSpec((tm, k), lambda i: (jnp.minimum(i, n_p - 1), 0)),
                  pl.BlockSpec((tm, k), lambda i: (jnp.maximum(i - n_p, 0), 0)),
                  pl.BlockSpec((k, n), lambda i: (0, 0))]
    return pl.pallas_call(
        functools.partial(_matmul_res_kernel, n_terms=len(terms), n_p=n_p),
        grid=(rows // tm,),
        in_specs=specs + [pl.BlockSpec((tm, n), lambda i: (i, 0))],
        out_specs=pl.BlockSpec((tm, n), lambda i: (i, 0)),
        out_shape=jax.ShapeDtypeStruct((rows, n), F32),
        compiler_params=_cp("arbitrary"),
    )(*args, res)


def _glu_mlp_kernel(src_ref, g_ref, wa_ref, wb_ref, wd_ref, res_ref, o_ref, h_scr, acc_scr, *, norm, swiglu, scale):
    j = pl.program_id(1)

    @pl.when(j == 0)
    def _():
        x = src_ref[...]
        if norm:
            x = _rms(x, g_ref[...])
        h_scr[...] = x.astype(BF16)
        acc_scr[...] = jnp.zeros_like(acc_scr)

    h = h_scr[...]
    a = _dot(h, wa_ref[...])
    b = _dot(h, wb_ref[...])
    s = _silu(a) * b if swiglu else a * _sigmoid(b)
    acc_scr[...] += _dot(s.astype(BF16), wd_ref[...])

    @pl.when(j == pl.num_programs(1) - 1)
    def _():
        o_ref[...] = res_ref[...] + scale * acc_scr[...]


def glu_mlp(src, g, wa, wb, wd, res, tm, tf, norm, swiglu, scale):
    rows, k = src.shape
    ff = wa.shape[1]
    n = wd.shape[1]
    return pl.pallas_call(
        functools.partial(_glu_mlp_kernel, norm=norm, swiglu=swiglu, scale=scale),
        grid=(rows // tm, ff // tf),
        in_specs=[pl.BlockSpec((tm, k), lambda i, j: (i, 0)),
                  pl.BlockSpec((1, k), lambda i, j: (0, 0)),
                  pl.BlockSpec((k, tf), lambda i, j: (0, j)),
                  pl.BlockSpec((k, tf), lambda i, j: (0, j)),
                  pl.BlockSpec((tf, n), lambda i, j: (j, 0)),
                  pl.BlockSpec((tm, n), lambda i, j: (i, 0))],
        out_specs=pl.BlockSpec((tm, n), lambda i, j: (i, 0)),
        out_shape=jax.ShapeDtypeStruct((rows, n), F32),
        scratch_shapes=[pltpu.VMEM((tm, k), BF16), pltpu.VMEM((tm, n), F32)],
        compiler_params=_cp("parallel", "arbitrary"),
    )(src, g.reshape(1, k), wa, wb, wd, res)


def _rmsnorm_kernel(x_ref, g_ref, o_ref):
    o_ref[...] = _rms(x_ref[...], g_ref[...])


def rmsnorm_rows(x, g, tm):
    rows, k = x.shape
    return pl.pallas_call(
        _rmsnorm_kernel,
        grid=(rows // tm,),
        in_specs=[pl.BlockSpec((tm, k), lambda i: (i, 0)), pl.BlockSpec((1, k), lambda i: (0, 0))],
        out_specs=pl.BlockSpec((tm, k), lambda i: (i, 0)),
        out_shape=jax.ShapeDtypeStruct((rows, k), F32),
        compiler_params=_cp("parallel"),
    )(x, g.reshape(1, k))


def _xattn_kernel(q_ref, k_ref, v_ref, o_ref):
    q = q_ref[...]
    for h in range(X_HEADS):
        sl = slice(h * X_HD, (h + 1) * X_HD)
        qh = (q[:, sl] * (X_HD ** -0.5)).astype(BF16)
        s = _dot_nt(qh, k_ref[:, sl].astype(BF16))
        m = jnp.max(s, axis=-1, keepdims=True)
        p = jnp.exp(s - m)
        p = p / jnp.sum(p, axis=-1, keepdims=True)
        o_ref[:, sl] = _dot(p.astype(BF16), v_ref[:, sl].astype(BF16))


def cross_attention(q_all, mk, mv, k_spec, v_spec, row_off, n_seq, t_seq, tq):
    nb = t_seq // tq
    off = row_off // tq
    return pl.pallas_call(
        _xattn_kernel,
        grid=(n_seq, nb),
        in_specs=[pl.BlockSpec((tq, D_MODEL), lambda n, i: (off + n * nb + i, 0)), k_spec, v_spec],
        out_specs=pl.BlockSpec((tq, D_MODEL), lambda n, i: (n * nb + i, 0)),
        out_shape=jax.ShapeDtypeStruct((n_seq * t_seq, D_MODEL), F32),
        compiler_params=_cp("parallel", "parallel"),
    )(q_all, mk, mv)


def _softplus(x):
    return jnp.maximum(x, 0.0) + jnp.log(1.0 + jnp.exp(-jnp.abs(x)))


def _split3(x, axis):
    hi = x.astype(BF16).astype(F32)
    return jnp.concatenate([hi, hi, x - hi], axis=axis).astype(BF16)


def _split3r(x, axis):
    hi = x.astype(BF16).astype(F32)
    return jnp.concatenate([hi, x - hi, hi], axis=axis).astype(BF16)


def _dot3(a, b):
    return _dot(_split3(a, 1), _split3r(b, 0))


def _cumsum_rows(tri3, g):
    g1 = g.astype(BF16).astype(F32)
    g2 = (g - g1).astype(BF16).astype(F32)
    g3 = g - g1 - g2
    return _dot(tri3, jnp.concatenate([g1, g2, g3], axis=0).astype(BF16))


def _gdn_kernel(qkv_ref, sm_ref, gate_ref, cw_ref, gp_ref, nw_ref, cb_ref, s0_ref, o_ref, sout_ref,
                xbuf, s_scr, *, seqs, rows, chunk):
    i = pl.program_id(1)
    n_chunks = rows // chunk

    @pl.when(i == 0)
    def _():
        xbuf[:, 0:8, :] = cb_ref[...]
        s_scr[...] = s0_ref[...]

    ri = lax.broadcasted_iota(jnp.int32, (chunk, chunk), 0)
    ci = lax.broadcasted_iota(jnp.int32, (chunk, chunk), 1)
    incl = ri >= ci
    strict = ri > ci
    tri = jnp.where(incl, 1.0, 0.0).astype(BF16)
    tri3 = jnp.concatenate([tri, tri, tri], axis=1)
    eye = jnp.where(ri == ci, 1.0, 0.0)
    levels = []
    w = 1
    while w < chunk:
        levels.append(((ri // (2 * w)) == (ci // (2 * w))) & ((ri % (2 * w)) >= w) & ((ci % (2 * w)) < w))
        w *= 2
    nw = nw_ref[...]
    cw = [cw_ref[j:j + 1, :] for j in range(GDN_CONV)]
    a_neg = -jnp.exp(gp_ref[0:1, :])
    dtb = gp_ref[1:2, :]

    probs = []
    for b in range(seqs):
        tok = slice(b * rows, (b + 1) * rows)
        xbuf[b, 8:8 + rows, :] = qkv_ref[tok, :]
        conv = xbuf[b, 5:5 + rows, :] * cw[0]
        for j in range(1, GDN_CONV):
            conv = conv + xbuf[b, 5 + j:5 + j + rows, :] * cw[j]
        tail = xbuf[b, 8 + rows - 3:8 + rows, :]
        xbuf[b, 5:8, :] = tail
        qkv = _silu(conv)
        sm = sm_ref[tok, :]
        beta_all = _sigmoid(sm)
        g_all = a_neg * _softplus(sm + dtb)
        for c in range(n_chunks):
            rs = slice(c * chunk, (c + 1) * chunk)
            gcum = _cumsum_rows(tri3, g_all[rs])
            gcum_t = gcum.T
            for h in range(GDN_HEADS):
                q = qkv[rs, h * GDN_DK:(h + 1) * GDN_DK]
                k = qkv[rs, 512 + h * GDN_DK:512 + (h + 1) * GDN_DK]
                v = qkv[rs, 1024 + h * GDN_DK:1024 + (h + 1) * GDN_DK]
                q = q * lax.rsqrt(jnp.sum(q * q, axis=-1, keepdims=True) + EPS) * (GDN_DK ** -0.5)
                k = k * lax.rsqrt(jnp.sum(k * k, axis=-1, keepdims=True) + EPS)
                beta = beta_all[rs, h:h + 1]
                gc_col = gcum[:, 4 + h:5 + h]
                gc_row = gcum_t[4 + h:5 + h, :]
                g_last = gcum[chunk - 1:chunk, 4 + h:5 + h]
                decay = jnp.where(incl, jnp.exp(jnp.minimum(gc_col - gc_row, 0.0)), 0.0)
                eg = jnp.exp(gc_col)
                kb = k * beta
                k3r = _split3r(k, 1)
                probs.append(dict(
                    b=b, c=c, h=h, decay=decay,
                    kk=_dot_nt(_split3(kb, 1), k3r), qk=_dot_nt(_split3(q, 1), k3r),
                    rhs=jnp.concatenate([v * beta, kb * eg], axis=1), qd=q * eg,
                    kd_t=(k * jnp.exp(g_last - gc_col)).T, gl=jnp.exp(g_last), minv=eye))
    for p in probs:
        p["lmat"] = jnp.where(strict, p["kk"] * p["decay"], 0.0)
        p["qk"] = jnp.where(incl, p["qk"] * p["decay"], 0.0)
    for off_blk in levels:
        for p in probs:
            p["t"] = _dot3(p["minv"], jnp.where(off_blk, p["lmat"], 0.0))
        for p in probs:
            p["minv"] = p["minv"] - _dot3(p["t"], p["minv"])
    for p in probs:
        p["uw"] = _dot3(p["minv"], p["rhs"])
    state = {(b, h): s_scr[b, h] for b in range(seqs) for h in range(GDN_HEADS)}
    for c in range(n_chunks):
        cur = [p for p in probs if p["c"] == c]
        for p in cur:
            p["s3r"] = _split3r(state[(p["b"], p["h"])], 0)
            p["v_new"] = p["uw"][:, :GDN_DK] - _dot(_split3(p["uw"][:, GDN_DK:], 1), p["s3r"])
        for p in cur:
            p["o"] = _dot(_split3(p["qd"], 1), p["s3r"]) + _dot3(p["qk"], p["v_new"])
            state[(p["b"], p["h"])] = state[(p["b"], p["h"])] * p["gl"] + _dot3(p["kd_t"], p["v_new"])
    for p in probs:
        b, c, h = p["b"], p["c"], p["h"]
        r0 = b * rows + c * chunk
        o = _rms(p["o"], nw) * _silu(gate_ref[r0:r0 + chunk, h * GDN_DK:(h + 1) * GDN_DK])
        o_ref[r0:r0 + chunk, h * GDN_DK:(h + 1) * GDN_DK] = o
    for (b, h), s in state.items():
        s_scr[b, h] = s

    @pl.when(i == pl.num_programs(1) - 1)
    def _():
        sout_ref[...] = s_scr[...]


def gdn(qkv, small, gate, conv_w, gpar, norm_w, conv_buf8, s0, row_off, n_seq, t_seq, seqs, rows, chunk):
    nb = t_seq // rows
    blk = seqs * rows
    off = row_off // blk
    rmap = lambda n, i: (off + n * nb + i, 0)
    return pl.pallas_call(
        functools.partial(_gdn_kernel, seqs=seqs, rows=rows, chunk=chunk),
        grid=(n_seq // seqs, nb),
        in_specs=[pl.BlockSpec((blk, GDN_QKV), rmap),
                  pl.BlockSpec((blk, 128), rmap),
                  pl.BlockSpec((blk, 512), rmap),
                  pl.BlockSpec((GDN_CONV, GDN_QKV), lambda n, i: (0, 0)),
                  pl.BlockSpec((8, 128), lambda n, i: (0, 0)),
                  pl.BlockSpec((1, GDN_DK), lambda n, i: (0, 0)),
                  pl.BlockSpec((seqs, 8, GDN_QKV), lambda n, i: (n, 0, 0)),
                  pl.BlockSpec((seqs, GDN_HEADS, GDN_DK, GDN_DK), lambda n, i: (n, 0, 0, 0))],
        out_specs=[pl.BlockSpec((blk, 512), lambda n, i: (n * nb + i, 0)),
                   pl.BlockSpec((seqs, GDN_HEADS, GDN_DK, GDN_DK), lambda n, i: (n, 0, 0, 0))],
        out_shape=[jax.ShapeDtypeStruct((n_seq * t_seq, 512), F32),
                   jax.ShapeDtypeStruct((n_seq, GDN_HEADS, GDN_DK, GDN_DK), F32)],
        scratch_shapes=[pltpu.VMEM((seqs, 8 + rows, GDN_QKV), F32),
                        pltpu.VMEM((seqs, GDN_HEADS, GDN_DK, GDN_DK), F32)],
        compiler_params=_cp("parallel", "arbitrary"),
    )(qkv, small, gate, conv_w, gpar, norm_w, conv_buf8, s0)


def _compress_core(xj, posab_ref, w1_ref, w2_ref, m):
    acc = jnp.zeros((m + 8, 4 * CMP_HIDDEN), F32)
    for j in range(CMP_STRIDE):
        lhs = jnp.concatenate([xj(j), posab_ref[j]], axis=0).astype(BF16)
        acc = acc + _dot(lhs, w1_ref[j])
    top = acc[0:m, 0:2 * CMP_HIDDEN]
    bot = acc[0:m, 2 * CMP_HIDDEN:]
    c = acc[m:m + 1, 0:2 * CMP_HIDDEN] + acc[m + 1:m + 2, 2 * CMP_HIDDEN:]
    h = top + pltpu.roll(bot, m - 1, 0) + c
    out = _dot(_silu(h).astype(BF16), w2_ref[...])
    row = lax.broadcasted_iota(jnp.int32, (m, 2 * NSA_HD), 0)
    return jnp.where(row < m - 1, out, 0.0)


def _compress_prompt_kernel(kc_ref, vc_ref, pk_ref, w1k_ref, w2k_ref, pv_ref, w1v_ref, w2v_ref, ok_ref, ov_ref, *, m):
    ok_ref[0] = _compress_core(lambda j: kc_ref[pl.ds(j, m, stride=CMP_STRIDE), :], pk_ref, w1k_ref, w2k_ref, m)
    ov_ref[0] = _compress_core(lambda j: vc_ref[pl.ds(j, m, stride=CMP_STRIDE), :], pv_ref, w1v_ref, w2v_ref, m)


def _cmp_weight_specs(nmap):
    return [pl.BlockSpec((CMP_STRIDE, 8, 128), nmap(3)),
            pl.BlockSpec((CMP_STRIDE, 128, 4 * CMP_HIDDEN), nmap(3)),
            pl.BlockSpec((2 * CMP_HIDDEN, 2 * NSA_HD), nmap(2))]


def compress_prompt(nkv, cw, n_seq, t_seq):
    m = t_seq // CMP_STRIDE
    zmap = lambda nd: (lambda n: (0,) * nd)
    out = jax.ShapeDtypeStruct((n_seq, m, 2 * NSA_HD), F32)
    return pl.pallas_call(
        functools.partial(_compress_prompt_kernel, m=m),
        grid=(n_seq,),
        in_specs=[pl.BlockSpec((t_seq, 128), lambda n: (n, 4)), pl.BlockSpec((t_seq, 128), lambda n: (n, 5))]
        + _cmp_weight_specs(zmap) + _cmp_weight_specs(zmap),
        out_specs=[pl.BlockSpec((1, m, 2 * NSA_HD), lambda n: (n, 0, 0))] * 2,
        out_shape=[out, out],
        compiler_params=_cp("parallel"),
    )(nkv, nkv, *cw)


def _compress_sample_kernel(pt_ref, *refs, m):
    del pt_ref
    kpages = refs[0:N_PAGES]
    vpages = refs[N_PAGES:2 * N_PAGES]
    pk_ref, w1k_ref, w2k_ref, pv_ref, w1v_ref, w2v_ref, ok_ref, ov_ref = refs[2 * N_PAGES:]
    per_page = PAGE_SIZE // CMP_STRIDE

    def rows_of(pages):
        return lambda j: jnp.concatenate([p[pl.ds(j, per_page, stride=CMP_STRIDE), :] for p in pages], axis=0)

    ok_ref[0] = _compress_core(rows_of(kpages), pk_ref, w1k_ref, w2k_ref, m)
    ov_ref[0] = _compress_core(rows_of(vpages), pv_ref, w1v_ref, w2v_ref, m)


def _page_specs(layer):
    return [pl.BlockSpec((None, None, PAGE_SIZE, 128), functools.partial(lambda n, pt, p: (layer, pt[n, p], 0, 0), p=p))
            for p in range(N_PAGES)]


def compress_sample(page_table, cache_k, cache_v, layer, cw):
    m = PAST_LEN // CMP_STRIDE
    zmap = lambda nd: (lambda n, pt: (0,) * nd)
    out = jax.ShapeDtypeStruct((DEC_BATCH, m, 2 * NSA_HD), F32)
    return pl.pallas_call(
        functools.partial(_compress_sample_kernel, m=m),
        grid_spec=pltpu.PrefetchScalarGridSpec(
            num_scalar_prefetch=1, grid=(DEC_BATCH,),
            in_specs=_page_specs(layer) + _page_specs(layer) + _cmp_weight_specs(zmap) + _cmp_weight_specs(zmap),
            out_specs=[pl.BlockSpec((1, m, 2 * NSA_HD), lambda n, pt: (n, 0, 0))] * 2),
        out_shape=[out, out],
        compiler_params=_cp("arbitrary"),
    )(page_table, *([cache_k] * N_PAGES), *([cache_v] * N_PAGES), *cw)


def _masked_softmax(s, mask):
    s = jnp.where(mask, s, NEG)
    m = jnp.max(s, axis=-1, keepdims=True)
    p = jnp.where(mask, jnp.exp(s - m), 0.0)
    l = jnp.sum(p, axis=-1, keepdims=True)
    return p / jnp.where(l > 0.0, l, 1.0)


def _stack_heads(nq, g, tq):
    parts = [nq[:, (g * NSA_REP + r) * NSA_HD:(g * NSA_REP + r + 1) * NSA_HD] for r in range(NSA_REP)]
    return (jnp.concatenate(parts, axis=0) * (NSA_HD ** -0.5)).astype(BF16)


def _rep(mask, tq):
    return jnp.concatenate([mask] * NSA_REP, axis=0)


def _cmp_branch(q4, kcmp, vcmp, qpos, tq, n_blk):
    n_cmp = kcmp.shape[0]
    s = _dot_nt(q4, kcmp.astype(BF16))
    cmp_end = lax.broadcasted_iota(jnp.int32, (tq, n_cmp), 1) * CMP_STRIDE + (CMP_LEN - 1)
    p = _masked_softmax(s, _rep(cmp_end <= qpos, tq))
    o = _dot(p.astype(BF16), vcmp.astype(BF16))
    psum = p[0:tq] + p[tq:2 * tq] + p[2 * tq:3 * tq] + p[3 * tq:4 * tq]
    sj = lax.broadcasted_iota(jnp.int32, (n_blk, n_cmp), 0) * SEL_BLOCK
    ci = lax.broadcasted_iota(jnp.int32, (n_blk, n_cmp), 1) * CMP_STRIDE
    ov = jnp.clip(jnp.minimum(ci + CMP_LEN, sj + SEL_BLOCK) - jnp.maximum(ci, sj), 0, CMP_LEN).astype(F32) / CMP_LEN
    ov = ov.astype(BF16)
    p1 = psum.astype(BF16).astype(F32)
    p2 = (psum - p1).astype(BF16).astype(F32)
    p3 = psum - p1 - p2
    imp_t = _dot_nt(jnp.concatenate([ov, ov, ov], axis=1), jnp.concatenate([p1, p2, p3], axis=1).astype(BF16))
    return o, imp_t


def _select_blocks(imp_ts, qpos_row, tq, n_blk):
    blk = lax.broadcasted_iota(jnp.int32, (n_blk, tq), 0)
    cur = qpos_row // SEL_BLOCK
    valid = blk <= cur
    forced = valid & ((blk == 0) | (blk >= cur - 1))
    works = [jnp.where(valid, jnp.where(forced, SEL_FORCED, imp_t), NEG) for imp_t in imp_ts]
    sels = [jnp.zeros((n_blk, tq), F32) for _ in imp_ts]
    for _ in range(SEL_TOPN):
        for j in range(len(works)):
            m = jnp.max(works[j], axis=0, keepdims=True)
            idx = jnp.min(jnp.where(works[j] == m, blk, n_blk), axis=0, keepdims=True)
            pick = blk == idx
            sels[j] = jnp.where(pick, 1.0, sels[j])
            works[j] = jnp.where(pick, -jnp.inf, works[j])
    return [s.T for s in sels]


def _expand_sel(sel, first_blk, n_keys):
    n_blk = sel.shape[1]
    bj = lax.broadcasted_iota(jnp.int32, (n_blk, n_keys), 0)
    kb = lax.broadcasted_iota(jnp.int32, (n_blk, n_keys), 1) // SEL_BLOCK + first_blk
    e = jnp.where(bj == kb, 1.0, 0.0).astype(BF16)
    return _dot(sel.astype(BF16), e) > 0.5


def _gate_mix(o_ref, gates, g, o_cmp, o_slc, o_win, tq):
    for r in range(NSA_REP):
        h = g * NSA_REP + r
        rs = slice(r * tq, (r + 1) * tq)
        c0 = 8 + 3 * h
        o = (gates[:, c0:c0 + 1] * o_cmp[rs] + gates[:, c0 + 1:c0 + 2] * o_slc[rs] + gates[:, c0 + 2:c0 + 3] * o_win[rs])
        o_ref[:, h * NSA_HD:(h + 1) * NSA_HD] = o


def _nsa_prompt_kernel(nq_ref, sm_ref, kv_ref, et_ref, kcmp_ref, vcmp_ref, o_ref):
    tq = Q_BLOCK
    i = pl.program_id(1)
    qpos = i * tq + lax.broadcasted_iota(jnp.int32, (tq, 1), 0)
    nq = nq_ref[...]
    gates = _sigmoid(sm_ref[...])
    kc = 512
    qpos_row = i * tq + lax.broadcasted_iota(jnp.int32, (1, tq), 1)
    q4s, o_cmps, imp_ts = [], [], []
    for g in range(NSA_KV):
        gs = slice(g * NSA_HD, (g + 1) * NSA_HD)
        q4s.append(_stack_heads(nq, g, tq))
        o_cmp, imp_t = _cmp_branch(q4s[g], kcmp_ref[0, :, gs], vcmp_ref[0, :, gs], qpos, tq, 128)
        o_cmps.append(o_cmp)
        imp_ts.append(imp_t)
    sels = _select_blocks(imp_ts, qpos_row, tq, 128)
    qas = [jnp.concatenate([_rep(jnp.where(sels[g] > 0.5, 0.0, -SEL_MASK), tq).astype(BF16), q4s[g]], axis=1)
           for g in range(NSA_KV)]

    def slc_step(c, carry, diagonal):
        start = pl.multiple_of(c * kc, kc)
        et = et_ref[pl.ds(start, kc), :]
        ss = [_dot_nt(qas[g], jnp.concatenate([et, kv_ref[pl.ds(start, kc), g * NSA_HD:(g + 1) * NSA_HD]], axis=1))
              for g in range(NSA_KV)]
        if diagonal:
            causal = _rep(start + lax.broadcasted_iota(jnp.int32, (tq, kc), 1) <= qpos, tq)
            ss = [jnp.where(causal, s, -SEL_MASK) for s in ss]
        out = []
        for g in range(NSA_KV):
            m_i, l_i, acc = carry[g]
            m_new = jnp.maximum(m_i, jnp.max(ss[g], axis=-1, keepdims=True))
            alpha = jnp.exp(m_i - m_new)
            p = jnp.exp(ss[g] - m_new)
            l_new = alpha * l_i + jnp.sum(p, axis=-1, keepdims=True)
            v = kv_ref[pl.ds(start, kc), 128 + g * NSA_HD:128 + (g + 1) * NSA_HD]
            out.append((m_new, l_new, alpha * acc + _dot(p.astype(BF16), v)))
        return tuple(out)

    init = tuple((jnp.full((NSA_REP * tq, 1), -3.0e38, F32), jnp.zeros((NSA_REP * tq, 1), F32),
                  jnp.zeros((NSA_REP * tq, NSA_HD), F32)) for _ in range(NSA_KV))
    n_full = i // (kc // tq)
    carry = lax.fori_loop(0, n_full, functools.partial(slc_step, diagonal=False), init)
    fin = slc_step(n_full, carry, True)

    for g in range(NSA_KV):
        q4, o_cmp = q4s[g], o_cmps[g]
        _, l_f, acc = fin[g]
        o_slc = acc / jnp.where(l_f > 0.0, l_f, 1.0)

        nband = WINDOW + tq
        wstart = pl.multiple_of(jnp.maximum(i - WINDOW // tq, 0) * tq, tq)
        kw = kv_ref[pl.ds(wstart, nband), 256 + g * NSA_HD:256 + (g + 1) * NSA_HD]
        vw = kv_ref[pl.ds(wstart, nband), 384 + g * NSA_HD:384 + (g + 1) * NSA_HD]
        d = qpos - (wstart + lax.broadcasted_iota(jnp.int32, (tq, nband), 1))
        pw = _masked_softmax(_dot_nt(q4, kw), _rep((d >= 0) & (d <= WINDOW), tq))
        o_win = _dot(pw.astype(BF16), vw)
        _gate_mix(o_ref, gates, g, o_cmp, o_slc, o_win, tq)


def nsa_prompt(nq, small, kv_bf, kcmp, vcmp, n_seq, t_seq):
    nb = t_seq // Q_BLOCK
    m = kcmp.shape[1]
    key_blk = jnp.arange(t_seq, dtype=jnp.int32)[:, None] // SEL_BLOCK
    et = (key_blk == jnp.arange(128, dtype=jnp.int32)[None, :]).astype(BF16)
    return pl.pallas_call(
        _nsa_prompt_kernel,
        grid=(n_seq, nb),
        in_specs=[pl.BlockSpec((Q_BLOCK, 512), lambda n, i: (n * nb + i, 0)),
                  pl.BlockSpec((Q_BLOCK, 128), lambda n, i: (n * nb + i, 0)),
                  pl.BlockSpec((t_seq, 512), lambda n, i: (n, 0)),
                  pl.BlockSpec((t_seq, 128), lambda n, i: (0, 0)),
                  pl.BlockSpec((1, m, 128), lambda n, i: (n, 0, 0)),
                  pl.BlockSpec((1, m, 128), lambda n, i: (n, 0, 0))],
        out_specs=pl.BlockSpec((Q_BLOCK, 512), lambda n, i: (n * nb + i, 0)),
        out_shape=jax.ShapeDtypeStruct((n_seq * t_seq, 512), F32),
        compiler_params=_cp("parallel", "arbitrary"),
    )(nq, small, kv_bf, et, kcmp, vcmp)


def _nsa_sample_kernel(pt_ref, *refs):
    del pt_ref
    kpages = refs[0:N_PAGES]
    vpages = refs[N_PAGES:2 * N_PAGES]
    nq_ref, sm_ref, kvn_ref, kcmp_ref, vcmp_ref, wk_ref, wv_ref, o_ref = refs[2 * N_PAGES:]
    tq = DEC_SEQ
    qpos = PAST_LEN + lax.broadcasted_iota(jnp.int32, (tq, 1), 0)
    nq = nq_ref[...]
    gates = _sigmoid(sm_ref[...])
    kvn = kvn_ref[...]
    pad = jnp.zeros((PAGE_SIZE - tq, 128), F32)
    new = [jnp.concatenate([kvn[:, c * 128:(c + 1) * 128], pad], axis=0) for c in range(4)]
    n_chunks = N_PAGES + 1
    n_keys = n_chunks * PAGE_SIZE
    n_blk = 40
    qpos_row = PAST_LEN + lax.broadcasted_iota(jnp.int32, (1, tq), 1)
    q4s, o_cmps, imp_ts = [], [], []
    for g in range(NSA_KV):
        gs = slice(g * NSA_HD, (g + 1) * NSA_HD)
        q4s.append(_stack_heads(nq, g, tq))
        o_cmp, imp_t = _cmp_branch(q4s[g], kcmp_ref[0, :, gs], vcmp_ref[0, :, gs], qpos, tq, n_blk)
        o_cmps.append(o_cmp)
        imp_ts.append(imp_t)
    sels = _select_blocks(imp_ts, qpos_row, tq, n_blk)
    for g in range(NSA_KV):
        gs = slice(g * NSA_HD, (g + 1) * NSA_HD)
        q4, sel, o_cmp = q4s[g], sels[g], o_cmps[g]
        kchunks = [p[:, gs].astype(BF16) for p in kpages] + [new[0][:, gs].astype(BF16)]
        vchunks = [p[:, gs].astype(BF16) for p in vpages] + [new[1][:, gs].astype(BF16)]
        s = jnp.concatenate([_dot_nt(q4, kch) for kch in kchunks], axis=1)
        kpos = lax.broadcasted_iota(jnp.int32, (tq, n_keys), 1)
        msk = _rep(_expand_sel(sel, 0, n_keys) & (kpos <= qpos), tq)
        p = _masked_softmax(s, msk).astype(BF16)
        o_slc = _dot(p[:, 0:PAGE_SIZE], vchunks[0])
        for c in range(1, n_chunks):
            o_slc = o_slc + _dot(p[:, c * PAGE_SIZE:(c + 1) * PAGE_SIZE], vchunks[c])

        kw = jnp.concatenate([wk_ref[0, :, gs], new[2][:, gs]], axis=0).astype(BF16)
        vw = jnp.concatenate([wv_ref[0, :, gs], new[3][:, gs]], axis=0).astype(BF16)
        d = qpos - (PAST_LEN - WINDOW + lax.broadcasted_iota(jnp.int32, (tq, WINDOW + PAGE_SIZE), 1))
        pw = _masked_softmax(_dot_nt(q4, kw), _rep((d >= 0) & (d <= WINDOW), tq))
        o_win = _dot(pw.astype(BF16), vw)
        _gate_mix(o_ref, gates, g, o_cmp, o_slc, o_win, tq)


def nsa_sample(page_table, cache_k, cache_v, layer, nq, small, nkv, kcmp, vcmp, win_k, win_v):
    off = RP // DEC_SEQ
    rmap = lambda n, pt: (off + n, 0)
    m = kcmp.shape[1]
    return pl.pallas_call(
        _nsa_sample_kernel,
        grid_spec=pltpu.PrefetchScalarGridSpec(
            num_scalar_prefetch=1, grid=(DEC_BATCH,),
            in_specs=_page_specs(layer) + _page_specs(layer) + [
                pl.BlockSpec((DEC_SEQ, 512), rmap),
                pl.BlockSpec((DEC_SEQ, 128), rmap),
                pl.BlockSpec((DEC_SEQ, 768), rmap),
                pl.BlockSpec((1, m, 128), lambda n, pt: (n, 0, 0)),
                pl.BlockSpec((1, m, 128), lambda n, pt: (n, 0, 0)),
                pl.BlockSpec((None, 1, WINDOW, 128), lambda n, pt: (layer, n, 0, 0)),
                pl.BlockSpec((None, 1, WINDOW, 128), lambda n, pt: (layer, n, 0, 0))],
            out_specs=pl.BlockSpec((DEC_SEQ, 512), lambda n, pt: (n, 0))),
        out_shape=jax.ShapeDtypeStruct((RS, 512), F32),
        compiler_params=_cp("arbitrary"),
    )(page_table, *([cache_k] * N_PAGES), *([cache_v] * N_PAGES), nq, small, nkv, kcmp, vcmp, win_k, win_v)


S5_TILE = 256
S5_LANES = 512
S5_BLK = 16


def _gelu_tanh(x):
    return 0.5 * x * (1.0 + jnp.tanh(math.sqrt(2.0 / math.pi) * (x + 0.044715 * (x * x * x))))


def _s5_kernel(u_ref, bdr_ref, bdi_ref, cr_ref, ci_ref, tab_ref, d_ref, x0r_ref, x0i_ref, *rest, chained, aliased):
    if aliased:
        rest = rest[1:]
    y_ref, fr_ref, fi_ref, xr_scr, xi_scr, car_scr = rest
    i = pl.program_id(1)
    nblk = S5_TILE // 8
    nb = S5_GROUPS // S5_BLK
    wi = S5_BLK * S5_CH
    ws = S5_BLK * S5_P
    u = u_ref[...]
    ub = u.astype(BF16)
    for b in range(nb):
        xr_scr[:, b * ws:(b + 1) * ws] = _dot(ub[:, b * wi:(b + 1) * wi], bdr_ref[b])
        xi_scr[:, b * ws:(b + 1) * ws] = _dot(ub[:, b * wi:(b + 1) * wi], bdi_ref[b])

    if chained:
        @pl.when(i == 0)
        def _():
            car_scr[0:1, :] = x0r_ref[0]
            car_scr[1:2, :] = x0i_ref[0]

    for c in range(S5_STATE // S5_LANES):
        ls = slice(c * S5_LANES, (c + 1) * S5_LANES)
        t1r, t1i, t2r, t2i, t4r, t4i, tpr, tpi = [tab_ref[k, :, ls] for k in range(8)]
        if chained:
            cr, ci = car_scr[0:1, ls], car_scr[1:2, ls]
        lasts = []
        for blk in range(nblk):
            rs = slice(blk * 8, (blk + 1) * 8)
            if not chained:
                cr, ci = x0r_ref[0, blk:blk + 1, ls], x0i_ref[0, blk:blk + 1, ls]
            xr, xi = xr_scr[rs, ls], xi_scr[rs, ls]
            for sh, (mr, mi) in ((1, (t1r, t1i)), (2, (t2r, t2i)), (4, (t4r, t4i))):
                sr, si = pltpu.roll(xr, sh, 0), pltpu.roll(xi, sh, 0)
                xr, xi = xr + mr * sr - mi * si, xi + mr * si + mi * sr
            xr, xi = xr + tpr * cr - tpi * ci, xi + tpr * ci + tpi * cr
            xr_scr[rs, ls] = xr
            xi_scr[rs, ls] = xi
            cr, ci = xr[7:8, :], xi[7:8, :]
            lasts.append((cr, ci))
        if chained:
            car_scr[0:1, ls] = cr
            car_scr[1:2, ls] = ci
        else:
            fr_ref[0, :, ls] = jnp.concatenate([p[0] for p in lasts], axis=0)
            fi_ref[0, :, ls] = jnp.concatenate([p[1] for p in lasts], axis=0)

    ys = []
    for b in range(nb):
        ys.append(_dot(xr_scr[:, b * ws:(b + 1) * ws].astype(BF16), cr_ref[b])
                  - _dot(xi_scr[:, b * ws:(b + 1) * ws].astype(BF16), ci_ref[b]))
    y_ref[...] = _gelu_tanh(jnp.concatenate(ys, axis=1) + d_ref[...] * u)

    if chained:
        @pl.when(i == pl.num_programs(1) - 1)
        def _():
            fr_ref[0] = car_scr[0:1, :]
            fi_ref[0] = car_scr[1:2, :]


def s5_scan(u, sp, x0r, x0i, row_off, n_seq, t_seq, chained, y_prev=None):
    if chained:
        grid = (n_seq, t_seq // S5_TILE)
        nb = grid[1]
        smap = lambda n, i: (n, 0, 0)
    else:
        grid = (1, n_seq * t_seq // S5_TILE)
        nb = grid[1]
        smap = lambda n, i: (i, 0, 0)
    off = row_off // S5_TILE
    rmap = lambda n, i: (off + n * nb + i, 0)
    sblk = (1,) + x0r.shape[1:]
    const = lambda nd: (lambda n, i: (0,) * nd)
    in_specs = [pl.BlockSpec((S5_TILE, D_MODEL), rmap),
                pl.BlockSpec(sp["bdr"].shape, const(3)), pl.BlockSpec(sp["bdi"].shape, const(3)),
                pl.BlockSpec(sp["cr"].shape, const(3)), pl.BlockSpec(sp["ci"].shape, const(3)),
                pl.BlockSpec(sp["tab"].shape, const(3)), pl.BlockSpec((1, D_MODEL), const(2)),
                pl.BlockSpec(sblk, smap), pl.BlockSpec(sblk, smap)]
    args = [u, sp["bdr"], sp["bdi"], sp["cr"], sp["ci"], sp["tab"], sp["d"], x0r, x0i]
    aliases = {}
    if y_prev is not None:
        in_specs.append(pl.BlockSpec(memory_space=pl.ANY))
        args.append(y_prev)
        aliases = {len(args) - 1: 0}
    return pl.pallas_call(
        functools.partial(_s5_kernel, chained=chained, aliased=y_prev is not None),
        grid=grid,
        in_specs=in_specs,
        out_specs=[pl.BlockSpec((S5_TILE, D_MODEL), rmap), pl.BlockSpec(sblk, smap), pl.BlockSpec(sblk, smap)],
        out_shape=[jax.ShapeDtypeStruct((ROWS, D_MODEL), F32), jax.ShapeDtypeStruct(x0r.shape, F32),
                   jax.ShapeDtypeStruct(x0r.shape, F32)],
        scratch_shapes=[pltpu.VMEM((S5_TILE, S5_STATE), F32), pltpu.VMEM((S5_TILE, S5_STATE), F32),
                        pltpu.VMEM((8, S5_STATE), F32)],
        input_output_aliases=aliases,
        compiler_params=_cp("arbitrary", "arbitrary"),
    )(*args)


def _prep_ab_in(w):
    qkv, b, a, gate, nq, nkv, ngate = jnp.split(w, (1536, 1540, 1544, 2056, 2568, 3336), axis=-1)
    kc, vc, ks, vs, kw, vw = jnp.split(nkv, 6, axis=-1)
    small = jnp.concatenate([b, a, ngate, jnp.zeros(w.shape[:-1] + (96,), w.dtype)], axis=-1)
    return jnp.concatenate([qkv, ks, vs, kw, vw, kc, vc, gate, nq, small], axis=-1).astype(BF16)


AB_SPLITS = (1536, 768, 512, 512, 128)


def _prep_cmp(pos, w1, w2):
    top = w1[:CMP_STRIDE * NSA_HD].reshape(CMP_STRIDE, NSA_HD, CMP_HIDDEN)
    bot = w1[CMP_STRIDE * NSA_HD:].reshape(CMP_STRIDE, NSA_HD, CMP_HIDDEN)
    w1bd = jnp.zeros((CMP_STRIDE, 2 * NSA_HD, 4 * CMP_HIDDEN), F32)
    w2bd = jnp.zeros((2 * CMP_HIDDEN, 2 * NSA_HD), F32)
    for g in range(NSA_KV):
        rs = slice(g * NSA_HD, (g + 1) * NSA_HD)
        w1bd = w1bd.at[:, rs, g * CMP_HIDDEN:(g + 1) * CMP_HIDDEN].set(top)
        w1bd = w1bd.at[:, rs, (2 + g) * CMP_HIDDEN:(3 + g) * CMP_HIDDEN].set(bot)
        w2bd = w2bd.at[g * CMP_HIDDEN:(g + 1) * CMP_HIDDEN, rs].set(w2)
    posab = jnp.zeros((CMP_STRIDE, 8, 2 * NSA_HD), F32)
    posab = posab.at[:, 0, :].set(jnp.tile(pos[:CMP_STRIDE], (1, NSA_KV)))
    posab = posab.at[:, 1, :].set(jnp.tile(pos[CMP_STRIDE:], (1, NSA_KV)))
    return posab, w1bd.astype(BF16), w2bd.astype(BF16)


def _cmul(ar, ai, br, bi):
    return ar * br - ai * bi, ar * bi + ai * br


def _prep_s5(a_re, a_im, b_re, b_im, c_re, c_im, d, log_dt):
    dt = jnp.exp(log_dt)[:, None]
    lr = jnp.minimum(a_re, S5_MAX_RE)
    li = a_im
    mag = jnp.exp(lr * dt)
    ar = mag * jnp.cos(li * dt)
    ai = mag * jnp.sin(li * dt)
    den = lr * lr + li * li
    fr = ((ar - 1.0) * lr + ai * li) / den
    fi = (ai * lr - (ar - 1.0) * li) / den
    bbr = fr[..., None] * b_re - fi[..., None] * b_im
    bbi = fr[..., None] * b_im + fi[..., None] * b_re
    nb = S5_GROUPS // S5_BLK
    eye = jnp.eye(S5_BLK, dtype=F32)

    def bd_in(m):
        m4 = jnp.swapaxes(m, 1, 2).reshape(nb, S5_BLK, S5_CH, S5_P)
        return jnp.einsum('bgcp,gh->bgchp', m4, eye).reshape(nb, S5_BLK * S5_CH, S5_BLK * S5_P).astype(BF16)

    def bd_out(m):
        m4 = jnp.swapaxes(m, 1, 2).reshape(nb, S5_BLK, S5_P, S5_CH)
        return jnp.einsum('bgpc,gh->bgphc', m4, eye).reshape(nb, S5_BLK * S5_P, S5_BLK * S5_CH).astype(BF16)

    a1 = (ar.reshape(1, S5_STATE), ai.reshape(1, S5_STATE))
    pw = [a1]
    for _ in range(7):
        pw.append(_cmul(*pw[-1], *a1))
    row = jnp.arange(8)[:, None]
    tabs = []
    for sh in (1, 2, 4):
        for part in pw[sh - 1]:
            tabs.append(jnp.where(row >= sh, part, 0.0))
    tabs.append(jnp.concatenate([p[0] for p in pw], axis=0))
    tabs.append(jnp.concatenate([p[1] for p in pw], axis=0))
    return {"bdr": bd_in(bbr), "bdi": bd_in(bbi), "cr": bd_out(c_re), "ci": bd_out(c_im),
            "tab": jnp.stack(tabs), "d": d.reshape(1, D_MODEL)}


def _heads(a, n, t):
    return a.reshape(n, t, NSA_KV, NSA_HD)


def kernel(x_prompt, x_sample, mem_prompt, cache_mem_k, cache_mem_v, state_gdn, state_gdn_conv, cache_cmp_k, cache_cmp_v, cache_slc_k, cache_slc_v, cache_win_k, cache_win_v, state_s5_re, state_s5_im, page_table, norm_ffn1, w_ffn1_gate, w_ffn1_up, w_ffn1_down, norm_mix, norm_xq, norm_mem, w_xq, w_xk, w_xv, w_xo, norm_ffn2, w_ffn2_gate, w_ffn2_up, w_ffn2_down, norm_final, w_in_ab, w_out_ab, gdn_conv, gdn_a_log, gdn_dt_bias, gdn_norm, cmp_pos_k, cmp_w1_k, cmp_w2_k, cmp_pos_v, cmp_w1_v, cmp_w2_v, w_in_c, s5_a_re, s5_a_im, s5_b_re, s5_b_im, s5_c_re, s5_c_im, s5_d, s5_log_dt, w_glu, w_out_c):
    bf = lambda w: w.astype(BF16)
    n_ab = w_in_ab.shape[0]
    n_pool = cache_cmp_k.shape[1]
    x = jnp.concatenate([x_prompt.reshape(RP, D_MODEL), x_sample.reshape(RS, D_MODEL)], axis=0)

    memkv = mem_kv_all(mem_prompt.reshape(BATCH * N_MEM, D_MODEL), norm_mem, bf(jnp.concatenate([w_xk, w_xv], axis=-1)))
    mem_k_prompt = memkv[:, :, :D_MODEL].reshape(DEPTH, BATCH, N_MEM, X_HEADS, X_HD)
    mem_v_prompt = memkv[:, :, D_MODEL:].reshape(DEPTH, BATCH, N_MEM, X_HEADS, X_HD)

    paged = lambda c: c.reshape(n_ab, n_pool, PAGE_SIZE, NSA_KV * NSA_HD)
    cck, ccv, csk, csv = paged(cache_cmp_k), paged(cache_cmp_v), paged(cache_slc_k), paged(cache_slc_v)
    cwk = cache_win_k.reshape(n_ab, DEC_BATCH, WINDOW, NSA_KV * NSA_HD)
    cwv = cache_win_v.reshape(n_ab, DEC_BATCH, WINDOW, NSA_KV * NSA_HD)
    cmk = cache_mem_k.reshape(DEPTH, DEC_BATCH, N_MEM, D_MODEL)
    cmv = cache_mem_v.reshape(DEPTH, DEC_BATCH, N_MEM, D_MODEL)

    ab_p, ab_s, c_p, c_s = [], [], [], []
    for l in range(DEPTH):
        i = l // 2
        x = glu_mlp(x, norm_ffn1[l], bf(w_ffn1_gate[l]), bf(w_ffn1_up[l]), bf(w_ffn1_down[l]), x, 1024, 256, True, True, 0.5)
        if l % 2 == 0:
            qkv, nkv, gate, nq, small = norm_matmul(x, norm_mix[l], _prep_ab_in(w_in_ab[i]), AB_SPLITS, 512)
            gpar = jnp.zeros((8, 128), F32).at[0, 4:8].set(gdn_a_log[i]).at[1, 4:8].set(gdn_dt_bias[i])
            nw = gdn_norm[i].reshape(1, GDN_DK)
            cb_s = jnp.zeros((DEC_BATCH, 8, GDN_QKV), F32).at[:, 5:8].set(state_gdn_conv[i])
            oa_p, st_p = gdn(qkv, small, gate, gdn_conv[i], gpar, nw, jnp.zeros((BATCH, 8, GDN_QKV), F32),
                             jnp.zeros((BATCH, GDN_HEADS, GDN_DK, GDN_DK), F32), 0, BATCH, SEQ, 1, 256, GDN_CHUNK)
            oa_s, st_s = gdn(qkv, small, gate, gdn_conv[i], gpar, nw, cb_s, state_gdn[i], RP, DEC_BATCH, DEC_SEQ,
                             8, DEC_SEQ, math.gcd(DEC_SEQ, GDN_CHUNK))
            cw = _prep_cmp(cmp_pos_k[i], cmp_w1_k[i], cmp_w2_k[i]) + _prep_cmp(cmp_pos_v[i], cmp_w1_v[i], cmp_w2_v[i])
            kcmp_p, vcmp_p = compress_prompt(nkv, cw, BATCH, SEQ)
            ob_p = nsa_prompt(nq, small, bf(nkv[:RP, :512]), kcmp_p, vcmp_p, BATCH, SEQ)
            kcmp_s, vcmp_s = compress_sample(page_table, cck, ccv, i, cw)
            ob_s = nsa_sample(page_table, csk, csv, i, nq, small, nkv, kcmp_s, vcmp_s, cwk, cwv)
            w_out = bf(w_out_ab[i])
            x = matmul_residual([(oa_p, oa_s, w_out[:512]), (ob_p, ob_s, w_out[512:])], x, 512)
            conv_p = jnp.stack([qkv[(n + 1) * SEQ - 3:(n + 1) * SEQ] for n in range(BATCH)])
            qkv_s = qkv[RP:].reshape(DEC_BATCH, DEC_SEQ, GDN_QKV)
            col = lambda a, c: a[:, c * 128:(c + 1) * 128]
            nkv_p, nkv_s = nkv[:RP], nkv[RP:]
            hp = lambda c: _heads(col(nkv_p, c), BATCH, SEQ)
            hs = lambda c: _heads(col(nkv_s, c), DEC_BATCH, DEC_SEQ)
            ab_p.append((conv_p, st_p, hp(4), hp(5), hp(0), hp(1), hp(2)[:, SEQ - WINDOW:], hp(3)[:, SEQ - WINDOW:]))
            ab_s.append((qkv_s[:, DEC_SEQ - 3:], st_s, hs(4), hs(5), hs(0), hs(1),
                         jnp.concatenate([cache_win_k[i][:, DEC_SEQ:], hs(2)], axis=1),
                         jnp.concatenate([cache_win_v[i][:, DEC_SEQ:], hs(3)], axis=1)))
        else:
            (u,) = norm_matmul(x, norm_mix[l], bf(w_in_c[i]), (D_MODEL,), 512)
            sp = _prep_s5(s5_a_re[i], s5_a_im[i], s5_b_re[i], s5_b_im[i], s5_c_re[i], s5_c_im[i], s5_d[i], s5_log_dt[i])
            z0 = jnp.zeros((BATCH, 1, S5_STATE), F32)
            y, fr_p, fi_p = s5_scan(u, sp, z0, z0, 0, BATCH, SEQ, True)
            per_tile = S5_TILE // DEC_SEQ
            x0r = state_s5_re[i].reshape(RS // S5_TILE, per_tile, S5_STATE)
            x0i = state_s5_im[i].reshape(RS // S5_TILE, per_tile, S5_STATE)
            y, fr_s, fi_s = s5_scan(u, sp, x0r, x0i, RP, DEC_BATCH, DEC_SEQ, False, y_prev=y)
            wg = bf(w_glu[i])
            x = glu_mlp(y, norm_mix[l], wg[:, :D_MODEL], wg[:, D_MODEL:], bf(w_out_c[i]), x, 1024, 256, False, False, 1.0)
            c_p.append((fr_p.reshape(BATCH, S5_GROUPS, S5_P), fi_p.reshape(BATCH, S5_GROUPS, S5_P)))
            c_s.append((fr_s.reshape(DEC_BATCH, S5_GROUPS, S5_P), fi_s.reshape(DEC_BATCH, S5_GROUPS, S5_P)))
        (q,) = norm_matmul(x, norm_xq[l], bf(w_xq[l]), (D_MODEL,), 512)
        o_p = cross_attention(q, memkv, memkv,
                              pl.BlockSpec((None, N_MEM, D_MODEL), functools.partial(lambda n, i, l: (l, n, 0), l=l)),
                              pl.BlockSpec((None, N_MEM, D_MODEL), functools.partial(lambda n, i, l: (l, n, 1), l=l)),
                              0, BATCH, SEQ, 512)
        cache_spec = pl.BlockSpec((None, None, N_MEM, D_MODEL), functools.partial(lambda n, i, l: (l, n, 0, 0), l=l))
        o_s = cross_attention(q, cmk, cmv, cache_spec, cache_spec, RP, DEC_BATCH, DEC_SEQ, DEC_SEQ)
        x = matmul_residual([(o_p, o_s, bf(w_xo[l]))], x, 512)
        x = glu_mlp(x, norm_ffn2[l], bf(w_ffn2_gate[l]), bf(w_ffn2_up[l]), bf(w_ffn2_down[l]), x, 1024, 256, True, True, 0.5)

    y = rmsnorm_rows(x, norm_final, 512)
    st = lambda grp, j: jnp.stack([t[j] for t in grp])
    return (y[:RP].reshape(BATCH, SEQ, D_MODEL), y[RP:].reshape(DEC_BATCH, DEC_SEQ, D_MODEL),
            mem_k_prompt, mem_v_prompt,
            st(ab_p, 1), st(ab_s, 1), st(ab_p, 0), st(ab_s, 0),
            st(ab_p, 2), st(ab_p, 3), st(ab_p, 4), st(ab_p, 5),
            st(ab_s, 2), st(ab_s, 3), st(ab_s, 4), st(ab_s, 5),
            st(ab_p, 6), st(ab_p, 7), st(ab_s, 6), st(ab_s, 7),
            st(c_p, 0), st(c_p, 1), st(c_s, 0), st(c_s, 1))
```

```python
import functools
import math

import jax
import jax.numpy as jnp
from jax import lax
from jax.experimental import pallas as pl
from jax.experimental.pallas import tpu as pltpu

F32 = jnp.float32
BF16 = jnp.bfloat16
HIGHEST = lax.Precision.HIGHEST

D_MODEL = 1024
BATCH = 2
SEQ = 8192
DEPTH = 4
DEC_BATCH = 128
DEC_SEQ = 8
PAST_LEN = 2048
PAGE_SIZE = 128
N_PAGES = PAST_LEN // PAGE_SIZE
RP = BATCH * SEQ
RS = DEC_BATCH * DEC_SEQ
ROWS = RP + RS

GDN_HEADS = 4
GDN_DK = 128
GDN_QKV = 1536
GDN_CONV = 4
GDN_CHUNK = 64
NSA_HEADS = 8
NSA_KV = 2
NSA_HD = 64
NSA_REP = 4
CMP_STRIDE = 16
CMP_LEN = 32
CMP_HIDDEN = 128
SEL_BLOCK = 64
SEL_TOPN = 16
WINDOW = 512
Q_BLOCK = 128
SEL_FORCED = 1.0e4
NEG = -1.0e30
SEL_MASK = 2.0 ** 100
S5_CH = 16
S5_GROUPS = 64
S5_P = 64
S5_STATE = S5_GROUPS * S5_P
S5_MAX_RE = -1.0e-4
N_MEM = 256
X_HEADS = 4
X_HD = 256
D_FF = 2816
EPS = 1.0e-6

VMEM_LIMIT = 56 * 1024 * 1024


def _cp(*sem):
    return pltpu.CompilerParams(dimension_semantics=sem, vmem_limit_bytes=VMEM_LIMIT)


def _dot(a, b, precision=None):
    return jnp.dot(a, b, preferred_element_type=F32, precision=precision)


def _dot_nt(a, b, precision=None):
    return lax.dot_general(a, b, (((1,), (1,)), ((), ())), preferred_element_type=F32, precision=precision)


def _sigmoid(x):
    return 1.0 / (1.0 + jnp.exp(-x))


def _silu(x):
    return x * _sigmoid(x)


def _rms(x, g):
    return x * lax.rsqrt(jnp.mean(x * x, axis=-1, keepdims=True) + EPS) * g


def _norm_matmul_kernel(x_ref, g_ref, w_ref, *o_refs, norm, splits):
    x = x_ref[...]
    if norm:
        x = _rms(x, g_ref[...])
    h = x.astype(BF16)
    off = 0
    for o_ref, wd in zip(o_refs, splits):
        o_ref[...] = _dot(h, w_ref[:, off:off + wd])
        off += wd


def norm_matmul(x, g, w, splits, tm, norm=True):
    rows, k = x.shape
    n = w.shape[1]
    assert sum(splits) == n and rows % tm == 0
    outs = pl.pallas_call(
        functools.partial(_norm_matmul_kernel, norm=norm, splits=tuple(splits)),
        grid=(rows // tm,),
        in_specs=[pl.BlockSpec((tm, k), lambda i: (i, 0)),
                  pl.BlockSpec((1, k), lambda i: (0, 0)),
                  pl.BlockSpec((k, n), lambda i: (0, 0))],
        out_specs=[pl.BlockSpec((tm, wd), lambda i: (i, 0)) for wd in splits],
        out_shape=[jax.ShapeDtypeStruct((rows, wd), F32) for wd in splits],
        compiler_params=_cp("parallel"),
    )(x, g.reshape(1, k), w)
    return outs


def _mem_kv_kernel(x_ref, g_ref, w_ref, o_ref):
    h = _rms(x_ref[...], g_ref[0]).astype(BF16)
    o_ref[0] = _dot(h, w_ref[0])


def mem_kv_all(mem2d, g, w):
    m, k = mem2d.shape
    nl, _, n = w.shape
    return pl.pallas_call(
        _mem_kv_kernel,
        grid=(nl,),
        in_specs=[pl.BlockSpec((m, k), lambda l: (0, 0)),
                  pl.BlockSpec((1, 1, k), lambda l: (l, 0, 0)),
                  pl.BlockSpec((1, k, n), lambda l: (l, 0, 0))],
        out_specs=pl.BlockSpec((1, m, n), lambda l: (l, 0, 0)),
        out_shape=jax.ShapeDtypeStruct((nl, m, n), F32),
        compiler_params=_cp("parallel"),
    )(mem2d, g.reshape(nl, 1, k), w)


def _matmul_res_kernel(*refs, n_terms, n_p):
    r_ref, o_ref = refs[3 * n_terms], refs[3 * n_terms + 1]
    i = pl.program_id(0)

    def run(sel):
        acc = r_ref[...]
        for t in range(n_terms):
            acc = acc + _dot(refs[3 * t + sel][...].astype(BF16), refs[3 * t + 2][...])
        o_ref[...] = acc

    @pl.when(i < n_p)
    def _():
        run(0)

    @pl.when(i >= n_p)
    def _():
        run(1)


def matmul_residual(terms, res, tm):
    rows, n = res.shape
    n_p = RP // tm
    args, specs = [], []
    for a_p, a_s, w in terms:
        k = w.shape[0]
        args += [a_p, a_s, w]
        specs += [pl.BlockSpec((tm, k), lambda i: (jnp.minimum(i, n_p - 1), 0)),
                  pl.BlockSpec((tm, k), lambda i: (jnp.maximum(i - n_p, 0), 0)),
                  pl.BlockSpec((k, n), lambda i: (0, 0))]
    return pl.pallas_call(
        functools.partial(_matmul_res_kernel, n_terms=len(terms), n_p=n_p),
        grid=(rows // tm,),
        in_specs=specs + [pl.BlockSpec((tm, n), lambda i: (i, 0))],
        out_specs=pl.BlockSpec((tm, n), lambda i: (i, 0)),
        out_shape=jax.ShapeDtypeStruct((rows, n), F32),
        compiler_params=_cp("arbitrary"),
    )(*args, res)


def _glu_mlp_kernel(*refs, norm, swiglu, scale, tf, own_res):
    if own_res:
        src_ref, g_ref, wa_ref, wb_ref, wd_ref, o_ref = refs
        res_ref = src_ref
    else:
        src_ref, g_ref, wa_ref, wb_ref, wd_ref, res_ref, o_ref = refs
    x = src_ref[...]
    if norm:
        x = _rms(x, g_ref[...])
    h = x.astype(BF16)
    acc = None
    for j in range(wa_ref.shape[1] // tf):
        cs = slice(j * tf, (j + 1) * tf)
        a = _dot(h, wa_ref[:, cs])
        b = _dot(h, wb_ref[:, cs])
        s = _silu(a) * b if swiglu else a * _sigmoid(b)
        d = _dot(s.astype(BF16), wd_ref[cs, :])
        acc = d if acc is None else acc + d
    o_ref[...] = res_ref[...] + scale * acc


def glu_mlp(src, g, wa, wb, wd, res, tm, tf, norm, swiglu, scale):
    rows, k = src.shape
    ff = wa.shape[1]
    n = wd.shape[1]
    resident = lambda shape: pl.BlockSpec(shape, lambda i: (0, 0), pipeline_mode=pl.Buffered(1))
    in_specs = [pl.BlockSpec((tm, k), lambda i: (i, 0)), pl.BlockSpec((1, k), lambda i: (0, 0)),
                resident((k, ff)), resident((k, ff)), resident((ff, n))]
    args = [src, g.reshape(1, k), wa, wb, wd]
    if res is not None:
        in_specs.append(pl.BlockSpec((tm, n), lambda i: (i, 0)))
        args.append(res)
    return pl.pallas_call(
        functools.partial(_glu_mlp_kernel, norm=norm, swiglu=swiglu, scale=scale, tf=tf, own_res=res is None),
        grid=(rows // tm,),
        in_specs=in_specs,
        out_specs=pl.BlockSpec((tm, n), lambda i: (i, 0)),
        out_shape=jax.ShapeDtypeStruct((rows, n), F32),
        compiler_params=_cp("parallel"),
    )(*args)


def _rmsnorm_kernel(x_ref, g_ref, o_ref):
    o_ref[...] = _rms(x_ref[...], g_ref[...])


def rmsnorm_rows(x, g, tm):
    rows, k = x.shape
    return pl.pallas_call(
        _rmsnorm_kernel,
        grid=(rows // tm,),
        in_specs=[pl.BlockSpec((tm, k), lambda i: (i, 0)), pl.BlockSpec((1, k), lambda i: (0, 0))],
        out_specs=pl.BlockSpec((tm, k), lambda i: (i, 0)),
        out_shape=jax.ShapeDtypeStruct((rows, k), F32),
        compiler_params=_cp("parallel"),
    )(x, g.reshape(1, k))


def _xattn_kernel(q_ref, k_ref, v_ref, o_ref, *, head_axis):
    q = q_ref[...]
    for h in range(X_HEADS):
        sl = slice(h * X_HD, (h + 1) * X_HD)
        kh = k_ref[:, h, :] if head_axis else k_ref[:, sl]
        vh = v_ref[:, h, :] if head_axis else v_ref[:, sl]
        qh = (q[:, sl] * (X_HD ** -0.5)).astype(BF16)
        s = _dot_nt(qh, kh.astype(BF16))
        m = jnp.max(s, axis=-1, keepdims=True)
        p = jnp.exp(s - m)
        p = p / jnp.sum(p, axis=-1, keepdims=True)
        o_ref[:, sl] = _dot(p.astype(BF16), vh.astype(BF16))


def cross_attention(q_all, mk, mv, k_spec, v_spec, row_off, n_seq, t_seq, tq, head_axis=False):
    nb = t_seq // tq
    off = row_off // tq
    return pl.pallas_call(
        functools.partial(_xattn_kernel, head_axis=head_axis),
        grid=(n_seq, nb),
        in_specs=[pl.BlockSpec((tq, D_MODEL), lambda n, i: (off + n * nb + i, 0)), k_spec, v_spec],
        out_specs=pl.BlockSpec((tq, D_MODEL), lambda n, i: (n * nb + i, 0)),
        out_shape=jax.ShapeDtypeStruct((n_seq * t_seq, D_MODEL), F32),
        compiler_params=_cp("parallel", "parallel"),
    )(q_all, mk, mv)


def _softplus(x):
    return jnp.maximum(x, 0.0) + jnp.log(1.0 + jnp.exp(-jnp.abs(x)))


def _split3(x, axis):
    hi = x.astype(BF16).astype(F32)
    return jnp.concatenate([hi, hi, x - hi], axis=axis).astype(BF16)


def _split3r(x, axis):
    hi = x.astype(BF16).astype(F32)
    return jnp.concatenate([hi, x - hi, hi], axis=axis).astype(BF16)


def _dot3(a, b):
    return _dot(_split3(a, 1), _split3r(b, 0))


def _cumsum_rows(tri3, g):
    g1 = g.astype(BF16).astype(F32)
    g2 = (g - g1).astype(BF16).astype(F32)
    g3 = g - g1 - g2
    return _dot(tri3, jnp.concatenate([g1, g2, g3], axis=0).astype(BF16))


def _gdn_kernel(qkv_ref, sm_ref, gate_ref, cw_ref, gp_ref, nw_ref, cb_ref, s0_ref, o_ref, sout_ref,
                xbuf, s_scr, *, seqs, rows, chunk):
    i = pl.program_id(1)
    n_chunks = rows // chunk

    @pl.when(i == 0)
    def _():
        xbuf[:, 0:8, :] = cb_ref[...]
        s_scr[...] = s0_ref[...]

    ri = lax.broadcasted_iota(jnp.int32, (chunk, chunk), 0)
    ci = lax.broadcasted_iota(jnp.int32, (chunk, chunk), 1)
    incl = ri >= ci
    strict = ri > ci
    tri = jnp.where(incl, 1.0, 0.0).astype(BF16)
    tri3 = jnp.concatenate([tri, tri, tri], axis=1)
    eye = jnp.where(ri == ci, 1.0, 0.0)
    levels = []
    w = 1
    while w < chunk:
        levels.append(((ri // (2 * w)) == (ci // (2 * w))) & ((ri % (2 * w)) >= w) & ((ci % (2 * w)) < w))
        w *= 2
    nw = nw_ref[...]
    cw = [cw_ref[j:j + 1, :] for j in range(GDN_CONV)]
    a_neg = -jnp.exp(gp_ref[0:1, :])
    dtb = gp_ref[1:2, :]

    probs = []
    for b in range(seqs):
        tok = slice(b * rows, (b + 1) * rows)
        xbuf[b, 8:8 + rows, :] = qkv_ref[tok, :]
        conv = xbuf[b, 5:5 + rows, :] * cw[0]
        for j in range(1, GDN_CONV):
            conv = conv + xbuf[b, 5 + j:5 + j + rows, :] * cw[j]
        tail = xbuf[b, 8 + rows - 3:8 + rows, :]
        xbuf[b, 5:8, :] = tail
        qkv = _silu(conv)
        sm = sm_ref[tok, :]
        beta_all = _sigmoid(sm)
        g_all = a_neg * _softplus(sm + dtb)
        for c in range(n_chunks):
            rs = slice(c * chunk, (c + 1) * chunk)
            gcum = _cumsum_rows(tri3, g_all[rs])
            gcum_t = gcum.T
            for h in range(GDN_HEADS):
                q = qkv[rs, h * GDN_DK:(h + 1) * GDN_DK]
                k = qkv[rs, 512 + h * GDN_DK:512 + (h + 1) * GDN_DK]
                v = qkv[rs, 1024 + h * GDN_DK:1024 + (h + 1) * GDN_DK]
                q = q * lax.rsqrt(jnp.sum(q * q, axis=-1, keepdims=True) + EPS) * (GDN_DK ** -0.5)
                k = k * lax.rsqrt(jnp.sum(k * k, axis=-1, keepdims=True) + EPS)
                beta = beta_all[rs, h:h + 1]
                gc_col = gcum[:, 4 + h:5 + h]
                gc_row = gcum_t[4 + h:5 + h, :]
                g_last = gcum[chunk - 1:chunk, 4 + h:5 + h]
                decay = jnp.where(incl, jnp.exp(jnp.minimum(gc_col - gc_row, 0.0)), 0.0)
                eg = jnp.exp(gc_col)
                kb = k * beta
                k3r = _split3r(k, 1)
                probs.append(dict(
                    b=b, c=c, h=h, decay=decay,
                    kk=_dot_nt(_split3(kb, 1), k3r), qk=_dot_nt(_split3(q, 1), k3r),
                    rhs=jnp.concatenate([v * beta, kb * eg], axis=1), qd=q * eg,
                    kd_t=(k * jnp.exp(g_last - gc_col)).T, gl=jnp.exp(g_last), minv=eye))
    for p in probs:
        p["lmat"] = jnp.where(strict, p["kk"] * p["decay"], 0.0)
        p["qk"] = jnp.where(incl, p["qk"] * p["decay"], 0.0)
    for off_blk in levels:
        for p in probs:
            p["t"] = _dot3(p["minv"], jnp.where(off_blk, p["lmat"], 0.0))
        for p in probs:
            p["minv"] = p["minv"] - _dot3(p["t"], p["minv"])
    for p in probs:
        p["uw"] = _dot3(p["minv"], p["rhs"])
    state = {(b, h): s_scr[b, h] for b in range(seqs) for h in range(GDN_HEADS)}
    for c in range(n_chunks):
        cur = [p for p in probs if p["c"] == c]
        for p in cur:
            p["s3r"] = _split3r(state[(p["b"], p["h"])], 0)
            p["v_new"] = p["uw"][:, :GDN_DK] - _dot(_split3(p["uw"][:, GDN_DK:], 1), p["s3r"])
        for p in cur:
            p["o"] = _dot(_split3(p["qd"], 1), p["s3r"]) + _dot3(p["qk"], p["v_new"])
            state[(p["b"], p["h"])] = state[(p["b"], p["h"])] * p["gl"] + _dot3(p["kd_t"], p["v_new"])
    for p in probs:
        b, c, h = p["b"], p["c"], p["h"]
        r0 = b * rows + c * chunk
        o = _rms(p["o"], nw) * _silu(gate_ref[r0:r0 + chunk, h * GDN_DK:(h + 1) * GDN_DK])
        o_ref[r0:r0 + chunk, h * GDN_DK:(h + 1) * GDN_DK] = o
    for (b, h), s in state.items():
        s_scr[b, h] = s

    @pl.when(i == pl.num_programs(1) - 1)
    def _():
        sout_ref[...] = s_scr[...]


def gdn(qkv, small, gate, conv_w, gpar, norm_w, conv_buf8, s0, row_off, n_seq, t_seq, seqs, rows, chunk):
    nb = t_seq // rows
    blk = seqs * rows
    off = row_off // blk
    rmap = lambda n, i: (off + n * nb + i, 0)
    return pl.pallas_call(
        functools.partial(_gdn_kernel, seqs=seqs, rows=rows, chunk=chunk),
        grid=(n_seq // seqs, nb),
        in_specs=[pl.BlockSpec((blk, GDN_QKV), rmap),
                  pl.BlockSpec((blk, 128), rmap),
                  pl.BlockSpec((blk, 512), rmap),
                  pl.BlockSpec((GDN_CONV, GDN_QKV), lambda n, i: (0, 0)),
                  pl.BlockSpec((8, 128), lambda n, i: (0, 0)),
                  pl.BlockSpec((1, GDN_DK), lambda n, i: (0, 0)),
                  pl.BlockSpec((seqs, 8, GDN_QKV), lambda n, i: (n, 0, 0)),
                  pl.BlockSpec((seqs, GDN_HEADS, GDN_DK, GDN_DK), lambda n, i: (n, 0, 0, 0))],
        out_specs=[pl.BlockSpec((blk, 512), lambda n, i: (n * nb + i, 0)),
                   pl.BlockSpec((seqs, GDN_HEADS, GDN_DK, GDN_DK), lambda n, i: (n, 0, 0, 0))],
        out_shape=[jax.ShapeDtypeStruct((n_seq * t_seq, 512), F32),
                   jax.ShapeDtypeStruct((n_seq, GDN_HEADS, GDN_DK, GDN_DK), F32)],
        scratch_shapes=[pltpu.VMEM((seqs, 8 + rows, GDN_QKV), F32),
                        pltpu.VMEM((seqs, GDN_HEADS, GDN_DK, GDN_DK), F32)],
        compiler_params=_cp("parallel", "arbitrary"),
    )(qkv, small, gate, conv_w, gpar, norm_w, conv_buf8, s0)


def _compress_core(xj, posab_ref, w1_ref, w2_ref, m):
    acc = jnp.zeros((m + 8, 4 * CMP_HIDDEN), F32)
    for j in range(CMP_STRIDE):
        lhs = jnp.concatenate([xj(j), posab_ref[j]], axis=0).astype(BF16)
        acc = acc + _dot(lhs, w1_ref[j])
    top = acc[0:m, 0:2 * CMP_HIDDEN]
    bot = acc[0:m, 2 * CMP_HIDDEN:]
    c = acc[m:m + 1, 0:2 * CMP_HIDDEN] + acc[m + 1:m + 2, 2 * CMP_HIDDEN:]
    h = top + pltpu.roll(bot, m - 1, 0) + c
    out = _dot(_silu(h).astype(BF16), w2_ref[...])
    row = lax.broadcasted_iota(jnp.int32, (m, 2 * NSA_HD), 0)
    return jnp.where(row < m - 1, out, 0.0)


def _compress_prompt_kernel(kc_ref, vc_ref, pk_ref, w1k_ref, w2k_ref, pv_ref, w1v_ref, w2v_ref, ok_ref, ov_ref, *, m):
    ok_ref[0] = _compress_core(lambda j: kc_ref[pl.ds(j, m, stride=CMP_STRIDE), :], pk_ref, w1k_ref, w2k_ref, m)
    ov_ref[0] = _compress_core(lambda j: vc_ref[pl.ds(j, m, stride=CMP_STRIDE), :], pv_ref, w1v_ref, w2v_ref, m)


def _cmp_weight_specs(nmap):
    return [pl.BlockSpec((CMP_STRIDE, 8, 128), nmap(3)),
            pl.BlockSpec((CMP_STRIDE, 128, 4 * CMP_HIDDEN), nmap(3)),
            pl.BlockSpec((2 * CMP_HIDDEN, 2 * NSA_HD), nmap(2))]


def compress_prompt(nkv, cw, n_seq, t_seq):
    m = t_seq // CMP_STRIDE
    zmap = lambda nd: (lambda n: (0,) * nd)
    out = jax.ShapeDtypeStruct((n_seq, m, 2 * NSA_HD), F32)
    return pl.pallas_call(
        functools.partial(_compress_prompt_kernel, m=m),
        grid=(n_seq,),
        in_specs=[pl.BlockSpec((t_seq, 128), lambda n: (n, 4)), pl.BlockSpec((t_seq, 128), lambda n: (n, 5))]
        + _cmp_weight_specs(zmap) + _cmp_weight_specs(zmap),
        out_specs=[pl.BlockSpec((1, m, 2 * NSA_HD), lambda n: (n, 0, 0))] * 2,
        out_shape=[out, out],
        compiler_params=_cp("parallel"),
    )(nkv, nkv, *cw)


def _compress_sample_kernel(pt_ref, *refs, m):
    del pt_ref
    kpages = refs[0:N_PAGES]
    vpages = refs[N_PAGES:2 * N_PAGES]
    pk_ref, w1k_ref, w2k_ref, pv_ref, w1v_ref, w2v_ref, ok_ref, ov_ref = refs[2 * N_PAGES:]
    per_page = PAGE_SIZE // CMP_STRIDE

    def rows_of(pages):
        return lambda j: jnp.concatenate([p[pl.ds(j, per_page, stride=CMP_STRIDE), :] for p in pages], axis=0)

    ok_ref[0] = _compress_core(rows_of(kpages), pk_ref, w1k_ref, w2k_ref, m)
    ov_ref[0] = _compress_core(rows_of(vpages), pv_ref, w1v_ref, w2v_ref, m)


def _page_specs(layer):
    return [pl.BlockSpec((None, None, PAGE_SIZE, 128), functools.partial(lambda n, pt, p: (layer, pt[n, p], 0, 0), p=p))
            for p in range(N_PAGES)]


def compress_sample(page_table, cache_k, cache_v, layer, cw):
    m = PAST_LEN // CMP_STRIDE
    zmap = lambda nd: (lambda n, pt: (0,) * nd)
    out = jax.ShapeDtypeStruct((DEC_BATCH, m, 2 * NSA_HD), F32)
    return pl.pallas_call(
        functools.partial(_compress_sample_kernel, m=m),
        grid_spec=pltpu.PrefetchScalarGridSpec(
            num_scalar_prefetch=1, grid=(DEC_BATCH,),
            in_specs=_page_specs(layer) + _page_specs(layer) + _cmp_weight_specs(zmap) + _cmp_weight_specs(zmap),
            out_specs=[pl.BlockSpec((1, m, 2 * NSA_HD), lambda n, pt: (n, 0, 0))] * 2),
        out_shape=[out, out],
        compiler_params=_cp("arbitrary"),
    )(page_table, *([cache_k] * N_PAGES), *([cache_v] * N_PAGES), *cw)


def _masked_softmax(s, mask):
    s = jnp.where(mask, s, NEG)
    m = jnp.max(s, axis=-1, keepdims=True)
    p = jnp.where(mask, jnp.exp(s - m), 0.0)
    l = jnp.sum(p, axis=-1, keepdims=True)
    return p / jnp.where(l > 0.0, l, 1.0)


def _stack_heads(nq, g, tq):
    parts = [nq[:, (g * NSA_REP + r) * NSA_HD:(g * NSA_REP + r + 1) * NSA_HD] for r in range(NSA_REP)]
    return (jnp.concatenate(parts, axis=0) * (NSA_HD ** -0.5)).astype(BF16)


def _rep(mask, tq):
    return jnp.concatenate([mask] * NSA_REP, axis=0)


def _cmp_branch(q4, kcmp, vcmp, qpos, tq, n_blk):
    n_cmp = kcmp.shape[0]
    s = _dot_nt(q4, kcmp.astype(BF16))
    cmp_end = lax.broadcasted_iota(jnp.int32, (tq, n_cmp), 1) * CMP_STRIDE + (CMP_LEN - 1)
    p = _masked_softmax(s, _rep(cmp_end <= qpos, tq))
    o = _dot(p.astype(BF16), vcmp.astype(BF16))
    psum = p[0:tq] + p[tq:2 * tq] + p[2 * tq:3 * tq] + p[3 * tq:4 * tq]
    sj = lax.broadcasted_iota(jnp.int32, (n_blk, n_cmp), 0) * SEL_BLOCK
    ci = lax.broadcasted_iota(jnp.int32, (n_blk, n_cmp), 1) * CMP_STRIDE
    ov = jnp.clip(jnp.minimum(ci + CMP_LEN, sj + SEL_BLOCK) - jnp.maximum(ci, sj), 0, CMP_LEN).astype(F32) / CMP_LEN
    ov = ov.astype(BF16)
    p1 = psum.astype(BF16).astype(F32)
    p2 = (psum - p1).astype(BF16).astype(F32)
    p3 = psum - p1 - p2
    imp_t = _dot_nt(jnp.concatenate([ov, ov, ov], axis=1), jnp.concatenate([p1, p2, p3], axis=1).astype(BF16))
    return o, imp_t


def _select_blocks(imp_ts, qpos_row, tq, n_blk):
    blk = lax.broadcasted_iota(jnp.int32, (n_blk, tq), 0)
    cur = qpos_row // SEL_BLOCK
    valid = blk <= cur
    forced = valid & ((blk == 0) | (blk >= cur - 1))
    works = [jnp.where(valid, jnp.where(forced, SEL_FORCED, imp_t), NEG) for imp_t in imp_ts]
    sels = [jnp.zeros((n_blk, tq), F32) for _ in imp_ts]
    for _ in range(SEL_TOPN):
        for j in range(len(works)):
            m = jnp.max(works[j], axis=0, keepdims=True)
            idx = jnp.min(jnp.where(works[j] == m, blk, n_blk), axis=0, keepdims=True)
            pick = blk == idx
            sels[j] = jnp.where(pick, 1.0, sels[j])
            works[j] = jnp.where(pick, -jnp.inf, works[j])
    return [s.T for s in sels]


def _expand_sel(sel, first_blk, n_keys):
    n_blk = sel.shape[1]
    bj = lax.broadcasted_iota(jnp.int32, (n_blk, n_keys), 0)
    kb = lax.broadcasted_iota(jnp.int32, (n_blk, n_keys), 1) // SEL_BLOCK + first_blk
    e = jnp.where(bj == kb, 1.0, 0.0).astype(BF16)
    return _dot(sel.astype(BF16), e) > 0.5


def _gate_mix(o_ref, gates, g, o_cmp, o_slc, o_win, tq):
    for r in range(NSA_REP):
        h = g * NSA_REP + r
        rs = slice(r * tq, (r + 1) * tq)
        c0 = 8 + 3 * h
        o = (gates[:, c0:c0 + 1] * o_cmp[rs] + gates[:, c0 + 1:c0 + 2] * o_slc[rs] + gates[:, c0 + 2:c0 + 3] * o_win[rs])
        o_ref[:, h * NSA_HD:(h + 1) * NSA_HD] = o


def _nsa_prompt_kernel(nq_ref, sm_ref, kv_ref, et_ref, kcmp_ref, vcmp_ref, o_ref):
    tq = Q_BLOCK
    i = pl.program_id(1)
    qpos = i * tq + lax.broadcasted_iota(jnp.int32, (tq, 1), 0)
    nq = nq_ref[...]
    gates = _sigmoid(sm_ref[...])
    kc = 512
    qpos_row = i * tq + lax.broadcasted_iota(jnp.int32, (1, tq), 1)
    q4s, o_cmps, imp_ts = [], [], []
    for g in range(NSA_KV):
        gs = slice(g * NSA_HD, (g + 1) * NSA_HD)
        q4s.append(_stack_heads(nq, g, tq))
        o_cmp, imp_t = _cmp_branch(q4s[g], kcmp_ref[0, :, gs], vcmp_ref[0, :, gs], qpos, tq, 128)
        o_cmps.append(o_cmp)
        imp_ts.append(imp_t)
    sels = _select_blocks(imp_ts, qpos_row, tq, 128)
    qas = [jnp.concatenate([_rep(jnp.where(sels[g] > 0.5, 0.0, -SEL_MASK), tq).astype(BF16), q4s[g]], axis=1)
           for g in range(NSA_KV)]

    def slc_step(c, carry, diagonal):
        start = pl.multiple_of(c * kc, kc)
        et = et_ref[pl.ds(start, kc), :]
        ss = [_dot_nt(qas[g], jnp.concatenate([et, kv_ref[pl.ds(start, kc), g * NSA_HD:(g + 1) * NSA_HD]], axis=1))
              for g in range(NSA_KV)]
        if diagonal:
            causal = _rep(start + lax.broadcasted_iota(jnp.int32, (tq, kc), 1) <= qpos, tq)
            ss = [jnp.where(causal, s, -SEL_MASK) for s in ss]
        out = []
        for g in range(NSA_KV):
            m_i, l_i, acc = carry[g]
            m_new = jnp.maximum(m_i, jnp.max(ss[g], axis=-1, keepdims=True))
            alpha = jnp.exp(m_i - m_new)
            p = jnp.exp(ss[g] - m_new)
            l_new = alpha * l_i + jnp.sum(p, axis=-1, keepdims=True)
            v = kv_ref[pl.ds(start, kc), 128 + g * NSA_HD:128 + (g + 1) * NSA_HD]
            out.append((m_new, l_new, alpha * acc + _dot(p.astype(BF16), v)))
        return tuple(out)

    init = tuple((jnp.full((NSA_REP * tq, 1), -3.0e38, F32), jnp.zeros((NSA_REP * tq, 1), F32),
                  jnp.zeros((NSA_REP * tq, NSA_HD), F32)) for _ in range(NSA_KV))
    n_full = i // (kc // tq)
    carry = lax.fori_loop(0, n_full, functools.partial(slc_step, diagonal=False), init)
    fin = slc_step(n_full, carry, True)

    for g in range(NSA_KV):
        q4, o_cmp = q4s[g], o_cmps[g]
        _, l_f, acc = fin[g]
        o_slc = acc / jnp.where(l_f > 0.0, l_f, 1.0)

        nband = WINDOW + tq
        wstart = pl.multiple_of(jnp.maximum(i - WINDOW // tq, 0) * tq, tq)
        kw = kv_ref[pl.ds(wstart, nband), 256 + g * NSA_HD:256 + (g + 1) * NSA_HD]
        vw = kv_ref[pl.ds(wstart, nband), 384 + g * NSA_HD:384 + (g + 1) * NSA_HD]
        d = qpos - (wstart + lax.broadcasted_iota(jnp.int32, (tq, nband), 1))
        pw = _masked_softmax(_dot_nt(q4, kw), _rep((d >= 0) & (d <= WINDOW), tq))
        o_win = _dot(pw.astype(BF16), vw)
        _gate_mix(o_ref, gates, g, o_cmp, o_slc, o_win, tq)


def nsa_prompt(nq, small, kv_bf, kcmp, vcmp, n_seq, t_seq):
    nb = t_seq // Q_BLOCK
    m = kcmp.shape[1]
    key_blk = jnp.arange(t_seq, dtype=jnp.int32)[:, None] // SEL_BLOCK
    et = (key_blk == jnp.arange(128, dtype=jnp.int32)[None, :]).astype(BF16)
    return pl.pallas_call(
        _nsa_prompt_kernel,
        grid=(n_seq, nb),
        in_specs=[pl.BlockSpec((Q_BLOCK, 512), lambda n, i: (n * nb + i, 0)),
                  pl.BlockSpec((Q_BLOCK, 128), lambda n, i: (n * nb + i, 0)),
                  pl.BlockSpec((t_seq, 512), lambda n, i: (n, 0)),
                  pl.BlockSpec((t_seq, 128), lambda n, i: (0, 0)),
                  pl.BlockSpec((1, m, 128), lambda n, i: (n, 0, 0)),
                  pl.BlockSpec((1, m, 128), lambda n, i: (n, 0, 0))],
        out_specs=pl.BlockSpec((Q_BLOCK, 512), lambda n, i: (n * nb + i, 0)),
        out_shape=jax.ShapeDtypeStruct((n_seq * t_seq, 512), F32),
        compiler_params=_cp("parallel", "arbitrary"),
    )(nq, small, kv_bf, et, kcmp, vcmp)


def _nsa_sample_kernel(pt_ref, *refs):
    del pt_ref
    kpages = refs[0:N_PAGES]
    vpages = refs[N_PAGES:2 * N_PAGES]
    nq_ref, sm_ref, kvn_ref, kcmp_ref, vcmp_ref, wk_ref, wv_ref, o_ref = refs[2 * N_PAGES:]
    tq = DEC_SEQ
    qpos = PAST_LEN + lax.broadcasted_iota(jnp.int32, (tq, 1), 0)
    nq = nq_ref[...]
    gates = _sigmoid(sm_ref[...])
    kvn = kvn_ref[...]
    pad = jnp.zeros((PAGE_SIZE - tq, 128), F32)
    new = [jnp.concatenate([kvn[:, c * 128:(c + 1) * 128], pad], axis=0) for c in range(4)]
    n_chunks = N_PAGES + 1
    n_keys = n_chunks * PAGE_SIZE
    n_blk = 40
    qpos_row = PAST_LEN + lax.broadcasted_iota(jnp.int32, (1, tq), 1)
    q4s, o_cmps, imp_ts = [], [], []
    for g in range(NSA_KV):
        gs = slice(g * NSA_HD, (g + 1) * NSA_HD)
        q4s.append(_stack_heads(nq, g, tq))
        o_cmp, imp_t = _cmp_branch(q4s[g], kcmp_ref[0, :, gs], vcmp_ref[0, :, gs], qpos, tq, n_blk)
        o_cmps.append(o_cmp)
        imp_ts.append(imp_t)
    sels = _select_blocks(imp_ts, qpos_row, tq, n_blk)
    for g in range(NSA_KV):
        gs = slice(g * NSA_HD, (g + 1) * NSA_HD)
        q4, sel, o_cmp = q4s[g], sels[g], o_cmps[g]
        s = jnp.concatenate([_dot(q4, p[gs, :].astype(BF16)) for p in kpages]
                            + [_dot_nt(q4, new[0][:, gs].astype(BF16))], axis=1)
        kpos = lax.broadcasted_iota(jnp.int32, (tq, n_keys), 1)
        msk = _rep(_expand_sel(sel, 0, n_keys) & (kpos <= qpos), tq)
        p = _masked_softmax(s, msk).astype(BF16)
        o_slc = _dot(p[:, N_PAGES * PAGE_SIZE:], new[1][:, gs].astype(BF16))
        for c in range(N_PAGES):
            o_slc = o_slc + _dot_nt(p[:, c * PAGE_SIZE:(c + 1) * PAGE_SIZE], vpages[c][gs, :].astype(BF16))

        sw = jnp.concatenate([_dot(q4, wk_ref[gs, :].astype(BF16)), _dot_nt(q4, new[2][:, gs].astype(BF16))], axis=1)
        d = qpos - (PAST_LEN - WINDOW + lax.broadcasted_iota(jnp.int32, (tq, WINDOW + PAGE_SIZE), 1))
        pw = _masked_softmax(sw, _rep((d >= 0) & (d <= WINDOW), tq)).astype(BF16)
        o_win = (_dot_nt(pw[:, :WINDOW], wv_ref[gs, :].astype(BF16)) + _dot(pw[:, WINDOW:], new[3][:, gs].astype(BF16)))
        _gate_mix(o_ref, gates, g, o_cmp, o_slc, o_win, tq)


def nsa_sample(page_table, cache_k, cache_v, layer, nq, small, nkv, kcmp, vcmp, win_k, win_v):
    off = RP // DEC_SEQ
    rmap = lambda n, pt: (off + n, 0)
    m = kcmp.shape[1]
    return pl.pallas_call(
        _nsa_sample_kernel,
        grid_spec=pltpu.PrefetchScalarGridSpec(
            num_scalar_prefetch=1, grid=(DEC_BATCH,),
            in_specs=_page_specs(layer) + _page_specs(layer) + [
                pl.BlockSpec((DEC_SEQ, 512), rmap),
                pl.BlockSpec((DEC_SEQ, 128), rmap),
                pl.BlockSpec((DEC_SEQ, 768), rmap),
                pl.BlockSpec((1, m, 128), lambda n, pt: (n, 0, 0)),
                pl.BlockSpec((1, m, 128), lambda n, pt: (n, 0, 0)),
                pl.BlockSpec((None, None, 128, WINDOW), lambda n, pt: (layer, n, 0, 0)),
                pl.BlockSpec((None, None, 128, WINDOW), lambda n, pt: (layer, n, 0, 0))],
            out_specs=pl.BlockSpec((DEC_SEQ, 512), lambda n, pt: (n, 0))),
        out_shape=jax.ShapeDtypeStruct((RS, 512), F32),
        compiler_params=_cp("arbitrary"),
    )(page_table, *([cache_k] * N_PAGES), *([cache_v] * N_PAGES), nq, small, nkv, kcmp, vcmp, win_k, win_v)


S5_TILE = 256
S5_LANES = 512
S5_BLK = 16


def _gelu_tanh(x):
    return 0.5 * x * (1.0 + jnp.tanh(math.sqrt(2.0 / math.pi) * (x + 0.044715 * (x * x * x))))


def _s5_kernel(u_ref, bdr_ref, bdi_ref, cr_ref, ci_ref, tab_ref, d_ref, x0r_ref, x0i_ref, *rest, chained, aliased):
    if aliased:
        rest = rest[1:]
    y_ref, fr_ref, fi_ref, xr_scr, xi_scr, car_scr = rest
    i = pl.program_id(1)
    nblk = S5_TILE // 8
    nb = S5_GROUPS // S5_BLK
    wi = S5_BLK * S5_CH
    ws = S5_BLK * S5_P
    u = u_ref[...]
    ub = u.astype(BF16)
    for b in range(nb):
        xr_scr[:, b * ws:(b + 1) * ws] = _dot(ub[:, b * wi:(b + 1) * wi], bdr_ref[b])
        xi_scr[:, b * ws:(b + 1) * ws] = _dot(ub[:, b * wi:(b + 1) * wi], bdi_ref[b])

    if chained:
        @pl.when(i == 0)
        def _():
            car_scr[0:1, :] = x0r_ref[0]
            car_scr[1:2, :] = x0i_ref[0]

    for c in range(S5_STATE // S5_LANES):
        ls = slice(c * S5_LANES, (c + 1) * S5_LANES)
        t1r, t1i, t2r, t2i, t4r, t4i, tpr, tpi = [tab_ref[k, :, ls] for k in range(8)]
        if chained:
            cr, ci = car_scr[0:1, ls], car_scr[1:2, ls]
        lasts = []
        for blk in range(nblk):
            rs = slice(blk * 8, (blk + 1) * 8)
            if not chained:
                cr, ci = x0r_ref[0, blk:blk + 1, ls], x0i_ref[0, blk:blk + 1, ls]
            xr, xi = xr_scr[rs, ls], xi_scr[rs, ls]
            for sh, (mr, mi) in ((1, (t1r, t1i)), (2, (t2r, t2i)), (4, (t4r, t4i))):
                sr, si = pltpu.roll(xr, sh, 0), pltpu.roll(xi, sh, 0)
                xr, xi = xr + mr * sr - mi * si, xi + mr * si + mi * sr
            xr, xi = xr + tpr * cr - tpi * ci, xi + tpr * ci + tpi * cr
            xr_scr[rs, ls] = xr
            xi_scr[rs, ls] = xi
            cr, ci = xr[7:8, :], xi[7:8, :]
            lasts.append((cr, ci))
        if chained:
            car_scr[0:1, ls] = cr
            car_scr[1:2, ls] = ci
        else:
            fr_ref[0, :, ls] = jnp.concatenate([p[0] for p in lasts], axis=0)
            fi_ref[0, :, ls] = jnp.concatenate([p[1] for p in lasts], axis=0)

    ys = []
    for b in range(nb):
        ys.append(_dot(xr_scr[:, b * ws:(b + 1) * ws].astype(BF16), cr_ref[b])
                  - _dot(xi_scr[:, b * ws:(b + 1) * ws].astype(BF16), ci_ref[b]))
    y_ref[...] = _gelu_tanh(jnp.concatenate(ys, axis=1) + d_ref[...] * u)

    if chained:
        @pl.when(i == pl.num_programs(1) - 1)
        def _():
            fr_ref[0] = car_scr[0:1, :]
            fi_ref[0] = car_scr[1:2, :]


def s5_scan(u, sp, x0r, x0i, row_off, n_seq, t_seq, chained, y_prev=None):
    if chained:
        grid = (n_seq, t_seq // S5_TILE)
        nb = grid[1]
        smap = lambda n, i: (n, 0, 0)
    else:
        grid = (1, n_seq * t_seq // S5_TILE)
        nb = grid[1]
        smap = lambda n, i: (i, 0, 0)
    off = row_off // S5_TILE
    rmap = lambda n, i: (off + n * nb + i, 0)
    sblk = (1,) + x0r.shape[1:]
    const = lambda nd: (lambda n, i: (0,) * nd)
    in_specs = [pl.BlockSpec((S5_TILE, D_MODEL), rmap),
                pl.BlockSpec(sp["bdr"].shape, const(3)), pl.BlockSpec(sp["bdi"].shape, const(3)),
                pl.BlockSpec(sp["cr"].shape, const(3)), pl.BlockSpec(sp["ci"].shape, const(3)),
                pl.BlockSpec(sp["tab"].shape, const(3)), pl.BlockSpec((1, D_MODEL), const(2)),
                pl.BlockSpec(sblk, smap), pl.BlockSpec(sblk, smap)]
    args = [u, sp["bdr"], sp["bdi"], sp["cr"], sp["ci"], sp["tab"], sp["d"], x0r, x0i]
    aliases = {}
    if y_prev is not None:
        in_specs.append(pl.BlockSpec(memory_space=pl.ANY))
        args.append(y_prev)
        aliases = {len(args) - 1: 0}
    return pl.pallas_call(
        functools.partial(_s5_kernel, chained=chained, aliased=y_prev is not None),
        grid=grid,
        in_specs=in_specs,
        out_specs=[pl.BlockSpec((S5_TILE, D_MODEL), rmap), pl.BlockSpec(sblk, smap), pl.BlockSpec(sblk, smap)],
        out_shape=[jax.ShapeDtypeStruct((ROWS, D_MODEL), F32), jax.ShapeDtypeStruct(x0r.shape, F32),
                   jax.ShapeDtypeStruct(x0r.shape, F32)],
        scratch_shapes=[pltpu.VMEM((S5_TILE, S5_STATE), F32), pltpu.VMEM((S5_TILE, S5_STATE), F32),
                        pltpu.VMEM((8, S5_STATE), F32)],
        input_output_aliases=aliases,
        compiler_params=_cp("arbitrary", "arbitrary"),
    )(*args)


def _prep_ab_in(w):
    qkv, b, a, gate, nq, nkv, ngate = jnp.split(w, (1536, 1540, 1544, 2056, 2568, 3336), axis=-1)
    kc, vc, ks, vs, kw, vw = jnp.split(nkv, 6, axis=-1)
    small = jnp.concatenate([b, a, ngate, jnp.zeros(w.shape[:-1] + (96,), w.dtype)], axis=-1)
    return jnp.concatenate([qkv, ks, vs, kw, vw, kc, vc, gate, nq, small], axis=-1).astype(BF16)


AB_SPLITS = (1536, 768, 512, 512, 128)


def _prep_cmp(pos, w1, w2):
    top = w1[:CMP_STRIDE * NSA_HD].reshape(CMP_STRIDE, NSA_HD, CMP_HIDDEN)
    bot = w1[CMP_STRIDE * NSA_HD:].reshape(CMP_STRIDE, NSA_HD, CMP_HIDDEN)
    w1bd = jnp.zeros((CMP_STRIDE, 2 * NSA_HD, 4 * CMP_HIDDEN), F32)
    w2bd = jnp.zeros((2 * CMP_HIDDEN, 2 * NSA_HD), F32)
    for g in range(NSA_KV):
        rs = slice(g * NSA_HD, (g + 1) * NSA_HD)
        w1bd = w1bd.at[:, rs, g * CMP_HIDDEN:(g + 1) * CMP_HIDDEN].set(top)
        w1bd = w1bd.at[:, rs, (2 + g) * CMP_HIDDEN:(3 + g) * CMP_HIDDEN].set(bot)
        w2bd = w2bd.at[g * CMP_HIDDEN:(g + 1) * CMP_HIDDEN, rs].set(w2)
    posab = jnp.zeros((CMP_STRIDE, 8, 2 * NSA_HD), F32)
    posab = posab.at[:, 0, :].set(jnp.tile(pos[:CMP_STRIDE], (1, NSA_KV)))
    posab = posab.at[:, 1, :].set(jnp.tile(pos[CMP_STRIDE:], (1, NSA_KV)))
    return posab, w1bd.astype(BF16), w2bd.astype(BF16)


def _cmul(ar, ai, br, bi):
    return ar * br - ai * bi, ar * bi + ai * br


def _prep_s5(a_re, a_im, b_re, b_im, c_re, c_im, d, log_dt):
    dt = jnp.exp(log_dt)[:, None]
    lr = jnp.minimum(a_re, S5_MAX_RE)
    li = a_im
    mag = jnp.exp(lr * dt)
    ar = mag * jnp.cos(li * dt)
    ai = mag * jnp.sin(li * dt)
    den = lr * lr + li * li
    fr = ((ar - 1.0) * lr + ai * li) / den
    fi = (ai * lr - (ar - 1.0) * li) / den
    bbr = fr[..., None] * b_re - fi[..., None] * b_im
    bbi = fr[..., None] * b_im + fi[..., None] * b_re
    nb = S5_GROUPS // S5_BLK
    eye = jnp.eye(S5_BLK, dtype=F32)

    def bd_in(m):
        m4 = jnp.swapaxes(m, 1, 2).reshape(nb, S5_BLK, S5_CH, S5_P)
        return jnp.einsum('bgcp,gh->bgchp', m4, eye).reshape(nb, S5_BLK * S5_CH, S5_BLK * S5_P).astype(BF16)

    def bd_out(m):
        m4 = jnp.swapaxes(m, 1, 2).reshape(nb, S5_BLK, S5_P, S5_CH)
        return jnp.einsum('bgpc,gh->bgphc', m4, eye).reshape(nb, S5_BLK * S5_P, S5_BLK * S5_CH).astype(BF16)

    a1 = (ar.reshape(1, S5_STATE), ai.reshape(1, S5_STATE))
    pw = [a1]
    for _ in range(7):
        pw.append(_cmul(*pw[-1], *a1))
    row = jnp.arange(8)[:, None]
    tabs = []
    for sh in (1, 2, 4):
        for part in pw[sh - 1]:
            tabs.append(jnp.where(row >= sh, part, 0.0))
    tabs.append(jnp.concatenate([p[0] for p in pw], axis=0))
    tabs.append(jnp.concatenate([p[1] for p in pw], axis=0))
    return {"bdr": bd_in(bbr), "bdi": bd_in(bbi), "cr": bd_out(c_re), "ci": bd_out(c_im),
            "tab": jnp.stack(tabs), "d": d.reshape(1, D_MODEL)}


def _heads(a, n, t):
    return a.reshape(n, t, NSA_KV, NSA_HD)


def kernel(x_prompt, x_sample, mem_prompt, cache_mem_k, cache_mem_v, state_gdn, state_gdn_conv, cache_cmp_k, cache_cmp_v, cache_slc_k, cache_slc_v, cache_win_k, cache_win_v, state_s5_re, state_s5_im, page_table, norm_ffn1, w_ffn1_gate, w_ffn1_up, w_ffn1_down, norm_mix, norm_xq, norm_mem, w_xq, w_xk, w_xv, w_xo, norm_ffn2, w_ffn2_gate, w_ffn2_up, w_ffn2_down, norm_final, w_in_ab, w_out_ab, gdn_conv, gdn_a_log, gdn_dt_bias, gdn_norm, cmp_pos_k, cmp_w1_k, cmp_w2_k, cmp_pos_v, cmp_w1_v, cmp_w2_v, w_in_c, s5_a_re, s5_a_im, s5_b_re, s5_b_im, s5_c_re, s5_c_im, s5_d, s5_log_dt, w_glu, w_out_c):
    bf = lambda w: w.astype(BF16)
    n_ab = w_in_ab.shape[0]
    n_pool = cache_cmp_k.shape[1]
    x = jnp.concatenate([x_prompt.reshape(RP, D_MODEL), x_sample.reshape(RS, D_MODEL)], axis=0)

    memkv = mem_kv_all(mem_prompt.reshape(BATCH * N_MEM, D_MODEL), norm_mem, bf(jnp.concatenate([w_xk, w_xv], axis=-1)))
    mem_k_prompt = memkv[:, :, :D_MODEL].reshape(DEPTH, BATCH, N_MEM, X_HEADS, X_HD)
    mem_v_prompt = memkv[:, :, D_MODEL:].reshape(DEPTH, BATCH, N_MEM, X_HEADS, X_HD)

    paged = lambda c: c.reshape(n_ab, n_pool, PAGE_SIZE, NSA_KV * NSA_HD)
    cck, ccv = paged(cache_cmp_k), paged(cache_cmp_v)
    fmaj = lambda c: jnp.transpose(c, (0, 1, 3, 4, 2)).reshape(c.shape[0], c.shape[1], NSA_KV * NSA_HD, c.shape[2])
    csk, csv, cwk, cwv = fmaj(cache_slc_k), fmaj(cache_slc_v), fmaj(cache_win_k), fmaj(cache_win_v)
    cmk = cache_mem_k.reshape(DEPTH, DEC_BATCH, N_MEM, D_MODEL)
    cmv = cache_mem_v.reshape(DEPTH, DEC_BATCH, N_MEM, D_MODEL)

    ab_p, ab_s, c_p, c_s = [], [], [], []
    for l in range(DEPTH):
        i = l // 2
        x = glu_mlp(x, norm_ffn1[l], bf(w_ffn1_gate[l]), bf(w_ffn1_up[l]), bf(w_ffn1_down[l]), None, 1024, 256, True, True, 0.5)
        if l % 2 == 0:
            qkv, nkv, gate, nq, small = norm_matmul(x, norm_mix[l], _prep_ab_in(w_in_ab[i]), AB_SPLITS, 512)
            gpar = jnp.zeros((8, 128), F32).at[0, 4:8].set(gdn_a_log[i]).at[1, 4:8].set(gdn_dt_bias[i])
            nw = gdn_norm[i].reshape(1, GDN_DK)
            cb_s = jnp.zeros((DEC_BATCH, 8, GDN_QKV), F32).at[:, 5:8].set(state_gdn_conv[i])
            oa_p, st_p = gdn(qkv, small, gate, gdn_conv[i], gpar, nw, jnp.zeros((BATCH, 8, GDN_QKV), F32),
                             jnp.zeros((BATCH, GDN_HEADS, GDN_DK, GDN_DK), F32), 0, BATCH, SEQ, 1, 256, GDN_CHUNK)
            oa_s, st_s = gdn(qkv, small, gate, gdn_conv[i], gpar, nw, cb_s, state_gdn[i], RP, DEC_BATCH, DEC_SEQ,
                             8, DEC_SEQ, math.gcd(DEC_SEQ, GDN_CHUNK))
            cw = _prep_cmp(cmp_pos_k[i], cmp_w1_k[i], cmp_w2_k[i]) + _prep_cmp(cmp_pos_v[i], cmp_w1_v[i], cmp_w2_v[i])
            kcmp_p, vcmp_p = compress_prompt(nkv, cw, BATCH, SEQ)
            ob_p = nsa_prompt(nq, small, bf(nkv[:RP, :512]), kcmp_p, vcmp_p, BATCH, SEQ)
            kcmp_s, vcmp_s = compress_sample(page_table, cck, ccv, i, cw)
            ob_s = nsa_sample(page_table, csk, csv, i, nq, small, nkv, kcmp_s, vcmp_s, cwk, cwv)
            w_out = bf(w_out_ab[i])
            x = matmul_residual([(oa_p, oa_s, w_out[:512]), (ob_p, ob_s, w_out[512:])], x, 512)
            conv_p = jnp.stack([qkv[(n + 1) * SEQ - 3:(n + 1) * SEQ] for n in range(BATCH)])
            qkv_s = qkv[RP:].reshape(DEC_BATCH, DEC_SEQ, GDN_QKV)
            col = lambda a, c: a[:, c * 128:(c + 1) * 128]
            nkv_p, nkv_s = nkv[:RP], nkv[RP:]
            hp = lambda c: _heads(col(nkv_p, c), BATCH, SEQ)
            hs = lambda c: _heads(col(nkv_s, c), DEC_BATCH, DEC_SEQ)
            ab_p.append((conv_p, st_p, hp(4), hp(5), hp(0), hp(1), hp(2)[:, SEQ - WINDOW:], hp(3)[:, SEQ - WINDOW:]))
            ab_s.append((qkv_s[:, DEC_SEQ - 3:], st_s, hs(4), hs(5), hs(0), hs(1),
                         jnp.concatenate([cache_win_k[i][:, DEC_SEQ:], hs(2)], axis=1),
                         jnp.concatenate([cache_win_v[i][:, DEC_SEQ:], hs(3)], axis=1)))
        else:
            (u,) = norm_matmul(x, norm_mix[l], bf(w_in_c[i]), (D_MODEL,), 512)
            sp = _prep_s5(s5_a_re[i], s5_a_im[i], s5_b_re[i], s5_b_im[i], s5_c_re[i], s5_c_im[i], s5_d[i], s5_log_dt[i])
            z0 = jnp.zeros((BATCH, 1, S5_STATE), F32)
            y, fr_p, fi_p = s5_scan(u, sp, z0, z0, 0, BATCH, SEQ, True)
            per_tile = S5_TILE // DEC_SEQ
            x0r = state_s5_re[i].reshape(RS // S5_TILE, per_tile, S5_STATE)
            x0i = state_s5_im[i].reshape(RS // S5_TILE, per_tile, S5_STATE)
            y, fr_s, fi_s = s5_scan(u, sp, x0r, x0i, RP, DEC_BATCH, DEC_SEQ, False, y_prev=y)
            wg = bf(w_glu[i])
            x = glu_mlp(y, norm_mix[l], wg[:, :D_MODEL], wg[:, D_MODEL:], bf(w_out_c[i]), x, 1024, 256, False, False, 1.0)
            c_p.append((fr_p.reshape(BATCH, S5_GROUPS, S5_P), fi_p.reshape(BATCH, S5_GROUPS, S5_P)))
            c_s.append((fr_s.reshape(DEC_BATCH, S5_GROUPS, S5_P), fi_s.reshape(DEC_BATCH, S5_GROUPS, S5_P)))
        (q,) = norm_matmul(x, norm_xq[l], bf(w_xq[l]), (D_MODEL,), 512)
        o_p = cross_attention(q, memkv, memkv,
                              pl.BlockSpec((None, N_MEM, D_MODEL), functools.partial(lambda n, i, l: (l, n, 0), l=l)),
                              pl.BlockSpec((None, N_MEM, D_MODEL), functools.partial(lambda n, i, l: (l, n, 1), l=l)),
                              0, BATCH, SEQ, 512)
        cache_spec = pl.BlockSpec((None, None, N_MEM, D_MODEL), functools.partial(lambda n, i, l: (l, n, 0, 0), l=l))
        o_s = cross_attention(q, cmk, cmv, cache_spec, cache_spec, RP, DEC_BATCH, DEC_SEQ, DEC_SEQ)
        x = matmul_residual([(o_p, o_s, bf(w_xo[l]))], x, 512)
        x = glu_mlp(x, norm_ffn2[l], bf(w_ffn2_gate[l]), bf(w_ffn2_up[l]), bf(w_ffn2_down[l]), None, 1024, 256, True, True, 0.5)

    y = rmsnorm_rows(x, norm_final, 512)
    st = lambda grp, j: jnp.stack([t[j] for t in grp])
    return (y[:RP].reshape(BATCH, SEQ, D_MODEL), y[RP:].reshape(DEC_BATCH, DEC_SEQ, D_MODEL),
            mem_k_prompt, mem_v_prompt,
            st(ab_p, 1), st(ab_s, 1), st(ab_p, 0), st(ab_s, 0),
            st(ab_p, 2), st(ab_p, 3), st(ab_p, 4), st(ab_p, 5),
            st(ab_s, 2), st(ab_s, 3), st(ab_s, 4), st(ab_s, 5),
            st(ab_p, 6), st(ab_p, 7), st(ab_s, 6), st(ab_s, 7),
            st(c_p, 0), st(c_p, 1), st(c_s, 0), st(c_s, 1))
```

```python
import functools
import math

import jax
import jax.numpy as jnp
from jax import lax
from jax.experimental import pallas as pl
from jax.experimental.pallas import tpu as pltpu

F32 = jnp.float32
BF16 = jnp.bfloat16
HIGHEST = lax.Precision.HIGHEST

D_MODEL = 1024
BATCH = 2
SEQ = 8192
DEPTH = 4
DEC_BATCH = 128
DEC_SEQ = 8
PAST_LEN = 2048
PAGE_SIZE = 128
N_PAGES = PAST_LEN // PAGE_SIZE
RP = BATCH * SEQ
RS = DEC_BATCH * DEC_SEQ
ROWS = RP + RS

GDN_HEADS = 4
GDN_DK = 128
GDN_QKV = 1536
GDN_CONV = 4
GDN_CHUNK = 64
NSA_HEADS = 8
NSA_KV = 2
NSA_HD = 64
NSA_REP = 4
CMP_STRIDE = 16
CMP_LEN = 32
CMP_HIDDEN = 128
SEL_BLOCK = 64
SEL_TOPN = 16
WINDOW = 512
Q_BLOCK = 128
SEL_FORCED = 1.0e4
NEG = -1.0e30
SEL_MASK = 2.0 ** 100
S5_CH = 16
S5_GROUPS = 64
S5_P = 64
S5_STATE = S5_GROUPS * S5_P
S5_MAX_RE = -1.0e-4
N_MEM = 256
X_HEADS = 4
X_HD = 256
D_FF = 2816
EPS = 1.0e-6

VMEM_LIMIT = 56 * 1024 * 1024


def _cp(*sem):
    return pltpu.CompilerParams(dimension_semantics=sem, vmem_limit_bytes=VMEM_LIMIT)


def _dot(a, b, precision=None):
    return jnp.dot(a, b, preferred_element_type=F32, precision=precision)


def _dot_nt(a, b, precision=None):
    return lax.dot_general(a, b, (((1,), (1,)), ((), ())), preferred_element_type=F32, precision=precision)


def _sigmoid(x):
    return 1.0 / (1.0 + jnp.exp(-x))


def _silu(x):
    return x * _sigmoid(x)


def _rms(x, g):
    return x * lax.rsqrt(jnp.mean(x * x, axis=-1, keepdims=True) + EPS) * g


def _norm_matmul_kernel(x_ref, g_ref, w_ref, *o_refs, norm, splits):
    x = x_ref[...]
    if norm:
        x = _rms(x, g_ref[...])
    h = x.astype(BF16)
    off = 0
    for o_ref, wd in zip(o_refs, splits):
        o_ref[...] = _dot(h, w_ref[:, off:off + wd])
        off += wd


def norm_matmul(x, g, w, splits, tm, norm=True):
    rows, k = x.shape
    n = w.shape[1]
    assert sum(splits) == n and rows % tm == 0
    outs = pl.pallas_call(
        functools.partial(_norm_matmul_kernel, norm=norm, splits=tuple(splits)),
        grid=(rows // tm,),
        in_specs=[pl.BlockSpec((tm, k), lambda i: (i, 0)),
                  pl.BlockSpec((1, k), lambda i: (0, 0)),
                  pl.BlockSpec((k, n), lambda i: (0, 0))],
        out_specs=[pl.BlockSpec((tm, wd), lambda i: (i, 0)) for wd in splits],
        out_shape=[jax.ShapeDtypeStruct((rows, wd), F32) for wd in splits],
        compiler_params=_cp("parallel"),
    )(x, g.reshape(1, k), w)
    return outs


def _mem_kv_kernel(x_ref, g_ref, w_ref, o_ref):
    h = _rms(x_ref[...], g_ref[0]).astype(BF16)
    o_ref[0] = _dot(h, w_ref[0])


def mem_kv_all(mem2d, g, w):
    m, k = mem2d.shape
    nl, _, n = w.shape
    return pl.pallas_call(
        _mem_kv_kernel,
        grid=(nl,),
        in_specs=[pl.BlockSpec((m, k), lambda l: (0, 0)),
                  pl.BlockSpec((1, 1, k), lambda l: (l, 0, 0)),
                  pl.BlockSpec((1, k, n), lambda l: (l, 0, 0))],
        out_specs=pl.BlockSpec((1, m, n), lambda l: (l, 0, 0)),
        out_shape=jax.ShapeDtypeStruct((nl, m, n), F32),
        compiler_params=_cp("parallel"),
    )(mem2d, g.reshape(nl, 1, k), w)


def _matmul_res_kernel(*refs, n_terms, n_p):
    r_ref, o_ref = refs[3 * n_terms], refs[3 * n_terms + 1]
    i = pl.program_id(0)

    def run(sel):
        acc = r_ref[...]
        for t in range(n_terms):
            acc = acc + _dot(refs[3 * t + sel][...].astype(BF16), refs[3 * t + 2][...])
        o_ref[...] = acc

    @pl.when(i < n_p)
    def _():
        run(0)

    @pl.when(i >= n_p)
    def _():
        run(1)


def matmul_residual(terms, res, tm):
    rows, n = res.shape
    n_p = RP // tm
    args, specs = [], []
    for a_p, a_s, w in terms:
        k = w.shape[0]
        args += [a_p, a_s, w]
        specs += [pl.BlockSpec((tm, k), lambda i: (jnp.minimum(i, n_p - 1), 0)),
                  pl.BlockSpec((tm, k), lambda i: (jnp.maximum(i - n_p, 0), 0)),
                  pl.BlockSpec((k, n), lambda i: (0, 0))]
    return pl.pallas_call(
        functools.partial(_matmul_res_kernel, n_terms=len(terms), n_p=n_p),
        grid=(rows // tm,),
        in_specs=specs + [pl.BlockSpec((tm, n), lambda i: (i, 0))],
        out_specs=pl.BlockSpec((tm, n), lambda i: (i, 0)),
        out_shape=jax.ShapeDtypeStruct((rows, n), F32),
        compiler_params=_cp("arbitrary"),
    )(*args, res)


def _glu_mlp_kernel(*refs, norm, swiglu, scale, tf, own_res, n_p):
    if own_res:
        src_ref, g_ref, wa_ref, wb_ref, wd_ref, o_ref = refs
        res_ref = src_ref
    elif n_p is None:
        src_ref, g_ref, wa_ref, wb_ref, wd_ref, res_ref, o_ref = refs
    else:
        src_ref, srcs_ref, g_ref, wa_ref, wb_ref, wd_ref, res_ref, o_ref, h_scr = refs
    if n_p is None:
        x = src_ref[...]
        if norm:
            x = _rms(x, g_ref[...])
        h = x.astype(BF16)
    else:
        @pl.when(pl.program_id(0) < n_p)
        def _():
            h_scr[...] = src_ref[...].astype(BF16)

        @pl.when(pl.program_id(0) >= n_p)
        def _():
            h_scr[...] = srcs_ref[...].astype(BF16)

        h = h_scr[...]
    acc = None
    for j in range(wa_ref.shape[1] // tf):
        cs = slice(j * tf, (j + 1) * tf)
        a = _dot(h, wa_ref[:, cs])
        b = _dot(h, wb_ref[:, cs])
        s = _silu(a) * b if swiglu else a * _sigmoid(b)
        d = _dot(s.astype(BF16), wd_ref[cs, :])
        acc = d if acc is None else acc + d
    o_ref[...] = res_ref[...] + scale * acc


def glu_mlp(src, g, wa, wb, wd, res, tm, tf, norm, swiglu, scale, src_s=None):
    k = src.shape[1]
    ff = wa.shape[1]
    n = wd.shape[1]
    rows = src.shape[0] if res is None else res.shape[0]
    resident = lambda shape: pl.BlockSpec(shape, lambda i: (0, 0), pipeline_mode=pl.Buffered(1))
    weights = [pl.BlockSpec((1, k), lambda i: (0, 0)), resident((k, ff)), resident((k, ff)), resident((ff, n))]
    n_p, scratch = None, []
    if src_s is None:
        in_specs = [pl.BlockSpec((tm, k), lambda i: (i, 0))] + weights
        args = [src, g.reshape(1, k), wa, wb, wd]
    else:
        assert not norm and res is not None
        n_p = src.shape[0] // tm
        in_specs = [pl.BlockSpec((tm, k), lambda i: (jnp.minimum(i, n_p - 1), 0)),
                    pl.BlockSpec((tm, k), lambda i: (jnp.maximum(i - n_p, 0), 0))] + weights
        args = [src, src_s, g.reshape(1, k), wa, wb, wd]
        scratch = [pltpu.VMEM((tm, k), BF16)]
    if res is not None:
        in_specs.append(pl.BlockSpec((tm, n), lambda i: (i, 0)))
        args.append(res)
    return pl.pallas_call(
        functools.partial(_glu_mlp_kernel, norm=norm, swiglu=swiglu, scale=scale, tf=tf, own_res=res is None, n_p=n_p),
        grid=(rows // tm,),
        in_specs=in_specs,
        out_specs=pl.BlockSpec((tm, n), lambda i: (i, 0)),
        out_shape=jax.ShapeDtypeStruct((rows, n), F32),
        scratch_shapes=scratch,
        compiler_params=_cp("arbitrary" if src_s is not None else "parallel"),
    )(*args)


def _rmsnorm_kernel(x_ref, g_ref, o_ref):
    o_ref[...] = _rms(x_ref[...], g_ref[...])


def rmsnorm_rows(x, g, tm):
    rows, k = x.shape
    return pl.pallas_call(
        _rmsnorm_kernel,
        grid=(rows // tm,),
        in_specs=[pl.BlockSpec((tm, k), lambda i: (i, 0)), pl.BlockSpec((1, k), lambda i: (0, 0))],
        out_specs=pl.BlockSpec((tm, k), lambda i: (i, 0)),
        out_shape=jax.ShapeDtypeStruct((rows, k), F32),
        compiler_params=_cp("parallel"),
    )(x, g.reshape(1, k))


def _head_attention(qh, kh, vh):
    s = _dot_nt((qh * (X_HD ** -0.5)).astype(BF16), kh.astype(BF16))
    p = jnp.exp(s - jnp.max(s, axis=-1, keepdims=True))
    p = p / jnp.sum(p, axis=-1, keepdims=True)
    return _dot(p.astype(BF16), vh.astype(BF16))


def _xattn_kernel(q_ref, k_ref, v_ref, o_ref):
    q = q_ref[...]
    for h in range(X_HEADS):
        sl = slice(h * X_HD, (h + 1) * X_HD)
        o_ref[:, sl] = _head_attention(q[:, sl], k_ref[:, sl], v_ref[:, sl])


def cross_attention(q_all, mk, mv, k_spec, v_spec, row_off, n_seq, t_seq, tq):
    nb = t_seq // tq
    off = row_off // tq
    return pl.pallas_call(
        _xattn_kernel,
        grid=(n_seq, nb),
        in_specs=[pl.BlockSpec((tq, D_MODEL), lambda n, i: (off + n * nb + i, 0)), k_spec, v_spec],
        out_specs=pl.BlockSpec((tq, D_MODEL), lambda n, i: (n * nb + i, 0)),
        out_shape=jax.ShapeDtypeStruct((n_seq * t_seq, D_MODEL), F32),
        compiler_params=_cp("parallel", "parallel"),
    )(q_all, mk, mv)


def _xattn_cache_kernel(q_ref, k_hbm, v_hbm, o_ref, kbuf, vbuf, sem, *, layer):
    n = pl.program_id(0)
    slot = n % 2

    def copies(seq, s):
        cs = []
        for h in range(X_HEADS):
            cs.append(pltpu.make_async_copy(k_hbm.at[layer, seq, :, h, :], kbuf.at[s, h], sem.at[s, h]))
            cs.append(pltpu.make_async_copy(v_hbm.at[layer, seq, :, h, :], vbuf.at[s, h], sem.at[s, X_HEADS + h]))
        return cs

    @pl.when(n == 0)
    def _():
        for c in copies(0, 0):
            c.start()

    @pl.when(n + 1 < pl.num_programs(0))
    def _():
        for c in copies(n + 1, 1 - slot):
            c.start()

    for c in copies(n, slot):
        c.wait()
    q = q_ref[...]
    hs = range(X_HEADS)
    ss = [_dot_nt((q[:, h * X_HD:(h + 1) * X_HD] * (X_HD ** -0.5)).astype(BF16), kbuf[slot, h].astype(BF16)) for h in hs]
    ps = [jnp.exp(s - jnp.max(s, axis=-1, keepdims=True)) for s in ss]
    ps = [(p / jnp.sum(p, axis=-1, keepdims=True)).astype(BF16) for p in ps]
    for h in hs:
        o_ref[:, h * X_HD:(h + 1) * X_HD] = _dot(ps[h], vbuf[slot, h].astype(BF16))


def cross_attention_cached(q_all, cache_k, cache_v, layer, row_off):
    off = row_off // DEC_SEQ
    return pl.pallas_call(
        functools.partial(_xattn_cache_kernel, layer=layer),
        grid=(DEC_BATCH,),
        in_specs=[pl.BlockSpec((DEC_SEQ, D_MODEL), lambda n: (off + n, 0)),
                  pl.BlockSpec(memory_space=pl.ANY), pl.BlockSpec(memory_space=pl.ANY)],
        out_specs=pl.BlockSpec((DEC_SEQ, D_MODEL), lambda n: (n, 0)),
        out_shape=jax.ShapeDtypeStruct((DEC_BATCH * DEC_SEQ, D_MODEL), F32),
        scratch_shapes=[pltpu.VMEM((2, X_HEADS, N_MEM, X_HD), F32), pltpu.VMEM((2, X_HEADS, N_MEM, X_HD), F32),
                        pltpu.SemaphoreType.DMA((2, 2 * X_HEADS))],
        compiler_params=_cp("arbitrary"),
    )(q_all, cache_k, cache_v)


def _softplus(x):
    return jnp.maximum(x, 0.0) + jnp.log(1.0 + jnp.exp(-jnp.abs(x)))


def _split3(x, axis):
    hi = x.astype(BF16).astype(F32)
    return jnp.concatenate([hi, hi, x - hi], axis=axis).astype(BF16)


def _split3r(x, axis):
    hi = x.astype(BF16).astype(F32)
    return jnp.concatenate([hi, x - hi, hi], axis=axis).astype(BF16)


def _dot3(a, b):
    return _dot(_split3(a, 1), _split3r(b, 0))


def _cumsum_rows(tri3, g):
    g1 = g.astype(BF16).astype(F32)
    g2 = (g - g1).astype(BF16).astype(F32)
    g3 = g - g1 - g2
    return _dot(tri3, jnp.concatenate([g1, g2, g3], axis=0).astype(BF16))


def _gdn_kernel(qkv_ref, sm_ref, gate_ref, cw_ref, gp_ref, nw_ref, cb_ref, s0_ref, o_ref, sout_ref,
                xbuf, s_scr, *, seqs, rows, chunk):
    i = pl.program_id(1)
    n_chunks = rows // chunk

    @pl.when(i == 0)
    def _():
        xbuf[:, 0:8, :] = cb_ref[...]
        s_scr[...] = s0_ref[...]

    ri = lax.broadcasted_iota(jnp.int32, (chunk, chunk), 0)
    ci = lax.broadcasted_iota(jnp.int32, (chunk, chunk), 1)
    incl = ri >= ci
    strict = ri > ci
    tri = jnp.where(incl, 1.0, 0.0).astype(BF16)
    tri3 = jnp.concatenate([tri, tri, tri], axis=1)
    eye = jnp.where(ri == ci, 1.0, 0.0)
    levels = []
    w = 1
    while w < chunk:
        levels.append(((ri // (2 * w)) == (ci // (2 * w))) & ((ri % (2 * w)) >= w) & ((ci % (2 * w)) < w))
        w *= 2
    nw = nw_ref[...]
    cw = [cw_ref[j:j + 1, :] for j in range(GDN_CONV)]
    a_neg = -jnp.exp(gp_ref[0:1, :])
    dtb = gp_ref[1:2, :]

    probs = []
    for b in range(seqs):
        tok = slice(b * rows, (b + 1) * rows)
        xbuf[b, 8:8 + rows, :] = qkv_ref[tok, :]
        conv = xbuf[b, 5:5 + rows, :] * cw[0]
        for j in range(1, GDN_CONV):
            conv = conv + xbuf[b, 5 + j:5 + j + rows, :] * cw[j]
        tail = xbuf[b, 8 + rows - 3:8 + rows, :]
        xbuf[b, 5:8, :] = tail
        qkv = _silu(conv)
        sm = sm_ref[tok, :]
        beta_all = _sigmoid(sm)
        g_all = a_neg * _softplus(sm + dtb)
        for c in range(n_chunks):
            rs = slice(c * chunk, (c + 1) * chunk)
            gcum = _cumsum_rows(tri3, g_all[rs])
            gcum_t = gcum.T
            for h in range(GDN_HEADS):
                q = qkv[rs, h * GDN_DK:(h + 1) * GDN_DK]
                k = qkv[rs, 512 + h * GDN_DK:512 + (h + 1) * GDN_DK]
                v = qkv[rs, 1024 + h * GDN_DK:1024 + (h + 1) * GDN_DK]
                q = q * lax.rsqrt(jnp.sum(q * q, axis=-1, keepdims=True) + EPS) * (GDN_DK ** -0.5)
                k = k * lax.rsqrt(jnp.sum(k * k, axis=-1, keepdims=True) + EPS)
                beta = beta_all[rs, h:h + 1]
                gc_col = gcum[:, 4 + h:5 + h]
                gc_row = gcum_t[4 + h:5 + h, :]
                g_last = gcum[chunk - 1:chunk, 4 + h:5 + h]
                decay = jnp.where(incl, jnp.exp(jnp.minimum(gc_col - gc_row, 0.0)), 0.0)
                eg = jnp.exp(gc_col)
                kb = k * beta
                k3r = _split3r(k, 1)
                probs.append(dict(
                    b=b, c=c, h=h, decay=decay,
                    kk=_dot_nt(_split3(kb, 1), k3r), qk=_dot_nt(_split3(q, 1), k3r),
                    rhs=jnp.concatenate([v * beta, kb * eg], axis=1), qd=q * eg,
                    kd_t=(k * jnp.exp(g_last - gc_col)).T, gl=jnp.exp(g_last), minv=eye))
    for p in probs:
        p["lmat"] = jnp.where(strict, p["kk"] * p["decay"], 0.0)
        p["qk"] = jnp.where(incl, p["qk"] * p["decay"], 0.0)
    for off_blk in levels:
        for p in probs:
            p["t"] = _dot3(p["minv"], jnp.where(off_blk, p["lmat"], 0.0))
        for p in probs:
            p["minv"] = p["minv"] - _dot3(p["t"], p["minv"])
    for p in probs:
        p["uw"] = _dot3(p["minv"], p["rhs"])
    state = {(b, h): s_scr[b, h] for b in range(seqs) for h in range(GDN_HEADS)}
    for c in range(n_chunks):
        cur = [p for p in probs if p["c"] == c]
        for p in cur:
            p["s3r"] = _split3r(state[(p["b"], p["h"])], 0)
            p["v_new"] = p["uw"][:, :GDN_DK] - _dot(_split3(p["uw"][:, GDN_DK:], 1), p["s3r"])
        for p in cur:
            p["o"] = _dot(_split3(p["qd"], 1), p["s3r"]) + _dot3(p["qk"], p["v_new"])
            state[(p["b"], p["h"])] = state[(p["b"], p["h"])] * p["gl"] + _dot3(p["kd_t"], p["v_new"])
    for p in probs:
        b, c, h = p["b"], p["c"], p["h"]
        r0 = b * rows + c * chunk
        o = _rms(p["o"], nw) * _silu(gate_ref[r0:r0 + chunk, h * GDN_DK:(h + 1) * GDN_DK])
        o_ref[r0:r0 + chunk, h * GDN_DK:(h + 1) * GDN_DK] = o
    for (b, h), s in state.items():
        s_scr[b, h] = s

    @pl.when(i == pl.num_programs(1) - 1)
    def _():
        sout_ref[...] = s_scr[...]


def gdn(qkv, small, gate, conv_w, gpar, norm_w, conv_buf8, s0, row_off, n_seq, t_seq, seqs, rows, chunk):
    nb = t_seq // rows
    blk = seqs * rows
    off = row_off // blk
    rmap = lambda n, i: (off + n * nb + i, 0)
    return pl.pallas_call(
        functools.partial(_gdn_kernel, seqs=seqs, rows=rows, chunk=chunk),
        grid=(n_seq // seqs, nb),
        in_specs=[pl.BlockSpec((blk, GDN_QKV), rmap),
                  pl.BlockSpec((blk, 128), rmap),
                  pl.BlockSpec((blk, 512), rmap),
                  pl.BlockSpec((GDN_CONV, GDN_QKV), lambda n, i: (0, 0)),
                  pl.BlockSpec((8, 128), lambda n, i: (0, 0)),
                  pl.BlockSpec((1, GDN_DK), lambda n, i: (0, 0)),
                  pl.BlockSpec((seqs, 8, GDN_QKV), lambda n, i: (n, 0, 0)),
                  pl.BlockSpec((seqs, GDN_HEADS, GDN_DK, GDN_DK), lambda n, i: (n, 0, 0, 0))],
        out_specs=[pl.BlockSpec((blk, 512), lambda n, i: (n * nb + i, 0)),
                   pl.BlockSpec((seqs, GDN_HEADS, GDN_DK, GDN_DK), lambda n, i: (n, 0, 0, 0))],
        out_shape=[jax.ShapeDtypeStruct((n_seq * t_seq, 512), F32),
                   jax.ShapeDtypeStruct((n_seq, GDN_HEADS, GDN_DK, GDN_DK), F32)],
        scratch_shapes=[pltpu.VMEM((seqs, 8 + rows, GDN_QKV), F32),
                        pltpu.VMEM((seqs, GDN_HEADS, GDN_DK, GDN_DK), F32)],
        compiler_params=_cp("parallel", "arbitrary"),
    )(qkv, small, gate, conv_w, gpar, norm_w, conv_buf8, s0)


def _compress_core(xj, posab_ref, w1_ref, w2_ref, m):
    acc = jnp.zeros((m + 8, 4 * CMP_HIDDEN), F32)
    for jj in range(CMP_STRIDE // 2):
        x2 = jnp.concatenate([xj(2 * jj), xj(2 * jj + 1)], axis=1)
        lhs = jnp.concatenate([x2, posab_ref[jj]], axis=0).astype(BF16)
        acc = acc + _dot(lhs, w1_ref[jj])
    top = acc[0:m, 0:2 * CMP_HIDDEN]
    bot = acc[0:m, 2 * CMP_HIDDEN:]
    c = acc[m:m + 1, 0:2 * CMP_HIDDEN] + acc[m + 1:m + 2, 2 * CMP_HIDDEN:]
    h = top + pltpu.roll(bot, m - 1, 0) + c
    out = _dot(_silu(h).astype(BF16), w2_ref[...])
    row = lax.broadcasted_iota(jnp.int32, (m, 2 * NSA_HD), 0)
    return jnp.where(row < m - 1, out, 0.0)


def _compress_prompt_kernel(kc_ref, vc_ref, pk_ref, w1k_ref, w2k_ref, pv_ref, w1v_ref, w2v_ref, ok_ref, ov_ref, *, m):
    ok_ref[0] = _compress_core(lambda j: kc_ref[pl.ds(j, m, stride=CMP_STRIDE), :], pk_ref, w1k_ref, w2k_ref, m)
    ov_ref[0] = _compress_core(lambda j: vc_ref[pl.ds(j, m, stride=CMP_STRIDE), :], pv_ref, w1v_ref, w2v_ref, m)


def _cmp_weight_specs(nmap):
    return [pl.BlockSpec((CMP_STRIDE // 2, 8, 256), nmap(3)),
            pl.BlockSpec((CMP_STRIDE // 2, 256, 4 * CMP_HIDDEN), nmap(3)),
            pl.BlockSpec((2 * CMP_HIDDEN, 2 * NSA_HD), nmap(2))]


def compress_prompt(nkv, cw, n_seq, t_seq):
    m = t_seq // CMP_STRIDE
    zmap = lambda nd: (lambda n: (0,) * nd)
    out = jax.ShapeDtypeStruct((n_seq, m, 2 * NSA_HD), F32)
    return pl.pallas_call(
        functools.partial(_compress_prompt_kernel, m=m),
        grid=(n_seq,),
        in_specs=[pl.BlockSpec((t_seq, 128), lambda n: (n, 4)), pl.BlockSpec((t_seq, 128), lambda n: (n, 5))]
        + _cmp_weight_specs(zmap) + _cmp_weight_specs(zmap),
        out_specs=[pl.BlockSpec((1, m, 2 * NSA_HD), lambda n: (n, 0, 0))] * 2,
        out_shape=[out, out],
        compiler_params=_cp("parallel"),
    )(nkv, nkv, *cw)


def _compress_sample_kernel(pt_ref, *refs, m):
    del pt_ref
    kpages = refs[0:N_PAGES]
    vpages = refs[N_PAGES:2 * N_PAGES]
    pk_ref, w1k_ref, w2k_ref, pv_ref, w1v_ref, w2v_ref, ok_ref, ov_ref = refs[2 * N_PAGES:]
    per_page = PAGE_SIZE // CMP_STRIDE

    def rows_of(pages):
        return lambda j: jnp.concatenate([p[pl.ds(j, per_page, stride=CMP_STRIDE), :] for p in pages], axis=0)

    ok_ref[0] = _compress_core(rows_of(kpages), pk_ref, w1k_ref, w2k_ref, m)
    ov_ref[0] = _compress_core(rows_of(vpages), pv_ref, w1v_ref, w2v_ref, m)


def _page_specs(layer):
    return [pl.BlockSpec((None, None, PAGE_SIZE, 128), functools.partial(lambda n, pt, p: (layer, pt[n, p], 0, 0), p=p))
            for p in range(N_PAGES)]


def compress_sample(page_table, cache_k, cache_v, layer, cw):
    m = PAST_LEN // CMP_STRIDE
    zmap = lambda nd: (lambda n, pt: (0,) * nd)
    out = jax.ShapeDtypeStruct((DEC_BATCH, m, 2 * NSA_HD), F32)
    return pl.pallas_call(
        functools.partial(_compress_sample_kernel, m=m),
        grid_spec=pltpu.PrefetchScalarGridSpec(
            num_scalar_prefetch=1, grid=(DEC_BATCH,),
            in_specs=_page_specs(layer) + _page_specs(layer) + _cmp_weight_specs(zmap) + _cmp_weight_specs(zmap),
            out_specs=[pl.BlockSpec((1, m, 2 * NSA_HD), lambda n, pt: (n, 0, 0))] * 2),
        out_shape=[out, out],
        compiler_params=_cp("arbitrary"),
    )(page_table, *([cache_k] * N_PAGES), *([cache_v] * N_PAGES), *cw)


def _masked_softmax(s, mask):
    s = jnp.where(mask, s, NEG)
    m = jnp.max(s, axis=-1, keepdims=True)
    p = jnp.where(mask, jnp.exp(s - m), 0.0)
    l = jnp.sum(p, axis=-1, keepdims=True)
    return p / jnp.where(l > 0.0, l, 1.0)


def _stack_heads(nq, g, tq):
    parts = [nq[:, (g * NSA_REP + r) * NSA_HD:(g * NSA_REP + r + 1) * NSA_HD] for r in range(NSA_REP)]
    return (jnp.concatenate(parts, axis=0) * (NSA_HD ** -0.5)).astype(BF16)


def _rep(mask, tq):
    return jnp.concatenate([mask] * NSA_REP, axis=0)


def _cmp_branch(q4, kcmp, vcmp, qpos, tq, n_blk):
    n_cmp = kcmp.shape[0]
    s = _dot_nt(q4, kcmp.astype(BF16))
    cmp_end = lax.broadcasted_iota(jnp.int32, (tq, n_cmp), 1) * CMP_STRIDE + (CMP_LEN - 1)
    p = _masked_softmax(s, _rep(cmp_end <= qpos, tq))
    o = _dot(p.astype(BF16), vcmp.astype(BF16))
    psum = p[0:tq] + p[tq:2 * tq] + p[2 * tq:3 * tq] + p[3 * tq:4 * tq]
    sj = lax.broadcasted_iota(jnp.int32, (n_blk, n_cmp), 0) * SEL_BLOCK
    ci = lax.broadcasted_iota(jnp.int32, (n_blk, n_cmp), 1) * CMP_STRIDE
    ov = jnp.clip(jnp.minimum(ci + CMP_LEN, sj + SEL_BLOCK) - jnp.maximum(ci, sj), 0, CMP_LEN).astype(F32) / CMP_LEN
    ov = ov.astype(BF16)
    p1 = psum.astype(BF16).astype(F32)
    p2 = (psum - p1).astype(BF16).astype(F32)
    p3 = psum - p1 - p2
    imp_t = _dot_nt(jnp.concatenate([ov, ov, ov], axis=1), jnp.concatenate([p1, p2, p3], axis=1).astype(BF16))
    return o, imp_t


def _select_blocks(imp_ts, qpos_row, tq, n_blk):
    blk = lax.broadcasted_iota(jnp.int32, (n_blk, tq), 0)
    cur = qpos_row // SEL_BLOCK
    valid = blk <= cur
    forced = valid & ((blk == 0) | (blk >= cur - 1))
    works = [jnp.where(valid, jnp.where(forced, SEL_FORCED, imp_t), NEG) for imp_t in imp_ts]
    sels = [jnp.zeros((n_blk, tq), F32) for _ in imp_ts]
    for _ in range(SEL_TOPN):
        for j in range(len(works)):
            m = jnp.max(works[j], axis=0, keepdims=True)
            idx = jnp.min(jnp.where(works[j] == m, blk, n_blk), axis=0, keepdims=True)
            pick = blk == idx
            sels[j] = jnp.where(pick, 1.0, sels[j])
            works[j] = jnp.where(pick, -jnp.inf, works[j])
    return [s.T for s in sels]


def _expand_sel(sel, first_blk, n_keys):
    n_blk = sel.shape[1]
    bj = lax.broadcasted_iota(jnp.int32, (n_blk, n_keys), 0)
    kb = lax.broadcasted_iota(jnp.int32, (n_blk, n_keys), 1) // SEL_BLOCK + first_blk
    e = jnp.where(bj == kb, 1.0, 0.0).astype(BF16)
    return _dot(sel.astype(BF16), e) > 0.5


def _gate_mix(o_ref, gates, g, o_cmp, o_slc, o_win, tq):
    for r in range(NSA_REP):
        h = g * NSA_REP + r
        rs = slice(r * tq, (r + 1) * tq)
        c0 = 8 + 3 * h
        o = (gates[:, c0:c0 + 1] * o_cmp[rs] + gates[:, c0 + 1:c0 + 2] * o_slc[rs] + gates[:, c0 + 2:c0 + 3] * o_win[rs])
        o_ref[:, h * NSA_HD:(h + 1) * NSA_HD] = o


def _nsa_prompt_kernel(nq_ref, sm_ref, kv_ref, et_ref, kcmp_ref, vcmp_ref, o_ref):
    tq = Q_BLOCK
    i = pl.program_id(1)
    qpos = i * tq + lax.broadcasted_iota(jnp.int32, (tq, 1), 0)
    nq = nq_ref[...]
    gates = _sigmoid(sm_ref[...])
    kc = 512
    qpos_row = i * tq + lax.broadcasted_iota(jnp.int32, (1, tq), 1)
    q4s, o_cmps, imp_ts = [], [], []
    for g in range(NSA_KV):
        gs = slice(g * NSA_HD, (g + 1) * NSA_HD)
        q4s.append(_stack_heads(nq, g, tq))
        o_cmp, imp_t = _cmp_branch(q4s[g], kcmp_ref[0, :, gs], vcmp_ref[0, :, gs], qpos, tq, 128)
        o_cmps.append(o_cmp)
        imp_ts.append(imp_t)
    sels = _select_blocks(imp_ts, qpos_row, tq, 128)
    qas = [jnp.concatenate([_rep(jnp.where(sels[g] > 0.5, 0.0, -SEL_MASK), tq).astype(BF16), q4s[g]], axis=1)
           for g in range(NSA_KV)]

    def slc_step(c, carry, diagonal):
        start = pl.multiple_of(c * kc, kc)
        et = et_ref[pl.ds(start, kc), :]
        ss = [_dot_nt(qas[g], jnp.concatenate([et, kv_ref[pl.ds(start, kc), g * NSA_HD:(g + 1) * NSA_HD]], axis=1))
              for g in range(NSA_KV)]
        if diagonal:
            causal = _rep(start + lax.broadcasted_iota(jnp.int32, (tq, kc), 1) <= qpos, tq)
            ss = [jnp.where(causal, s, -SEL_MASK) for s in ss]
        out = []
        for g in range(NSA_KV):
            m_i, l_i, acc = carry[g]
            m_new = jnp.maximum(m_i, jnp.max(ss[g], axis=-1, keepdims=True))
            alpha = jnp.exp(m_i - m_new)
            p = jnp.exp(ss[g] - m_new)
            l_new = alpha * l_i + jnp.sum(p, axis=-1, keepdims=True)
            v = kv_ref[pl.ds(start, kc), 128 + g * NSA_HD:128 + (g + 1) * NSA_HD]
            out.append((m_new, l_new, alpha * acc + _dot(p.astype(BF16), v)))
        return tuple(out)

    init = tuple((jnp.full((NSA_REP * tq, 1), -3.0e38, F32), jnp.zeros((NSA_REP * tq, 1), F32),
                  jnp.zeros((NSA_REP * tq, NSA_HD), F32)) for _ in range(NSA_KV))
    n_full = i // (kc // tq)
    carry = lax.fori_loop(0, n_full, functools.partial(slc_step, diagonal=False), init)
    fin = slc_step(n_full, carry, True)

    for g in range(NSA_KV):
        q4, o_cmp = q4s[g], o_cmps[g]
        _, l_f, acc = fin[g]
        o_slc = acc / jnp.where(l_f > 0.0, l_f, 1.0)

        nband = WINDOW + tq
        wstart = pl.multiple_of(jnp.maximum(i - WINDOW // tq, 0) * tq, tq)
        kw = kv_ref[pl.ds(wstart, nband), 256 + g * NSA_HD:256 + (g + 1) * NSA_HD]
        vw = kv_ref[pl.ds(wstart, nband), 384 + g * NSA_HD:384 + (g + 1) * NSA_HD]
        d = qpos - (wstart + lax.broadcasted_iota(jnp.int32, (tq, nband), 1))
        pw = _masked_softmax(_dot_nt(q4, kw), _rep((d >= 0) & (d <= WINDOW), tq))
        o_win = _dot(pw.astype(BF16), vw)
        _gate_mix(o_ref, gates, g, o_cmp, o_slc, o_win, tq)


def nsa_prompt(nq, small, kv_bf, kcmp, vcmp, n_seq, t_seq):
    nb = t_seq // Q_BLOCK
    m = kcmp.shape[1]
    key_blk = jnp.arange(t_seq, dtype=jnp.int32)[:, None] // SEL_BLOCK
    et = (key_blk == jnp.arange(128, dtype=jnp.int32)[None, :]).astype(BF16)
    return pl.pallas_call(
        _nsa_prompt_kernel,
        grid=(n_seq, nb),
        in_specs=[pl.BlockSpec((Q_BLOCK, 512), lambda n, i: (n * nb + i, 0)),
                  pl.BlockSpec((Q_BLOCK, 128), lambda n, i: (n * nb + i, 0)),
                  pl.BlockSpec((t_seq, 512), lambda n, i: (n, 0)),
                  pl.BlockSpec((t_seq, 128), lambda n, i: (0, 0)),
                  pl.BlockSpec((1, m, 128), lambda n, i: (n, 0, 0)),
                  pl.BlockSpec((1, m, 128), lambda n, i: (n, 0, 0))],
        out_specs=pl.BlockSpec((Q_BLOCK, 512), lambda n, i: (n * nb + i, 0)),
        out_shape=jax.ShapeDtypeStruct((n_seq * t_seq, 512), F32),
        compiler_params=_cp("parallel", "arbitrary"),
    )(nq, small, kv_bf, et, kcmp, vcmp)


def _nsa_sample_kernel(pt_ref, *refs):
    del pt_ref
    kpages = refs[0:N_PAGES]
    vpages = refs[N_PAGES:2 * N_PAGES]
    nq_ref, sm_ref, kvn_ref, kcmp_ref, vcmp_ref, wk_ref, wv_ref, o_ref = refs[2 * N_PAGES:]
    tq = DEC_SEQ
    qpos = PAST_LEN + lax.broadcasted_iota(jnp.int32, (tq, 1), 0)
    nq = nq_ref[...]
    gates = _sigmoid(sm_ref[...])
    kvn = kvn_ref[...]
    pad = jnp.zeros((PAGE_SIZE - tq, 128), F32)
    new = [jnp.concatenate([kvn[:, c * 128:(c + 1) * 128], pad], axis=0) for c in range(4)]
    n_chunks = N_PAGES + 1
    n_keys = n_chunks * PAGE_SIZE
    n_blk = 40
    qpos_row = PAST_LEN + lax.broadcasted_iota(jnp.int32, (1, tq), 1)
    q4s, o_cmps, imp_ts = [], [], []
    for g in range(NSA_KV):
        gs = slice(g * NSA_HD, (g + 1) * NSA_HD)
        q4s.append(_stack_heads(nq, g, tq))
        o_cmp, imp_t = _cmp_branch(q4s[g], kcmp_ref[0, :, gs], vcmp_ref[0, :, gs], qpos, tq, n_blk)
        o_cmps.append(o_cmp)
        imp_ts.append(imp_t)
    sels = _select_blocks(imp_ts, qpos_row, tq, n_blk)
    for g in range(NSA_KV):
        gs = slice(g * NSA_HD, (g + 1) * NSA_HD)
        q4, sel, o_cmp = q4s[g], sels[g], o_cmps[g]
        s = jnp.concatenate([_dot(q4, p[gs, :].astype(BF16)) for p in kpages]
                            + [_dot_nt(q4, new[0][:, gs].astype(BF16))], axis=1)
        kpos = lax.broadcasted_iota(jnp.int32, (tq, n_keys), 1)
        msk = _rep(_expand_sel(sel, 0, n_keys) & (kpos <= qpos), tq)
        p = _masked_softmax(s, msk).astype(BF16)
        o_slc = _dot(p[:, N_PAGES * PAGE_SIZE:], new[1][:, gs].astype(BF16))
        for c in range(N_PAGES):
            o_slc = o_slc + _dot_nt(p[:, c * PAGE_SIZE:(c + 1) * PAGE_SIZE], vpages[c][gs, :].astype(BF16))

        sw = jnp.concatenate([_dot(q4, wk_ref[gs, :].astype(BF16)), _dot_nt(q4, new[2][:, gs].astype(BF16))], axis=1)
        d = qpos - (PAST_LEN - WINDOW + lax.broadcasted_iota(jnp.int32, (tq, WINDOW + PAGE_SIZE), 1))
        pw = _masked_softmax(sw, _rep((d >= 0) & (d <= WINDOW), tq)).astype(BF16)
        o_win = (_dot_nt(pw[:, :WINDOW], wv_ref[gs, :].astype(BF16)) + _dot(pw[:, WINDOW:], new[3][:, gs].astype(BF16)))
        _gate_mix(o_ref, gates, g, o_cmp, o_slc, o_win, tq)


def nsa_sample(page_table, cache_k, cache_v, layer, nq, small, nkv, kcmp, vcmp, win_k, win_v):
    off = RP // DEC_SEQ
    rmap = lambda n, pt: (off + n, 0)
    m = kcmp.shape[1]
    return pl.pallas_call(
        _nsa_sample_kernel,
        grid_spec=pltpu.PrefetchScalarGridSpec(
            num_scalar_prefetch=1, grid=(DEC_BATCH,),
            in_specs=_page_specs(layer) + _page_specs(layer) + [
                pl.BlockSpec((DEC_SEQ, 512), rmap),
                pl.BlockSpec((DEC_SEQ, 128), rmap),
                pl.BlockSpec((DEC_SEQ, 768), rmap),
                pl.BlockSpec((1, m, 128), lambda n, pt: (n, 0, 0)),
                pl.BlockSpec((1, m, 128), lambda n, pt: (n, 0, 0)),
                pl.BlockSpec((None, None, 128, WINDOW), lambda n, pt: (layer, n, 0, 0)),
                pl.BlockSpec((None, None, 128, WINDOW), lambda n, pt: (layer, n, 0, 0))],
            out_specs=pl.BlockSpec((DEC_SEQ, 512), lambda n, pt: (n, 0))),
        out_shape=jax.ShapeDtypeStruct((RS, 512), F32),
        compiler_params=_cp("arbitrary"),
    )(page_table, *([cache_k] * N_PAGES), *([cache_v] * N_PAGES), nq, small, nkv, kcmp, vcmp, win_k, win_v)


S5_TILE = 256
S5_LANES = 512
S5_BLK = 16


def _gelu_tanh(x):
    return 0.5 * x * (1.0 + jnp.tanh(math.sqrt(2.0 / math.pi) * (x + 0.044715 * (x * x * x))))


def _s5_kernel(u_ref, bdr_ref, bdi_ref, cr_ref, ci_ref, tab_ref, d_ref, x0r_ref, x0i_ref,
               y_ref, fr_ref, fi_ref, xr_scr, xi_scr, car_scr, *, chained):
    i = pl.program_id(1)
    nblk = S5_TILE // 8
    nb = S5_GROUPS // S5_BLK
    wi = S5_BLK * S5_CH
    ws = S5_BLK * S5_P
    u = u_ref[...]
    ub = u.astype(BF16)
    for b in range(nb):
        xr_scr[:, b * ws:(b + 1) * ws] = _dot(ub[:, b * wi:(b + 1) * wi], bdr_ref[b])
        xi_scr[:, b * ws:(b + 1) * ws] = _dot(ub[:, b * wi:(b + 1) * wi], bdi_ref[b])

    if chained:
        @pl.when(i == 0)
        def _():
            car_scr[0:1, :] = x0r_ref[0]
            car_scr[1:2, :] = x0i_ref[0]

    for c in range(S5_STATE // S5_LANES):
        ls = slice(c * S5_LANES, (c + 1) * S5_LANES)
        t1r, t1i, t2r, t2i, t4r, t4i, tpr, tpi = [tab_ref[k, :, ls] for k in range(8)]
        if chained:
            cr, ci = car_scr[0:1, ls], car_scr[1:2, ls]
        lasts = []
        for blk in range(nblk):
            rs = slice(blk * 8, (blk + 1) * 8)
            if not chained:
                cr, ci = x0r_ref[0, blk:blk + 1, ls], x0i_ref[0, blk:blk + 1, ls]
            xr, xi = xr_scr[rs, ls], xi_scr[rs, ls]
            for sh, (mr, mi) in ((1, (t1r, t1i)), (2, (t2r, t2i)), (4, (t4r, t4i))):
                sr, si = pltpu.roll(xr, sh, 0), pltpu.roll(xi, sh, 0)
                xr, xi = xr + mr * sr - mi * si, xi + mr * si + mi * sr
            xr, xi = xr + tpr * cr - tpi * ci, xi + tpr * ci + tpi * cr
            xr_scr[rs, ls] = xr
            xi_scr[rs, ls] = xi
            cr, ci = xr[7:8, :], xi[7:8, :]
            lasts.append((cr, ci))
        if chained:
            car_scr[0:1, ls] = cr
            car_scr[1:2, ls] = ci
        else:
            fr_ref[0, :, ls] = jnp.concatenate([p[0] for p in lasts], axis=0)
            fi_ref[0, :, ls] = jnp.concatenate([p[1] for p in lasts], axis=0)

    ys = []
    for b in range(nb):
        ys.append(_dot(xr_scr[:, b * ws:(b + 1) * ws].astype(BF16), cr_ref[b])
                  - _dot(xi_scr[:, b * ws:(b + 1) * ws].astype(BF16), ci_ref[b]))
    y_ref[...] = _gelu_tanh(jnp.concatenate(ys, axis=1) + d_ref[...] * u).astype(BF16)

    if chained:
        @pl.when(i == pl.num_programs(1) - 1)
        def _():
            fr_ref[0] = car_scr[0:1, :]
            fi_ref[0] = car_scr[1:2, :]


def s5_scan(u, sp, x0r, x0i, row_off, n_seq, t_seq, chained):
    if chained:
        grid = (n_seq, t_seq // S5_TILE)
        nb = grid[1]
        smap = lambda n, i: (n, 0, 0)
    else:
        grid = (1, n_seq * t_seq // S5_TILE)
        nb = grid[1]
        smap = lambda n, i: (i, 0, 0)
    off = row_off // S5_TILE
    rmap = lambda n, i: (off + n * nb + i, 0)
    sblk = (1,) + x0r.shape[1:]
    const = lambda nd: (lambda n, i: (0,) * nd)
    in_specs = [pl.BlockSpec((S5_TILE, D_MODEL), rmap),
                pl.BlockSpec(sp["bdr"].shape, const(3)), pl.BlockSpec(sp["bdi"].shape, const(3)),
                pl.BlockSpec(sp["cr"].shape, const(3)), pl.BlockSpec(sp["ci"].shape, const(3)),
                pl.BlockSpec(sp["tab"].shape, const(3)), pl.BlockSpec((1, D_MODEL), const(2)),
                pl.BlockSpec(sblk, smap), pl.BlockSpec(sblk, smap)]
    args = [u, sp["bdr"], sp["bdi"], sp["cr"], sp["ci"], sp["tab"], sp["d"], x0r, x0i]
    return pl.pallas_call(
        functools.partial(_s5_kernel, chained=chained),
        grid=grid,
        in_specs=in_specs,
        out_specs=[pl.BlockSpec((S5_TILE, D_MODEL), lambda n, i: (n * nb + i, 0)),
                   pl.BlockSpec(sblk, smap), pl.BlockSpec(sblk, smap)],
        out_shape=[jax.ShapeDtypeStruct((n_seq * t_seq, D_MODEL), BF16), jax.ShapeDtypeStruct(x0r.shape, F32),
                   jax.ShapeDtypeStruct(x0r.shape, F32)],
        scratch_shapes=[pltpu.VMEM((S5_TILE, S5_STATE), F32), pltpu.VMEM((S5_TILE, S5_STATE), F32),
                        pltpu.VMEM((8, S5_STATE), F32)],
        compiler_params=_cp("arbitrary", "arbitrary"),
    )(*args)


def _prep_ab_in(w):
    qkv, b, a, gate, nq, nkv, ngate = jnp.split(w, (1536, 1540, 1544, 2056, 2568, 3336), axis=-1)
    kc, vc, ks, vs, kw, vw = jnp.split(nkv, 6, axis=-1)
    small = jnp.concatenate([b, a, ngate, jnp.zeros(w.shape[:-1] + (96,), w.dtype)], axis=-1)
    return jnp.concatenate([qkv, ks, vs, kw, vw, kc, vc, gate, nq, small], axis=-1).astype(BF16)


AB_SPLITS = (1536, 768, 512, 512, 128)


def _prep_cmp(pos, w1, w2):
    top = w1[:CMP_STRIDE * NSA_HD].reshape(CMP_STRIDE, NSA_HD, CMP_HIDDEN)
    bot = w1[CMP_STRIDE * NSA_HD:].reshape(CMP_STRIDE, NSA_HD, CMP_HIDDEN)
    w1bd = jnp.zeros((CMP_STRIDE, 2 * NSA_HD, 4 * CMP_HIDDEN), F32)
    w2bd = jnp.zeros((2 * CMP_HIDDEN, 2 * NSA_HD), F32)
    for g in range(NSA_KV):
        rs = slice(g * NSA_HD, (g + 1) * NSA_HD)
        w1bd = w1bd.at[:, rs, g * CMP_HIDDEN:(g + 1) * CMP_HIDDEN].set(top)
        w1bd = w1bd.at[:, rs, (2 + g) * CMP_HIDDEN:(3 + g) * CMP_HIDDEN].set(bot)
        w2bd = w2bd.at[g * CMP_HIDDEN:(g + 1) * CMP_HIDDEN, rs].set(w2)
    posab = jnp.zeros((CMP_STRIDE, 8, 2 * NSA_HD), F32)
    posab = posab.at[:, 0, :].set(jnp.tile(pos[:CMP_STRIDE], (1, NSA_KV)))
    posab = posab.at[:, 1, :].set(jnp.tile(pos[CMP_STRIDE:], (1, NSA_KV)))
    half = CMP_STRIDE // 2
    posab = posab.reshape(half, 2, 8, 2 * NSA_HD).transpose(0, 2, 1, 3).reshape(half, 8, 4 * NSA_HD)
    w1bd = w1bd.reshape(half, 4 * NSA_HD, 4 * CMP_HIDDEN)
    return posab, w1bd.astype(BF16), w2bd.astype(BF16)


def _cmul(ar, ai, br, bi):
    return ar * br - ai * bi, ar * bi + ai * br


def _prep_s5(a_re, a_im, b_re, b_im, c_re, c_im, d, log_dt):
    dt = jnp.exp(log_dt)[:, None]
    lr = jnp.minimum(a_re, S5_MAX_RE)
    li = a_im
    mag = jnp.exp(lr * dt)
    ar = mag * jnp.cos(li * dt)
    ai = mag * jnp.sin(li * dt)
    den = lr * lr + li * li
    fr = ((ar - 1.0) * lr + ai * li) / den
    fi = (ai * lr - (ar - 1.0) * li) / den
    bbr = fr[..., None] * b_re - fi[..., None] * b_im
    bbi = fr[..., None] * b_im + fi[..., None] * b_re
    nb = S5_GROUPS // S5_BLK
    eye = jnp.eye(S5_BLK, dtype=F32)

    def bd_in(m):
        m4 = jnp.swapaxes(m, 1, 2).reshape(nb, S5_BLK, S5_CH, S5_P)
        return jnp.einsum('bgcp,gh->bgchp', m4, eye).reshape(nb, S5_BLK * S5_CH, S5_BLK * S5_P).astype(BF16)

    def bd_out(m):
        m4 = jnp.swapaxes(m, 1, 2).reshape(nb, S5_BLK, S5_P, S5_CH)
        return jnp.einsum('bgpc,gh->bgphc', m4, eye).reshape(nb, S5_BLK * S5_P, S5_BLK * S5_CH).astype(BF16)

    a1 = (ar.reshape(1, S5_STATE), ai.reshape(1, S5_STATE))
    pw = [a1]
    for _ in range(7):
        pw.append(_cmul(*pw[-1], *a1))
    row = jnp.arange(8)[:, None]
    tabs = []
    for sh in (1, 2, 4):
        for part in pw[sh - 1]:
            tabs.append(jnp.where(row >= sh, part, 0.0))
    tabs.append(jnp.concatenate([p[0] for p in pw], axis=0))
    tabs.append(jnp.concatenate([p[1] for p in pw], axis=0))
    return {"bdr": bd_in(bbr), "bdi": bd_in(bbi), "cr": bd_out(c_re), "ci": bd_out(c_im),
            "tab": jnp.stack(tabs), "d": d.reshape(1, D_MODEL)}


def _heads(a, n, t):
    return a.reshape(n, t, NSA_KV, NSA_HD)


def kernel(x_prompt, x_sample, mem_prompt, cache_mem_k, cache_mem_v, state_gdn, state_gdn_conv, cache_cmp_k, cache_cmp_v, cache_slc_k, cache_slc_v, cache_win_k, cache_win_v, state_s5_re, state_s5_im, page_table, norm_ffn1, w_ffn1_gate, w_ffn1_up, w_ffn1_down, norm_mix, norm_xq, norm_mem, w_xq, w_xk, w_xv, w_xo, norm_ffn2, w_ffn2_gate, w_ffn2_up, w_ffn2_down, norm_final, w_in_ab, w_out_ab, gdn_conv, gdn_a_log, gdn_dt_bias, gdn_norm, cmp_pos_k, cmp_w1_k, cmp_w2_k, cmp_pos_v, cmp_w1_v, cmp_w2_v, w_in_c, s5_a_re, s5_a_im, s5_b_re, s5_b_im, s5_c_re, s5_c_im, s5_d, s5_log_dt, w_glu, w_out_c):
    bf = lambda w: w.astype(BF16)
    n_ab = w_in_ab.shape[0]
    n_pool = cache_cmp_k.shape[1]
    x = jnp.concatenate([x_prompt.reshape(RP, D_MODEL), x_sample.reshape(RS, D_MODEL)], axis=0)

    memkv = mem_kv_all(mem_prompt.reshape(BATCH * N_MEM, D_MODEL), norm_mem, bf(jnp.concatenate([w_xk, w_xv], axis=-1)))
    mem_k_prompt = memkv[:, :, :D_MODEL].reshape(DEPTH, BATCH, N_MEM, X_HEADS, X_HD)
    mem_v_prompt = memkv[:, :, D_MODEL:].reshape(DEPTH, BATCH, N_MEM, X_HEADS, X_HD)

    paged = lambda c: c.reshape(n_ab, n_pool, PAGE_SIZE, NSA_KV * NSA_HD)
    cck, ccv = paged(cache_cmp_k), paged(cache_cmp_v)
    fmaj = lambda c: jnp.transpose(c, (0, 1, 3, 4, 2)).reshape(c.shape[0], c.shape[1], NSA_KV * NSA_HD, c.shape[2])
    csk, csv, cwk, cwv = fmaj(cache_slc_k), fmaj(cache_slc_v), fmaj(cache_win_k), fmaj(cache_win_v)
    ab_p, ab_s, c_p, c_s = [], [], [], []
    for l in range(DEPTH):
        i = l // 2
        x = glu_mlp(x, norm_ffn1[l], bf(w_ffn1_gate[l]), bf(w_ffn1_up[l]), bf(w_ffn1_down[l]), None, 1024, 256, True, True, 0.5)
        if l % 2 == 0:
            qkv, nkv, gate, nq, small = norm_matmul(x, norm_mix[l], _prep_ab_in(w_in_ab[i]), AB_SPLITS, 512)
            gpar = jnp.zeros((8, 128), F32).at[0, 4:8].set(gdn_a_log[i]).at[1, 4:8].set(gdn_dt_bias[i])
            nw = gdn_norm[i].reshape(1, GDN_DK)
            cb_s = jnp.zeros((DEC_BATCH, 8, GDN_QKV), F32).at[:, 5:8].set(state_gdn_conv[i])
            oa_p, st_p = gdn(qkv, small, gate, gdn_conv[i], gpar, nw, jnp.zeros((BATCH, 8, GDN_QKV), F32),
                             jnp.zeros((BATCH, GDN_HEADS, GDN_DK, GDN_DK), F32), 0, BATCH, SEQ, 1, 256, GDN_CHUNK)
            oa_s, st_s = gdn(qkv, small, gate, gdn_conv[i], gpar, nw, cb_s, state_gdn[i], RP, DEC_BATCH, DEC_SEQ,
                             8, DEC_SEQ, math.gcd(DEC_SEQ, GDN_CHUNK))
            cw = _prep_cmp(cmp_pos_k[i], cmp_w1_k[i], cmp_w2_k[i]) + _prep_cmp(cmp_pos_v[i], cmp_w1_v[i], cmp_w2_v[i])
            kcmp_p, vcmp_p = compress_prompt(nkv, cw, BATCH, SEQ)
            ob_p = nsa_prompt(nq, small, bf(nkv[:RP, :512]), kcmp_p, vcmp_p, BATCH, SEQ)
            kcmp_s, vcmp_s = compress_sample(page_table, cck, ccv, i, cw)
            ob_s = nsa_sample(page_table, csk, csv, i, nq, small, nkv, kcmp_s, vcmp_s, cwk, cwv)
            w_out = bf(w_out_ab[i])
            x = matmul_residual([(oa_p, oa_s, w_out[:512]), (ob_p, ob_s, w_out[512:])], x, 512)
            conv_p = jnp.stack([qkv[(n + 1) * SEQ - 3:(n + 1) * SEQ] for n in range(BATCH)])
            qkv_s = qkv[RP:].reshape(DEC_BATCH, DEC_SEQ, GDN_QKV)
            col = lambda a, c: a[:, c * 128:(c + 1) * 128]
            nkv_p, nkv_s = nkv[:RP], nkv[RP:]
            hp = lambda c: _heads(col(nkv_p, c), BATCH, SEQ)
            hs = lambda c: _heads(col(nkv_s, c), DEC_BATCH, DEC_SEQ)
            ab_p.append((conv_p, st_p, hp(4), hp(5), hp(0), hp(1), hp(2)[:, SEQ - WINDOW:], hp(3)[:, SEQ - WINDOW:]))
            ab_s.append((qkv_s[:, DEC_SEQ - 3:], st_s, hs(4), hs(5), hs(0), hs(1),
                         jnp.concatenate([cache_win_k[i][:, DEC_SEQ:], hs(2)], axis=1),
                         jnp.concatenate([cache_win_v[i][:, DEC_SEQ:], hs(3)], axis=1)))
        else:
            (u,) = norm_matmul(x, norm_mix[l], bf(w_in_c[i]), (D_MODEL,), 512)
            sp = _prep_s5(s5_a_re[i], s5_a_im[i], s5_b_re[i], s5_b_im[i], s5_c_re[i], s5_c_im[i], s5_d[i], s5_log_dt[i])
            z0 = jnp.zeros((BATCH, 1, S5_STATE), F32)
            y_p, fr_p, fi_p = s5_scan(u, sp, z0, z0, 0, BATCH, SEQ, True)
            per_tile = S5_TILE // DEC_SEQ
            x0r = state_s5_re[i].reshape(RS // S5_TILE, per_tile, S5_STATE)
            x0i = state_s5_im[i].reshape(RS // S5_TILE, per_tile, S5_STATE)
            y_s, fr_s, fi_s = s5_scan(u, sp, x0r, x0i, RP, DEC_BATCH, DEC_SEQ, False)
            wg = bf(w_glu[i])
            x = glu_mlp(y_p, norm_mix[l], wg[:, :D_MODEL], wg[:, D_MODEL:], bf(w_out_c[i]), x, 1024, 256, False, False, 1.0,
                        src_s=y_s)
            c_p.append((fr_p.reshape(BATCH, S5_GROUPS, S5_P), fi_p.reshape(BATCH, S5_GROUPS, S5_P)))
            c_s.append((fr_s.reshape(DEC_BATCH, S5_GROUPS, S5_P), fi_s.reshape(DEC_BATCH, S5_GROUPS, S5_P)))
        (q,) = norm_matmul(x, norm_xq[l], bf(w_xq[l]), (D_MODEL,), 512)
        o_p = cross_attention(q, memkv, memkv,
                              pl.BlockSpec((None, N_MEM, D_MODEL), functools.partial(lambda n, i, l: (l, n, 0), l=l)),
                              pl.BlockSpec((None, N_MEM, D_MODEL), functools.partial(lambda n, i, l: (l, n, 1), l=l)),
                              0, BATCH, SEQ, 512)
        o_s = cross_attention_cached(q, cache_mem_k, cache_mem_v, l, RP)
        x = matmul_residual([(o_p, o_s, bf(w_xo[l]))], x, 512)
        x = glu_mlp(x, norm_ffn2[l], bf(w_ffn2_gate[l]), bf(w_ffn2_up[l]), bf(w_ffn2_down[l]), None, 1024, 256, True, True, 0.5)

    y = rmsnorm_rows(x, norm_final, 512)
    st = lambda grp, j: jnp.stack([t[j] for t in grp])
    return (y[:RP].reshape(BATCH, SEQ, D_MODEL), y[RP:].reshape(DEC_BATCH, DEC_SEQ, D_MODEL),
            mem_k_prompt, mem_v_prompt,
            st(ab_p, 1), st(ab_s, 1), st(ab_p, 0), st(ab_s, 0),
            st(ab_p, 2), st(ab_p, 3), st(ab_p, 4), st(ab_p, 5),
            st(ab_s, 2), st(ab_s, 3), st(ab_s, 4), st(ab_s, 5),
            st(ab_p, 6), st(ab_p, 7), st(ab_s, 6), st(ab_s, 7),
            st(c_p, 0), st(c_p, 1), st(c_s, 0), st(c_s, 1))
```

```python
import functools
import math

import jax
import jax.numpy as jnp
from jax import lax
from jax.experimental import pallas as pl
from jax.experimental.pallas import tpu as pltpu

F32 = jnp.float32
BF16 = jnp.bfloat16
HIGHEST = lax.Precision.HIGHEST

D_MODEL = 1024
BATCH = 2
SEQ = 8192
DEPTH = 4
DEC_BATCH = 128
DEC_SEQ = 8
PAST_LEN = 2048
PAGE_SIZE = 128
N_PAGES = PAST_LEN // PAGE_SIZE
RP = BATCH * SEQ
RS = DEC_BATCH * DEC_SEQ
ROWS = RP + RS

GDN_HEADS = 4
GDN_DK = 128
GDN_QKV = 1536
GDN_CONV = 4
GDN_CHUNK = 64
NSA_HEADS = 8
NSA_KV = 2
NSA_HD = 64
NSA_REP = 4
CMP_STRIDE = 16
CMP_LEN = 32
CMP_HIDDEN = 128
SEL_BLOCK = 64
SEL_TOPN = 16
WINDOW = 512
Q_BLOCK = 128
SEL_FORCED = 1.0e4
NEG = -1.0e30
SEL_MASK = 2.0 ** 100
S5_CH = 16
S5_GROUPS = 64
S5_P = 64
S5_STATE = S5_GROUPS * S5_P
S5_MAX_RE = -1.0e-4
N_MEM = 256
X_HEADS = 4
X_HD = 256
D_FF = 2816
EPS = 1.0e-6

VMEM_LIMIT = 56 * 1024 * 1024


def _cp(*sem):
    return pltpu.CompilerParams(dimension_semantics=sem, vmem_limit_bytes=VMEM_LIMIT)


def _dot(a, b, precision=None):
    return jnp.dot(a, b, preferred_element_type=F32, precision=precision)


def _dot_nt(a, b, precision=None):
    return lax.dot_general(a, b, (((1,), (1,)), ((), ())), preferred_element_type=F32, precision=precision)


def _sigmoid(x):
    return 1.0 / (1.0 + jnp.exp(-x))


def _silu(x):
    return x * _sigmoid(x)


def _rms(x, g):
    return x * lax.rsqrt(jnp.mean(x * x, axis=-1, keepdims=True) + EPS) * g


def _norm_matmul_kernel(x_ref, g_ref, w_ref, *o_refs, norm, splits):
    x = x_ref[...]
    if norm:
        x = _rms(x, g_ref[...])
    h = x.astype(BF16)
    off = 0
    for o_ref, wd in zip(o_refs, splits):
        o_ref[...] = _dot(h, w_ref[:, off:off + wd])
        off += wd


def norm_matmul(x, g, w, splits, tm, norm=True):
    rows, k = x.shape
    n = w.shape[1]
    assert sum(splits) == n and rows % tm == 0
    outs = pl.pallas_call(
        functools.partial(_norm_matmul_kernel, norm=norm, splits=tuple(splits)),
        grid=(rows // tm,),
        in_specs=[pl.BlockSpec((tm, k), lambda i: (i, 0)),
                  pl.BlockSpec((1, k), lambda i: (0, 0)),
                  pl.BlockSpec((k, n), lambda i: (0, 0))],
        out_specs=[pl.BlockSpec((tm, wd), lambda i: (i, 0)) for wd in splits],
        out_shape=[jax.ShapeDtypeStruct((rows, wd), F32) for wd in splits],
        compiler_params=_cp("parallel"),
    )(x, g.reshape(1, k), w)
    return outs


def _mem_kv_kernel(x_ref, g_ref, w_ref, o_ref):
    h = _rms(x_ref[...], g_ref[0]).astype(BF16)
    o_ref[0] = _dot(h, w_ref[0])


def mem_kv_all(mem2d, g, w):
    m, k = mem2d.shape
    nl, _, n = w.shape
    return pl.pallas_call(
        _mem_kv_kernel,
        grid=(nl,),
        in_specs=[pl.BlockSpec((m, k), lambda l: (0, 0)),
                  pl.BlockSpec((1, 1, k), lambda l: (l, 0, 0)),
                  pl.BlockSpec((1, k, n), lambda l: (l, 0, 0))],
        out_specs=pl.BlockSpec((1, m, n), lambda l: (l, 0, 0)),
        out_shape=jax.ShapeDtypeStruct((nl, m, n), F32),
        compiler_params=_cp("parallel"),
    )(mem2d, g.reshape(nl, 1, k), w)


def _matmul_res_kernel(*refs, n_terms, n_p):
    r_ref, o_ref = refs[3 * n_terms], refs[3 * n_terms + 1]
    i = pl.program_id(0)

    def run(sel):
        acc = r_ref[...]
        for t in range(n_terms):
            acc = acc + _dot(refs[3 * t + sel][...].astype(BF16), refs[3 * t + 2][...])
        o_ref[...] = acc

    @pl.when(i < n_p)
    def _():
        run(0)

    @pl.when(i >= n_p)
    def _():
        run(1)


def matmul_residual(terms, res, tm):
    rows, n = res.shape
    n_p = RP // tm
    args, specs = [], []
    for a_p, a_s, w in terms:
        k = w.shape[0]
        args += [a_p, a_s, w]
        specs += [pl.BlockSpec((tm, k), lambda i: (jnp.minimum(i, n_p - 1), 0)),
                  pl.BlockSpec((tm, k), lambda i: (jnp.maximum(i - n_p, 0), 0)),
                  pl.BlockSpec((k, n), lambda i: (0, 0))]
    return pl.pallas_call(
        functools.partial(_matmul_res_kernel, n_terms=len(terms), n_p=n_p),
        grid=(rows // tm,),
        in_specs=specs + [pl.BlockSpec((tm, n), lambda i: (i, 0))],
        out_specs=pl.BlockSpec((tm, n), lambda i: (i, 0)),
        out_shape=jax.ShapeDtypeStruct((rows, n), F32),
        compiler_params=_cp("arbitrary"),
    )(*args, res)


def _glu_mlp_kernel(*refs, norm, swiglu, scale, tf, own_res, n_p):
    if own_res:
        src_ref, g_ref, wa_ref, wb_ref, wd_ref, o_ref = refs
        res_ref = src_ref
    elif n_p is None:
        src_ref, g_ref, wa_ref, wb_ref, wd_ref, res_ref, o_ref = refs
    else:
        src_ref, srcs_ref, g_ref, wa_ref, wb_ref, wd_ref, res_ref, o_ref, h_scr = refs
    if n_p is None:
        x = src_ref[...]
        if norm:
            x = _rms(x, g_ref[...])
        h = x.astype(BF16)
    else:
        @pl.when(pl.program_id(0) < n_p)
        def _():
            h_scr[...] = src_ref[...].astype(BF16)

        @pl.when(pl.program_id(0) >= n_p)
        def _():
            h_scr[...] = srcs_ref[...].astype(BF16)

        h = h_scr[...]
    acc = None
    for j in range(wa_ref.shape[1] // tf):
        cs = slice(j * tf, (j + 1) * tf)
        a = _dot(h, wa_ref[:, cs])
        b = _dot(h, wb_ref[:, cs])
        s = _silu(a) * b if swiglu else a * _sigmoid(b)
        d = _dot(s.astype(BF16), wd_ref[cs, :])
        acc = d if acc is None else acc + d
    o_ref[...] = res_ref[...] + scale * acc


def glu_mlp(src, g, wa, wb, wd, res, tm, tf, norm, swiglu, scale, src_s=None):
    k = src.shape[1]
    ff = wa.shape[1]
    n = wd.shape[1]
    rows = src.shape[0] if res is None else res.shape[0]
    resident = lambda shape: pl.BlockSpec(shape, lambda i: (0, 0), pipeline_mode=pl.Buffered(1))
    weights = [pl.BlockSpec((1, k), lambda i: (0, 0)), resident((k, ff)), resident((k, ff)), resident((ff, n))]
    n_p, scratch = None, []
    if src_s is None:
        in_specs = [pl.BlockSpec((tm, k), lambda i: (i, 0))] + weights
        args = [src, g.reshape(1, k), wa, wb, wd]
    else:
        assert not norm and res is not None
        n_p = src.shape[0] // tm
        in_specs = [pl.BlockSpec((tm, k), lambda i: (jnp.minimum(i, n_p - 1), 0)),
                    pl.BlockSpec((tm, k), lambda i: (jnp.maximum(i - n_p, 0), 0))] + weights
        args = [src, src_s, g.reshape(1, k), wa, wb, wd]
        scratch = [pltpu.VMEM((tm, k), BF16)]
    if res is not None:
        in_specs.append(pl.BlockSpec((tm, n), lambda i: (i, 0)))
        args.append(res)
    return pl.pallas_call(
        functools.partial(_glu_mlp_kernel, norm=norm, swiglu=swiglu, scale=scale, tf=tf, own_res=res is None, n_p=n_p),
        grid=(rows // tm,),
        in_specs=in_specs,
        out_specs=pl.BlockSpec((tm, n), lambda i: (i, 0)),
        out_shape=jax.ShapeDtypeStruct((rows, n), F32),
        scratch_shapes=scratch,
        compiler_params=_cp("arbitrary" if src_s is not None else "parallel"),
    )(*args)


def _rmsnorm_kernel(x_ref, g_ref, o_ref):
    o_ref[...] = _rms(x_ref[...], g_ref[...])


def rmsnorm_rows(x, g, tm):
    rows, k = x.shape
    return pl.pallas_call(
        _rmsnorm_kernel,
        grid=(rows // tm,),
        in_specs=[pl.BlockSpec((tm, k), lambda i: (i, 0)), pl.BlockSpec((1, k), lambda i: (0, 0))],
        out_specs=pl.BlockSpec((tm, k), lambda i: (i, 0)),
        out_shape=jax.ShapeDtypeStruct((rows, k), F32),
        compiler_params=_cp("parallel"),
    )(x, g.reshape(1, k))


def _head_attention(qh, kh, vh):
    s = _dot_nt((qh * (X_HD ** -0.5)).astype(BF16), kh.astype(BF16))
    p = jnp.exp(s - jnp.max(s, axis=-1, keepdims=True))
    p = p / jnp.sum(p, axis=-1, keepdims=True)
    return _dot(p.astype(BF16), vh.astype(BF16))


def _xattn_kernel(q_ref, k_ref, v_ref, o_ref):
    q = q_ref[...]
    for h in range(X_HEADS):
        sl = slice(h * X_HD, (h + 1) * X_HD)
        o_ref[:, sl] = _head_attention(q[:, sl], k_ref[:, sl], v_ref[:, sl])


def cross_attention(q_all, mk, mv, k_spec, v_spec, row_off, n_seq, t_seq, tq):
    nb = t_seq // tq
    off = row_off // tq
    return pl.pallas_call(
        _xattn_kernel,
        grid=(n_seq, nb),
        in_specs=[pl.BlockSpec((tq, D_MODEL), lambda n, i: (off + n * nb + i, 0)), k_spec, v_spec],
        out_specs=pl.BlockSpec((tq, D_MODEL), lambda n, i: (n * nb + i, 0)),
        out_shape=jax.ShapeDtypeStruct((n_seq * t_seq, D_MODEL), F32),
        compiler_params=_cp("parallel", "parallel"),
    )(q_all, mk, mv)


def _xattn_cache_kernel(q_ref, k_hbm, v_hbm, o_ref, kbuf, vbuf, sem, *, layer):
    n = pl.program_id(0)
    slot = n % 2

    def copies(seq, s):
        cs = []
        for h in range(X_HEADS):
            cs.append(pltpu.make_async_copy(k_hbm.at[layer, seq, :, h, :], kbuf.at[s, h], sem.at[s, h]))
            cs.append(pltpu.make_async_copy(v_hbm.at[layer, seq, :, h, :], vbuf.at[s, h], sem.at[s, X_HEADS + h]))
        return cs

    @pl.when(n == 0)
    def _():
        for c in copies(0, 0):
            c.start()

    @pl.when(n + 1 < pl.num_programs(0))
    def _():
        for c in copies(n + 1, 1 - slot):
            c.start()

    for c in copies(n, slot):
        c.wait()
    q = q_ref[...]
    hs = range(X_HEADS)
    ss = [_dot_nt((q[:, h * X_HD:(h + 1) * X_HD] * (X_HD ** -0.5)).astype(BF16), kbuf[slot, h].astype(BF16)) for h in hs]
    ps = [jnp.exp(s - jnp.max(s, axis=-1, keepdims=True)) for s in ss]
    ps = [(p / jnp.sum(p, axis=-1, keepdims=True)).astype(BF16) for p in ps]
    for h in hs:
        o_ref[:, h * X_HD:(h + 1) * X_HD] = _dot(ps[h], vbuf[slot, h].astype(BF16))


def cross_attention_cached(q_all, cache_k, cache_v, layer, row_off):
    off = row_off // DEC_SEQ
    return pl.pallas_call(
        functools.partial(_xattn_cache_kernel, layer=layer),
        grid=(DEC_BATCH,),
        in_specs=[pl.BlockSpec((DEC_SEQ, D_MODEL), lambda n: (off + n, 0)),
                  pl.BlockSpec(memory_space=pl.ANY), pl.BlockSpec(memory_space=pl.ANY)],
        out_specs=pl.BlockSpec((DEC_SEQ, D_MODEL), lambda n: (n, 0)),
        out_shape=jax.ShapeDtypeStruct((DEC_BATCH * DEC_SEQ, D_MODEL), F32),
        scratch_shapes=[pltpu.VMEM((2, X_HEADS, N_MEM, X_HD), F32), pltpu.VMEM((2, X_HEADS, N_MEM, X_HD), F32),
                        pltpu.SemaphoreType.DMA((2, 2 * X_HEADS))],
        compiler_params=_cp("arbitrary"),
    )(q_all, cache_k, cache_v)


def _softplus(x):
    return jnp.maximum(x, 0.0) + jnp.log(1.0 + jnp.exp(-jnp.abs(x)))


def _split3(x, axis):
    hi = x.astype(BF16).astype(F32)
    return jnp.concatenate([hi, hi, x - hi], axis=axis).astype(BF16)


def _split3r(x, axis):
    hi = x.astype(BF16).astype(F32)
    return jnp.concatenate([hi, x - hi, hi], axis=axis).astype(BF16)


def _dot3(a, b):
    return _dot(_split3(a, 1), _split3r(b, 0))


def _cumsum_rows(tri3, g):
    g1 = g.astype(BF16).astype(F32)
    g2 = (g - g1).astype(BF16).astype(F32)
    g3 = g - g1 - g2
    return _dot(tri3, jnp.concatenate([g1, g2, g3], axis=0).astype(BF16))


def _gdn_kernel(qkv_ref, sm_ref, gate_ref, cw_ref, gp_ref, nw_ref, cb_ref, s0_ref, o_ref, sout_ref,
                xbuf, s_scr, *, seqs, rows, chunk):
    i = pl.program_id(1)
    n_chunks = rows // chunk

    @pl.when(i == 0)
    def _():
        xbuf[:, 0:8, :] = cb_ref[...]
        s_scr[...] = s0_ref[...]

    ri = lax.broadcasted_iota(jnp.int32, (chunk, chunk), 0)
    ci = lax.broadcasted_iota(jnp.int32, (chunk, chunk), 1)
    incl = ri >= ci
    strict = ri > ci
    tri = jnp.where(incl, 1.0, 0.0).astype(BF16)
    tri3 = jnp.concatenate([tri, tri, tri], axis=1)
    eye = jnp.where(ri == ci, 1.0, 0.0)
    levels = []
    w = 1
    while w < chunk:
        levels.append(((ri // (2 * w)) == (ci // (2 * w))) & ((ri % (2 * w)) >= w) & ((ci % (2 * w)) < w))
        w *= 2
    nw = nw_ref[...]
    cw = [cw_ref[j:j + 1, :] for j in range(GDN_CONV)]
    a_neg = -jnp.exp(gp_ref[0:1, :])
    dtb = gp_ref[1:2, :]

    probs = []
    for b in range(seqs):
        tok = slice(b * rows, (b + 1) * rows)
        xbuf[b, 8:8 + rows, :] = qkv_ref[tok, :]
        conv = xbuf[b, 5:5 + rows, :] * cw[0]
        for j in range(1, GDN_CONV):
            conv = conv + xbuf[b, 5 + j:5 + j + rows, :] * cw[j]
        tail = xbuf[b, 8 + rows - 3:8 + rows, :]
        xbuf[b, 5:8, :] = tail
        qkv = _silu(conv)
        sm = sm_ref[tok, :]
        beta_all = _sigmoid(sm)
        g_all = a_neg * _softplus(sm + dtb)
        for c in range(n_chunks):
            rs = slice(c * chunk, (c + 1) * chunk)
            gcum = _cumsum_rows(tri3, g_all[rs])
            gcum_t = gcum.T
            for h in range(GDN_HEADS):
                q = qkv[rs, h * GDN_DK:(h + 1) * GDN_DK]
                k = qkv[rs, 512 + h * GDN_DK:512 + (h + 1) * GDN_DK]
                v = qkv[rs, 1024 + h * GDN_DK:1024 + (h + 1) * GDN_DK]
                q = q * lax.rsqrt(jnp.sum(q * q, axis=-1, keepdims=True) + EPS) * (GDN_DK ** -0.5)
                k = k * lax.rsqrt(jnp.sum(k * k, axis=-1, keepdims=True) + EPS)
                beta = beta_all[rs, h:h + 1]
                gc_col = gcum[:, 4 + h:5 + h]
                gc_row = gcum_t[4 + h:5 + h, :]
                g_last = gcum[chunk - 1:chunk, 4 + h:5 + h]
                decay = jnp.where(incl, jnp.exp(jnp.minimum(gc_col - gc_row, 0.0)), 0.0)
                eg = jnp.exp(gc_col)
                kb = k * beta
                k3r = _split3r(k, 1)
                probs.append(dict(
                    b=b, c=c, h=h, decay=decay,
                    kk=_dot_nt(_split3(kb, 1), k3r), qk=_dot_nt(_split3(q, 1), k3r),
                    rhs=jnp.concatenate([v * beta, kb * eg], axis=1), qd=q * eg,
                    kd_t=(k * jnp.exp(g_last - gc_col)).T, gl=jnp.exp(g_last), minv=eye))
    for p in probs:
        p["lmat"] = jnp.where(strict, p["kk"] * p["decay"], 0.0)
        p["qk"] = jnp.where(incl, p["qk"] * p["decay"], 0.0)
    for off_blk in levels:
        for p in probs:
            p["t"] = _dot3(p["minv"], jnp.where(off_blk, p["lmat"], 0.0))
        for p in probs:
            p["minv"] = p["minv"] - _dot3(p["t"], p["minv"])
    for p in probs:
        p["uw"] = _dot3(p["minv"], p["rhs"])
    state = {(b, h): s_scr[b, h] for b in range(seqs) for h in range(GDN_HEADS)}
    for c in range(n_chunks):
        cur = [p for p in probs if p["c"] == c]
        for p in cur:
            p["s3r"] = _split3r(state[(p["b"], p["h"])], 0)
            p["v_new"] = p["uw"][:, :GDN_DK] - _dot(_split3(p["uw"][:, GDN_DK:], 1), p["s3r"])
        for p in cur:
            p["o"] = _dot(_split3(p["qd"], 1), p["s3r"]) + _dot3(p["qk"], p["v_new"])
            state[(p["b"], p["h"])] = state[(p["b"], p["h"])] * p["gl"] + _dot3(p["kd_t"], p["v_new"])
    for p in probs:
        b, c, h = p["b"], p["c"], p["h"]
        r0 = b * rows + c * chunk
        o = _rms(p["o"], nw) * _silu(gate_ref[r0:r0 + chunk, h * GDN_DK:(h + 1) * GDN_DK])
        o_ref[r0:r0 + chunk, h * GDN_DK:(h + 1) * GDN_DK] = o
    for (b, h), s in state.items():
        s_scr[b, h] = s

    @pl.when(i == pl.num_programs(1) - 1)
    def _():
        sout_ref[...] = s_scr[...]


def gdn(qkv, small, gate, conv_w, gpar, norm_w, conv_buf8, s0, row_off, n_seq, t_seq, seqs, rows, chunk):
    nb = t_seq // rows
    blk = seqs * rows
    off = row_off // blk
    rmap = lambda n, i: (off + n * nb + i, 0)
    return pl.pallas_call(
        functools.partial(_gdn_kernel, seqs=seqs, rows=rows, chunk=chunk),
        grid=(n_seq // seqs, nb),
        in_specs=[pl.BlockSpec((blk, GDN_QKV), rmap),
                  pl.BlockSpec((blk, 128), rmap),
                  pl.BlockSpec((blk, 512), rmap),
                  pl.BlockSpec((GDN_CONV, GDN_QKV), lambda n, i: (0, 0)),
                  pl.BlockSpec((8, 128), lambda n, i: (0, 0)),
                  pl.BlockSpec((1, GDN_DK), lambda n, i: (0, 0)),
                  pl.BlockSpec((seqs, 8, GDN_QKV), lambda n, i: (n, 0, 0)),
                  pl.BlockSpec((seqs, GDN_HEADS, GDN_DK, GDN_DK), lambda n, i: (n, 0, 0, 0))],
        out_specs=[pl.BlockSpec((blk, 512), lambda n, i: (n * nb + i, 0)),
                   pl.BlockSpec((seqs, GDN_HEADS, GDN_DK, GDN_DK), lambda n, i: (n, 0, 0, 0))],
        out_shape=[jax.ShapeDtypeStruct((n_seq * t_seq, 512), F32),
                   jax.ShapeDtypeStruct((n_seq, GDN_HEADS, GDN_DK, GDN_DK), F32)],
        scratch_shapes=[pltpu.VMEM((seqs, 8 + rows, GDN_QKV), F32),
                        pltpu.VMEM((seqs, GDN_HEADS, GDN_DK, GDN_DK), F32)],
        compiler_params=_cp("parallel", "arbitrary"),
    )(qkv, small, gate, conv_w, gpar, norm_w, conv_buf8, s0)


def _compress_core(xj, posab_ref, w1_ref, w2_ref, m):
    acc = jnp.zeros((m + 8, 4 * CMP_HIDDEN), F32)
    for jj in range(CMP_STRIDE // 2):
        x2 = jnp.concatenate([xj(2 * jj), xj(2 * jj + 1)], axis=1)
        lhs = jnp.concatenate([x2, posab_ref[jj]], axis=0).astype(BF16)
        acc = acc + _dot(lhs, w1_ref[jj])
    top = acc[0:m, 0:2 * CMP_HIDDEN]
    bot = acc[0:m, 2 * CMP_HIDDEN:]
    c = acc[m:m + 1, 0:2 * CMP_HIDDEN] + acc[m + 1:m + 2, 2 * CMP_HIDDEN:]
    h = top + pltpu.roll(bot, m - 1, 0) + c
    out = _dot(_silu(h).astype(BF16), w2_ref[...])
    row = lax.broadcasted_iota(jnp.int32, (m, 2 * NSA_HD), 0)
    return jnp.where(row < m - 1, out, 0.0)


def _compress_prompt_kernel(kc_ref, vc_ref, pk_ref, w1k_ref, w2k_ref, pv_ref, w1v_ref, w2v_ref, ok_ref, ov_ref, *, m):
    ok_ref[0] = _compress_core(lambda j: kc_ref[pl.ds(j, m, stride=CMP_STRIDE), :], pk_ref, w1k_ref, w2k_ref, m)
    ov_ref[0] = _compress_core(lambda j: vc_ref[pl.ds(j, m, stride=CMP_STRIDE), :], pv_ref, w1v_ref, w2v_ref, m)


def _cmp_weight_specs(nmap):
    return [pl.BlockSpec((CMP_STRIDE // 2, 8, 256), nmap(3)),
            pl.BlockSpec((CMP_STRIDE // 2, 256, 4 * CMP_HIDDEN), nmap(3)),
            pl.BlockSpec((2 * CMP_HIDDEN, 2 * NSA_HD), nmap(2))]


def compress_prompt(nkv, cw, n_seq, t_seq):
    m = t_seq // CMP_STRIDE
    zmap = lambda nd: (lambda n: (0,) * nd)
    out = jax.ShapeDtypeStruct((n_seq, m, 2 * NSA_HD), F32)
    return pl.pallas_call(
        functools.partial(_compress_prompt_kernel, m=m),
        grid=(n_seq,),
        in_specs=[pl.BlockSpec((t_seq, 128), lambda n: (n, 4)), pl.BlockSpec((t_seq, 128), lambda n: (n, 5))]
        + _cmp_weight_specs(zmap) + _cmp_weight_specs(zmap),
        out_specs=[pl.BlockSpec((1, m, 2 * NSA_HD), lambda n: (n, 0, 0))] * 2,
        out_shape=[out, out],
        compiler_params=_cp("parallel"),
    )(nkv, nkv, *cw)


def _compress_sample_kernel(pt_ref, *refs, m):
    del pt_ref
    kpages = refs[0:N_PAGES]
    vpages = refs[N_PAGES:2 * N_PAGES]
    pk_ref, w1k_ref, w2k_ref, pv_ref, w1v_ref, w2v_ref, ok_ref, ov_ref = refs[2 * N_PAGES:]
    per_page = PAGE_SIZE // CMP_STRIDE

    def rows_of(pages):
        return lambda j: jnp.concatenate([p[pl.ds(j, per_page, stride=CMP_STRIDE), :] for p in pages], axis=0)

    ok_ref[0] = _compress_core(rows_of(kpages), pk_ref, w1k_ref, w2k_ref, m)
    ov_ref[0] = _compress_core(rows_of(vpages), pv_ref, w1v_ref, w2v_ref, m)


def _page_specs(layer):
    return [pl.BlockSpec((None, None, PAGE_SIZE, 128), functools.partial(lambda n, pt, p: (layer, pt[n, p], 0, 0), p=p))
            for p in range(N_PAGES)]


def compress_sample(page_table, cache_k, cache_v, layer, cw):
    m = PAST_LEN // CMP_STRIDE
    zmap = lambda nd: (lambda n, pt: (0,) * nd)
    out = jax.ShapeDtypeStruct((DEC_BATCH, m, 2 * NSA_HD), F32)
    return pl.pallas_call(
        functools.partial(_compress_sample_kernel, m=m),
        grid_spec=pltpu.PrefetchScalarGridSpec(
            num_scalar_prefetch=1, grid=(DEC_BATCH,),
            in_specs=_page_specs(layer) + _page_specs(layer) + _cmp_weight_specs(zmap) + _cmp_weight_specs(zmap),
            out_specs=[pl.BlockSpec((1, m, 2 * NSA_HD), lambda n, pt: (n, 0, 0))] * 2),
        out_shape=[out, out],
        compiler_params=_cp("arbitrary"),
    )(page_table, *([cache_k] * N_PAGES), *([cache_v] * N_PAGES), *cw)


def _masked_softmax(s, mask):
    s = jnp.where(mask, s, NEG)
    m = jnp.max(s, axis=-1, keepdims=True)
    p = jnp.where(mask, jnp.exp(s - m), 0.0)
    l = jnp.sum(p, axis=-1, keepdims=True)
    return p / jnp.where(l > 0.0, l, 1.0)


def _stack_heads(nq, g, tq):
    parts = [nq[:, (g * NSA_REP + r) * NSA_HD:(g * NSA_REP + r + 1) * NSA_HD] for r in range(NSA_REP)]
    return (jnp.concatenate(parts, axis=0) * (NSA_HD ** -0.5)).astype(BF16)


def _rep(mask, tq):
    return jnp.concatenate([mask] * NSA_REP, axis=0)


def _cmp_branch(q4, kcmp, vcmp, qpos, tq, n_blk):
    n_cmp = kcmp.shape[0]
    s = _dot_nt(q4, kcmp.astype(BF16))
    cmp_end = lax.broadcasted_iota(jnp.int32, (tq, n_cmp), 1) * CMP_STRIDE + (CMP_LEN - 1)
    p = _masked_softmax(s, _rep(cmp_end <= qpos, tq))
    o = _dot(p.astype(BF16), vcmp.astype(BF16))
    return o, _importance_t(p, tq, n_blk)


def _importance_t(p, tq, n_blk):
    n_cmp = p.shape[1]
    psum = p[0:tq] + p[tq:2 * tq] + p[2 * tq:3 * tq] + p[3 * tq:4 * tq]
    sj = lax.broadcasted_iota(jnp.int32, (n_blk, n_cmp), 0) * SEL_BLOCK
    ci = lax.broadcasted_iota(jnp.int32, (n_blk, n_cmp), 1) * CMP_STRIDE
    ov = jnp.clip(jnp.minimum(ci + CMP_LEN, sj + SEL_BLOCK) - jnp.maximum(ci, sj), 0, CMP_LEN).astype(F32) / CMP_LEN
    ov = ov.astype(BF16)
    p1 = psum.astype(BF16).astype(F32)
    p2 = (psum - p1).astype(BF16).astype(F32)
    p3 = psum - p1 - p2
    return _dot_nt(jnp.concatenate([ov, ov, ov], axis=1), jnp.concatenate([p1, p2, p3], axis=1).astype(BF16))


def _select_blocks(imp_ts, qpos_row, tq, n_blk):
    blk = lax.broadcasted_iota(jnp.int32, (n_blk, tq), 0)
    cur = qpos_row // SEL_BLOCK
    valid = blk <= cur
    forced = valid & ((blk == 0) | (blk >= cur - 1))
    works = [jnp.where(valid, jnp.where(forced, SEL_FORCED, imp_t), NEG) for imp_t in imp_ts]
    sels = [jnp.zeros((n_blk, tq), F32) for _ in imp_ts]
    for _ in range(SEL_TOPN):
        for j in range(len(works)):
            m = jnp.max(works[j], axis=0, keepdims=True)
            idx = jnp.min(jnp.where(works[j] == m, blk, n_blk), axis=0, keepdims=True)
            pick = blk == idx
            sels[j] = jnp.where(pick, 1.0, sels[j])
            works[j] = jnp.where(pick, -jnp.inf, works[j])
    return [s.T for s in sels]


def _expand_sel(sel, first_blk, n_keys):
    n_blk = sel.shape[1]
    bj = lax.broadcasted_iota(jnp.int32, (n_blk, n_keys), 0)
    kb = lax.broadcasted_iota(jnp.int32, (n_blk, n_keys), 1) // SEL_BLOCK + first_blk
    e = jnp.where(bj == kb, 1.0, 0.0).astype(BF16)
    return _dot(sel.astype(BF16), e) > 0.5


def _gate_mix(o_ref, gates, g, o_cmp, o_slc, o_win, tq):
    for r in range(NSA_REP):
        h = g * NSA_REP + r
        rs = slice(r * tq, (r + 1) * tq)
        c0 = 8 + 3 * h
        o = (gates[:, c0:c0 + 1] * o_cmp[rs] + gates[:, c0 + 1:c0 + 2] * o_slc[rs] + gates[:, c0 + 2:c0 + 3] * o_win[rs])
        o_ref[:, h * NSA_HD:(h + 1) * NSA_HD] = o


def _nsa_prompt_kernel(nq_ref, sm_ref, kv_ref, et_ref, kcmp_ref, vcmp_ref, o_ref):
    tq = Q_BLOCK
    i = pl.program_id(1)
    qpos = i * tq + lax.broadcasted_iota(jnp.int32, (tq, 1), 0)
    nq = nq_ref[...]
    gates = _sigmoid(sm_ref[...])
    kc = 512
    qpos_row = i * tq + lax.broadcasted_iota(jnp.int32, (1, tq), 1)
    q4s, o_cmps, imp_ts = [], [], []
    for g in range(NSA_KV):
        gs = slice(g * NSA_HD, (g + 1) * NSA_HD)
        q4s.append(_stack_heads(nq, g, tq))
        o_cmp, imp_t = _cmp_branch(q4s[g], kcmp_ref[0, :, gs], vcmp_ref[0, :, gs], qpos, tq, 128)
        o_cmps.append(o_cmp)
        imp_ts.append(imp_t)
    sels = _select_blocks(imp_ts, qpos_row, tq, 128)
    qas = [jnp.concatenate([_rep(jnp.where(sels[g] > 0.5, 0.0, -SEL_MASK), tq).astype(BF16), q4s[g]], axis=1)
           for g in range(NSA_KV)]

    def slc_step(c, carry, diagonal):
        start = pl.multiple_of(c * kc, kc)
        et = et_ref[pl.ds(start, kc), :]
        ss = [_dot_nt(qas[g], jnp.concatenate([et, kv_ref[pl.ds(start, kc), g * NSA_HD:(g + 1) * NSA_HD]], axis=1))
              for g in range(NSA_KV)]
        if diagonal:
            causal = _rep(start + lax.broadcasted_iota(jnp.int32, (tq, kc), 1) <= qpos, tq)
            ss = [jnp.where(causal, s, -SEL_MASK) for s in ss]
        out = []
        for g in range(NSA_KV):
            m_i, l_i, acc = carry[g]
            m_new = jnp.maximum(m_i, jnp.max(ss[g], axis=-1, keepdims=True))
            alpha = jnp.exp(m_i - m_new)
            p = jnp.exp(ss[g] - m_new)
            l_new = alpha * l_i + jnp.sum(p, axis=-1, keepdims=True)
            v = kv_ref[pl.ds(start, kc), 128 + g * NSA_HD:128 + (g + 1) * NSA_HD]
            out.append((m_new, l_new, alpha * acc + _dot(p.astype(BF16), v)))
        return tuple(out)

    init = tuple((jnp.full((NSA_REP * tq, 1), -3.0e38, F32), jnp.zeros((NSA_REP * tq, 1), F32),
                  jnp.zeros((NSA_REP * tq, NSA_HD), F32)) for _ in range(NSA_KV))
    n_full = i // (kc // tq)
    carry = lax.fori_loop(0, n_full, functools.partial(slc_step, diagonal=False), init)
    fin = slc_step(n_full, carry, True)

    for g in range(NSA_KV):
        q4, o_cmp = q4s[g], o_cmps[g]
        _, l_f, acc = fin[g]
        o_slc = acc / jnp.where(l_f > 0.0, l_f, 1.0)

        nband = WINDOW + tq
        wstart = pl.multiple_of(jnp.maximum(i - WINDOW // tq, 0) * tq, tq)
        kw = kv_ref[pl.ds(wstart, nband), 256 + g * NSA_HD:256 + (g + 1) * NSA_HD]
        vw = kv_ref[pl.ds(wstart, nband), 384 + g * NSA_HD:384 + (g + 1) * NSA_HD]
        d = qpos - (wstart + lax.broadcasted_iota(jnp.int32, (tq, nband), 1))
        pw = _masked_softmax(_dot_nt(q4, kw), _rep((d >= 0) & (d <= WINDOW), tq))
        o_win = _dot(pw.astype(BF16), vw)
        _gate_mix(o_ref, gates, g, o_cmp, o_slc, o_win, tq)


def nsa_prompt(nq, small, kv_bf, kcmp, vcmp, n_seq, t_seq):
    nb = t_seq // Q_BLOCK
    m = kcmp.shape[1]
    key_blk = jnp.arange(t_seq, dtype=jnp.int32)[:, None] // SEL_BLOCK
    et = (key_blk == jnp.arange(128, dtype=jnp.int32)[None, :]).astype(BF16)
    return pl.pallas_call(
        _nsa_prompt_kernel,
        grid=(n_seq, nb),
        in_specs=[pl.BlockSpec((Q_BLOCK, 512), lambda n, i: (n * nb + i, 0)),
                  pl.BlockSpec((Q_BLOCK, 128), lambda n, i: (n * nb + i, 0)),
                  pl.BlockSpec((t_seq, 512), lambda n, i: (n, 0)),
                  pl.BlockSpec((t_seq, 128), lambda n, i: (0, 0)),
                  pl.BlockSpec((1, m, 128), lambda n, i: (n, 0, 0)),
                  pl.BlockSpec((1, m, 128), lambda n, i: (n, 0, 0))],
        out_specs=pl.BlockSpec((Q_BLOCK, 512), lambda n, i: (n * nb + i, 0)),
        out_shape=jax.ShapeDtypeStruct((n_seq * t_seq, 512), F32),
        compiler_params=_cp("parallel", "arbitrary"),
    )(nq, small, kv_bf, et, kcmp, vcmp)


def _nsa_sample_kernel(pt_ref, *refs):
    del pt_ref
    kpages = refs[0:N_PAGES]
    vpages = refs[N_PAGES:2 * N_PAGES]
    nq_ref, sm_ref, kvn_ref, kcmp_ref, vcmp_ref, wk_ref, wv_ref, o_ref = refs[2 * N_PAGES:]
    tq = DEC_SEQ
    qpos = PAST_LEN + lax.broadcasted_iota(jnp.int32, (tq, 1), 0)
    nq = nq_ref[...]
    gates = _sigmoid(sm_ref[...])
    kvn = kvn_ref[...]
    pad = jnp.zeros((PAGE_SIZE - tq, 128), F32)
    new = [jnp.concatenate([kvn[:, c * 128:(c + 1) * 128], pad], axis=0) for c in range(4)]
    n_chunks = N_PAGES + 1
    n_keys = n_chunks * PAGE_SIZE
    n_blk = 40
    nr = NSA_REP * tq
    qpos_row = PAST_LEN + lax.broadcasted_iota(jnp.int32, (1, tq), 1)
    z = jnp.zeros((nr, NSA_HD), BF16)
    q_bd = jnp.concatenate([jnp.concatenate([_stack_heads(nq, 0, tq), z], axis=1),
                            jnp.concatenate([z, _stack_heads(nq, 1, tq)], axis=1)], axis=0)
    both = lambda m: jnp.concatenate([_rep(m, tq)] * NSA_KV, axis=0)

    n_cmp = kcmp_ref.shape[1]
    s_c = _dot_nt(q_bd, kcmp_ref[0].astype(BF16))
    s = jnp.concatenate([_dot(q_bd, p[...].astype(BF16)) for p in kpages]
                        + [_dot_nt(q_bd, new[0].astype(BF16))], axis=1)
    sw = jnp.concatenate([_dot(q_bd, wk_ref[...].astype(BF16)), _dot_nt(q_bd, new[2].astype(BF16))], axis=1)

    cmp_end = lax.broadcasted_iota(jnp.int32, (tq, n_cmp), 1) * CMP_STRIDE + (CMP_LEN - 1)
    p_c = _masked_softmax(s_c, both(cmp_end <= qpos))
    o_c = _dot(p_c.astype(BF16), vcmp_ref[0].astype(BF16))
    d = qpos - (PAST_LEN - WINDOW + lax.broadcasted_iota(jnp.int32, (tq, WINDOW + PAGE_SIZE), 1))
    pw = _masked_softmax(sw, both((d >= 0) & (d <= WINDOW))).astype(BF16)
    o_w = _dot_nt(pw[:, :WINDOW], wv_ref[...].astype(BF16)) + _dot(pw[:, WINDOW:], new[3].astype(BF16))
    sels = _select_blocks([_importance_t(p_c[g * nr:(g + 1) * nr], tq, n_blk) for g in range(NSA_KV)],
                          qpos_row, tq, n_blk)

    kpos = lax.broadcasted_iota(jnp.int32, (tq, n_keys), 1)
    causal = kpos <= qpos
    msk = jnp.concatenate([_rep(_expand_sel(sel, 0, n_keys) & causal, tq) for sel in sels], axis=0)
    p = _masked_softmax(s, msk).astype(BF16)
    o_s = _dot(p[:, N_PAGES * PAGE_SIZE:], new[1].astype(BF16))
    for c in range(N_PAGES):
        o_s = o_s + _dot_nt(p[:, c * PAGE_SIZE:(c + 1) * PAGE_SIZE], vpages[c][...].astype(BF16))
    for g in range(NSA_KV):
        rs, gs = slice(g * nr, (g + 1) * nr), slice(g * NSA_HD, (g + 1) * NSA_HD)
        _gate_mix(o_ref, gates, g, o_c[rs, gs], o_s[rs, gs], o_w[rs, gs], tq)


def nsa_sample(page_table, cache_k, cache_v, layer, nq, small, nkv, kcmp, vcmp, win_k, win_v):
    off = RP // DEC_SEQ
    rmap = lambda n, pt: (off + n, 0)
    m = kcmp.shape[1]
    return pl.pallas_call(
        _nsa_sample_kernel,
        grid_spec=pltpu.PrefetchScalarGridSpec(
            num_scalar_prefetch=1, grid=(DEC_BATCH,),
            in_specs=_page_specs(layer) + _page_specs(layer) + [
                pl.BlockSpec((DEC_SEQ, 512), rmap),
                pl.BlockSpec((DEC_SEQ, 128), rmap),
                pl.BlockSpec((DEC_SEQ, 768), rmap),
                pl.BlockSpec((1, m, 128), lambda n, pt: (n, 0, 0)),
                pl.BlockSpec((1, m, 128), lambda n, pt: (n, 0, 0)),
                pl.BlockSpec((None, None, 128, WINDOW), lambda n, pt: (layer, n, 0, 0)),
                pl.BlockSpec((None, None, 128, WINDOW), lambda n, pt: (layer, n, 0, 0))],
            out_specs=pl.BlockSpec((DEC_SEQ, 512), lambda n, pt: (n, 0))),
        out_shape=jax.ShapeDtypeStruct((RS, 512), F32),
        compiler_params=_cp("arbitrary"),
    )(page_table, *([cache_k] * N_PAGES), *([cache_v] * N_PAGES), nq, small, nkv, kcmp, vcmp, win_k, win_v)


S5_TILE = 256
S5_LANES = 2048
S5_BLK = 16


def _gelu_tanh(x):
    return 0.5 * x * (1.0 + jnp.tanh(math.sqrt(2.0 / math.pi) * (x + 0.044715 * (x * x * x))))


def _s5_kernel(u_ref, pm_ref, pt3_ref, bdr_ref, bdi_ref, cr_ref, ci_ref, tab_ref, d_ref, x0r_ref, x0i_ref,
               y_ref, fr_ref, fi_ref, xr_scr, xi_scr, car_scr, *, chained):
    i = pl.program_id(1)
    nb = S5_GROUPS // S5_BLK
    wi = S5_BLK * S5_CH
    ws = S5_BLK * S5_P
    seg = S5_TILE // 8
    u = u_ref[...]
    ub = _dot(pm_ref[...], u.astype(BF16)).astype(BF16)
    for b in range(nb):
        xr_scr[:, b * ws:(b + 1) * ws] = _dot(ub[:, b * wi:(b + 1) * wi], bdr_ref[b])
        xi_scr[:, b * ws:(b + 1) * ws] = _dot(ub[:, b * wi:(b + 1) * wi], bdi_ref[b])

    if chained:
        @pl.when(i == 0)
        def _():
            car_scr[0:1, :] = x0r_ref[0]
            car_scr[1:2, :] = x0i_ref[0]

    def cmad(br, bi, mr, mi, xr, xi):
        return br + mr * xr - mi * xi, bi + mr * xi + mi * xr

    for c in range(S5_STATE // S5_LANES):
        ls = slice(c * S5_LANES, (c + 1) * S5_LANES)
        ar, ai = tab_ref[0, 0:1, ls], tab_ref[1, 0:1, ls]
        if chained:
            xr, xi = jnp.zeros((8, S5_LANES), F32), jnp.zeros((8, S5_LANES), F32)
            for k in range(seg):
                rs = slice(8 * k, 8 * k + 8)
                xr, xi = cmad(xr_scr[rs, ls], xi_scr[rs, ls], ar, ai, xr, xi)
                xr_scr[rs, ls] = xr
                xi_scr[rs, ls] = xi
            yr, yi = xr, xi
            for sh, r0 in ((1, seg), (2, seg + 8), (4, seg + 16)):
                yr, yi = cmad(yr, yi, tab_ref[0, r0:r0 + 8, ls], tab_ref[1, r0:r0 + 8, ls],
                              pltpu.roll(yr, sh, 0), pltpu.roll(yi, sh, 0))
            first = lax.broadcasted_iota(jnp.int32, (8, S5_LANES), 0) == 0
            yr = jnp.where(first, 0.0, pltpu.roll(yr, 1, 0))
            yi = jnp.where(first, 0.0, pltpu.roll(yi, 1, 0))
            sr, si = cmad(yr, yi, tab_ref[0, seg + 24:seg + 32, ls], tab_ref[1, seg + 24:seg + 32, ls],
                          car_scr[0:1, ls], car_scr[1:2, ls])
            for k in range(seg):
                rs = slice(8 * k, 8 * k + 8)
                xr, xi = cmad(xr_scr[rs, ls], xi_scr[rs, ls], tab_ref[0, k:k + 1, ls], tab_ref[1, k:k + 1, ls], sr, si)
                xr_scr[rs, ls] = xr
                xi_scr[rs, ls] = xi
            car_scr[0:1, ls] = xr[7:8, :]
            car_scr[1:2, ls] = xi[7:8, :]
        else:
            for q in range(S5_TILE // 64):
                xr, xi = x0r_ref[0, 8 * q:8 * q + 8, ls], x0i_ref[0, 8 * q:8 * q + 8, ls]
                for t in range(8):
                    rs = slice(8 * (8 * q + t), 8 * (8 * q + t) + 8)
                    xr, xi = cmad(xr_scr[rs, ls], xi_scr[rs, ls], ar, ai, xr, xi)
                    xr_scr[rs, ls] = xr
                    xi_scr[rs, ls] = xi
                fr_ref[0, 8 * q:8 * q + 8, ls] = xr
                fi_ref[0, 8 * q:8 * q + 8, ls] = xi

    ys = []
    for b in range(nb):
        ys.append(_dot(xr_scr[:, b * ws:(b + 1) * ws].astype(BF16), cr_ref[b])
                  - _dot(xi_scr[:, b * ws:(b + 1) * ws].astype(BF16), ci_ref[b]))
    yp = jnp.concatenate(ys, axis=1)
    h1 = yp.astype(BF16).astype(F32)
    h2 = (yp - h1).astype(BF16).astype(F32)
    ylin = _dot(pt3_ref[...], jnp.concatenate([h1, h2, yp - h1 - h2], axis=0).astype(BF16))
    y_ref[...] = _gelu_tanh(ylin + d_ref[...] * u).astype(BF16)

    if chained:
        @pl.when(i == pl.num_programs(1) - 1)
        def _():
            fr_ref[0] = car_scr[0:1, :]
            fi_ref[0] = car_scr[1:2, :]


def s5_scan(u, sp, x0r, x0i, row_off, n_seq, t_seq, chained):
    if chained:
        grid = (n_seq, t_seq // S5_TILE)
        nb = grid[1]
        smap = lambda n, i: (n, 0, 0)
    else:
        grid = (1, n_seq * t_seq // S5_TILE)
        nb = grid[1]
        smap = lambda n, i: (i, 0, 0)
    off = row_off // S5_TILE
    rmap = lambda n, i: (off + n * nb + i, 0)
    sblk = (1,) + x0r.shape[1:]
    const = lambda nd: (lambda n, i: (0,) * nd)
    rho = jnp.arange(S5_TILE)
    if chained:
        src = (S5_TILE // 8) * (rho % 8) + rho // 8
    else:
        src = 64 * (rho // 64) + 8 * (rho % 8) + (rho // 8) % 8
    pm = (src[:, None] == jnp.arange(S5_TILE)[None, :]).astype(BF16)
    pt3 = jnp.tile(pm.T, (1, 3))
    in_specs = [pl.BlockSpec((S5_TILE, D_MODEL), rmap),
                pl.BlockSpec((S5_TILE, S5_TILE), const(2)), pl.BlockSpec((S5_TILE, 3 * S5_TILE), const(2)),
                pl.BlockSpec(sp["bdr"].shape, const(3)), pl.BlockSpec(sp["bdi"].shape, const(3)),
                pl.BlockSpec(sp["cr"].shape, const(3)), pl.BlockSpec(sp["ci"].shape, const(3)),
                pl.BlockSpec(sp["tab"].shape, const(3)), pl.BlockSpec((1, D_MODEL), const(2)),
                pl.BlockSpec(sblk, smap), pl.BlockSpec(sblk, smap)]
    args = [u, pm, pt3, sp["bdr"], sp["bdi"], sp["cr"], sp["ci"], sp["tab"], sp["d"], x0r, x0i]
    return pl.pallas_call(
        functools.partial(_s5_kernel, chained=chained),
        grid=grid,
        in_specs=in_specs,
        out_specs=[pl.BlockSpec((S5_TILE, D_MODEL), lambda n, i: (n * nb + i, 0)),
                   pl.BlockSpec(sblk, smap), pl.BlockSpec(sblk, smap)],
        out_shape=[jax.ShapeDtypeStruct((n_seq * t_seq, D_MODEL), BF16), jax.ShapeDtypeStruct(x0r.shape, F32),
                   jax.ShapeDtypeStruct(x0r.shape, F32)],
        scratch_shapes=[pltpu.VMEM((S5_TILE, S5_STATE), F32), pltpu.VMEM((S5_TILE, S5_STATE), F32),
                        pltpu.VMEM((8, S5_STATE), F32)],
        compiler_params=_cp("arbitrary", "arbitrary"),
    )(*args)


def _prep_ab_in(w):
    qkv, b, a, gate, nq, nkv, ngate = jnp.split(w, (1536, 1540, 1544, 2056, 2568, 3336), axis=-1)
    kc, vc, ks, vs, kw, vw = jnp.split(nkv, 6, axis=-1)
    small = jnp.concatenate([b, a, ngate, jnp.zeros(w.shape[:-1] + (96,), w.dtype)], axis=-1)
    return jnp.concatenate([qkv, ks, vs, kw, vw, kc, vc, gate, nq, small], axis=-1).astype(BF16)


AB_SPLITS = (1536, 768, 512, 512, 128)


def _prep_cmp(pos, w1, w2):
    top = w1[:CMP_STRIDE * NSA_HD].reshape(CMP_STRIDE, NSA_HD, CMP_HIDDEN)
    bot = w1[CMP_STRIDE * NSA_HD:].reshape(CMP_STRIDE, NSA_HD, CMP_HIDDEN)
    w1bd = jnp.zeros((CMP_STRIDE, 2 * NSA_HD, 4 * CMP_HIDDEN), F32)
    w2bd = jnp.zeros((2 * CMP_HIDDEN, 2 * NSA_HD), F32)
    for g in range(NSA_KV):
        rs = slice(g * NSA_HD, (g + 1) * NSA_HD)
        w1bd = w1bd.at[:, rs, g * CMP_HIDDEN:(g + 1) * CMP_HIDDEN].set(top)
        w1bd = w1bd.at[:, rs, (2 + g) * CMP_HIDDEN:(3 + g) * CMP_HIDDEN].set(bot)
        w2bd = w2bd.at[g * CMP_HIDDEN:(g + 1) * CMP_HIDDEN, rs].set(w2)
    posab = jnp.zeros((CMP_STRIDE, 8, 2 * NSA_HD), F32)
    posab = posab.at[:, 0, :].set(jnp.tile(pos[:CMP_STRIDE], (1, NSA_KV)))
    posab = posab.at[:, 1, :].set(jnp.tile(pos[CMP_STRIDE:], (1, NSA_KV)))
    half = CMP_STRIDE // 2
    posab = posab.reshape(half, 2, 8, 2 * NSA_HD).transpose(0, 2, 1, 3).reshape(half, 8, 4 * NSA_HD)
    w1bd = w1bd.reshape(half, 4 * NSA_HD, 4 * CMP_HIDDEN)
    return posab, w1bd.astype(BF16), w2bd.astype(BF16)


def _cmul(ar, ai, br, bi):
    return ar * br - ai * bi, ar * bi + ai * br


def _prep_s5(a_re, a_im, b_re, b_im, c_re, c_im, d, log_dt):
    dt = jnp.exp(log_dt)[:, None]
    lr = jnp.minimum(a_re, S5_MAX_RE)
    li = a_im
    mag = jnp.exp(lr * dt)
    ar = mag * jnp.cos(li * dt)
    ai = mag * jnp.sin(li * dt)
    den = lr * lr + li * li
    fr = ((ar - 1.0) * lr + ai * li) / den
    fi = (ai * lr - (ar - 1.0) * li) / den
    bbr = fr[..., None] * b_re - fi[..., None] * b_im
    bbi = fr[..., None] * b_im + fi[..., None] * b_re
    nb = S5_GROUPS // S5_BLK
    eye = jnp.eye(S5_BLK, dtype=F32)

    def bd_in(m):
        m4 = jnp.swapaxes(m, 1, 2).reshape(nb, S5_BLK, S5_CH, S5_P)
        return jnp.einsum('bgcp,gh->bgchp', m4, eye).reshape(nb, S5_BLK * S5_CH, S5_BLK * S5_P).astype(BF16)

    def bd_out(m):
        m4 = jnp.swapaxes(m, 1, 2).reshape(nb, S5_BLK, S5_P, S5_CH)
        return jnp.einsum('bgpc,gh->bgphc', m4, eye).reshape(nb, S5_BLK * S5_P, S5_BLK * S5_CH).astype(BF16)

    seg = S5_TILE // 8
    a1 = (ar.reshape(1, S5_STATE), ai.reshape(1, S5_STATE))
    pw = [a1]
    for _ in range(seg - 1):
        pw.append(_cmul(*pw[-1], *a1))
    row = jnp.arange(8)[:, None]
    s1 = pw[-1]
    s2 = _cmul(*s1, *s1)
    s4 = _cmul(*s2, *s2)
    sp = [(jnp.ones_like(a1[0]), jnp.zeros_like(a1[1]))]
    for _ in range(7):
        sp.append(_cmul(*sp[-1], *s1))
    parts = []
    for j in range(2):
        parts.append(jnp.concatenate([p[j] for p in pw]
                                     + [jnp.where(row >= sh, s[j], 0.0) for sh, s in ((1, s1), (2, s2), (4, s4))]
                                     + [p[j] for p in sp], axis=0))
    return {"bdr": bd_in(bbr), "bdi": bd_in(bbi), "cr": bd_out(c_re), "ci": bd_out(c_im),
            "tab": jnp.stack(parts), "d": d.reshape(1, D_MODEL)}


def _heads(a, n, t):
    return a.reshape(n, t, NSA_KV, NSA_HD)


def kernel(x_prompt, x_sample, mem_prompt, cache_mem_k, cache_mem_v, state_gdn, state_gdn_conv, cache_cmp_k, cache_cmp_v, cache_slc_k, cache_slc_v, cache_win_k, cache_win_v, state_s5_re, state_s5_im, page_table, norm_ffn1, w_ffn1_gate, w_ffn1_up, w_ffn1_down, norm_mix, norm_xq, norm_mem, w_xq, w_xk, w_xv, w_xo, norm_ffn2, w_ffn2_gate, w_ffn2_up, w_ffn2_down, norm_final, w_in_ab, w_out_ab, gdn_conv, gdn_a_log, gdn_dt_bias, gdn_norm, cmp_pos_k, cmp_w1_k, cmp_w2_k, cmp_pos_v, cmp_w1_v, cmp_w2_v, w_in_c, s5_a_re, s5_a_im, s5_b_re, s5_b_im, s5_c_re, s5_c_im, s5_d, s5_log_dt, w_glu, w_out_c):
    bf = lambda w: w.astype(BF16)
    n_ab = w_in_ab.shape[0]
    n_pool = cache_cmp_k.shape[1]
    x = jnp.concatenate([x_prompt.reshape(RP, D_MODEL), x_sample.reshape(RS, D_MODEL)], axis=0)

    memkv = mem_kv_all(mem_prompt.reshape(BATCH * N_MEM, D_MODEL), norm_mem, bf(jnp.concatenate([w_xk, w_xv], axis=-1)))
    mem_k_prompt = memkv[:, :, :D_MODEL].reshape(DEPTH, BATCH, N_MEM, X_HEADS, X_HD)
    mem_v_prompt = memkv[:, :, D_MODEL:].reshape(DEPTH, BATCH, N_MEM, X_HEADS, X_HD)

    paged = lambda c: c.reshape(n_ab, n_pool, PAGE_SIZE, NSA_KV * NSA_HD)
    cck, ccv = paged(cache_cmp_k), paged(cache_cmp_v)
    fmaj = lambda c: jnp.transpose(c, (0, 1, 3, 4, 2)).reshape(c.shape[0], c.shape[1], NSA_KV * NSA_HD, c.shape[2])
    csk, csv, cwk, cwv = fmaj(cache_slc_k), fmaj(cache_slc_v), fmaj(cache_win_k), fmaj(cache_win_v)
    ab_p, ab_s, c_p, c_s = [], [], [], []
    for l in range(DEPTH):
        i = l // 2
        x = glu_mlp(x, norm_ffn1[l], bf(w_ffn1_gate[l]), bf(w_ffn1_up[l]), bf(w_ffn1_down[l]), None, 1024, 256, True, True, 0.5)
        if l % 2 == 0:
            qkv, nkv, gate, nq, small = norm_matmul(x, norm_mix[l], _prep_ab_in(w_in_ab[i]), AB_SPLITS, 512)
            gpar = jnp.zeros((8, 128), F32).at[0, 4:8].set(gdn_a_log[i]).at[1, 4:8].set(gdn_dt_bias[i])
            nw = gdn_norm[i].reshape(1, GDN_DK)
            cb_s = jnp.zeros((DEC_BATCH, 8, GDN_QKV), F32).at[:, 5:8].set(state_gdn_conv[i])
            oa_p, st_p = gdn(qkv, small, gate, gdn_conv[i], gpar, nw, jnp.zeros((BATCH, 8, GDN_QKV), F32),
                             jnp.zeros((BATCH, GDN_HEADS, GDN_DK, GDN_DK), F32), 0, BATCH, SEQ, 1, 256, GDN_CHUNK)
            oa_s, st_s = gdn(qkv, small, gate, gdn_conv[i], gpar, nw, cb_s, state_gdn[i], RP, DEC_BATCH, DEC_SEQ,
                             8, DEC_SEQ, math.gcd(DEC_SEQ, GDN_CHUNK))
            cw = _prep_cmp(cmp_pos_k[i], cmp_w1_k[i], cmp_w2_k[i]) + _prep_cmp(cmp_pos_v[i], cmp_w1_v[i], cmp_w2_v[i])
            kcmp_p, vcmp_p = compress_prompt(nkv, cw, BATCH, SEQ)
            ob_p = nsa_prompt(nq, small, bf(nkv[:RP, :512]), kcmp_p, vcmp_p, BATCH, SEQ)
            kcmp_s, vcmp_s = compress_sample(page_table, cck, ccv, i, cw)
            ob_s = nsa_sample(page_table, csk, csv, i, nq, small, nkv, kcmp_s, vcmp_s, cwk, cwv)
            w_out = bf(w_out_ab[i])
            x = matmul_residual([(oa_p, oa_s, w_out[:512]), (ob_p, ob_s, w_out[512:])], x, 512)
            conv_p = jnp.stack([qkv[(n + 1) * SEQ - 3:(n + 1) * SEQ] for n in range(BATCH)])
            qkv_s = qkv[RP:].reshape(DEC_BATCH, DEC_SEQ, GDN_QKV)
            col = lambda a, c: a[:, c * 128:(c + 1) * 128]
            nkv_p, nkv_s = nkv[:RP], nkv[RP:]
            hp = lambda c: _heads(col(nkv_p, c), BATCH, SEQ)
            hs = lambda c: _heads(col(nkv_s, c), DEC_BATCH, DEC_SEQ)
            ab_p.append((conv_p, st_p, hp(4), hp(5), hp(0), hp(1), hp(2)[:, SEQ - WINDOW:], hp(3)[:, SEQ - WINDOW:]))
            ab_s.append((qkv_s[:, DEC_SEQ - 3:], st_s, hs(4), hs(5), hs(0), hs(1),
                         jnp.concatenate([cache_win_k[i][:, DEC_SEQ:], hs(2)], axis=1),
                         jnp.concatenate([cache_win_v[i][:, DEC_SEQ:], hs(3)], axis=1)))
        else:
            (u,) = norm_matmul(x, norm_mix[l], bf(w_in_c[i]), (D_MODEL,), 512)
            sp = _prep_s5(s5_a_re[i], s5_a_im[i], s5_b_re[i], s5_b_im[i], s5_c_re[i], s5_c_im[i], s5_d[i], s5_log_dt[i])
            z0 = jnp.zeros((BATCH, 1, S5_STATE), F32)
            y_p, fr_p, fi_p = s5_scan(u, sp, z0, z0, 0, BATCH, SEQ, True)
            per_tile = S5_TILE // DEC_SEQ
            x0r = state_s5_re[i].reshape(RS // S5_TILE, per_tile, S5_STATE)
            x0i = state_s5_im[i].reshape(RS // S5_TILE, per_tile, S5_STATE)
            y_s, fr_s, fi_s = s5_scan(u, sp, x0r, x0i, RP, DEC_BATCH, DEC_SEQ, False)
            wg = bf(w_glu[i])
            x = glu_mlp(y_p, norm_mix[l], wg[:, :D_MODEL], wg[:, D_MODEL:], bf(w_out_c[i]), x, 1024, 256, False, False, 1.0,
                        src_s=y_s)
            c_p.append((fr_p.reshape(BATCH, S5_GROUPS, S5_P), fi_p.reshape(BATCH, S5_GROUPS, S5_P)))
            c_s.append((fr_s.reshape(DEC_BATCH, S5_GROUPS, S5_P), fi_s.reshape(DEC_BATCH, S5_GROUPS, S5_P)))
        (q,) = norm_matmul(x, norm_xq[l], bf(w_xq[l]), (D_MODEL,), 512)
        o_p = cross_attention(q, memkv, memkv,
                              pl.BlockSpec((None, N_MEM, D_MODEL), functools.partial(lambda n, i, l: (l, n, 0), l=l)),
                              pl.BlockSpec((None, N_MEM, D_MODEL), functools.partial(lambda n, i, l: (l, n, 1), l=l)),
                              0, BATCH, SEQ, 512)
        o_s = cross_attention_cached(q, cache_mem_k, cache_mem_v, l, RP)
        x = matmul_residual([(o_p, o_s, bf(w_xo[l]))], x, 512)
        x = glu_mlp(x, norm_ffn2[l], bf(w_ffn2_gate[l]), bf(w_ffn2_up[l]), bf(w_ffn2_down[l]), None, 1024, 256, True, True, 0.5)

    y = rmsnorm_rows(x, norm_final, 512)
    st = lambda grp, j: jnp.stack([t[j] for t in grp])
    return (y[:RP].reshape(BATCH, SEQ, D_MODEL), y[RP:].reshape(DEC_BATCH, DEC_SEQ, D_MODEL),
            mem_k_prompt, mem_v_prompt,
            st(ab_p, 1), st(ab_s, 1), st(ab_p, 0), st(ab_s, 0),
            st(ab_p, 2), st(ab_p, 3), st(ab_p, 4), st(ab_p, 5),
            st(ab_s, 2), st(ab_s, 3), st(ab_s, 4), st(ab_s, 5),
            st(ab_p, 6), st(ab_p, 7), st(ab_s, 6), st(ab_s, 7),
            st(c_p, 0), st(c_p, 1), st(c_s, 0), st(c_s, 1))
```

```python
import functools
import math

import jax
import jax.numpy as jnp
from jax import lax
from jax.experimental import pallas as pl
from jax.experimental.pallas import tpu as pltpu

F32 = jnp.float32
BF16 = jnp.bfloat16
HIGHEST = lax.Precision.HIGHEST

D_MODEL = 1024
BATCH = 2
SEQ = 8192
DEPTH = 4
DEC_BATCH = 128
DEC_SEQ = 8
PAST_LEN = 2048
PAGE_SIZE = 128
N_PAGES = PAST_LEN // PAGE_SIZE
RP = BATCH * SEQ
RS = DEC_BATCH * DEC_SEQ
ROWS = RP + RS

GDN_HEADS = 4
GDN_DK = 128
GDN_QKV = 1536
GDN_CONV = 4
GDN_CHUNK = 64
NSA_HEADS = 8
NSA_KV = 2
NSA_HD = 64
NSA_REP = 4
CMP_STRIDE = 16
CMP_LEN = 32
CMP_HIDDEN = 128
SEL_BLOCK = 64
SEL_TOPN = 16
WINDOW = 512
Q_BLOCK = 128
SEL_FORCED = 1.0e4
NEG = -1.0e30
SEL_MASK = 2.0 ** 100
S5_CH = 16
S5_GROUPS = 64
S5_P = 64
S5_STATE = S5_GROUPS * S5_P
S5_MAX_RE = -1.0e-4
N_MEM = 256
X_HEADS = 4
X_HD = 256
D_FF = 2816
EPS = 1.0e-6

VMEM_LIMIT = 56 * 1024 * 1024


def _cp(*sem):
    return pltpu.CompilerParams(dimension_semantics=sem, vmem_limit_bytes=VMEM_LIMIT)


def _dot(a, b, precision=None):
    return jnp.dot(a, b, preferred_element_type=F32, precision=precision)


def _dot_nt(a, b, precision=None):
    return lax.dot_general(a, b, (((1,), (1,)), ((), ())), preferred_element_type=F32, precision=precision)


def _sigmoid(x):
    return 1.0 / (1.0 + jnp.exp(-x))


def _silu(x):
    return x * _sigmoid(x)


def _rms(x, g):
    return x * lax.rsqrt(jnp.mean(x * x, axis=-1, keepdims=True) + EPS) * g


def _norm_matmul_kernel(x_ref, g_ref, w_ref, *o_refs, norm, splits):
    x = x_ref[...]
    if norm:
        x = _rms(x, g_ref[...])
    h = x.astype(BF16)
    off = 0
    for o_ref, wd in zip(o_refs, splits):
        o_ref[...] = _dot(h, w_ref[:, off:off + wd])
        off += wd


def norm_matmul(x, g, w, splits, tm, norm=True):
    rows, k = x.shape
    n = w.shape[1]
    assert sum(splits) == n and rows % tm == 0
    outs = pl.pallas_call(
        functools.partial(_norm_matmul_kernel, norm=norm, splits=tuple(splits)),
        grid=(rows // tm,),
        in_specs=[pl.BlockSpec((tm, k), lambda i: (i, 0)),
                  pl.BlockSpec((1, k), lambda i: (0, 0)),
                  pl.BlockSpec((k, n), lambda i: (0, 0))],
        out_specs=[pl.BlockSpec((tm, wd), lambda i: (i, 0)) for wd in splits],
        out_shape=[jax.ShapeDtypeStruct((rows, wd), F32) for wd in splits],
        compiler_params=_cp("parallel"),
    )(x, g.reshape(1, k), w)
    return outs


def _mem_kv_kernel(x_ref, g_ref, w_ref, o_ref):
    h = _rms(x_ref[...], g_ref[0]).astype(BF16)
    o_ref[0] = _dot(h, w_ref[0])


def mem_kv_all(mem2d, g, w):
    m, k = mem2d.shape
    nl, _, n = w.shape
    return pl.pallas_call(
        _mem_kv_kernel,
        grid=(nl,),
        in_specs=[pl.BlockSpec((m, k), lambda l: (0, 0)),
                  pl.BlockSpec((1, 1, k), lambda l: (l, 0, 0)),
                  pl.BlockSpec((1, k, n), lambda l: (l, 0, 0))],
        out_specs=pl.BlockSpec((1, m, n), lambda l: (l, 0, 0)),
        out_shape=jax.ShapeDtypeStruct((nl, m, n), F32),
        compiler_params=_cp("parallel"),
    )(mem2d, g.reshape(nl, 1, k), w)


def _matmul_res_kernel(*refs, n_terms, n_p):
    r_ref, o_ref = refs[3 * n_terms], refs[3 * n_terms + 1]
    i = pl.program_id(0)

    def run(sel):
        acc = r_ref[...]
        for t in range(n_terms):
            acc = acc + _dot(refs[3 * t + sel][...].astype(BF16), refs[3 * t + 2][...])
        o_ref[...] = acc

    @pl.when(i < n_p)
    def _():
        run(0)

    @pl.when(i >= n_p)
    def _():
        run(1)


def matmul_residual(terms, res, tm):
    rows, n = res.shape
    n_p = RP // tm
    args, specs = [], []
    for a_p, a_s, w in terms:
        k = w.shape[0]
        args += [a_p, a_s, w]
        specs += [pl.BlockSpec((tm, k), lambda i: (jnp.minimum(i, n_p - 1), 0)),
                  pl.BlockSpec((tm, k), lambda i: (jnp.maximum(i - n_p, 0), 0)),
                  pl.BlockSpec((k, n), lambda i: (0, 0))]
    return pl.pallas_call(
        functools.partial(_matmul_res_kernel, n_terms=len(terms), n_p=n_p),
        grid=(rows // tm,),
        in_specs=specs + [pl.BlockSpec((tm, n), lambda i: (i, 0))],
        out_specs=pl.BlockSpec((tm, n), lambda i: (i, 0)),
        out_shape=jax.ShapeDtypeStruct((rows, n), F32),
        compiler_params=_cp("arbitrary"),
    )(*args, res)


def _glu_mlp_kernel(*refs, norm, swiglu, scale, tf, own_res, n_p):
    if own_res:
        src_ref, g_ref, wa_ref, wb_ref, wd_ref, o_ref = refs
        res_ref = src_ref
    elif n_p is None:
        src_ref, g_ref, wa_ref, wb_ref, wd_ref, res_ref, o_ref = refs
    else:
        src_ref, srcs_ref, g_ref, wa_ref, wb_ref, wd_ref, res_ref, o_ref, h_scr = refs
    if n_p is None:
        x = src_ref[...]
        if norm:
            x = _rms(x, g_ref[...])
        h = x.astype(BF16)
    else:
        @pl.when(pl.program_id(0) < n_p)
        def _():
            h_scr[...] = src_ref[...].astype(BF16)

        @pl.when(pl.program_id(0) >= n_p)
        def _():
            h_scr[...] = srcs_ref[...].astype(BF16)

        h = h_scr[...]
    acc = None
    for j in range(wa_ref.shape[1] // tf):
        cs = slice(j * tf, (j + 1) * tf)
        a = _dot(h, wa_ref[:, cs])
        b = _dot(h, wb_ref[:, cs])
        s = _silu(a) * b if swiglu else a * _sigmoid(b)
        d = _dot(s.astype(BF16), wd_ref[cs, :])
        acc = d if acc is None else acc + d
    o_ref[...] = res_ref[...] + scale * acc


def glu_mlp(src, g, wa, wb, wd, res, tm, tf, norm, swiglu, scale, src_s=None):
    k = src.shape[1]
    ff = wa.shape[1]
    n = wd.shape[1]
    rows = src.shape[0] if res is None else res.shape[0]
    resident = lambda shape: pl.BlockSpec(shape, lambda i: (0, 0), pipeline_mode=pl.Buffered(1))
    weights = [pl.BlockSpec((1, k), lambda i: (0, 0)), resident((k, ff)), resident((k, ff)), resident((ff, n))]
    n_p, scratch = None, []
    if src_s is None:
        in_specs = [pl.BlockSpec((tm, k), lambda i: (i, 0))] + weights
        args = [src, g.reshape(1, k), wa, wb, wd]
    else:
        assert not norm and res is not None
        n_p = src.shape[0] // tm
        in_specs = [pl.BlockSpec((tm, k), lambda i: (jnp.minimum(i, n_p - 1), 0)),
                    pl.BlockSpec((tm, k), lambda i: (jnp.maximum(i - n_p, 0), 0))] + weights
        args = [src, src_s, g.reshape(1, k), wa, wb, wd]
        scratch = [pltpu.VMEM((tm, k), BF16)]
    if res is not None:
        in_specs.append(pl.BlockSpec((tm, n), lambda i: (i, 0)))
        args.append(res)
    return pl.pallas_call(
        functools.partial(_glu_mlp_kernel, norm=norm, swiglu=swiglu, scale=scale, tf=tf, own_res=res is None, n_p=n_p),
        grid=(rows // tm,),
        in_specs=in_specs,
        out_specs=pl.BlockSpec((tm, n), lambda i: (i, 0)),
        out_shape=jax.ShapeDtypeStruct((rows, n), F32),
        scratch_shapes=scratch,
        compiler_params=_cp("arbitrary" if src_s is not None else "parallel"),
    )(*args)


def _rmsnorm_kernel(x_ref, g_ref, o_ref):
    o_ref[...] = _rms(x_ref[...], g_ref[...])


def rmsnorm_rows(x, g, tm):
    rows, k = x.shape
    return pl.pallas_call(
        _rmsnorm_kernel,
        grid=(rows // tm,),
        in_specs=[pl.BlockSpec((tm, k), lambda i: (i, 0)), pl.BlockSpec((1, k), lambda i: (0, 0))],
        out_specs=pl.BlockSpec((tm, k), lambda i: (i, 0)),
        out_shape=jax.ShapeDtypeStruct((rows, k), F32),
        compiler_params=_cp("parallel"),
    )(x, g.reshape(1, k))


def _head_attention(qh, kh, vh):
    s = _dot_nt((qh * (X_HD ** -0.5)).astype(BF16), kh.astype(BF16))
    p = jnp.exp(s - jnp.max(s, axis=-1, keepdims=True))
    p = p / jnp.sum(p, axis=-1, keepdims=True)
    return _dot(p.astype(BF16), vh.astype(BF16))


def _xattn_kernel(q_ref, k_ref, v_ref, o_ref):
    q = q_ref[...]
    for h in range(X_HEADS):
        sl = slice(h * X_HD, (h + 1) * X_HD)
        o_ref[:, sl] = _head_attention(q[:, sl], k_ref[:, sl], v_ref[:, sl])


def cross_attention(q_all, mk, mv, k_spec, v_spec, row_off, n_seq, t_seq, tq):
    nb = t_seq // tq
    off = row_off // tq
    return pl.pallas_call(
        _xattn_kernel,
        grid=(n_seq, nb),
        in_specs=[pl.BlockSpec((tq, D_MODEL), lambda n, i: (off + n * nb + i, 0)), k_spec, v_spec],
        out_specs=pl.BlockSpec((tq, D_MODEL), lambda n, i: (n * nb + i, 0)),
        out_shape=jax.ShapeDtypeStruct((n_seq * t_seq, D_MODEL), F32),
        compiler_params=_cp("parallel", "parallel"),
    )(q_all, mk, mv)


def _xattn_cache_kernel(q_ref, k_hbm, v_hbm, o_ref, kbuf, vbuf, sem, *, layer):
    n = pl.program_id(0)
    slot = n % 2

    def copies(seq, s):
        cs = []
        for h in range(X_HEADS):
            cs.append(pltpu.make_async_copy(k_hbm.at[layer, seq, :, h, :], kbuf.at[s, h], sem.at[s, h]))
            cs.append(pltpu.make_async_copy(v_hbm.at[layer, seq, :, h, :], vbuf.at[s, h], sem.at[s, X_HEADS + h]))
        return cs

    @pl.when(n == 0)
    def _():
        for c in copies(0, 0):
            c.start()

    @pl.when(n + 1 < pl.num_programs(0))
    def _():
        for c in copies(n + 1, 1 - slot):
            c.start()

    for c in copies(n, slot):
        c.wait()
    q = q_ref[...]
    hs = range(X_HEADS)
    ss = [_dot_nt((q[:, h * X_HD:(h + 1) * X_HD] * (X_HD ** -0.5)).astype(BF16), kbuf[slot, h].astype(BF16)) for h in hs]
    ps = [jnp.exp(s - jnp.max(s, axis=-1, keepdims=True)) for s in ss]
    ps = [(p / jnp.sum(p, axis=-1, keepdims=True)).astype(BF16) for p in ps]
    for h in hs:
        o_ref[:, h * X_HD:(h + 1) * X_HD] = _dot(ps[h], vbuf[slot, h].astype(BF16))


def cross_attention_cached(q_all, cache_k, cache_v, layer, row_off):
    off = row_off // DEC_SEQ
    return pl.pallas_call(
        functools.partial(_xattn_cache_kernel, layer=layer),
        grid=(DEC_BATCH,),
        in_specs=[pl.BlockSpec((DEC_SEQ, D_MODEL), lambda n: (off + n, 0)),
                  pl.BlockSpec(memory_space=pl.ANY), pl.BlockSpec(memory_space=pl.ANY)],
        out_specs=pl.BlockSpec((DEC_SEQ, D_MODEL), lambda n: (n, 0)),
        out_shape=jax.ShapeDtypeStruct((DEC_BATCH * DEC_SEQ, D_MODEL), F32),
        scratch_shapes=[pltpu.VMEM((2, X_HEADS, N_MEM, X_HD), F32), pltpu.VMEM((2, X_HEADS, N_MEM, X_HD), F32),
                        pltpu.SemaphoreType.DMA((2, 2 * X_HEADS))],
        compiler_params=_cp("arbitrary"),
    )(q_all, cache_k, cache_v)


def _softplus(x):
    return jnp.maximum(x, 0.0) + jnp.log(1.0 + jnp.exp(-jnp.abs(x)))


def _split3(x, axis):
    hi = x.astype(BF16).astype(F32)
    return jnp.concatenate([hi, hi, x - hi], axis=axis).astype(BF16)


def _split3r(x, axis):
    hi = x.astype(BF16).astype(F32)
    return jnp.concatenate([hi, x - hi, hi], axis=axis).astype(BF16)


def _dot3(a, b):
    return _dot(_split3(a, 1), _split3r(b, 0))


def _cumsum_rows(tri3, g):
    g1 = g.astype(BF16).astype(F32)
    g2 = (g - g1).astype(BF16).astype(F32)
    g3 = g - g1 - g2
    return _dot(tri3, jnp.concatenate([g1, g2, g3], axis=0).astype(BF16))


def _gdn_kernel(qkv_ref, sm_ref, gate_ref, cw_ref, gp_ref, nw_ref, cb_ref, s0_ref, o_ref, sout_ref,
                xbuf, s_scr, *, seqs, rows, chunk):
    i = pl.program_id(1)
    n_chunks = rows // chunk

    @pl.when(i == 0)
    def _():
        xbuf[:, 0:8, :] = cb_ref[...]
        s_scr[...] = s0_ref[...]

    ri = lax.broadcasted_iota(jnp.int32, (chunk, chunk), 0)
    ci = lax.broadcasted_iota(jnp.int32, (chunk, chunk), 1)
    incl = ri >= ci
    strict = ri > ci
    tri = jnp.where(incl, 1.0, 0.0).astype(BF16)
    tri3 = jnp.concatenate([tri, tri, tri], axis=1)
    eye = jnp.where(ri == ci, 1.0, 0.0)
    levels = []
    w = 1
    while w < chunk:
        levels.append(((ri // (2 * w)) == (ci // (2 * w))) & ((ri % (2 * w)) >= w) & ((ci % (2 * w)) < w))
        w *= 2
    nw = nw_ref[...]
    cw = [cw_ref[j:j + 1, :] for j in range(GDN_CONV)]
    a_neg = -jnp.exp(gp_ref[0:1, :])
    dtb = gp_ref[1:2, :]

    probs = []
    for b in range(seqs):
        tok = slice(b * rows, (b + 1) * rows)
        xbuf[b, 8:8 + rows, :] = qkv_ref[tok, :]
        conv = xbuf[b, 5:5 + rows, :] * cw[0]
        for j in range(1, GDN_CONV):
            conv = conv + xbuf[b, 5 + j:5 + j + rows, :] * cw[j]
        tail = xbuf[b, 8 + rows - 3:8 + rows, :]
        xbuf[b, 5:8, :] = tail
        qkv = _silu(conv)
        sm = sm_ref[tok, :]
        beta_all = _sigmoid(sm)
        g_all = a_neg * _softplus(sm + dtb)
        for c in range(n_chunks):
            rs = slice(c * chunk, (c + 1) * chunk)
            gcum = _cumsum_rows(tri3, g_all[rs])
            gcum_t = gcum.T
            for h in range(GDN_HEADS):
                q = qkv[rs, h * GDN_DK:(h + 1) * GDN_DK]
                k = qkv[rs, 512 + h * GDN_DK:512 + (h + 1) * GDN_DK]
                v = qkv[rs, 1024 + h * GDN_DK:1024 + (h + 1) * GDN_DK]
                q = q * lax.rsqrt(jnp.sum(q * q, axis=-1, keepdims=True) + EPS) * (GDN_DK ** -0.5)
                k = k * lax.rsqrt(jnp.sum(k * k, axis=-1, keepdims=True) + EPS)
                beta = beta_all[rs, h:h + 1]
                gc_col = gcum[:, 4 + h:5 + h]
                gc_row = gcum_t[4 + h:5 + h, :]
                g_last = gcum[chunk - 1:chunk, 4 + h:5 + h]
                decay = jnp.where(incl, jnp.exp(jnp.minimum(gc_col - gc_row, 0.0)), 0.0)
                eg = jnp.exp(gc_col)
                kb = k * beta
                k3r = _split3r(k, 1)
                probs.append(dict(
                    b=b, c=c, h=h, decay=decay,
                    kk=_dot_nt(_split3(kb, 1), k3r), qk=_dot_nt(_split3(q, 1), k3r),
                    rhs=jnp.concatenate([v * beta, kb * eg], axis=1), qd=q * eg,
                    kd_t=(k * jnp.exp(g_last - gc_col)).T, gl=jnp.exp(g_last), minv=eye))
    for p in probs:
        p["lmat"] = jnp.where(strict, p["kk"] * p["decay"], 0.0)
        p["qk"] = jnp.where(incl, p["qk"] * p["decay"], 0.0)
    for p in probs:
        p["minv"] = eye - jnp.where(levels[0], p["lmat"], 0.0)
        p["lh"] = p["lmat"].astype(BF16).astype(F32)
        p["ll"] = p["lmat"] - p["lh"]
    for off_blk in levels[1:]:
        for p in probs:
            mh = p["minv"].astype(BF16).astype(F32)
            ml = p["minv"] - mh
            p["mr3"] = jnp.concatenate([mh, ml, mh], axis=0).astype(BF16)
            ch, cl = jnp.where(off_blk, p["lh"], 0.0), jnp.where(off_blk, p["ll"], 0.0)
            p["t"] = _dot(jnp.concatenate([mh, mh, ml], axis=1).astype(BF16),
                          jnp.concatenate([ch, cl, ch], axis=0).astype(BF16))
        for p in probs:
            p["minv"] = p["minv"] - _dot(_split3(p["t"], 1), p["mr3"])
    for p in probs:
        p["uw"] = _dot3(p["minv"], p["rhs"])
    state = {(b, h): s_scr[b, h] for b in range(seqs) for h in range(GDN_HEADS)}
    for c in range(n_chunks):
        cur = [p for p in probs if p["c"] == c]
        for p in cur:
            p["s3r"] = _split3r(state[(p["b"], p["h"])], 0)
            p["v_new"] = p["uw"][:, :GDN_DK] - _dot(_split3(p["uw"][:, GDN_DK:], 1), p["s3r"])
        for p in cur:
            p["o"] = _dot(_split3(p["qd"], 1), p["s3r"]) + _dot3(p["qk"], p["v_new"])
            state[(p["b"], p["h"])] = state[(p["b"], p["h"])] * p["gl"] + _dot3(p["kd_t"], p["v_new"])
    for p in probs:
        b, c, h = p["b"], p["c"], p["h"]
        r0 = b * rows + c * chunk
        o = _rms(p["o"], nw) * _silu(gate_ref[r0:r0 + chunk, h * GDN_DK:(h + 1) * GDN_DK])
        o_ref[r0:r0 + chunk, h * GDN_DK:(h + 1) * GDN_DK] = o
    for (b, h), s in state.items():
        s_scr[b, h] = s

    @pl.when(i == pl.num_programs(1) - 1)
    def _():
        sout_ref[...] = s_scr[...]


def gdn(qkv, small, gate, conv_w, gpar, norm_w, conv_buf8, s0, row_off, n_seq, t_seq, seqs, rows, chunk):
    nb = t_seq // rows
    blk = seqs * rows
    off = row_off // blk
    rmap = lambda n, i: (off + n * nb + i, 0)
    return pl.pallas_call(
        functools.partial(_gdn_kernel, seqs=seqs, rows=rows, chunk=chunk),
        grid=(n_seq // seqs, nb),
        in_specs=[pl.BlockSpec((blk, GDN_QKV), rmap),
                  pl.BlockSpec((blk, 128), rmap),
                  pl.BlockSpec((blk, 512), rmap),
                  pl.BlockSpec((GDN_CONV, GDN_QKV), lambda n, i: (0, 0)),
                  pl.BlockSpec((8, 128), lambda n, i: (0, 0)),
                  pl.BlockSpec((1, GDN_DK), lambda n, i: (0, 0)),
                  pl.BlockSpec((seqs, 8, GDN_QKV), lambda n, i: (n, 0, 0)),
                  pl.BlockSpec((seqs, GDN_HEADS, GDN_DK, GDN_DK), lambda n, i: (n, 0, 0, 0))],
        out_specs=[pl.BlockSpec((blk, 512), lambda n, i: (n * nb + i, 0)),
                   pl.BlockSpec((seqs, GDN_HEADS, GDN_DK, GDN_DK), lambda n, i: (n, 0, 0, 0))],
        out_shape=[jax.ShapeDtypeStruct((n_seq * t_seq, 512), F32),
                   jax.ShapeDtypeStruct((n_seq, GDN_HEADS, GDN_DK, GDN_DK), F32)],
        scratch_shapes=[pltpu.VMEM((seqs, 8 + rows, GDN_QKV), F32),
                        pltpu.VMEM((seqs, GDN_HEADS, GDN_DK, GDN_DK), F32)],
        compiler_params=_cp("parallel", "arbitrary"),
    )(qkv, small, gate, conv_w, gpar, norm_w, conv_buf8, s0)


def _compress_core(xj, posab_ref, w1_ref, w2_ref, m):
    acc = jnp.zeros((m + 8, 4 * CMP_HIDDEN), F32)
    for jj in range(CMP_STRIDE // 2):
        x2 = jnp.concatenate([xj(2 * jj), xj(2 * jj + 1)], axis=1)
        lhs = jnp.concatenate([x2, posab_ref[jj]], axis=0).astype(BF16)
        acc = acc + _dot(lhs, w1_ref[jj])
    top = acc[0:m, 0:2 * CMP_HIDDEN]
    bot = acc[0:m, 2 * CMP_HIDDEN:]
    c = acc[m:m + 1, 0:2 * CMP_HIDDEN] + acc[m + 1:m + 2, 2 * CMP_HIDDEN:]
    h = top + pltpu.roll(bot, m - 1, 0) + c
    out = _dot(_silu(h).astype(BF16), w2_ref[...])
    row = lax.broadcasted_iota(jnp.int32, (m, 2 * NSA_HD), 0)
    return jnp.where(row < m - 1, out, 0.0)


def _compress_prompt_kernel(kc_ref, vc_ref, pk_ref, w1k_ref, w2k_ref, pv_ref, w1v_ref, w2v_ref, ok_ref, ov_ref, *, m):
    ok_ref[0] = _compress_core(lambda j: kc_ref[pl.ds(j, m, stride=CMP_STRIDE), :], pk_ref, w1k_ref, w2k_ref, m)
    ov_ref[0] = _compress_core(lambda j: vc_ref[pl.ds(j, m, stride=CMP_STRIDE), :], pv_ref, w1v_ref, w2v_ref, m)


def _cmp_weight_specs(nmap):
    return [pl.BlockSpec((CMP_STRIDE // 2, 8, 256), nmap(3)),
            pl.BlockSpec((CMP_STRIDE // 2, 256, 4 * CMP_HIDDEN), nmap(3)),
            pl.BlockSpec((2 * CMP_HIDDEN, 2 * NSA_HD), nmap(2))]


def compress_prompt(nkv, cw, n_seq, t_seq):
    m = t_seq // CMP_STRIDE
    zmap = lambda nd: (lambda n: (0,) * nd)
    out = jax.ShapeDtypeStruct((n_seq, m, 2 * NSA_HD), F32)
    return pl.pallas_call(
        functools.partial(_compress_prompt_kernel, m=m),
        grid=(n_seq,),
        in_specs=[pl.BlockSpec((t_seq, 128), lambda n: (n, 4)), pl.BlockSpec((t_seq, 128), lambda n: (n, 5))]
        + _cmp_weight_specs(zmap) + _cmp_weight_specs(zmap),
        out_specs=[pl.BlockSpec((1, m, 2 * NSA_HD), lambda n: (n, 0, 0))] * 2,
        out_shape=[out, out],
        compiler_params=_cp("parallel"),
    )(nkv, nkv, *cw)


def _compress_sample_kernel(pt_ref, *refs, m):
    del pt_ref
    kpages = refs[0:N_PAGES]
    vpages = refs[N_PAGES:2 * N_PAGES]
    pk_ref, w1k_ref, w2k_ref, pv_ref, w1v_ref, w2v_ref, ok_ref, ov_ref, xk_scr, xv_scr = refs[2 * N_PAGES:]
    for p in range(N_PAGES):
        xk_scr[p * PAGE_SIZE:(p + 1) * PAGE_SIZE, :] = kpages[p][...].T
        xv_scr[p * PAGE_SIZE:(p + 1) * PAGE_SIZE, :] = vpages[p][...].T
    ok_ref[0] = _compress_core(lambda j: xk_scr[pl.ds(j, m, stride=CMP_STRIDE), :], pk_ref, w1k_ref, w2k_ref, m)
    ov_ref[0] = _compress_core(lambda j: xv_scr[pl.ds(j, m, stride=CMP_STRIDE), :], pv_ref, w1v_ref, w2v_ref, m)


def _page_specs(layer, nseq=1, b=0):
    return [pl.BlockSpec((None, None, PAGE_SIZE, 128),
                         functools.partial(lambda n, pt, p: (layer, pt[n * nseq + b, p], 0, 0), p=p))
            for p in range(N_PAGES)]


def compress_sample(page_table, cache_k, cache_v, layer, cw):
    m = PAST_LEN // CMP_STRIDE
    zmap = lambda nd: (lambda n, pt: (0,) * nd)
    out = jax.ShapeDtypeStruct((DEC_BATCH, m, 2 * NSA_HD), F32)
    return pl.pallas_call(
        functools.partial(_compress_sample_kernel, m=m),
        grid_spec=pltpu.PrefetchScalarGridSpec(
            num_scalar_prefetch=1, grid=(DEC_BATCH,),
            in_specs=_page_specs(layer) + _page_specs(layer) + _cmp_weight_specs(zmap) + _cmp_weight_specs(zmap),
            out_specs=[pl.BlockSpec((1, m, 2 * NSA_HD), lambda n, pt: (n, 0, 0))] * 2,
            scratch_shapes=[pltpu.VMEM((PAST_LEN, 2 * NSA_HD), F32), pltpu.VMEM((PAST_LEN, 2 * NSA_HD), F32)]),
        out_shape=[out, out],
        compiler_params=_cp("arbitrary"),
    )(page_table, *([cache_k] * N_PAGES), *([cache_v] * N_PAGES), *cw)


def _masked_softmax(s, mask):
    s = jnp.where(mask, s, NEG)
    m = jnp.max(s, axis=-1, keepdims=True)
    p = jnp.where(mask, jnp.exp(s - m), 0.0)
    l = jnp.sum(p, axis=-1, keepdims=True)
    return p / jnp.where(l > 0.0, l, 1.0)


def _stack_heads(nq, g, tq):
    parts = [nq[:, (g * NSA_REP + r) * NSA_HD:(g * NSA_REP + r + 1) * NSA_HD] for r in range(NSA_REP)]
    return (jnp.concatenate(parts, axis=0) * (NSA_HD ** -0.5)).astype(BF16)


def _rep(mask, tq):
    return jnp.concatenate([mask] * NSA_REP, axis=0)


def _cmp_branch(q4, kcmp, vcmp, qpos, tq, n_blk):
    n_cmp = kcmp.shape[0]
    s = _dot_nt(q4, kcmp.astype(BF16))
    cmp_end = lax.broadcasted_iota(jnp.int32, (tq, n_cmp), 1) * CMP_STRIDE + (CMP_LEN - 1)
    p = _masked_softmax(s, _rep(cmp_end <= qpos, tq))
    o = _dot(p.astype(BF16), vcmp.astype(BF16))
    return o, _importance_t(p, tq, n_blk)


def _importance_t(p, tq, n_blk):
    n_cmp = p.shape[1]
    psum = p[0:tq] + p[tq:2 * tq] + p[2 * tq:3 * tq] + p[3 * tq:4 * tq]
    sj = lax.broadcasted_iota(jnp.int32, (n_blk, n_cmp), 0) * SEL_BLOCK
    ci = lax.broadcasted_iota(jnp.int32, (n_blk, n_cmp), 1) * CMP_STRIDE
    ov = jnp.clip(jnp.minimum(ci + CMP_LEN, sj + SEL_BLOCK) - jnp.maximum(ci, sj), 0, CMP_LEN).astype(F32) / CMP_LEN
    ov = ov.astype(BF16)
    p1 = psum.astype(BF16).astype(F32)
    p2 = (psum - p1).astype(BF16).astype(F32)
    p3 = psum - p1 - p2
    return _dot_nt(jnp.concatenate([ov, ov, ov], axis=1), jnp.concatenate([p1, p2, p3], axis=1).astype(BF16))


def _select_blocks(imp_ts, qpos_row, tq, n_blk):
    blk = lax.broadcasted_iota(jnp.int32, (n_blk, tq), 0)
    cur = qpos_row // SEL_BLOCK
    valid = blk <= cur
    forced = valid & ((blk == 0) | (blk >= cur - 1))
    works = [jnp.where(valid, jnp.where(forced, SEL_FORCED, imp_t), NEG) for imp_t in imp_ts]
    sels = [jnp.zeros((n_blk, tq), F32) for _ in imp_ts]
    for _ in range(SEL_TOPN):
        for j in range(len(works)):
            m = jnp.max(works[j], axis=0, keepdims=True)
            idx = jnp.min(jnp.where(works[j] == m, blk, n_blk), axis=0, keepdims=True)
            pick = blk == idx
            sels[j] = jnp.where(pick, 1.0, sels[j])
            works[j] = jnp.where(pick, -jnp.inf, works[j])
    return [s.T for s in sels]


def _expand_sel(sel, first_blk, n_keys):
    n_blk = sel.shape[1]
    bj = lax.broadcasted_iota(jnp.int32, (n_blk, n_keys), 0)
    kb = lax.broadcasted_iota(jnp.int32, (n_blk, n_keys), 1) // SEL_BLOCK + first_blk
    e = jnp.where(bj == kb, 1.0, 0.0).astype(BF16)
    return _dot(sel.astype(BF16), e) > 0.5


def _gate_mix(o_ref, gates, g, o_cmp, o_slc, o_win, tq):
    for r in range(NSA_REP):
        h = g * NSA_REP + r
        rs = slice(r * tq, (r + 1) * tq)
        c0 = 8 + 3 * h
        o = (gates[:, c0:c0 + 1] * o_cmp[rs] + gates[:, c0 + 1:c0 + 2] * o_slc[rs] + gates[:, c0 + 2:c0 + 3] * o_win[rs])
        o_ref[:, h * NSA_HD:(h + 1) * NSA_HD] = o


def _nsa_prompt_kernel(nq_ref, sm_ref, kv_ref, et_ref, kcmp_ref, vcmp_ref, o_ref):
    tq = Q_BLOCK
    i = pl.program_id(1)
    qpos = i * tq + lax.broadcasted_iota(jnp.int32, (tq, 1), 0)
    nq = nq_ref[...]
    gates = _sigmoid(sm_ref[...])
    kc = 512
    qpos_row = i * tq + lax.broadcasted_iota(jnp.int32, (1, tq), 1)
    q4s, o_cmps, imp_ts = [], [], []
    for g in range(NSA_KV):
        gs = slice(g * NSA_HD, (g + 1) * NSA_HD)
        q4s.append(_stack_heads(nq, g, tq))
        o_cmp, imp_t = _cmp_branch(q4s[g], kcmp_ref[0, :, gs], vcmp_ref[0, :, gs], qpos, tq, 128)
        o_cmps.append(o_cmp)
        imp_ts.append(imp_t)
    sels = _select_blocks(imp_ts, qpos_row, tq, 128)
    qas = [jnp.concatenate([_rep(jnp.where(sels[g] > 0.5, 0.0, -SEL_MASK), tq).astype(BF16), q4s[g]], axis=1)
           for g in range(NSA_KV)]

    def slc_step(c, carry, diagonal):
        start = pl.multiple_of(c * kc, kc)
        et = et_ref[pl.ds(start, kc), :]
        ss = [_dot_nt(qas[g], jnp.concatenate([et, kv_ref[pl.ds(start, kc), g * NSA_HD:(g + 1) * NSA_HD]], axis=1))
              for g in range(NSA_KV)]
        if diagonal:
            causal = _rep(start + lax.broadcasted_iota(jnp.int32, (tq, kc), 1) <= qpos, tq)
            ss = [jnp.where(causal, s, -SEL_MASK) for s in ss]
        out = []
        for g in range(NSA_KV):
            m_i, l_i, acc = carry[g]
            m_new = jnp.maximum(m_i, jnp.max(ss[g], axis=-1, keepdims=True))
            alpha = jnp.exp(m_i - m_new)
            p = jnp.exp(ss[g] - m_new)
            l_new = alpha * l_i + jnp.sum(p, axis=-1, keepdims=True)
            v = kv_ref[pl.ds(start, kc), 128 + g * NSA_HD:128 + (g + 1) * NSA_HD]
            out.append((m_new, l_new, alpha * acc + _dot(p.astype(BF16), v)))
        return tuple(out)

    init = tuple((jnp.full((NSA_REP * tq, 1), -3.0e38, F32), jnp.zeros((NSA_REP * tq, 1), F32),
                  jnp.zeros((NSA_REP * tq, NSA_HD), F32)) for _ in range(NSA_KV))
    n_full = i // (kc // tq)
    carry = lax.fori_loop(0, n_full, functools.partial(slc_step, diagonal=False), init)
    fin = slc_step(n_full, carry, True)

    for g in range(NSA_KV):
        q4, o_cmp = q4s[g], o_cmps[g]
        _, l_f, acc = fin[g]
        o_slc = acc / jnp.where(l_f > 0.0, l_f, 1.0)

        nband = WINDOW + tq
        wstart = pl.multiple_of(jnp.maximum(i - WINDOW // tq, 0) * tq, tq)
        kw = kv_ref[pl.ds(wstart, nband), 256 + g * NSA_HD:256 + (g + 1) * NSA_HD]
        vw = kv_ref[pl.ds(wstart, nband), 384 + g * NSA_HD:384 + (g + 1) * NSA_HD]
        d = qpos - (wstart + lax.broadcasted_iota(jnp.int32, (tq, nband), 1))
        pw = _masked_softmax(_dot_nt(q4, kw), _rep((d >= 0) & (d <= WINDOW), tq))
        o_win = _dot(pw.astype(BF16), vw)
        _gate_mix(o_ref, gates, g, o_cmp, o_slc, o_win, tq)


def nsa_prompt(nq, small, kv_bf, kcmp, vcmp, n_seq, t_seq):
    nb = t_seq // Q_BLOCK
    m = kcmp.shape[1]
    key_blk = jnp.arange(t_seq, dtype=jnp.int32)[:, None] // SEL_BLOCK
    et = (key_blk == jnp.arange(128, dtype=jnp.int32)[None, :]).astype(BF16)
    return pl.pallas_call(
        _nsa_prompt_kernel,
        grid=(n_seq, nb),
        in_specs=[pl.BlockSpec((Q_BLOCK, 512), lambda n, i: (n * nb + i, 0)),
                  pl.BlockSpec((Q_BLOCK, 128), lambda n, i: (n * nb + i, 0)),
                  pl.BlockSpec((t_seq, 512), lambda n, i: (n, 0)),
                  pl.BlockSpec((t_seq, 128), lambda n, i: (0, 0)),
                  pl.BlockSpec((1, m, 128), lambda n, i: (n, 0, 0)),
                  pl.BlockSpec((1, m, 128), lambda n, i: (n, 0, 0))],
        out_specs=pl.BlockSpec((Q_BLOCK, 512), lambda n, i: (n * nb + i, 0)),
        out_shape=jax.ShapeDtypeStruct((n_seq * t_seq, 512), F32),
        compiler_params=_cp("parallel", "arbitrary"),
    )(nq, small, kv_bf, et, kcmp, vcmp)


NSA_SAMPLE_SEQS = 2


def _nsa_sample_kernel(pt_ref, *refs):
    del pt_ref
    nseq = NSA_SAMPLE_SEQS
    per = 2 * N_PAGES + 2
    seq_refs = [refs[b * per:(b + 1) * per] for b in range(nseq)]
    nq_ref, sm_ref, kvn_ref, kcmp_ref, vcmp_ref, o_ref = refs[nseq * per:]
    tq = DEC_SEQ
    seqs = range(nseq)
    qpos = PAST_LEN + lax.broadcasted_iota(jnp.int32, (tq, 1), 0)
    qpos_row = PAST_LEN + lax.broadcasted_iota(jnp.int32, (1, tq), 1)
    n_keys = (N_PAGES + 1) * PAGE_SIZE
    n_blk = 40
    nr = NSA_REP * tq
    pad = jnp.zeros((PAGE_SIZE - tq, 128), F32)
    z = jnp.zeros((nr, NSA_HD), BF16)
    both = lambda m: jnp.concatenate([_rep(m, tq)] * NSA_KV, axis=0)
    n_cmp = kcmp_ref.shape[1]
    cmp_ok = both(lax.broadcasted_iota(jnp.int32, (tq, n_cmp), 1) * CMP_STRIDE + (CMP_LEN - 1) <= qpos)
    d = qpos - (PAST_LEN - WINDOW + lax.broadcasted_iota(jnp.int32, (tq, WINDOW + PAGE_SIZE), 1))
    win_ok = both((d >= 0) & (d <= WINDOW))
    causal = lax.broadcasted_iota(jnp.int32, (tq, n_keys), 1) <= qpos

    new, q_bd = [], []
    for b in seqs:
        rows = slice(b * tq, (b + 1) * tq)
        kvn = kvn_ref[rows, :]
        new.append([jnp.concatenate([kvn[:, c * 128:(c + 1) * 128], pad], axis=0).astype(BF16) for c in range(4)])
        nq = nq_ref[rows, :]
        q_bd.append(jnp.concatenate([jnp.concatenate([_stack_heads(nq, 0, tq), z], axis=1),
                                     jnp.concatenate([z, _stack_heads(nq, 1, tq)], axis=1)], axis=0))
    s_c = [_dot_nt(q_bd[b], kcmp_ref[b].astype(BF16)) for b in seqs]
    s = [jnp.concatenate([_dot(q_bd[b], p[...].astype(BF16)) for p in seq_refs[b][:N_PAGES]]
                         + [_dot_nt(q_bd[b], new[b][0])], axis=1) for b in seqs]
    sw = [jnp.concatenate([_dot(q_bd[b], seq_refs[b][2 * N_PAGES][...].astype(BF16)), _dot_nt(q_bd[b], new[b][2])],
                          axis=1) for b in seqs]
    p_c = [_masked_softmax(s_c[b], cmp_ok) for b in seqs]
    o_c = [_dot(p_c[b].astype(BF16), vcmp_ref[b].astype(BF16)) for b in seqs]
    pw = [_masked_softmax(sw[b], win_ok).astype(BF16) for b in seqs]
    o_w = [_dot_nt(pw[b][:, :WINDOW], seq_refs[b][2 * N_PAGES + 1][...].astype(BF16)) + _dot(pw[b][:, WINDOW:], new[b][3])
           for b in seqs]
    sels = _select_blocks([_importance_t(p_c[b][g * nr:(g + 1) * nr], tq, n_blk) for b in seqs for g in range(NSA_KV)],
                          qpos_row, tq, n_blk)
    p = []
    for b in seqs:
        msk = jnp.concatenate([_rep(_expand_sel(sels[b * NSA_KV + g], 0, n_keys) & causal, tq) for g in range(NSA_KV)],
                              axis=0)
        p.append(_masked_softmax(s[b], msk).astype(BF16))
    o_s = [_dot(p[b][:, N_PAGES * PAGE_SIZE:], new[b][1]) for b in seqs]
    for c in range(N_PAGES):
        o_s = [o_s[b] + _dot_nt(p[b][:, c * PAGE_SIZE:(c + 1) * PAGE_SIZE], seq_refs[b][N_PAGES + c][...].astype(BF16))
               for b in seqs]
    for b in seqs:
        gates = _sigmoid(sm_ref[b * tq:(b + 1) * tq, :])
        for g in range(NSA_KV):
            rs, gs = slice(g * nr, (g + 1) * nr), slice(g * NSA_HD, (g + 1) * NSA_HD)
            _gate_mix(o_ref.at[b * tq:(b + 1) * tq], gates, g, o_c[b][rs, gs], o_s[b][rs, gs], o_w[b][rs, gs], tq)


def nsa_sample(page_table, cache_k, cache_v, layer, nq, small, nkv, kcmp, vcmp, win_k, win_v):
    nseq = NSA_SAMPLE_SEQS
    rows = nseq * DEC_SEQ
    off = RP // rows
    rmap = lambda n, pt: (off + n, 0)
    m = kcmp.shape[1]
    specs, args = [], []
    for b in range(nseq):
        specs += _page_specs(layer, nseq, b) + _page_specs(layer, nseq, b)
        args += [cache_k] * N_PAGES + [cache_v] * N_PAGES
        wspec = pl.BlockSpec((None, None, 128, WINDOW), functools.partial(lambda n, pt, b: (layer, n * nseq + b, 0, 0), b=b))
        specs += [wspec, wspec]
        args += [win_k, win_v]
    return pl.pallas_call(
        _nsa_sample_kernel,
        grid_spec=pltpu.PrefetchScalarGridSpec(
            num_scalar_prefetch=1, grid=(DEC_BATCH // nseq,),
            in_specs=specs + [
                pl.BlockSpec((rows, 512), rmap),
                pl.BlockSpec((rows, 128), rmap),
                pl.BlockSpec((rows, 768), rmap),
                pl.BlockSpec((nseq, m, 128), lambda n, pt: (n, 0, 0)),
                pl.BlockSpec((nseq, m, 128), lambda n, pt: (n, 0, 0))],
            out_specs=pl.BlockSpec((rows, 512), lambda n, pt: (n, 0))),
        out_shape=jax.ShapeDtypeStruct((RS, 512), F32),
        compiler_params=_cp("arbitrary"),
    )(page_table, *args, nq, small, nkv, kcmp, vcmp)


S5_TILE = 256
S5_LANES = 2048
S5_BLK = 16


def _gelu_tanh(x):
    return 0.5 * x * (1.0 + jnp.tanh(math.sqrt(2.0 / math.pi) * (x + 0.044715 * (x * x * x))))


def _s5_kernel(u_ref, pm_ref, pt3_ref, bdr_ref, bdi_ref, cr_ref, ci_ref, tab_ref, d_ref, x0r_ref, x0i_ref,
               y_ref, fr_ref, fi_ref, xr_scr, xi_scr, car_scr, *, chained):
    i = pl.program_id(1)
    nb = S5_GROUPS // S5_BLK
    wi = S5_BLK * S5_CH
    ws = S5_BLK * S5_P
    seg = S5_TILE // 8
    u = u_ref[...]
    ub = _dot(pm_ref[...], u.astype(BF16)).astype(BF16)
    for b in range(nb):
        xr_scr[:, b * ws:(b + 1) * ws] = _dot(ub[:, b * wi:(b + 1) * wi], bdr_ref[b])
        xi_scr[:, b * ws:(b + 1) * ws] = _dot(ub[:, b * wi:(b + 1) * wi], bdi_ref[b])

    if chained:
        @pl.when(i == 0)
        def _():
            car_scr[0:1, :] = x0r_ref[0]
            car_scr[1:2, :] = x0i_ref[0]

    def cmad(br, bi, mr, mi, xr, xi):
        return br + mr * xr - mi * xi, bi + mr * xi + mi * xr

    for c in range(S5_STATE // S5_LANES):
        ls = slice(c * S5_LANES, (c + 1) * S5_LANES)
        ar, ai = tab_ref[0, 0:1, ls], tab_ref[1, 0:1, ls]
        if chained:
            xr, xi = jnp.zeros((8, S5_LANES), F32), jnp.zeros((8, S5_LANES), F32)
            for k in range(seg):
                rs = slice(8 * k, 8 * k + 8)
                xr, xi = cmad(xr_scr[rs, ls], xi_scr[rs, ls], ar, ai, xr, xi)
                xr_scr[rs, ls] = xr
                xi_scr[rs, ls] = xi
            yr, yi = xr, xi
            for sh, r0 in ((1, seg), (2, seg + 8), (4, seg + 16)):
                yr, yi = cmad(yr, yi, tab_ref[0, r0:r0 + 8, ls], tab_ref[1, r0:r0 + 8, ls],
                              pltpu.roll(yr, sh, 0), pltpu.roll(yi, sh, 0))
            first = lax.broadcasted_iota(jnp.int32, (8, S5_LANES), 0) == 0
            yr = jnp.where(first, 0.0, pltpu.roll(yr, 1, 0))
            yi = jnp.where(first, 0.0, pltpu.roll(yi, 1, 0))
            sr, si = cmad(yr, yi, tab_ref[0, seg + 24:seg + 32, ls], tab_ref[1, seg + 24:seg + 32, ls],
                          car_scr[0:1, ls], car_scr[1:2, ls])
            for k in range(seg):
                rs = slice(8 * k, 8 * k + 8)
                xr, xi = cmad(xr_scr[rs, ls], xi_scr[rs, ls], tab_ref[0, k:k + 1, ls], tab_ref[1, k:k + 1, ls], sr, si)
                xr_scr[rs, ls] = xr
                xi_scr[rs, ls] = xi
            car_scr[0:1, ls] = xr[7:8, :]
            car_scr[1:2, ls] = xi[7:8, :]
        else:
            for q in range(S5_TILE // 64):
                xr, xi = x0r_ref[0, 8 * q:8 * q + 8, ls], x0i_ref[0, 8 * q:8 * q + 8, ls]
                for t in range(8):
                    rs = slice(8 * (8 * q + t), 8 * (8 * q + t) + 8)
                    xr, xi = cmad(xr_scr[rs, ls], xi_scr[rs, ls], ar, ai, xr, xi)
                    xr_scr[rs, ls] = xr
                    xi_scr[rs, ls] = xi
                fr_ref[0, 8 * q:8 * q + 8, ls] = xr
                fi_ref[0, 8 * q:8 * q + 8, ls] = xi

    ys = []
    for b in range(nb):
        ys.append(_dot(xr_scr[:, b * ws:(b + 1) * ws].astype(BF16), cr_ref[b])
                  - _dot(xi_scr[:, b * ws:(b + 1) * ws].astype(BF16), ci_ref[b]))
    yp = jnp.concatenate(ys, axis=1)
    h1 = yp.astype(BF16).astype(F32)
    h2 = (yp - h1).astype(BF16).astype(F32)
    ylin = _dot(pt3_ref[...], jnp.concatenate([h1, h2, yp - h1 - h2], axis=0).astype(BF16))
    y_ref[...] = _gelu_tanh(ylin + d_ref[...] * u).astype(BF16)

    if chained:
        @pl.when(i == pl.num_programs(1) - 1)
        def _():
            fr_ref[0] = car_scr[0:1, :]
            fi_ref[0] = car_scr[1:2, :]


def s5_scan(u, sp, x0r, x0i, row_off, n_seq, t_seq, chained):
    if chained:
        grid = (n_seq, t_seq // S5_TILE)
        nb = grid[1]
        smap = lambda n, i: (n, 0, 0)
    else:
        grid = (1, n_seq * t_seq // S5_TILE)
        nb = grid[1]
        smap = lambda n, i: (i, 0, 0)
    off = row_off // S5_TILE
    rmap = lambda n, i: (off + n * nb + i, 0)
    sblk = (1,) + x0r.shape[1:]
    const = lambda nd: (lambda n, i: (0,) * nd)
    rho = jnp.arange(S5_TILE)
    if chained:
        src = (S5_TILE // 8) * (rho % 8) + rho // 8
    else:
        src = 64 * (rho // 64) + 8 * (rho % 8) + (rho // 8) % 8
    pm = (src[:, None] == jnp.arange(S5_TILE)[None, :]).astype(BF16)
    pt3 = jnp.tile(pm.T, (1, 3))
    in_specs = [pl.BlockSpec((S5_TILE, D_MODEL), rmap),
                pl.BlockSpec((S5_TILE, S5_TILE), const(2)), pl.BlockSpec((S5_TILE, 3 * S5_TILE), const(2)),
                pl.BlockSpec(sp["bdr"].shape, const(3)), pl.BlockSpec(sp["bdi"].shape, const(3)),
                pl.BlockSpec(sp["cr"].shape, const(3)), pl.BlockSpec(sp["ci"].shape, const(3)),
                pl.BlockSpec(sp["tab"].shape, const(3)), pl.BlockSpec((1, D_MODEL), const(2)),
                pl.BlockSpec(sblk, smap), pl.BlockSpec(sblk, smap)]
    args = [u, pm, pt3, sp["bdr"], sp["bdi"], sp["cr"], sp["ci"], sp["tab"], sp["d"], x0r, x0i]
    return pl.pallas_call(
        functools.partial(_s5_kernel, chained=chained),
        grid=grid,
        in_specs=in_specs,
        out_specs=[pl.BlockSpec((S5_TILE, D_MODEL), lambda n, i: (n * nb + i, 0)),
                   pl.BlockSpec(sblk, smap), pl.BlockSpec(sblk, smap)],
        out_shape=[jax.ShapeDtypeStruct((n_seq * t_seq, D_MODEL), BF16), jax.ShapeDtypeStruct(x0r.shape, F32),
                   jax.ShapeDtypeStruct(x0r.shape, F32)],
        scratch_shapes=[pltpu.VMEM((S5_TILE, S5_STATE), F32), pltpu.VMEM((S5_TILE, S5_STATE), F32),
                        pltpu.VMEM((8, S5_STATE), F32)],
        compiler_params=_cp("arbitrary", "arbitrary"),
    )(*args)


def _prep_ab_in(w):
    qkv, b, a, gate, nq, nkv, ngate = jnp.split(w, (1536, 1540, 1544, 2056, 2568, 3336), axis=-1)
    kc, vc, ks, vs, kw, vw = jnp.split(nkv, 6, axis=-1)
    small = jnp.concatenate([b, a, ngate, jnp.zeros(w.shape[:-1] + (96,), w.dtype)], axis=-1)
    return jnp.concatenate([qkv, ks, vs, kw, vw, kc, vc, gate, nq, small], axis=-1).astype(BF16)


AB_SPLITS = (1536, 768, 512, 512, 128)


def _prep_cmp(pos, w1, w2):
    top = w1[:CMP_STRIDE * NSA_HD].reshape(CMP_STRIDE, NSA_HD, CMP_HIDDEN)
    bot = w1[CMP_STRIDE * NSA_HD:].reshape(CMP_STRIDE, NSA_HD, CMP_HIDDEN)
    w1bd = jnp.zeros((CMP_STRIDE, 2 * NSA_HD, 4 * CMP_HIDDEN), F32)
    w2bd = jnp.zeros((2 * CMP_HIDDEN, 2 * NSA_HD), F32)
    for g in range(NSA_KV):
        rs = slice(g * NSA_HD, (g + 1) * NSA_HD)
        w1bd = w1bd.at[:, rs, g * CMP_HIDDEN:(g + 1) * CMP_HIDDEN].set(top)
        w1bd = w1bd.at[:, rs, (2 + g) * CMP_HIDDEN:(3 + g) * CMP_HIDDEN].set(bot)
        w2bd = w2bd.at[g * CMP_HIDDEN:(g + 1) * CMP_HIDDEN, rs].set(w2)
    posab = jnp.zeros((CMP_STRIDE, 8, 2 * NSA_HD), F32)
    posab = posab.at[:, 0, :].set(jnp.tile(pos[:CMP_STRIDE], (1, NSA_KV)))
    posab = posab.at[:, 1, :].set(jnp.tile(pos[CMP_STRIDE:], (1, NSA_KV)))
    half = CMP_STRIDE // 2
    posab = posab.reshape(half, 2, 8, 2 * NSA_HD).transpose(0, 2, 1, 3).reshape(half, 8, 4 * NSA_HD)
    w1bd = w1bd.reshape(half, 4 * NSA_HD, 4 * CMP_HIDDEN)
    return posab, w1bd.astype(BF16), w2bd.astype(BF16)


def _cmul(ar, ai, br, bi):
    return ar * br - ai * bi, ar * bi + ai * br


def _prep_s5(a_re, a_im, b_re, b_im, c_re, c_im, d, log_dt):
    dt = jnp.exp(log_dt)[:, None]
    lr = jnp.minimum(a_re, S5_MAX_RE)
    li = a_im
    mag = jnp.exp(lr * dt)
    ar = mag * jnp.cos(li * dt)
    ai = mag * jnp.sin(li * dt)
    den = lr * lr + li * li
    fr = ((ar - 1.0) * lr + ai * li) / den
    fi = (ai * lr - (ar - 1.0) * li) / den
    bbr = fr[..., None] * b_re - fi[..., None] * b_im
    bbi = fr[..., None] * b_im + fi[..., None] * b_re
    nb = S5_GROUPS // S5_BLK
    eye = jnp.eye(S5_BLK, dtype=F32)

    def bd_in(m):
        m4 = jnp.swapaxes(m, 1, 2).reshape(nb, S5_BLK, S5_CH, S5_P)
        return jnp.einsum('bgcp,gh->bgchp', m4, eye).reshape(nb, S5_BLK * S5_CH, S5_BLK * S5_P).astype(BF16)

    def bd_out(m):
        m4 = jnp.swapaxes(m, 1, 2).reshape(nb, S5_BLK, S5_P, S5_CH)
        return jnp.einsum('bgpc,gh->bgphc', m4, eye).reshape(nb, S5_BLK * S5_P, S5_BLK * S5_CH).astype(BF16)

    seg = S5_TILE // 8
    a1 = (ar.reshape(1, S5_STATE), ai.reshape(1, S5_STATE))
    pw = [a1]
    for _ in range(seg - 1):
        pw.append(_cmul(*pw[-1], *a1))
    row = jnp.arange(8)[:, None]
    s1 = pw[-1]
    s2 = _cmul(*s1, *s1)
    s4 = _cmul(*s2, *s2)
    sp = [(jnp.ones_like(a1[0]), jnp.zeros_like(a1[1]))]
    for _ in range(7):
        sp.append(_cmul(*sp[-1], *s1))
    parts = []
    for j in range(2):
        parts.append(jnp.concatenate([p[j] for p in pw]
                                     + [jnp.where(row >= sh, s[j], 0.0) for sh, s in ((1, s1), (2, s2), (4, s4))]
                                     + [p[j] for p in sp], axis=0))
    return {"bdr": bd_in(bbr), "bdi": bd_in(bbi), "cr": bd_out(c_re), "ci": bd_out(c_im),
            "tab": jnp.stack(parts), "d": d.reshape(1, D_MODEL)}


def _heads(a, n, t):
    return a.reshape(n, t, NSA_KV, NSA_HD)


def kernel(x_prompt, x_sample, mem_prompt, cache_mem_k, cache_mem_v, state_gdn, state_gdn_conv, cache_cmp_k, cache_cmp_v, cache_slc_k, cache_slc_v, cache_win_k, cache_win_v, state_s5_re, state_s5_im, page_table, norm_ffn1, w_ffn1_gate, w_ffn1_up, w_ffn1_down, norm_mix, norm_xq, norm_mem, w_xq, w_xk, w_xv, w_xo, norm_ffn2, w_ffn2_gate, w_ffn2_up, w_ffn2_down, norm_final, w_in_ab, w_out_ab, gdn_conv, gdn_a_log, gdn_dt_bias, gdn_norm, cmp_pos_k, cmp_w1_k, cmp_w2_k, cmp_pos_v, cmp_w1_v, cmp_w2_v, w_in_c, s5_a_re, s5_a_im, s5_b_re, s5_b_im, s5_c_re, s5_c_im, s5_d, s5_log_dt, w_glu, w_out_c):
    bf = lambda w: w.astype(BF16)
    n_ab = w_in_ab.shape[0]
    n_pool = cache_cmp_k.shape[1]
    x = jnp.concatenate([x_prompt.reshape(RP, D_MODEL), x_sample.reshape(RS, D_MODEL)], axis=0)

    memkv = mem_kv_all(mem_prompt.reshape(BATCH * N_MEM, D_MODEL), norm_mem, bf(jnp.concatenate([w_xk, w_xv], axis=-1)))
    mem_k_prompt = memkv[:, :, :D_MODEL].reshape(DEPTH, BATCH, N_MEM, X_HEADS, X_HD)
    mem_v_prompt = memkv[:, :, D_MODEL:].reshape(DEPTH, BATCH, N_MEM, X_HEADS, X_HD)

    fmaj = lambda c: jnp.transpose(c, (0, 1, 3, 4, 2)).reshape(c.shape[0], c.shape[1], NSA_KV * NSA_HD, c.shape[2])
    cck, ccv = fmaj(cache_cmp_k), fmaj(cache_cmp_v)
    csk, csv, cwk, cwv = fmaj(cache_slc_k), fmaj(cache_slc_v), fmaj(cache_win_k), fmaj(cache_win_v)
    ab_p, ab_s, c_p, c_s = [], [], [], []
    for l in range(DEPTH):
        i = l // 2
        x = glu_mlp(x, norm_ffn1[l], bf(w_ffn1_gate[l]), bf(w_ffn1_up[l]), bf(w_ffn1_down[l]), None, 1024, 256, True, True, 0.5)
        if l % 2 == 0:
            qkv, nkv, gate, nq, small = norm_matmul(x, norm_mix[l], _prep_ab_in(w_in_ab[i]), AB_SPLITS, 512)
            gpar = jnp.zeros((8, 128), F32).at[0, 4:8].set(gdn_a_log[i]).at[1, 4:8].set(gdn_dt_bias[i])
            nw = gdn_norm[i].reshape(1, GDN_DK)
            cb_s = jnp.zeros((DEC_BATCH, 8, GDN_QKV), F32).at[:, 5:8].set(state_gdn_conv[i])
            oa_p, st_p = gdn(qkv, small, gate, gdn_conv[i], gpar, nw, jnp.zeros((BATCH, 8, GDN_QKV), F32),
                             jnp.zeros((BATCH, GDN_HEADS, GDN_DK, GDN_DK), F32), 0, BATCH, SEQ, 1, 256, GDN_CHUNK)
            oa_s, st_s = gdn(qkv, small, gate, gdn_conv[i], gpar, nw, cb_s, state_gdn[i], RP, DEC_BATCH, DEC_SEQ,
                             8, DEC_SEQ, math.gcd(DEC_SEQ, GDN_CHUNK))
            cw = _prep_cmp(cmp_pos_k[i], cmp_w1_k[i], cmp_w2_k[i]) + _prep_cmp(cmp_pos_v[i], cmp_w1_v[i], cmp_w2_v[i])
            kcmp_p, vcmp_p = compress_prompt(nkv, cw, BATCH, SEQ)
            ob_p = nsa_prompt(nq, small, bf(nkv[:RP, :512]), kcmp_p, vcmp_p, BATCH, SEQ)
            kcmp_s, vcmp_s = compress_sample(page_table, cck, ccv, i, cw)
            ob_s = nsa_sample(page_table, csk, csv, i, nq, small, nkv, kcmp_s, vcmp_s, cwk, cwv)
            w_out = bf(w_out_ab[i])
            x = matmul_residual([(oa_p, oa_s, w_out[:512]), (ob_p, ob_s, w_out[512:])], x, 512)
            conv_p = jnp.stack([qkv[(n + 1) * SEQ - 3:(n + 1) * SEQ] for n in range(BATCH)])
            qkv_s = qkv[RP:].reshape(DEC_BATCH, DEC_SEQ, GDN_QKV)
            col = lambda a, c: a[:, c * 128:(c + 1) * 128]
            nkv_p, nkv_s = nkv[:RP], nkv[RP:]
            hp = lambda c: _heads(col(nkv_p, c), BATCH, SEQ)
            hs = lambda c: _heads(col(nkv_s, c), DEC_BATCH, DEC_SEQ)
            ab_p.append((conv_p, st_p, hp(4), hp(5), hp(0), hp(1), hp(2)[:, SEQ - WINDOW:], hp(3)[:, SEQ - WINDOW:]))
            ab_s.append((qkv_s[:, DEC_SEQ - 3:], st_s, hs(4), hs(5), hs(0), hs(1),
                         jnp.concatenate([cache_win_k[i][:, DEC_SEQ:], hs(2)], axis=1),
                         jnp.concatenate([cache_win_v[i][:, DEC_SEQ:], hs(3)], axis=1)))
        else:
            (u,) = norm_matmul(x, norm_mix[l], bf(w_in_c[i]), (D_MODEL,), 512)
            sp = _prep_s5(s5_a_re[i], s5_a_im[i], s5_b_re[i], s5_b_im[i], s5_c_re[i], s5_c_im[i], s5_d[i], s5_log_dt[i])
            z0 = jnp.zeros((BATCH, 1, S5_STATE), F32)
            y_p, fr_p, fi_p = s5_scan(u, sp, z0, z0, 0, BATCH, SEQ, True)
            per_tile = S5_TILE // DEC_SEQ
            x0r = state_s5_re[i].reshape(RS // S5_TILE, per_tile, S5_STATE)
            x0i = state_s5_im[i].reshape(RS // S5_TILE, per_tile, S5_STATE)
            y_s, fr_s, fi_s = s5_scan(u, sp, x0r, x0i, RP, DEC_BATCH, DEC_SEQ, False)
            wg = bf(w_glu[i])
            x = glu_mlp(y_p, norm_mix[l], wg[:, :D_MODEL], wg[:, D_MODEL:], bf(w_out_c[i]), x, 1024, 256, False, False, 1.0,
                        src_s=y_s)
            c_p.append((fr_p.reshape(BATCH, S5_GROUPS, S5_P), fi_p.reshape(BATCH, S5_GROUPS, S5_P)))
            c_s.append((fr_s.reshape(DEC_BATCH, S5_GROUPS, S5_P), fi_s.reshape(DEC_BATCH, S5_GROUPS, S5_P)))
        (q,) = norm_matmul(x, norm_xq[l], bf(w_xq[l]), (D_MODEL,), 512)
        o_p = cross_attention(q, memkv, memkv,
                              pl.BlockSpec((None, N_MEM, D_MODEL), functools.partial(lambda n, i, l: (l, n, 0), l=l)),
                              pl.BlockSpec((None, N_MEM, D_MODEL), functools.partial(lambda n, i, l: (l, n, 1), l=l)),
                              0, BATCH, SEQ, 512)
        o_s = cross_attention_cached(q, cache_mem_k, cache_mem_v, l, RP)
        x = matmul_residual([(o_p, o_s, bf(w_xo[l]))], x, 512)
        x = glu_mlp(x, norm_ffn2[l], bf(w_ffn2_gate[l]), bf(w_ffn2_up[l]), bf(w_ffn2_down[l]), None, 1024, 256, True, True, 0.5)

    y = rmsnorm_rows(x, norm_final, 512)
    st = lambda grp, j: jnp.stack([t[j] for t in grp])
    return (y[:RP].reshape(BATCH, SEQ, D_MODEL), y[RP:].reshape(DEC_BATCH, DEC_SEQ, D_MODEL),
            mem_k_prompt, mem_v_prompt,
            st(ab_p, 1), st(ab_s, 1), st(ab_p, 0), st(ab_s, 0),
            st(ab_p, 2), st(ab_p, 3), st(ab_p, 4), st(ab_p, 5),
            st(ab_s, 2), st(ab_s, 3), st(ab_s, 4), st(ab_s, 5),
            st(ab_p, 6), st(ab_p, 7), st(ab_s, 6), st(ab_s, 7),
            st(c_p, 0), st(c_p, 1), st(c_s, 0), st(c_s, 1))
```

```python
import functools
import math

import jax
import jax.numpy as jnp
from jax import lax
from jax.experimental import pallas as pl
from jax.experimental.pallas import tpu as pltpu

F32 = jnp.float32
BF16 = jnp.bfloat16
HIGHEST = lax.Precision.HIGHEST

D_MODEL = 1024
BATCH = 2
SEQ = 8192
DEPTH = 4
DEC_BATCH = 128
DEC_SEQ = 8
PAST_LEN = 2048
PAGE_SIZE = 128
N_PAGES = PAST_LEN // PAGE_SIZE
RP = BATCH * SEQ
RS = DEC_BATCH * DEC_SEQ
ROWS = RP + RS

GDN_HEADS = 4
GDN_DK = 128
GDN_QKV = 1536
GDN_CONV = 4
GDN_CHUNK = 64
NSA_HEADS = 8
NSA_KV = 2
NSA_HD = 64
NSA_REP = 4
CMP_STRIDE = 16
CMP_LEN = 32
CMP_HIDDEN = 128
SEL_BLOCK = 64
SEL_TOPN = 16
WINDOW = 512
Q_BLOCK = 128
SEL_FORCED = 1.0e4
NEG = -1.0e30
SEL_MASK = 2.0 ** 100
S5_CH = 16
S5_GROUPS = 64
S5_P = 64
S5_STATE = S5_GROUPS * S5_P
S5_MAX_RE = -1.0e-4
N_MEM = 256
X_HEADS = 4
X_HD = 256
D_FF = 2816
EPS = 1.0e-6

VMEM_LIMIT = 56 * 1024 * 1024


def _cp(*sem):
    return pltpu.CompilerParams(dimension_semantics=sem, vmem_limit_bytes=VMEM_LIMIT)


def _dot(a, b, precision=None):
    return jnp.dot(a, b, preferred_element_type=F32, precision=precision)


def _dot_nt(a, b, precision=None):
    return lax.dot_general(a, b, (((1,), (1,)), ((), ())), preferred_element_type=F32, precision=precision)


def _sigmoid(x):
    return 1.0 / (1.0 + jnp.exp(-x))


def _silu(x):
    return x * _sigmoid(x)


def _rms(x, g):
    return x * lax.rsqrt(jnp.mean(x * x, axis=-1, keepdims=True) + EPS) * g


def _norm_matmul_kernel(x_ref, g_ref, w_ref, *o_refs, norm, splits):
    x = x_ref[...]
    if norm:
        x = _rms(x, g_ref[...])
    h = x.astype(BF16)
    off = 0
    for o_ref, wd in zip(o_refs, splits):
        o_ref[...] = _dot(h, w_ref[:, off:off + wd])
        off += wd


def norm_matmul(x, g, w, splits, tm, norm=True, row_off=0, rows=None):
    k = x.shape[1]
    rows = x.shape[0] if rows is None else rows
    n = w.shape[1]
    off = row_off // tm
    assert sum(splits) == n and rows % tm == 0 and row_off % tm == 0
    outs = pl.pallas_call(
        functools.partial(_norm_matmul_kernel, norm=norm, splits=tuple(splits)),
        grid=(rows // tm,),
        in_specs=[pl.BlockSpec((tm, k), lambda i: (off + i, 0)),
                  pl.BlockSpec((1, k), lambda i: (0, 0)),
                  pl.BlockSpec((k, n), lambda i: (0, 0))],
        out_specs=[pl.BlockSpec((tm, wd), lambda i: (i, 0)) for wd in splits],
        out_shape=[jax.ShapeDtypeStruct((rows, wd), F32) for wd in splits],
        compiler_params=_cp("parallel"),
    )(x, g.reshape(1, k), w)
    return outs


def _mem_kv_kernel(x_ref, g_ref, w_ref, o_ref):
    h = _rms(x_ref[...], g_ref[0]).astype(BF16)
    o_ref[0] = _dot(h, w_ref[0])


def mem_kv_all(mem2d, g, w):
    m, k = mem2d.shape
    nl, _, n = w.shape
    return pl.pallas_call(
        _mem_kv_kernel,
        grid=(nl,),
        in_specs=[pl.BlockSpec((m, k), lambda l: (0, 0)),
                  pl.BlockSpec((1, 1, k), lambda l: (l, 0, 0)),
                  pl.BlockSpec((1, k, n), lambda l: (l, 0, 0))],
        out_specs=pl.BlockSpec((1, m, n), lambda l: (l, 0, 0)),
        out_shape=jax.ShapeDtypeStruct((nl, m, n), F32),
        compiler_params=_cp("parallel"),
    )(mem2d, g.reshape(nl, 1, k), w)


def _matmul_res_kernel(*refs, n_terms, n_p):
    r_ref, o_ref = refs[3 * n_terms], refs[3 * n_terms + 1]
    i = pl.program_id(0)

    def run(sel):
        acc = r_ref[...]
        for t in range(n_terms):
            acc = acc + _dot(refs[3 * t + sel][...].astype(BF16), refs[3 * t + 2][...])
        o_ref[...] = acc

    @pl.when(i < n_p)
    def _():
        run(0)

    @pl.when(i >= n_p)
    def _():
        run(1)


def matmul_residual(terms, res, tm):
    rows, n = res.shape
    n_p = RP // tm
    args, specs = [], []
    for a_p, a_s, w in terms:
        k = w.shape[0]
        args += [a_p, a_s, w]
        specs += [pl.BlockSpec((tm, k), lambda i: (jnp.minimum(i, n_p - 1), 0)),
                  pl.BlockSpec((tm, k), lambda i: (jnp.maximum(i - n_p, 0), 0)),
                  pl.BlockSpec((k, n), lambda i: (0, 0))]
    return pl.pallas_call(
        functools.partial(_matmul_res_kernel, n_terms=len(terms), n_p=n_p),
        grid=(rows // tm,),
        in_specs=specs + [pl.BlockSpec((tm, n), lambda i: (i, 0))],
        out_specs=pl.BlockSpec((tm, n), lambda i: (i, 0)),
        out_shape=jax.ShapeDtypeStruct((rows, n), F32),
        compiler_params=_cp("arbitrary"),
    )(*args, res)


def _glu_mlp_kernel(*refs, norm, swiglu, scale, tf, own_res, n_p):
    if own_res:
        src_ref, g_ref, wa_ref, wb_ref, wd_ref, o_ref = refs
        res_ref = src_ref
    elif n_p is None:
        src_ref, g_ref, wa_ref, wb_ref, wd_ref, res_ref, o_ref = refs
    else:
        src_ref, srcs_ref, g_ref, wa_ref, wb_ref, wd_ref, res_ref, o_ref, h_scr = refs
    if n_p is None:
        x = src_ref[...]
        if norm:
            x = _rms(x, g_ref[...])
        h = x.astype(BF16)
    else:
        @pl.when(pl.program_id(0) < n_p)
        def _():
            h_scr[...] = src_ref[...].astype(BF16)

        @pl.when(pl.program_id(0) >= n_p)
        def _():
            h_scr[...] = srcs_ref[...].astype(BF16)

        h = h_scr[...]
    acc = None
    for j in range(wa_ref.shape[1] // tf):
        cs = slice(j * tf, (j + 1) * tf)
        a = _dot(h, wa_ref[:, cs])
        b = _dot(h, wb_ref[:, cs])
        s = _silu(a) * b if swiglu else a * _sigmoid(b)
        d = _dot(s.astype(BF16), wd_ref[cs, :])
        acc = d if acc is None else acc + d
    o_ref[...] = res_ref[...] + scale * acc


def glu_mlp(src, g, wa, wb, wd, res, tm, tf, norm, swiglu, scale, src_s=None):
    k = src.shape[1]
    ff = wa.shape[1]
    n = wd.shape[1]
    rows = src.shape[0] if res is None else res.shape[0]
    resident = lambda shape: pl.BlockSpec(shape, lambda i: (0, 0), pipeline_mode=pl.Buffered(1))
    weights = [pl.BlockSpec((1, k), lambda i: (0, 0)), resident((k, ff)), resident((k, ff)), resident((ff, n))]
    n_p, scratch = None, []
    if src_s is None:
        in_specs = [pl.BlockSpec((tm, k), lambda i: (i, 0))] + weights
        args = [src, g.reshape(1, k), wa, wb, wd]
    else:
        assert not norm and res is not None
        n_p = src.shape[0] // tm
        in_specs = [pl.BlockSpec((tm, k), lambda i: (jnp.minimum(i, n_p - 1), 0)),
                    pl.BlockSpec((tm, k), lambda i: (jnp.maximum(i - n_p, 0), 0))] + weights
        args = [src, src_s, g.reshape(1, k), wa, wb, wd]
        scratch = [pltpu.VMEM((tm, k), BF16)]
    if res is not None:
        in_specs.append(pl.BlockSpec((tm, n), lambda i: (i, 0)))
        args.append(res)
    return pl.pallas_call(
        functools.partial(_glu_mlp_kernel, norm=norm, swiglu=swiglu, scale=scale, tf=tf, own_res=res is None, n_p=n_p),
        grid=(rows // tm,),
        in_specs=in_specs,
        out_specs=pl.BlockSpec((tm, n), lambda i: (i, 0)),
        out_shape=jax.ShapeDtypeStruct((rows, n), F32),
        scratch_shapes=scratch,
        compiler_params=_cp("arbitrary" if src_s is not None else "parallel"),
    )(*args)


def _rmsnorm_kernel(x_ref, g_ref, o_ref):
    o_ref[...] = _rms(x_ref[...], g_ref[...])


def rmsnorm_rows(x, g, tm):
    rows, k = x.shape
    return pl.pallas_call(
        _rmsnorm_kernel,
        grid=(rows // tm,),
        in_specs=[pl.BlockSpec((tm, k), lambda i: (i, 0)), pl.BlockSpec((1, k), lambda i: (0, 0))],
        out_specs=pl.BlockSpec((tm, k), lambda i: (i, 0)),
        out_shape=jax.ShapeDtypeStruct((rows, k), F32),
        compiler_params=_cp("parallel"),
    )(x, g.reshape(1, k))


def _head_attention(qh, kh, vh):
    s = _dot_nt((qh * (X_HD ** -0.5)).astype(BF16), kh.astype(BF16))
    p = jnp.exp(s - jnp.max(s, axis=-1, keepdims=True))
    p = p / jnp.sum(p, axis=-1, keepdims=True)
    return _dot(p.astype(BF16), vh.astype(BF16))


def _xattn_kernel(q_ref, k_ref, v_ref, o_ref):
    q = q_ref[...]
    for h in range(X_HEADS):
        sl = slice(h * X_HD, (h + 1) * X_HD)
        o_ref[:, sl] = _head_attention(q[:, sl], k_ref[:, sl], v_ref[:, sl])


def cross_attention(q_all, mk, mv, k_spec, v_spec, row_off, n_seq, t_seq, tq):
    nb = t_seq // tq
    off = row_off // tq
    return pl.pallas_call(
        _xattn_kernel,
        grid=(n_seq, nb),
        in_specs=[pl.BlockSpec((tq, D_MODEL), lambda n, i: (off + n * nb + i, 0)), k_spec, v_spec],
        out_specs=pl.BlockSpec((tq, D_MODEL), lambda n, i: (n * nb + i, 0)),
        out_shape=jax.ShapeDtypeStruct((n_seq * t_seq, D_MODEL), F32),
        compiler_params=_cp("parallel", "parallel"),
    )(q_all, mk, mv)


def _xattn_block_kernel(x_ref, g_ref, wq_ref, wo_ref, k_ref, v_ref, o_ref):
    x = x_ref[...]
    q = _dot(_rms(x, g_ref[...]).astype(BF16), wq_ref[...])
    heads = [(q[:, h * X_HD:(h + 1) * X_HD] * (X_HD ** -0.5)).astype(BF16) for h in range(X_HEADS)]
    ss = [_dot_nt(heads[h], k_ref[:, h * X_HD:(h + 1) * X_HD].astype(BF16)) for h in range(X_HEADS)]
    ps = [jnp.exp(s - jnp.max(s, axis=-1, keepdims=True)) for s in ss]
    ps = [(p / jnp.sum(p, axis=-1, keepdims=True)).astype(BF16) for p in ps]
    o = jnp.concatenate([_dot(ps[h], v_ref[:, h * X_HD:(h + 1) * X_HD].astype(BF16)) for h in range(X_HEADS)], axis=1)
    o_ref[...] = x + _dot(o.astype(BF16), wo_ref[...])


def cross_attention_block(x, g, wq, wo, memkv, layer, tq):
    nb = SEQ // tq
    resident = lambda shape: pl.BlockSpec(shape, lambda n, i: (0, 0), pipeline_mode=pl.Buffered(1))
    return pl.pallas_call(
        _xattn_block_kernel,
        grid=(BATCH, nb),
        in_specs=[pl.BlockSpec((tq, D_MODEL), lambda n, i: (n * nb + i, 0)),
                  pl.BlockSpec((1, D_MODEL), lambda n, i: (0, 0)),
                  resident((D_MODEL, D_MODEL)), resident((D_MODEL, D_MODEL)),
                  pl.BlockSpec((None, N_MEM, D_MODEL), lambda n, i: (layer, n, 0)),
                  pl.BlockSpec((None, N_MEM, D_MODEL), lambda n, i: (layer, n, 1))],
        out_specs=pl.BlockSpec((tq, D_MODEL), lambda n, i: (n * nb + i, 0)),
        out_shape=jax.ShapeDtypeStruct(x.shape, F32),
        input_output_aliases={0: 0},
        compiler_params=_cp("parallel", "parallel"),
    )(x, g.reshape(1, D_MODEL), wq, wo, memkv, memkv)


def _rows_matmul_res_kernel(a_ref, w_ref, x_ref, o_ref):
    o_ref[...] = x_ref[...] + _dot(a_ref[...].astype(BF16), w_ref[...])


def matmul_residual_rows(a, w, x, row_off, tm):
    off = row_off // tm
    k, n = w.shape
    return pl.pallas_call(
        _rows_matmul_res_kernel,
        grid=(a.shape[0] // tm,),
        in_specs=[pl.BlockSpec((tm, k), lambda i: (i, 0)),
                  pl.BlockSpec((k, n), lambda i: (0, 0)),
                  pl.BlockSpec((tm, n), lambda i: (off + i, 0))],
        out_specs=pl.BlockSpec((tm, n), lambda i: (off + i, 0)),
        out_shape=jax.ShapeDtypeStruct(x.shape, F32),
        input_output_aliases={2: 0},
        compiler_params=_cp("parallel"),
    )(a, w, x)


def _xattn_cache_kernel(q_ref, k_hbm, v_hbm, o_ref, kbuf, vbuf, sem, *, layer):
    n = pl.program_id(0)
    slot = n % 2

    def copies(seq, s):
        cs = []
        for h in range(X_HEADS):
            cs.append(pltpu.make_async_copy(k_hbm.at[layer, seq, :, h, :], kbuf.at[s, h], sem.at[s, h]))
            cs.append(pltpu.make_async_copy(v_hbm.at[layer, seq, :, h, :], vbuf.at[s, h], sem.at[s, X_HEADS + h]))
        return cs

    @pl.when(n == 0)
    def _():
        for c in copies(0, 0):
            c.start()

    @pl.when(n + 1 < pl.num_programs(0))
    def _():
        for c in copies(n + 1, 1 - slot):
            c.start()

    for c in copies(n, slot):
        c.wait()
    q = q_ref[...]
    hs = range(X_HEADS)
    ss = [_dot_nt((q[:, h * X_HD:(h + 1) * X_HD] * (X_HD ** -0.5)).astype(BF16), kbuf[slot, h].astype(BF16)) for h in hs]
    ps = [jnp.exp(s - jnp.max(s, axis=-1, keepdims=True)) for s in ss]
    ps = [(p / jnp.sum(p, axis=-1, keepdims=True)).astype(BF16) for p in ps]
    for h in hs:
        o_ref[:, h * X_HD:(h + 1) * X_HD] = _dot(ps[h], vbuf[slot, h].astype(BF16))


def cross_attention_cached(q_all, cache_k, cache_v, layer, row_off):
    off = row_off // DEC_SEQ
    return pl.pallas_call(
        functools.partial(_xattn_cache_kernel, layer=layer),
        grid=(DEC_BATCH,),
        in_specs=[pl.BlockSpec((DEC_SEQ, D_MODEL), lambda n: (off + n, 0)),
                  pl.BlockSpec(memory_space=pl.ANY), pl.BlockSpec(memory_space=pl.ANY)],
        out_specs=pl.BlockSpec((DEC_SEQ, D_MODEL), lambda n: (n, 0)),
        out_shape=jax.ShapeDtypeStruct((DEC_BATCH * DEC_SEQ, D_MODEL), F32),
        scratch_shapes=[pltpu.VMEM((2, X_HEADS, N_MEM, X_HD), F32), pltpu.VMEM((2, X_HEADS, N_MEM, X_HD), F32),
                        pltpu.SemaphoreType.DMA((2, 2 * X_HEADS))],
        compiler_params=_cp("arbitrary"),
    )(q_all, cache_k, cache_v)


def _softplus(x):
    return jnp.maximum(x, 0.0) + jnp.log(1.0 + jnp.exp(-jnp.abs(x)))


def _split3(x, axis):
    hi = x.astype(BF16).astype(F32)
    return jnp.concatenate([hi, hi, x - hi], axis=axis).astype(BF16)


def _split3r(x, axis):
    hi = x.astype(BF16).astype(F32)
    return jnp.concatenate([hi, x - hi, hi], axis=axis).astype(BF16)


def _dot3(a, b):
    return _dot(_split3(a, 1), _split3r(b, 0))


def _cumsum_rows(tri3, g):
    g1 = g.astype(BF16).astype(F32)
    g2 = (g - g1).astype(BF16).astype(F32)
    g3 = g - g1 - g2
    return _dot(tri3, jnp.concatenate([g1, g2, g3], axis=0).astype(BF16))


def _gdn_kernel(qkv_ref, sm_ref, gate_ref, cw_ref, gp_ref, nw_ref, cb_ref, s0_ref, o_ref, sout_ref,
                xbuf, s_scr, *, seqs, rows, chunk):
    i = pl.program_id(1)
    n_chunks = rows // chunk

    @pl.when(i == 0)
    def _():
        xbuf[:, 0:8, :] = cb_ref[...]
        s_scr[...] = s0_ref[...]

    ri = lax.broadcasted_iota(jnp.int32, (chunk, chunk), 0)
    ci = lax.broadcasted_iota(jnp.int32, (chunk, chunk), 1)
    incl = ri >= ci
    strict = ri > ci
    tri = jnp.where(incl, 1.0, 0.0).astype(BF16)
    tri3 = jnp.concatenate([tri, tri, tri], axis=1)
    eye = jnp.where(ri == ci, 1.0, 0.0)
    levels = []
    w = 1
    while w < chunk:
        levels.append(((ri // (2 * w)) == (ci // (2 * w))) & ((ri % (2 * w)) >= w) & ((ci % (2 * w)) < w))
        w *= 2
    nw = nw_ref[...]
    cw = [cw_ref[j:j + 1, :] for j in range(GDN_CONV)]
    a_neg = -jnp.exp(gp_ref[0:1, :])
    dtb = gp_ref[1:2, :]

    probs = []
    for b in range(seqs):
        tok = slice(b * rows, (b + 1) * rows)
        xbuf[b, 8:8 + rows, :] = qkv_ref[tok, :]
        conv = xbuf[b, 5:5 + rows, :] * cw[0]
        for j in range(1, GDN_CONV):
            conv = conv + xbuf[b, 5 + j:5 + j + rows, :] * cw[j]
        tail = xbuf[b, 8 + rows - 3:8 + rows, :]
        xbuf[b, 5:8, :] = tail
        qkv = _silu(conv)
        sm = sm_ref[tok, :]
        beta_all = _sigmoid(sm)
        g_all = a_neg * _softplus(sm + dtb)
        for c in range(n_chunks):
            rs = slice(c * chunk, (c + 1) * chunk)
            gcum = _cumsum_rows(tri3, g_all[rs])
            gcum_t = gcum.T
            for h in range(GDN_HEADS):
                q = qkv[rs, h * GDN_DK:(h + 1) * GDN_DK]
                k = qkv[rs, 512 + h * GDN_DK:512 + (h + 1) * GDN_DK]
                v = qkv[rs, 1024 + h * GDN_DK:1024 + (h + 1) * GDN_DK]
                q = q * lax.rsqrt(jnp.sum(q * q, axis=-1, keepdims=True) + EPS) * (GDN_DK ** -0.5)
                k = k * lax.rsqrt(jnp.sum(k * k, axis=-1, keepdims=True) + EPS)
                beta = beta_all[rs, h:h + 1]
                gc_col = gcum[:, 4 + h:5 + h]
                gc_row = gcum_t[4 + h:5 + h, :]
                g_last = gcum[chunk - 1:chunk, 4 + h:5 + h]
                decay = jnp.where(incl, jnp.exp(jnp.minimum(gc_col - gc_row, 0.0)), 0.0)
                eg = jnp.exp(gc_col)
                kb = k * beta
                k3r = _split3r(k, 1)
                probs.append(dict(
                    b=b, c=c, h=h, decay=decay,
                    kk=_dot_nt(_split3(kb, 1), k3r), qk=_dot_nt(_split3(q, 1), k3r),
                    rhs=jnp.concatenate([v * beta, kb * eg], axis=1), qd=q * eg,
                    kd_t=(k * jnp.exp(g_last - gc_col)).T, gl=jnp.exp(g_last), minv=eye))
    for p in probs:
        p["lmat"] = jnp.where(strict, p["kk"] * p["decay"], 0.0)
        p["qk"] = jnp.where(incl, p["qk"] * p["decay"], 0.0)
    for p in probs:
        p["minv"] = eye - jnp.where(levels[0], p["lmat"], 0.0)
        p["lh"] = p["lmat"].astype(BF16).astype(F32)
        p["ll"] = p["lmat"] - p["lh"]
    for off_blk in levels[1:]:
        for p in probs:
            mh = p["minv"].astype(BF16).astype(F32)
            ml = p["minv"] - mh
            p["mr3"] = jnp.concatenate([mh, ml, mh], axis=0).astype(BF16)
            ch, cl = jnp.where(off_blk, p["lh"], 0.0), jnp.where(off_blk, p["ll"], 0.0)
            p["t"] = _dot(jnp.concatenate([mh, mh, ml], axis=1).astype(BF16),
                          jnp.concatenate([ch, cl, ch], axis=0).astype(BF16))
        for p in probs:
            p["minv"] = p["minv"] - _dot(_split3(p["t"], 1), p["mr3"])
    for p in probs:
        p["uw"] = _dot3(p["minv"], p["rhs"])
    state = {(b, h): s_scr[b, h] for b in range(seqs) for h in range(GDN_HEADS)}
    for c in range(n_chunks):
        cur = [p for p in probs if p["c"] == c]
        for p in cur:
            p["s3r"] = _split3r(state[(p["b"], p["h"])], 0)
            p["v_new"] = p["uw"][:, :GDN_DK] - _dot(_split3(p["uw"][:, GDN_DK:], 1), p["s3r"])
        for p in cur:
            p["o"] = _dot(_split3(p["qd"], 1), p["s3r"]) + _dot3(p["qk"], p["v_new"])
            state[(p["b"], p["h"])] = state[(p["b"], p["h"])] * p["gl"] + _dot3(p["kd_t"], p["v_new"])
    for p in probs:
        b, c, h = p["b"], p["c"], p["h"]
        r0 = b * rows + c * chunk
        o = _rms(p["o"], nw) * _silu(gate_ref[r0:r0 + chunk, h * GDN_DK:(h + 1) * GDN_DK])
        o_ref[r0:r0 + chunk, h * GDN_DK:(h + 1) * GDN_DK] = o
    for (b, h), s in state.items():
        s_scr[b, h] = s

    @pl.when(i == pl.num_programs(1) - 1)
    def _():
        sout_ref[...] = s_scr[...]


def gdn(qkv, small, gate, conv_w, gpar, norm_w, conv_buf8, s0, row_off, n_seq, t_seq, seqs, rows, chunk):
    nb = t_seq // rows
    blk = seqs * rows
    off = row_off // blk
    rmap = lambda n, i: (off + n * nb + i, 0)
    return pl.pallas_call(
        functools.partial(_gdn_kernel, seqs=seqs, rows=rows, chunk=chunk),
        grid=(n_seq // seqs, nb),
        in_specs=[pl.BlockSpec((blk, GDN_QKV), rmap),
                  pl.BlockSpec((blk, 128), rmap),
                  pl.BlockSpec((blk, 512), rmap),
                  pl.BlockSpec((GDN_CONV, GDN_QKV), lambda n, i: (0, 0)),
                  pl.BlockSpec((8, 128), lambda n, i: (0, 0)),
                  pl.BlockSpec((1, GDN_DK), lambda n, i: (0, 0)),
                  pl.BlockSpec((seqs, 8, GDN_QKV), lambda n, i: (n, 0, 0)),
                  pl.BlockSpec((seqs, GDN_HEADS, GDN_DK, GDN_DK), lambda n, i: (n, 0, 0, 0))],
        out_specs=[pl.BlockSpec((blk, 512), lambda n, i: (n * nb + i, 0)),
                   pl.BlockSpec((seqs, GDN_HEADS, GDN_DK, GDN_DK), lambda n, i: (n, 0, 0, 0))],
        out_shape=[jax.ShapeDtypeStruct((n_seq * t_seq, 512), F32),
                   jax.ShapeDtypeStruct((n_seq, GDN_HEADS, GDN_DK, GDN_DK), F32)],
        scratch_shapes=[pltpu.VMEM((seqs, 8 + rows, GDN_QKV), F32),
                        pltpu.VMEM((seqs, GDN_HEADS, GDN_DK, GDN_DK), F32)],
        compiler_params=_cp("parallel", "arbitrary"),
    )(qkv, small, gate, conv_w, gpar, norm_w, conv_buf8, s0)


def _compress_core(xj, posab_ref, w1_ref, w2_ref, m):
    acc = jnp.zeros((m + 8, 4 * CMP_HIDDEN), F32)
    for jj in range(CMP_STRIDE // 2):
        x2 = jnp.concatenate([xj(2 * jj), xj(2 * jj + 1)], axis=1)
        lhs = jnp.concatenate([x2, posab_ref[jj]], axis=0).astype(BF16)
        acc = acc + _dot(lhs, w1_ref[jj])
    top = acc[0:m, 0:2 * CMP_HIDDEN]
    bot = acc[0:m, 2 * CMP_HIDDEN:]
    c = acc[m:m + 1, 0:2 * CMP_HIDDEN] + acc[m + 1:m + 2, 2 * CMP_HIDDEN:]
    h = top + pltpu.roll(bot, m - 1, 0) + c
    out = _dot(_silu(h).astype(BF16), w2_ref[...])
    row = lax.broadcasted_iota(jnp.int32, (m, 2 * NSA_HD), 0)
    return jnp.where(row < m - 1, out, 0.0)


def _compress_prompt_kernel(kc_ref, vc_ref, pk_ref, w1k_ref, w2k_ref, pv_ref, w1v_ref, w2v_ref, ok_ref, ov_ref, *, m):
    ok_ref[0] = _compress_core(lambda j: kc_ref[pl.ds(j, m, stride=CMP_STRIDE), :], pk_ref, w1k_ref, w2k_ref, m)
    ov_ref[0] = _compress_core(lambda j: vc_ref[pl.ds(j, m, stride=CMP_STRIDE), :], pv_ref, w1v_ref, w2v_ref, m)


def _cmp_weight_specs(nmap):
    return [pl.BlockSpec((CMP_STRIDE // 2, 8, 256), nmap(3)),
            pl.BlockSpec((CMP_STRIDE // 2, 256, 4 * CMP_HIDDEN), nmap(3)),
            pl.BlockSpec((2 * CMP_HIDDEN, 2 * NSA_HD), nmap(2))]


def compress_prompt(nkv, cw, n_seq, t_seq):
    m = t_seq // CMP_STRIDE
    zmap = lambda nd: (lambda n: (0,) * nd)
    out = jax.ShapeDtypeStruct((n_seq, m, 2 * NSA_HD), F32)
    return pl.pallas_call(
        functools.partial(_compress_prompt_kernel, m=m),
        grid=(n_seq,),
        in_specs=[pl.BlockSpec((t_seq, 128), lambda n: (n, 4)), pl.BlockSpec((t_seq, 128), lambda n: (n, 5))]
        + _cmp_weight_specs(zmap) + _cmp_weight_specs(zmap),
        out_specs=[pl.BlockSpec((1, m, 2 * NSA_HD), lambda n: (n, 0, 0))] * 2,
        out_shape=[out, out],
        compiler_params=_cp("parallel"),
    )(nkv, nkv, *cw)


def _compress_sample_kernel(pt_ref, *refs, m):
    del pt_ref
    kpages = refs[0:N_PAGES]
    vpages = refs[N_PAGES:2 * N_PAGES]
    pk_ref, w1k_ref, w2k_ref, pv_ref, w1v_ref, w2v_ref, ok_ref, ov_ref, xk_scr, xv_scr = refs[2 * N_PAGES:]
    for p in range(N_PAGES):
        xk_scr[p * PAGE_SIZE:(p + 1) * PAGE_SIZE, :] = kpages[p][...].T
        xv_scr[p * PAGE_SIZE:(p + 1) * PAGE_SIZE, :] = vpages[p][...].T
    ok_ref[0] = _compress_core(lambda j: xk_scr[pl.ds(j, m, stride=CMP_STRIDE), :], pk_ref, w1k_ref, w2k_ref, m)
    ov_ref[0] = _compress_core(lambda j: xv_scr[pl.ds(j, m, stride=CMP_STRIDE), :], pv_ref, w1v_ref, w2v_ref, m)


def _page_specs(layer, nseq=1, b=0):
    return [pl.BlockSpec((None, None, PAGE_SIZE, 128),
                         functools.partial(lambda n, pt, p: (layer, pt[n * nseq + b, p], 0, 0), p=p))
            for p in range(N_PAGES)]


def compress_sample(page_table, cache_k, cache_v, layer, cw):
    m = PAST_LEN // CMP_STRIDE
    zmap = lambda nd: (lambda n, pt: (0,) * nd)
    out = jax.ShapeDtypeStruct((DEC_BATCH, m, 2 * NSA_HD), F32)
    return pl.pallas_call(
        functools.partial(_compress_sample_kernel, m=m),
        grid_spec=pltpu.PrefetchScalarGridSpec(
            num_scalar_prefetch=1, grid=(DEC_BATCH,),
            in_specs=_page_specs(layer) + _page_specs(layer) + _cmp_weight_specs(zmap) + _cmp_weight_specs(zmap),
            out_specs=[pl.BlockSpec((1, m, 2 * NSA_HD), lambda n, pt: (n, 0, 0))] * 2,
            scratch_shapes=[pltpu.VMEM((PAST_LEN, 2 * NSA_HD), F32), pltpu.VMEM((PAST_LEN, 2 * NSA_HD), F32)]),
        out_shape=[out, out],
        compiler_params=_cp("arbitrary"),
    )(page_table, *([cache_k] * N_PAGES), *([cache_v] * N_PAGES), *cw)


def _masked_softmax(s, mask):
    s = jnp.where(mask, s, NEG)
    m = jnp.max(s, axis=-1, keepdims=True)
    p = jnp.where(mask, jnp.exp(s - m), 0.0)
    l = jnp.sum(p, axis=-1, keepdims=True)
    return p / jnp.where(l > 0.0, l, 1.0)


def _stack_heads(nq, g, tq):
    parts = [nq[:, (g * NSA_REP + r) * NSA_HD:(g * NSA_REP + r + 1) * NSA_HD] for r in range(NSA_REP)]
    return (jnp.concatenate(parts, axis=0) * (NSA_HD ** -0.5)).astype(BF16)


def _rep(mask, tq):
    return jnp.concatenate([mask] * NSA_REP, axis=0)


def _cmp_branch(q4, kcmp, vcmp, qpos, tq, n_blk):
    n_cmp = kcmp.shape[0]
    s = _dot_nt(q4, kcmp.astype(BF16))
    cmp_end = lax.broadcasted_iota(jnp.int32, (tq, n_cmp), 1) * CMP_STRIDE + (CMP_LEN - 1)
    p = _masked_softmax(s, _rep(cmp_end <= qpos, tq))
    o = _dot(p.astype(BF16), vcmp.astype(BF16))
    return o, _importance_t(p, tq, n_blk)


def _importance_t(p, tq, n_blk):
    n_cmp = p.shape[1]
    psum = p[0:tq] + p[tq:2 * tq] + p[2 * tq:3 * tq] + p[3 * tq:4 * tq]
    sj = lax.broadcasted_iota(jnp.int32, (n_blk, n_cmp), 0) * SEL_BLOCK
    ci = lax.broadcasted_iota(jnp.int32, (n_blk, n_cmp), 1) * CMP_STRIDE
    ov = jnp.clip(jnp.minimum(ci + CMP_LEN, sj + SEL_BLOCK) - jnp.maximum(ci, sj), 0, CMP_LEN).astype(F32) / CMP_LEN
    ov = ov.astype(BF16)
    p1 = psum.astype(BF16).astype(F32)
    p2 = (psum - p1).astype(BF16).astype(F32)
    p3 = psum - p1 - p2
    return _dot_nt(jnp.concatenate([ov, ov, ov], axis=1), jnp.concatenate([p1, p2, p3], axis=1).astype(BF16))


def _select_blocks(imp_ts, qpos_row, tq, n_blk):
    blk = lax.broadcasted_iota(jnp.int32, (n_blk, tq), 0)
    cur = qpos_row // SEL_BLOCK
    valid = blk <= cur
    forced = valid & ((blk == 0) | (blk >= cur - 1))
    works = [jnp.where(valid, jnp.where(forced, SEL_FORCED, imp_t), NEG) for imp_t in imp_ts]
    sels = [jnp.zeros((n_blk, tq), F32) for _ in imp_ts]
    for _ in range(SEL_TOPN):
        for j in range(len(works)):
            m = jnp.max(works[j], axis=0, keepdims=True)
            idx = jnp.min(jnp.where(works[j] == m, blk, n_blk), axis=0, keepdims=True)
            pick = blk == idx
            sels[j] = jnp.where(pick, 1.0, sels[j])
            works[j] = jnp.where(pick, -jnp.inf, works[j])
    return [s.T for s in sels]


def _expand_sel(sel, first_blk, n_keys):
    n_blk = sel.shape[1]
    bj = lax.broadcasted_iota(jnp.int32, (n_blk, n_keys), 0)
    kb = lax.broadcasted_iota(jnp.int32, (n_blk, n_keys), 1) // SEL_BLOCK + first_blk
    e = jnp.where(bj == kb, 1.0, 0.0).astype(BF16)
    return _dot(sel.astype(BF16), e) > 0.5


def _gate_mix(o_ref, gates, g, o_cmp, o_slc, o_win, tq):
    for r in range(NSA_REP):
        h = g * NSA_REP + r
        rs = slice(r * tq, (r + 1) * tq)
        c0 = 8 + 3 * h
        o = (gates[:, c0:c0 + 1] * o_cmp[rs] + gates[:, c0 + 1:c0 + 2] * o_slc[rs] + gates[:, c0 + 2:c0 + 3] * o_win[rs])
        o_ref[:, h * NSA_HD:(h + 1) * NSA_HD] = o


def _nsa_prompt_kernel(nq_ref, sm_ref, kv_ref, et_ref, kcmp_ref, vcmp_ref, o_ref):
    tq = Q_BLOCK
    i = pl.program_id(1)
    qpos = i * tq + lax.broadcasted_iota(jnp.int32, (tq, 1), 0)
    nq = nq_ref[...]
    gates = _sigmoid(sm_ref[...])
    kc = 512
    qpos_row = i * tq + lax.broadcasted_iota(jnp.int32, (1, tq), 1)
    q4s, o_cmps, imp_ts = [], [], []
    for g in range(NSA_KV):
        gs = slice(g * NSA_HD, (g + 1) * NSA_HD)
        q4s.append(_stack_heads(nq, g, tq))
        o_cmp, imp_t = _cmp_branch(q4s[g], kcmp_ref[0, :, gs], vcmp_ref[0, :, gs], qpos, tq, 128)
        o_cmps.append(o_cmp)
        imp_ts.append(imp_t)
    sels = _select_blocks(imp_ts, qpos_row, tq, 128)
    qas = [jnp.concatenate([_rep(jnp.where(sels[g] > 0.5, 0.0, -SEL_MASK), tq).astype(BF16), q4s[g]], axis=1)
           for g in range(NSA_KV)]

    def slc_step(c, carry, diagonal):
        start = pl.multiple_of(c * kc, kc)
        et = et_ref[pl.ds(start, kc), :]
        ss = [_dot_nt(qas[g], jnp.concatenate([et, kv_ref[pl.ds(start, kc), g * NSA_HD:(g + 1) * NSA_HD]], axis=1))
              for g in range(NSA_KV)]
        if diagonal:
            causal = _rep(start + lax.broadcasted_iota(jnp.int32, (tq, kc), 1) <= qpos, tq)
            ss = [jnp.where(causal, s, -SEL_MASK) for s in ss]
        out = []
        for g in range(NSA_KV):
            m_i, l_i, acc = carry[g]
            m_new = jnp.maximum(m_i, jnp.max(ss[g], axis=-1, keepdims=True))
            alpha = jnp.exp(m_i - m_new)
            p = jnp.exp(ss[g] - m_new)
            l_new = alpha * l_i + jnp.sum(p, axis=-1, keepdims=True)
            v = kv_ref[pl.ds(start, kc), 128 + g * NSA_HD:128 + (g + 1) * NSA_HD]
            out.append((m_new, l_new, alpha * acc + _dot(p.astype(BF16), v)))
        return tuple(out)

    init = tuple((jnp.full((NSA_REP * tq, 1), -3.0e38, F32), jnp.zeros((NSA_REP * tq, 1), F32),
                  jnp.zeros((NSA_REP * tq, NSA_HD), F32)) for _ in range(NSA_KV))
    n_full = i // (kc // tq)
    carry = lax.fori_loop(0, n_full, functools.partial(slc_step, diagonal=False), init)
    fin = slc_step(n_full, carry, True)

    for g in range(NSA_KV):
        q4, o_cmp = q4s[g], o_cmps[g]
        _, l_f, acc = fin[g]
        o_slc = acc / jnp.where(l_f > 0.0, l_f, 1.0)

        nband = WINDOW + tq
        wstart = pl.multiple_of(jnp.maximum(i - WINDOW // tq, 0) * tq, tq)
        kw = kv_ref[pl.ds(wstart, nband), 256 + g * NSA_HD:256 + (g + 1) * NSA_HD]
        vw = kv_ref[pl.ds(wstart, nband), 384 + g * NSA_HD:384 + (g + 1) * NSA_HD]
        d = qpos - (wstart + lax.broadcasted_iota(jnp.int32, (tq, nband), 1))
        pw = _masked_softmax(_dot_nt(q4, kw), _rep((d >= 0) & (d <= WINDOW), tq))
        o_win = _dot(pw.astype(BF16), vw)
        _gate_mix(o_ref, gates, g, o_cmp, o_slc, o_win, tq)


def nsa_prompt(nq, small, kv_bf, kcmp, vcmp, n_seq, t_seq):
    nb = t_seq // Q_BLOCK
    m = kcmp.shape[1]
    key_blk = jnp.arange(t_seq, dtype=jnp.int32)[:, None] // SEL_BLOCK
    et = (key_blk == jnp.arange(128, dtype=jnp.int32)[None, :]).astype(BF16)
    return pl.pallas_call(
        _nsa_prompt_kernel,
        grid=(n_seq, nb),
        in_specs=[pl.BlockSpec((Q_BLOCK, 512), lambda n, i: (n * nb + i, 0)),
                  pl.BlockSpec((Q_BLOCK, 128), lambda n, i: (n * nb + i, 0)),
                  pl.BlockSpec((t_seq, 512), lambda n, i: (n, 0)),
                  pl.BlockSpec((t_seq, 128), lambda n, i: (0, 0)),
                  pl.BlockSpec((1, m, 128), lambda n, i: (n, 0, 0)),
                  pl.BlockSpec((1, m, 128), lambda n, i: (n, 0, 0))],
        out_specs=pl.BlockSpec((Q_BLOCK, 512), lambda n, i: (n * nb + i, 0)),
        out_shape=jax.ShapeDtypeStruct((n_seq * t_seq, 512), F32),
        compiler_params=_cp("parallel", "arbitrary"),
    )(nq, small, kv_bf, et, kcmp, vcmp)


NSA_SAMPLE_SEQS = 2


def _nsa_sample_kernel(pt_ref, *refs):
    del pt_ref
    nseq = NSA_SAMPLE_SEQS
    per = 2 * N_PAGES + 2
    seq_refs = [refs[b * per:(b + 1) * per] for b in range(nseq)]
    nq_ref, sm_ref, kvn_ref, kcmp_ref, vcmp_ref, o_ref = refs[nseq * per:]
    tq = DEC_SEQ
    seqs = range(nseq)
    qpos = PAST_LEN + lax.broadcasted_iota(jnp.int32, (tq, 1), 0)
    qpos_row = PAST_LEN + lax.broadcasted_iota(jnp.int32, (1, tq), 1)
    n_keys = (N_PAGES + 1) * PAGE_SIZE
    n_blk = 40
    nr = NSA_REP * tq
    pad = jnp.zeros((PAGE_SIZE - tq, 128), F32)
    z = jnp.zeros((nr, NSA_HD), BF16)
    both = lambda m: jnp.concatenate([_rep(m, tq)] * NSA_KV, axis=0)
    n_cmp = kcmp_ref.shape[1]
    cmp_ok = both(lax.broadcasted_iota(jnp.int32, (tq, n_cmp), 1) * CMP_STRIDE + (CMP_LEN - 1) <= qpos)
    d = qpos - (PAST_LEN - WINDOW + lax.broadcasted_iota(jnp.int32, (tq, WINDOW + PAGE_SIZE), 1))
    win_ok = both((d >= 0) & (d <= WINDOW))
    causal = lax.broadcasted_iota(jnp.int32, (tq, n_keys), 1) <= qpos

    new, q_bd = [], []
    for b in seqs:
        rows = slice(b * tq, (b + 1) * tq)
        kvn = kvn_ref[rows, :]
        new.append([jnp.concatenate([kvn[:, c * 128:(c + 1) * 128], pad], axis=0).astype(BF16) for c in range(4)])
        nq = nq_ref[rows, :]
        q_bd.append(jnp.concatenate([jnp.concatenate([_stack_heads(nq, 0, tq), z], axis=1),
                                     jnp.concatenate([z, _stack_heads(nq, 1, tq)], axis=1)], axis=0))
    s_c = [_dot_nt(q_bd[b], kcmp_ref[b].astype(BF16)) for b in seqs]
    s = [jnp.concatenate([_dot(q_bd[b], p[...].astype(BF16)) for p in seq_refs[b][:N_PAGES]]
                         + [_dot_nt(q_bd[b], new[b][0])], axis=1) for b in seqs]
    sw = [jnp.concatenate([_dot(q_bd[b], seq_refs[b][2 * N_PAGES][...].astype(BF16)), _dot_nt(q_bd[b], new[b][2])],
                          axis=1) for b in seqs]
    p_c = [_masked_softmax(s_c[b], cmp_ok) for b in seqs]
    o_c = [_dot(p_c[b].astype(BF16), vcmp_ref[b].astype(BF16)) for b in seqs]
    pw = [_masked_softmax(sw[b], win_ok).astype(BF16) for b in seqs]
    o_w = [_dot_nt(pw[b][:, :WINDOW], seq_refs[b][2 * N_PAGES + 1][...].astype(BF16)) + _dot(pw[b][:, WINDOW:], new[b][3])
           for b in seqs]
    sels = _select_blocks([_importance_t(p_c[b][g * nr:(g + 1) * nr], tq, n_blk) for b in seqs for g in range(NSA_KV)],
                          qpos_row, tq, n_blk)
    p = []
    for b in seqs:
        msk = jnp.concatenate([_rep(_expand_sel(sels[b * NSA_KV + g], 0, n_keys) & causal, tq) for g in range(NSA_KV)],
                              axis=0)
        p.append(_masked_softmax(s[b], msk).astype(BF16))
    o_s = [_dot(p[b][:, N_PAGES * PAGE_SIZE:], new[b][1]) for b in seqs]
    for c in range(N_PAGES):
        o_s = [o_s[b] + _dot_nt(p[b][:, c * PAGE_SIZE:(c + 1) * PAGE_SIZE], seq_refs[b][N_PAGES + c][...].astype(BF16))
               for b in seqs]
    for b in seqs:
        gates = _sigmoid(sm_ref[b * tq:(b + 1) * tq, :])
        for g in range(NSA_KV):
            rs, gs = slice(g * nr, (g + 1) * nr), slice(g * NSA_HD, (g + 1) * NSA_HD)
            _gate_mix(o_ref.at[b * tq:(b + 1) * tq], gates, g, o_c[b][rs, gs], o_s[b][rs, gs], o_w[b][rs, gs], tq)


def nsa_sample(page_table, cache_k, cache_v, layer, nq, small, nkv, kcmp, vcmp, win_k, win_v):
    nseq = NSA_SAMPLE_SEQS
    rows = nseq * DEC_SEQ
    off = RP // rows
    rmap = lambda n, pt: (off + n, 0)
    m = kcmp.shape[1]
    specs, args = [], []
    for b in range(nseq):
        specs += _page_specs(layer, nseq, b) + _page_specs(layer, nseq, b)
        args += [cache_k] * N_PAGES + [cache_v] * N_PAGES
        wspec = pl.BlockSpec((None, None, 128, WINDOW), functools.partial(lambda n, pt, b: (layer, n * nseq + b, 0, 0), b=b))
        specs += [wspec, wspec]
        args += [win_k, win_v]
    return pl.pallas_call(
        _nsa_sample_kernel,
        grid_spec=pltpu.PrefetchScalarGridSpec(
            num_scalar_prefetch=1, grid=(DEC_BATCH // nseq,),
            in_specs=specs + [
                pl.BlockSpec((rows, 512), rmap),
                pl.BlockSpec((rows, 128), rmap),
                pl.BlockSpec((rows, 768), rmap),
                pl.BlockSpec((nseq, m, 128), lambda n, pt: (n, 0, 0)),
                pl.BlockSpec((nseq, m, 128), lambda n, pt: (n, 0, 0))],
            out_specs=pl.BlockSpec((rows, 512), lambda n, pt: (n, 0))),
        out_shape=jax.ShapeDtypeStruct((RS, 512), F32),
        compiler_params=_cp("arbitrary"),
    )(page_table, *args, nq, small, nkv, kcmp, vcmp)


S5_TILE = 256
S5_LANES = 2048
S5_BLK = 16


def _gelu_tanh(x):
    return 0.5 * x * (1.0 + jnp.tanh(math.sqrt(2.0 / math.pi) * (x + 0.044715 * (x * x * x))))


def _s5_kernel(u_ref, pm_ref, pt3_ref, bdr_ref, bdi_ref, cr_ref, ci_ref, tab_ref, d_ref, x0r_ref, x0i_ref,
               y_ref, fr_ref, fi_ref, xr_scr, xi_scr, car_scr, *, chained):
    i = pl.program_id(1)
    nb = S5_GROUPS // S5_BLK
    wi = S5_BLK * S5_CH
    ws = S5_BLK * S5_P
    seg = S5_TILE // 8
    u = u_ref[...]
    ub = _dot(pm_ref[...], u.astype(BF16)).astype(BF16)
    for b in range(nb):
        xr_scr[:, b * ws:(b + 1) * ws] = _dot(ub[:, b * wi:(b + 1) * wi], bdr_ref[b])
        xi_scr[:, b * ws:(b + 1) * ws] = _dot(ub[:, b * wi:(b + 1) * wi], bdi_ref[b])

    if chained:
        @pl.when(i == 0)
        def _():
            car_scr[0:1, :] = x0r_ref[0]
            car_scr[1:2, :] = x0i_ref[0]

    def cmad(br, bi, mr, mi, xr, xi):
        return br + mr * xr - mi * xi, bi + mr * xi + mi * xr

    for c in range(S5_STATE // S5_LANES):
        ls = slice(c * S5_LANES, (c + 1) * S5_LANES)
        ar, ai = tab_ref[0, 0:1, ls], tab_ref[1, 0:1, ls]
        if chained:
            xr, xi = jnp.zeros((8, S5_LANES), F32), jnp.zeros((8, S5_LANES), F32)
            for k in range(seg):
                rs = slice(8 * k, 8 * k + 8)
                xr, xi = cmad(xr_scr[rs, ls], xi_scr[rs, ls], ar, ai, xr, xi)
                xr_scr[rs, ls] = xr
                xi_scr[rs, ls] = xi
            yr, yi = xr, xi
            for sh, r0 in ((1, seg), (2, seg + 8), (4, seg + 16)):
                yr, yi = cmad(yr, yi, tab_ref[0, r0:r0 + 8, ls], tab_ref[1, r0:r0 + 8, ls],
                              pltpu.roll(yr, sh, 0), pltpu.roll(yi, sh, 0))
            first = lax.broadcasted_iota(jnp.int32, (8, S5_LANES), 0) == 0
            yr = jnp.where(first, 0.0, pltpu.roll(yr, 1, 0))
            yi = jnp.where(first, 0.0, pltpu.roll(yi, 1, 0))
            sr, si = cmad(yr, yi, tab_ref[0, seg + 24:seg + 32, ls], tab_ref[1, seg + 24:seg + 32, ls],
                          car_scr[0:1, ls], car_scr[1:2, ls])
            for k in range(seg):
                rs = slice(8 * k, 8 * k + 8)
                xr, xi = cmad(xr_scr[rs, ls], xi_scr[rs, ls], tab_ref[0, k:k + 1, ls], tab_ref[1, k:k + 1, ls], sr, si)
                xr_scr[rs, ls] = xr
                xi_scr[rs, ls] = xi
            car_scr[0:1, ls] = xr[7:8, :]
            car_scr[1:2, ls] = xi[7:8, :]
        else:
            for q in range(S5_TILE // 64):
                xr, xi = x0r_ref[0, 8 * q:8 * q + 8, ls], x0i_ref[0, 8 * q:8 * q + 8, ls]
                for t in range(8):
                    rs = slice(8 * (8 * q + t), 8 * (8 * q + t) + 8)
                    xr, xi = cmad(xr_scr[rs, ls], xi_scr[rs, ls], ar, ai, xr, xi)
                    xr_scr[rs, ls] = xr
                    xi_scr[rs, ls] = xi
                fr_ref[0, 8 * q:8 * q + 8, ls] = xr
                fi_ref[0, 8 * q:8 * q + 8, ls] = xi

    ys = []
    for b in range(nb):
        ys.append(_dot(xr_scr[:, b * ws:(b + 1) * ws].astype(BF16), cr_ref[b])
                  - _dot(xi_scr[:, b * ws:(b + 1) * ws].astype(BF16), ci_ref[b]))
    yp = jnp.concatenate(ys, axis=1)
    h1 = yp.astype(BF16).astype(F32)
    h2 = (yp - h1).astype(BF16).astype(F32)
    ylin = _dot(pt3_ref[...], jnp.concatenate([h1, h2, yp - h1 - h2], axis=0).astype(BF16))
    y_ref[...] = _gelu_tanh(ylin + d_ref[...] * u).astype(BF16)

    if chained:
        @pl.when(i == pl.num_programs(1) - 1)
        def _():
            fr_ref[0] = car_scr[0:1, :]
            fi_ref[0] = car_scr[1:2, :]


def s5_scan(u, sp, x0r, x0i, row_off, n_seq, t_seq, chained):
    if chained:
        grid = (n_seq, t_seq // S5_TILE)
        nb = grid[1]
        smap = lambda n, i: (n, 0, 0)
    else:
        grid = (1, n_seq * t_seq // S5_TILE)
        nb = grid[1]
        smap = lambda n, i: (i, 0, 0)
    off = row_off // S5_TILE
    rmap = lambda n, i: (off + n * nb + i, 0)
    sblk = (1,) + x0r.shape[1:]
    const = lambda nd: (lambda n, i: (0,) * nd)
    rho = jnp.arange(S5_TILE)
    if chained:
        src = (S5_TILE // 8) * (rho % 8) + rho // 8
    else:
        src = 64 * (rho // 64) + 8 * (rho % 8) + (rho // 8) % 8
    pm = (src[:, None] == jnp.arange(S5_TILE)[None, :]).astype(BF16)
    pt3 = jnp.tile(pm.T, (1, 3))
    in_specs = [pl.BlockSpec((S5_TILE, D_MODEL), rmap),
                pl.BlockSpec((S5_TILE, S5_TILE), const(2)), pl.BlockSpec((S5_TILE, 3 * S5_TILE), const(2)),
                pl.BlockSpec(sp["bdr"].shape, const(3)), pl.BlockSpec(sp["bdi"].shape, const(3)),
                pl.BlockSpec(sp["cr"].shape, const(3)), pl.BlockSpec(sp["ci"].shape, const(3)),
                pl.BlockSpec(sp["tab"].shape, const(3)), pl.BlockSpec((1, D_MODEL), const(2)),
                pl.BlockSpec(sblk, smap), pl.BlockSpec(sblk, smap)]
    args = [u, pm, pt3, sp["bdr"], sp["bdi"], sp["cr"], sp["ci"], sp["tab"], sp["d"], x0r, x0i]
    return pl.pallas_call(
        functools.partial(_s5_kernel, chained=chained),
        grid=grid,
        in_specs=in_specs,
        out_specs=[pl.BlockSpec((S5_TILE, D_MODEL), lambda n, i: (n * nb + i, 0)),
                   pl.BlockSpec(sblk, smap), pl.BlockSpec(sblk, smap)],
        out_shape=[jax.ShapeDtypeStruct((n_seq * t_seq, D_MODEL), BF16), jax.ShapeDtypeStruct(x0r.shape, F32),
                   jax.ShapeDtypeStruct(x0r.shape, F32)],
        scratch_shapes=[pltpu.VMEM((S5_TILE, S5_STATE), F32), pltpu.VMEM((S5_TILE, S5_STATE), F32),
                        pltpu.VMEM((8, S5_STATE), F32)],
        compiler_params=_cp("arbitrary", "arbitrary"),
    )(*args)


def _prep_ab_in(w):
    qkv, b, a, gate, nq, nkv, ngate = jnp.split(w, (1536, 1540, 1544, 2056, 2568, 3336), axis=-1)
    kc, vc, ks, vs, kw, vw = jnp.split(nkv, 6, axis=-1)
    small = jnp.concatenate([b, a, ngate, jnp.zeros(w.shape[:-1] + (96,), w.dtype)], axis=-1)
    return jnp.concatenate([qkv, ks, vs, kw, vw, kc, vc, gate, nq, small], axis=-1).astype(BF16)


AB_SPLITS = (1536, 768, 512, 512, 128)


def _prep_cmp(pos, w1, w2):
    top = w1[:CMP_STRIDE * NSA_HD].reshape(CMP_STRIDE, NSA_HD, CMP_HIDDEN)
    bot = w1[CMP_STRIDE * NSA_HD:].reshape(CMP_STRIDE, NSA_HD, CMP_HIDDEN)
    w1bd = jnp.zeros((CMP_STRIDE, 2 * NSA_HD, 4 * CMP_HIDDEN), F32)
    w2bd = jnp.zeros((2 * CMP_HIDDEN, 2 * NSA_HD), F32)
    for g in range(NSA_KV):
        rs = slice(g * NSA_HD, (g + 1) * NSA_HD)
        w1bd = w1bd.at[:, rs, g * CMP_HIDDEN:(g + 1) * CMP_HIDDEN].set(top)
        w1bd = w1bd.at[:, rs, (2 + g) * CMP_HIDDEN:(3 + g) * CMP_HIDDEN].set(bot)
        w2bd = w2bd.at[g * CMP_HIDDEN:(g + 1) * CMP_HIDDEN, rs].set(w2)
    posab = jnp.zeros((CMP_STRIDE, 8, 2 * NSA_HD), F32)
    posab = posab.at[:, 0, :].set(jnp.tile(pos[:CMP_STRIDE], (1, NSA_KV)))
    posab = posab.at[:, 1, :].set(jnp.tile(pos[CMP_STRIDE:], (1, NSA_KV)))
    half = CMP_STRIDE // 2
    posab = posab.reshape(half, 2, 8, 2 * NSA_HD).transpose(0, 2, 1, 3).reshape(half, 8, 4 * NSA_HD)
    w1bd = w1bd.reshape(half, 4 * NSA_HD, 4 * CMP_HIDDEN)
    return posab, w1bd.astype(BF16), w2bd.astype(BF16)


def _cmul(ar, ai, br, bi):
    return ar * br - ai * bi, ar * bi + ai * br


def _prep_s5(a_re, a_im, b_re, b_im, c_re, c_im, d, log_dt):
    dt = jnp.exp(log_dt)[:, None]
    lr = jnp.minimum(a_re, S5_MAX_RE)
    li = a_im
    mag = jnp.exp(lr * dt)
    ar = mag * jnp.cos(li * dt)
    ai = mag * jnp.sin(li * dt)
    den = lr * lr + li * li
    fr = ((ar - 1.0) * lr + ai * li) / den
    fi = (ai * lr - (ar - 1.0) * li) / den
    bbr = fr[..., None] * b_re - fi[..., None] * b_im
    bbi = fr[..., None] * b_im + fi[..., None] * b_re
    nb = S5_GROUPS // S5_BLK
    eye = jnp.eye(S5_BLK, dtype=F32)

    def bd_in(m):
        m4 = jnp.swapaxes(m, 1, 2).reshape(nb, S5_BLK, S5_CH, S5_P)
        return jnp.einsum('bgcp,gh->bgchp', m4, eye).reshape(nb, S5_BLK * S5_CH, S5_BLK * S5_P).astype(BF16)

    def bd_out(m):
        m4 = jnp.swapaxes(m, 1, 2).reshape(nb, S5_BLK, S5_P, S5_CH)
        return jnp.einsum('bgpc,gh->bgphc', m4, eye).reshape(nb, S5_BLK * S5_P, S5_BLK * S5_CH).astype(BF16)

    seg = S5_TILE // 8
    a1 = (ar.reshape(1, S5_STATE), ai.reshape(1, S5_STATE))
    pw = [a1]
    for _ in range(seg - 1):
        pw.append(_cmul(*pw[-1], *a1))
    row = jnp.arange(8)[:, None]
    s1 = pw[-1]
    s2 = _cmul(*s1, *s1)
    s4 = _cmul(*s2, *s2)
    sp = [(jnp.ones_like(a1[0]), jnp.zeros_like(a1[1]))]
    for _ in range(7):
        sp.append(_cmul(*sp[-1], *s1))
    parts = []
    for j in range(2):
        parts.append(jnp.concatenate([p[j] for p in pw]
                                     + [jnp.where(row >= sh, s[j], 0.0) for sh, s in ((1, s1), (2, s2), (4, s4))]
                                     + [p[j] for p in sp], axis=0))
    return {"bdr": bd_in(bbr), "bdi": bd_in(bbi), "cr": bd_out(c_re), "ci": bd_out(c_im),
            "tab": jnp.stack(parts), "d": d.reshape(1, D_MODEL)}


def _heads(a, n, t):
    return a.reshape(n, t, NSA_KV, NSA_HD)


def kernel(x_prompt, x_sample, mem_prompt, cache_mem_k, cache_mem_v, state_gdn, state_gdn_conv, cache_cmp_k, cache_cmp_v, cache_slc_k, cache_slc_v, cache_win_k, cache_win_v, state_s5_re, state_s5_im, page_table, norm_ffn1, w_ffn1_gate, w_ffn1_up, w_ffn1_down, norm_mix, norm_xq, norm_mem, w_xq, w_xk, w_xv, w_xo, norm_ffn2, w_ffn2_gate, w_ffn2_up, w_ffn2_down, norm_final, w_in_ab, w_out_ab, gdn_conv, gdn_a_log, gdn_dt_bias, gdn_norm, cmp_pos_k, cmp_w1_k, cmp_w2_k, cmp_pos_v, cmp_w1_v, cmp_w2_v, w_in_c, s5_a_re, s5_a_im, s5_b_re, s5_b_im, s5_c_re, s5_c_im, s5_d, s5_log_dt, w_glu, w_out_c):
    bf = lambda w: w.astype(BF16)
    n_ab = w_in_ab.shape[0]
    n_pool = cache_cmp_k.shape[1]
    x = jnp.concatenate([x_prompt.reshape(RP, D_MODEL), x_sample.reshape(RS, D_MODEL)], axis=0)

    memkv = mem_kv_all(mem_prompt.reshape(BATCH * N_MEM, D_MODEL), norm_mem, bf(jnp.concatenate([w_xk, w_xv], axis=-1)))
    mem_k_prompt = memkv[:, :, :D_MODEL].reshape(DEPTH, BATCH, N_MEM, X_HEADS, X_HD)
    mem_v_prompt = memkv[:, :, D_MODEL:].reshape(DEPTH, BATCH, N_MEM, X_HEADS, X_HD)

    fmaj = lambda c: jnp.transpose(c, (0, 1, 3, 4, 2)).reshape(c.shape[0], c.shape[1], NSA_KV * NSA_HD, c.shape[2])
    cck, ccv = fmaj(cache_cmp_k), fmaj(cache_cmp_v)
    csk, csv, cwk, cwv = fmaj(cache_slc_k), fmaj(cache_slc_v), fmaj(cache_win_k), fmaj(cache_win_v)
    ab_p, ab_s, c_p, c_s = [], [], [], []
    for l in range(DEPTH):
        i = l // 2
        x = glu_mlp(x, norm_ffn1[l], bf(w_ffn1_gate[l]), bf(w_ffn1_up[l]), bf(w_ffn1_down[l]), None, 1024, 256, True, True, 0.5)
        if l % 2 == 0:
            qkv, nkv, gate, nq, small = norm_matmul(x, norm_mix[l], _prep_ab_in(w_in_ab[i]), AB_SPLITS, 512)
            gpar = jnp.zeros((8, 128), F32).at[0, 4:8].set(gdn_a_log[i]).at[1, 4:8].set(gdn_dt_bias[i])
            nw = gdn_norm[i].reshape(1, GDN_DK)
            cb_s = jnp.zeros((DEC_BATCH, 8, GDN_QKV), F32).at[:, 5:8].set(state_gdn_conv[i])
            oa_p, st_p = gdn(qkv, small, gate, gdn_conv[i], gpar, nw, jnp.zeros((BATCH, 8, GDN_QKV), F32),
                             jnp.zeros((BATCH, GDN_HEADS, GDN_DK, GDN_DK), F32), 0, BATCH, SEQ, 1, 256, GDN_CHUNK)
            oa_s, st_s = gdn(qkv, small, gate, gdn_conv[i], gpar, nw, cb_s, state_gdn[i], RP, DEC_BATCH, DEC_SEQ,
                             8, DEC_SEQ, math.gcd(DEC_SEQ, GDN_CHUNK))
            cw = _prep_cmp(cmp_pos_k[i], cmp_w1_k[i], cmp_w2_k[i]) + _prep_cmp(cmp_pos_v[i], cmp_w1_v[i], cmp_w2_v[i])
            kcmp_p, vcmp_p = compress_prompt(nkv, cw, BATCH, SEQ)
            ob_p = nsa_prompt(nq, small, bf(nkv[:RP, :512]), kcmp_p, vcmp_p, BATCH, SEQ)
            kcmp_s, vcmp_s = compress_sample(page_table, cck, ccv, i, cw)
            ob_s = nsa_sample(page_table, csk, csv, i, nq, small, nkv, kcmp_s, vcmp_s, cwk, cwv)
            w_out = bf(w_out_ab[i])
            x = matmul_residual([(oa_p, oa_s, w_out[:512]), (ob_p, ob_s, w_out[512:])], x, 512)
            conv_p = jnp.stack([qkv[(n + 1) * SEQ - 3:(n + 1) * SEQ] for n in range(BATCH)])
            qkv_s = qkv[RP:].reshape(DEC_BATCH, DEC_SEQ, GDN_QKV)
            col = lambda a, c: a[:, c * 128:(c + 1) * 128]
            nkv_p, nkv_s = nkv[:RP], nkv[RP:]
            hp = lambda c: _heads(col(nkv_p, c), BATCH, SEQ)
            hs = lambda c: _heads(col(nkv_s, c), DEC_BATCH, DEC_SEQ)
            ab_p.append((conv_p, st_p, hp(4), hp(5), hp(0), hp(1), hp(2)[:, SEQ - WINDOW:], hp(3)[:, SEQ - WINDOW:]))
            ab_s.append((qkv_s[:, DEC_SEQ - 3:], st_s, hs(4), hs(5), hs(0), hs(1),
                         jnp.concatenate([cache_win_k[i][:, DEC_SEQ:], hs(2)], axis=1),
                         jnp.concatenate([cache_win_v[i][:, DEC_SEQ:], hs(3)], axis=1)))
        else:
            (u,) = norm_matmul(x, norm_mix[l], bf(w_in_c[i]), (D_MODEL,), 512)
            sp = _prep_s5(s5_a_re[i], s5_a_im[i], s5_b_re[i], s5_b_im[i], s5_c_re[i], s5_c_im[i], s5_d[i], s5_log_dt[i])
            z0 = jnp.zeros((BATCH, 1, S5_STATE), F32)
            y_p, fr_p, fi_p = s5_scan(u, sp, z0, z0, 0, BATCH, SEQ, True)
            per_tile = S5_TILE // DEC_SEQ
            x0r = state_s5_re[i].reshape(RS // S5_TILE, per_tile, S5_STATE)
            x0i = state_s5_im[i].reshape(RS // S5_TILE, per_tile, S5_STATE)
            y_s, fr_s, fi_s = s5_scan(u, sp, x0r, x0i, RP, DEC_BATCH, DEC_SEQ, False)
            wg = bf(w_glu[i])
            x = glu_mlp(y_p, norm_mix[l], wg[:, :D_MODEL], wg[:, D_MODEL:], bf(w_out_c[i]), x, 1024, 256, False, False, 1.0,
                        src_s=y_s)
            c_p.append((fr_p.reshape(BATCH, S5_GROUPS, S5_P), fi_p.reshape(BATCH, S5_GROUPS, S5_P)))
            c_s.append((fr_s.reshape(DEC_BATCH, S5_GROUPS, S5_P), fi_s.reshape(DEC_BATCH, S5_GROUPS, S5_P)))
        wq, wo = bf(w_xq[l]), bf(w_xo[l])
        (q_s,) = norm_matmul(x, norm_xq[l], wq, (D_MODEL,), 512, row_off=RP, rows=RS)
        o_s = cross_attention_cached(q_s, cache_mem_k, cache_mem_v, l, 0)
        x = cross_attention_block(x, norm_xq[l], wq, wo, memkv, l, 512)
        x = matmul_residual_rows(o_s, wo, x, RP, 512)
        x = glu_mlp(x, norm_ffn2[l], bf(w_ffn2_gate[l]), bf(w_ffn2_up[l]), bf(w_ffn2_down[l]), None, 1024, 256, True, True, 0.5)

    y = rmsnorm_rows(x, norm_final, 512)
    st = lambda grp, j: jnp.stack([t[j] for t in grp])
    return (y[:RP].reshape(BATCH, SEQ, D_MODEL), y[RP:].reshape(DEC_BATCH, DEC_SEQ, D_MODEL),
            mem_k_prompt, mem_v_prompt,
            st(ab_p, 1), st(ab_s, 1), st(ab_p, 0), st(ab_s, 0),
            st(ab_p, 2), st(ab_p, 3), st(ab_p, 4), st(ab_p, 5),
            st(ab_s, 2), st(ab_s, 3), st(ab_s, 4), st(ab_s, 5),
            st(ab_p, 6), st(ab_p, 7), st(ab_s, 6), st(ab_s, 7),
            st(c_p, 0), st(c_p, 1), st(c_s, 0), st(c_s, 1))
```

```python
import functools
import math

import jax
import jax.numpy as jnp
from jax import lax
from jax.experimental import pallas as pl
from jax.experimental.pallas import tpu as pltpu

F32 = jnp.float32
BF16 = jnp.bfloat16
HIGHEST = lax.Precision.HIGHEST

D_MODEL = 1024
BATCH = 2
SEQ = 8192
DEPTH = 4
DEC_BATCH = 128
DEC_SEQ = 8
PAST_LEN = 2048
PAGE_SIZE = 128
N_PAGES = PAST_LEN // PAGE_SIZE
RP = BATCH * SEQ
RS = DEC_BATCH * DEC_SEQ
ROWS = RP + RS

GDN_HEADS = 4
GDN_DK = 128
GDN_QKV = 1536
GDN_CONV = 4
GDN_CHUNK = 64
NSA_HEADS = 8
NSA_KV = 2
NSA_HD = 64
NSA_REP = 4
CMP_STRIDE = 16
CMP_LEN = 32
CMP_HIDDEN = 128
SEL_BLOCK = 64
SEL_TOPN = 16
WINDOW = 512
Q_BLOCK = 128
SEL_FORCED = 1.0e4
NEG = -1.0e30
SEL_MASK = 2.0 ** 100
S5_CH = 16
S5_GROUPS = 64
S5_P = 64
S5_STATE = S5_GROUPS * S5_P
S5_MAX_RE = -1.0e-4
N_MEM = 256
X_HEADS = 4
X_HD = 256
D_FF = 2816
EPS = 1.0e-6

VMEM_LIMIT = 56 * 1024 * 1024


def _cp(*sem):
    return pltpu.CompilerParams(dimension_semantics=sem, vmem_limit_bytes=VMEM_LIMIT)


def _dot(a, b, precision=None):
    return jnp.dot(a, b, preferred_element_type=F32, precision=precision)


def _dot_nt(a, b, precision=None):
    return lax.dot_general(a, b, (((1,), (1,)), ((), ())), preferred_element_type=F32, precision=precision)


def _sigmoid(x):
    return 1.0 / (1.0 + jnp.exp(-x))


def _silu(x):
    return x * _sigmoid(x)


def _rms(x, g):
    return x * lax.rsqrt(jnp.mean(x * x, axis=-1, keepdims=True) + EPS) * g


def _norm_matmul_kernel(x_ref, g_ref, w_ref, *o_refs, norm, splits):
    x = x_ref[...]
    if norm:
        x = _rms(x, g_ref[...])
    h = x.astype(BF16)
    off = 0
    for o_ref, wd in zip(o_refs, splits):
        o_ref[...] = _dot(h, w_ref[:, off:off + wd])
        off += wd


def norm_matmul(x, g, w, splits, tm, norm=True, row_off=0, rows=None):
    k = x.shape[1]
    rows = x.shape[0] if rows is None else rows
    n = w.shape[1]
    off = row_off // tm
    assert sum(splits) == n and rows % tm == 0 and row_off % tm == 0
    outs = pl.pallas_call(
        functools.partial(_norm_matmul_kernel, norm=norm, splits=tuple(splits)),
        grid=(rows // tm,),
        in_specs=[pl.BlockSpec((tm, k), lambda i: (off + i, 0)),
                  pl.BlockSpec((1, k), lambda i: (0, 0)),
                  pl.BlockSpec((k, n), lambda i: (0, 0))],
        out_specs=[pl.BlockSpec((tm, wd), lambda i: (i, 0)) for wd in splits],
        out_shape=[jax.ShapeDtypeStruct((rows, wd), F32) for wd in splits],
        compiler_params=_cp("parallel"),
    )(x, g.reshape(1, k), w)
    return outs


def _mem_kv_kernel(x_ref, g_ref, w_ref, o_ref):
    h = _rms(x_ref[...], g_ref[0]).astype(BF16)
    o_ref[0] = _dot(h, w_ref[0])


def mem_kv_all(mem2d, g, w):
    m, k = mem2d.shape
    nl, _, n = w.shape
    return pl.pallas_call(
        _mem_kv_kernel,
        grid=(nl,),
        in_specs=[pl.BlockSpec((m, k), lambda l: (0, 0)),
                  pl.BlockSpec((1, 1, k), lambda l: (l, 0, 0)),
                  pl.BlockSpec((1, k, n), lambda l: (l, 0, 0))],
        out_specs=pl.BlockSpec((1, m, n), lambda l: (l, 0, 0)),
        out_shape=jax.ShapeDtypeStruct((nl, m, n), F32),
        compiler_params=_cp("parallel"),
    )(mem2d, g.reshape(nl, 1, k), w)


def _matmul_res_kernel(*refs, n_terms, n_p):
    r_ref, o_ref = refs[3 * n_terms], refs[3 * n_terms + 1]
    i = pl.program_id(0)

    def run(sel):
        acc = r_ref[...]
        for t in range(n_terms):
            acc = acc + _dot(refs[3 * t + sel][...].astype(BF16), refs[3 * t + 2][...])
        o_ref[...] = acc

    @pl.when(i < n_p)
    def _():
        run(0)

    @pl.when(i >= n_p)
    def _():
        run(1)


def matmul_residual(terms, res, tm):
    rows, n = res.shape
    n_p = RP // tm
    args, specs = [], []
    for a_p, a_s, w in terms:
        k = w.shape[0]
        args += [a_p, a_s, w]
        specs += [pl.BlockSpec((tm, k), lambda i: (jnp.minimum(i, n_p - 1), 0)),
                  pl.BlockSpec((tm, k), lambda i: (jnp.maximum(i - n_p, 0), 0)),
                  pl.BlockSpec((k, n), lambda i: (0, 0))]
    return pl.pallas_call(
        functools.partial(_matmul_res_kernel, n_terms=len(terms), n_p=n_p),
        grid=(rows // tm,),
        in_specs=specs + [pl.BlockSpec((tm, n), lambda i: (i, 0))],
        out_specs=pl.BlockSpec((tm, n), lambda i: (i, 0)),
        out_shape=jax.ShapeDtypeStruct((rows, n), F32),
        compiler_params=_cp("arbitrary"),
    )(*args, res)


def _glu_mlp_kernel(*refs, norm, swiglu, scale, tf, own_res, n_p):
    if own_res:
        src_ref, g_ref, wa_ref, wb_ref, wd_ref, o_ref = refs
        res_ref = src_ref
    elif n_p is None:
        src_ref, g_ref, wa_ref, wb_ref, wd_ref, res_ref, o_ref = refs
    else:
        src_ref, srcs_ref, g_ref, wa_ref, wb_ref, wd_ref, res_ref, o_ref, h_scr = refs
    if n_p is None:
        x = src_ref[...]
        if norm:
            x = _rms(x, g_ref[...])
        h = x.astype(BF16)
    else:
        @pl.when(pl.program_id(0) < n_p)
        def _():
            h_scr[...] = src_ref[...].astype(BF16)

        @pl.when(pl.program_id(0) >= n_p)
        def _():
            h_scr[...] = srcs_ref[...].astype(BF16)

        h = h_scr[...]
    acc = None
    for j in range(wa_ref.shape[1] // tf):
        cs = slice(j * tf, (j + 1) * tf)
        a = _dot(h, wa_ref[:, cs])
        b = _dot(h, wb_ref[:, cs])
        s = _silu(a) * b if swiglu else a * _sigmoid(b)
        d = _dot(s.astype(BF16), wd_ref[cs, :])
        acc = d if acc is None else acc + d
    o_ref[...] = res_ref[...] + scale * acc


def glu_mlp(src, g, wa, wb, wd, res, tm, tf, norm, swiglu, scale, src_s=None):
    k = src.shape[1]
    ff = wa.shape[1]
    n = wd.shape[1]
    rows = src.shape[0] if res is None else res.shape[0]
    resident = lambda shape: pl.BlockSpec(shape, lambda i: (0, 0), pipeline_mode=pl.Buffered(1))
    weights = [pl.BlockSpec((1, k), lambda i: (0, 0)), resident((k, ff)), resident((k, ff)), resident((ff, n))]
    n_p, scratch = None, []
    if src_s is None:
        in_specs = [pl.BlockSpec((tm, k), lambda i: (i, 0))] + weights
        args = [src, g.reshape(1, k), wa, wb, wd]
    else:
        assert not norm and res is not None
        n_p = src.shape[0] // tm
        in_specs = [pl.BlockSpec((tm, k), lambda i: (jnp.minimum(i, n_p - 1), 0)),
                    pl.BlockSpec((tm, k), lambda i: (jnp.maximum(i - n_p, 0), 0))] + weights
        args = [src, src_s, g.reshape(1, k), wa, wb, wd]
        scratch = [pltpu.VMEM((tm, k), BF16)]
    if res is not None:
        in_specs.append(pl.BlockSpec((tm, n), lambda i: (i, 0)))
        args.append(res)
    return pl.pallas_call(
        functools.partial(_glu_mlp_kernel, norm=norm, swiglu=swiglu, scale=scale, tf=tf, own_res=res is None, n_p=n_p),
        grid=(rows // tm,),
        in_specs=in_specs,
        out_specs=pl.BlockSpec((tm, n), lambda i: (i, 0)),
        out_shape=jax.ShapeDtypeStruct((rows, n), F32),
        scratch_shapes=scratch,
        compiler_params=_cp("arbitrary" if src_s is not None else "parallel"),
    )(*args)


def _rmsnorm_kernel(x_ref, g_ref, o_ref):
    o_ref[...] = _rms(x_ref[...], g_ref[...])


def rmsnorm_rows(x, g, tm):
    rows, k = x.shape
    return pl.pallas_call(
        _rmsnorm_kernel,
        grid=(rows // tm,),
        in_specs=[pl.BlockSpec((tm, k), lambda i: (i, 0)), pl.BlockSpec((1, k), lambda i: (0, 0))],
        out_specs=pl.BlockSpec((tm, k), lambda i: (i, 0)),
        out_shape=jax.ShapeDtypeStruct((rows, k), F32),
        compiler_params=_cp("parallel"),
    )(x, g.reshape(1, k))


def _xattn_block_kernel(x_ref, g_ref, wq_ref, wo_ref, k_ref, v_ref, o_ref):
    x = x_ref[...]
    q = _dot(_rms(x, g_ref[...]).astype(BF16), wq_ref[...])
    heads = [(q[:, h * X_HD:(h + 1) * X_HD] * (X_HD ** -0.5)).astype(BF16) for h in range(X_HEADS)]
    ss = [_dot_nt(heads[h], k_ref[:, h * X_HD:(h + 1) * X_HD].astype(BF16)) for h in range(X_HEADS)]
    ps = [jnp.exp(s - jnp.max(s, axis=-1, keepdims=True)) for s in ss]
    ps = [(p / jnp.sum(p, axis=-1, keepdims=True)).astype(BF16) for p in ps]
    o = jnp.concatenate([_dot(ps[h], v_ref[:, h * X_HD:(h + 1) * X_HD].astype(BF16)) for h in range(X_HEADS)], axis=1)
    o_ref[...] = x + _dot(o.astype(BF16), wo_ref[...])


def cross_attention_block(x, g, wq, wo, memkv, layer, tq):
    nb = SEQ // tq
    resident = lambda shape: pl.BlockSpec(shape, lambda n, i: (0, 0), pipeline_mode=pl.Buffered(1))
    return pl.pallas_call(
        _xattn_block_kernel,
        grid=(BATCH, nb),
        in_specs=[pl.BlockSpec((tq, D_MODEL), lambda n, i: (n * nb + i, 0)),
                  pl.BlockSpec((1, D_MODEL), lambda n, i: (0, 0)),
                  resident((D_MODEL, D_MODEL)), resident((D_MODEL, D_MODEL)),
                  pl.BlockSpec((None, N_MEM, D_MODEL), lambda n, i: (layer, n, 0)),
                  pl.BlockSpec((None, N_MEM, D_MODEL), lambda n, i: (layer, n, 1))],
        out_specs=pl.BlockSpec((tq, D_MODEL), lambda n, i: (n * nb + i, 0)),
        out_shape=jax.ShapeDtypeStruct(x.shape, F32),
        input_output_aliases={0: 0},
        compiler_params=_cp("parallel", "parallel"),
    )(x, g.reshape(1, D_MODEL), wq, wo, memkv, memkv)


def _rows_matmul_res_kernel(a_ref, w_ref, x_ref, o_ref):
    o_ref[...] = x_ref[...] + _dot(a_ref[...].astype(BF16), w_ref[...])


def matmul_residual_rows(a, w, x, row_off, tm):
    off = row_off // tm
    k, n = w.shape
    return pl.pallas_call(
        _rows_matmul_res_kernel,
        grid=(a.shape[0] // tm,),
        in_specs=[pl.BlockSpec((tm, k), lambda i: (i, 0)),
                  pl.BlockSpec((k, n), lambda i: (0, 0)),
                  pl.BlockSpec((tm, n), lambda i: (off + i, 0))],
        out_specs=pl.BlockSpec((tm, n), lambda i: (off + i, 0)),
        out_shape=jax.ShapeDtypeStruct(x.shape, F32),
        input_output_aliases={2: 0},
        compiler_params=_cp("parallel"),
    )(a, w, x)


def _xattn_cache_kernel(q_ref, k_hbm, v_hbm, o_ref, kbuf, vbuf, sem, *, layer):
    n = pl.program_id(0)
    slot = n % 2

    def copies(seq, s):
        cs = []
        for h in range(X_HEADS):
            cs.append(pltpu.make_async_copy(k_hbm.at[layer, seq, :, h, :], kbuf.at[s, h], sem.at[s, h]))
            cs.append(pltpu.make_async_copy(v_hbm.at[layer, seq, :, h, :], vbuf.at[s, h], sem.at[s, X_HEADS + h]))
        return cs

    @pl.when(n == 0)
    def _():
        for c in copies(0, 0):
            c.start()

    @pl.when(n + 1 < pl.num_programs(0))
    def _():
        for c in copies(n + 1, 1 - slot):
            c.start()

    for c in copies(n, slot):
        c.wait()
    q = q_ref[...]
    hs = range(X_HEADS)
    ss = [_dot_nt((q[:, h * X_HD:(h + 1) * X_HD] * (X_HD ** -0.5)).astype(BF16), kbuf[slot, h].astype(BF16)) for h in hs]
    ps = [jnp.exp(s - jnp.max(s, axis=-1, keepdims=True)) for s in ss]
    ps = [(p / jnp.sum(p, axis=-1, keepdims=True)).astype(BF16) for p in ps]
    for h in hs:
        o_ref[:, h * X_HD:(h + 1) * X_HD] = _dot(ps[h], vbuf[slot, h].astype(BF16))


def cross_attention_cached(q_all, cache_k, cache_v, layer, row_off):
    off = row_off // DEC_SEQ
    return pl.pallas_call(
        functools.partial(_xattn_cache_kernel, layer=layer),
        grid=(DEC_BATCH,),
        in_specs=[pl.BlockSpec((DEC_SEQ, D_MODEL), lambda n: (off + n, 0)),
                  pl.BlockSpec(memory_space=pl.ANY), pl.BlockSpec(memory_space=pl.ANY)],
        out_specs=pl.BlockSpec((DEC_SEQ, D_MODEL), lambda n: (n, 0)),
        out_shape=jax.ShapeDtypeStruct((DEC_BATCH * DEC_SEQ, D_MODEL), F32),
        scratch_shapes=[pltpu.VMEM((2, X_HEADS, N_MEM, X_HD), F32), pltpu.VMEM((2, X_HEADS, N_MEM, X_HD), F32),
                        pltpu.SemaphoreType.DMA((2, 2 * X_HEADS))],
        compiler_params=_cp("arbitrary"),
    )(q_all, cache_k, cache_v)


def _softplus(x):
    return jnp.maximum(x, 0.0) + jnp.log(1.0 + jnp.exp(-jnp.abs(x)))


def _split3(x, axis):
    hi = x.astype(BF16).astype(F32)
    return jnp.concatenate([hi, hi, x - hi], axis=axis).astype(BF16)


def _split3r(x, axis):
    hi = x.astype(BF16).astype(F32)
    return jnp.concatenate([hi, x - hi, hi], axis=axis).astype(BF16)


def _dot3(a, b):
    return _dot(_split3(a, 1), _split3r(b, 0))


def _cumsum_rows(tri3, g):
    g1 = g.astype(BF16).astype(F32)
    g2 = (g - g1).astype(BF16).astype(F32)
    g3 = g - g1 - g2
    return _dot(tri3, jnp.concatenate([g1, g2, g3], axis=0).astype(BF16))


def _gdn_kernel(qkv_ref, sm_ref, gate_ref, cw_ref, gp_ref, nw_ref, cb_ref, s0_ref, o_ref, sout_ref,
                xbuf, s_scr, *, seqs, rows, chunk):
    i = pl.program_id(1)
    n_chunks = rows // chunk

    @pl.when(i == 0)
    def _():
        xbuf[:, 0:8, :] = cb_ref[...]
        s_scr[...] = s0_ref[...]

    ri = lax.broadcasted_iota(jnp.int32, (chunk, chunk), 0)
    ci = lax.broadcasted_iota(jnp.int32, (chunk, chunk), 1)
    incl = ri >= ci
    strict = ri > ci
    tri = jnp.where(incl, 1.0, 0.0).astype(BF16)
    tri3 = jnp.concatenate([tri, tri, tri], axis=1)
    eye = jnp.where(ri == ci, 1.0, 0.0)
    levels = []
    w = 1
    while w < chunk:
        levels.append(((ri // (2 * w)) == (ci // (2 * w))) & ((ri % (2 * w)) >= w) & ((ci % (2 * w)) < w))
        w *= 2
    nw = nw_ref[...]
    cw = [cw_ref[j:j + 1, :] for j in range(GDN_CONV)]
    a_neg = -jnp.exp(gp_ref[0:1, :])
    dtb = gp_ref[1:2, :]

    probs = []
    for b in range(seqs):
        tok = slice(b * rows, (b + 1) * rows)
        xbuf[b, 8:8 + rows, :] = qkv_ref[tok, :]
        conv = xbuf[b, 5:5 + rows, :] * cw[0]
        for j in range(1, GDN_CONV):
            conv = conv + xbuf[b, 5 + j:5 + j + rows, :] * cw[j]
        tail = xbuf[b, 8 + rows - 3:8 + rows, :]
        xbuf[b, 5:8, :] = tail
        qkv = _silu(conv)
        sm = sm_ref[tok, :]
        beta_all = _sigmoid(sm)
        g_all = a_neg * _softplus(sm + dtb)
        for c in range(n_chunks):
            rs = slice(c * chunk, (c + 1) * chunk)
            gcum = _cumsum_rows(tri3, g_all[rs])
            gcum_t = gcum.T
            for h in range(GDN_HEADS):
                q = qkv[rs, h * GDN_DK:(h + 1) * GDN_DK]
                k = qkv[rs, 512 + h * GDN_DK:512 + (h + 1) * GDN_DK]
                v = qkv[rs, 1024 + h * GDN_DK:1024 + (h + 1) * GDN_DK]
                q = q * lax.rsqrt(jnp.sum(q * q, axis=-1, keepdims=True) + EPS) * (GDN_DK ** -0.5)
                k = k * lax.rsqrt(jnp.sum(k * k, axis=-1, keepdims=True) + EPS)
                beta = beta_all[rs, h:h + 1]
                gc_col = gcum[:, 4 + h:5 + h]
                gc_row = gcum_t[4 + h:5 + h, :]
                g_last = gcum[chunk - 1:chunk, 4 + h:5 + h]
                decay = jnp.where(incl, jnp.exp(jnp.minimum(gc_col - gc_row, 0.0)), 0.0)
                eg = jnp.exp(gc_col)
                kb = k * beta
                k3r = _split3r(k, 1)
                probs.append(dict(
                    b=b, c=c, h=h, decay=decay,
                    kk=_dot_nt(_split3(kb, 1), k3r), qk=_dot_nt(_split3(q, 1), k3r),
                    rhs=jnp.concatenate([v * beta, kb * eg], axis=1), qd=q * eg,
                    kd_t=(k * jnp.exp(g_last - gc_col)).T, gl=jnp.exp(g_last), minv=eye))
    for p in probs:
        p["lmat"] = jnp.where(strict, p["kk"] * p["decay"], 0.0)
        p["qk"] = jnp.where(incl, p["qk"] * p["decay"], 0.0)
    for p in probs:
        p["minv"] = eye - jnp.where(levels[0], p["lmat"], 0.0)
        p["lh"] = p["lmat"].astype(BF16).astype(F32)
        p["ll"] = p["lmat"] - p["lh"]
    for off_blk in levels[1:]:
        for p in probs:
            mh = p["minv"].astype(BF16).astype(F32)
            ml = p["minv"] - mh
            p["mr3"] = jnp.concatenate([mh, ml, mh], axis=0).astype(BF16)
            ch, cl = jnp.where(off_blk, p["lh"], 0.0), jnp.where(off_blk, p["ll"], 0.0)
            p["t"] = _dot(jnp.concatenate([mh, mh, ml], axis=1).astype(BF16),
                          jnp.concatenate([ch, cl, ch], axis=0).astype(BF16))
        for p in probs:
            p["minv"] = p["minv"] - _dot(_split3(p["t"], 1), p["mr3"])
    for p in probs:
        p["uw"] = _dot3(p["minv"], p["rhs"])
    state = {(b, h): s_scr[b, h] for b in range(seqs) for h in range(GDN_HEADS)}
    for c in range(n_chunks):
        cur = [p for p in probs if p["c"] == c]
        for p in cur:
            p["s3r"] = _split3r(state[(p["b"], p["h"])], 0)
            p["v_new"] = p["uw"][:, :GDN_DK] - _dot(_split3(p["uw"][:, GDN_DK:], 1), p["s3r"])
        for p in cur:
            p["o"] = _dot(_split3(p["qd"], 1), p["s3r"]) + _dot3(p["qk"], p["v_new"])
            state[(p["b"], p["h"])] = state[(p["b"], p["h"])] * p["gl"] + _dot3(p["kd_t"], p["v_new"])
    for p in probs:
        b, c, h = p["b"], p["c"], p["h"]
        r0 = b * rows + c * chunk
        o = _rms(p["o"], nw) * _silu(gate_ref[r0:r0 + chunk, h * GDN_DK:(h + 1) * GDN_DK])
        o_ref[r0:r0 + chunk, h * GDN_DK:(h + 1) * GDN_DK] = o
    for (b, h), s in state.items():
        s_scr[b, h] = s

    @pl.when(i == pl.num_programs(1) - 1)
    def _():
        sout_ref[...] = s_scr[...]


def gdn(qkv, small, gate, conv_w, gpar, norm_w, conv_buf8, s0, row_off, n_seq, t_seq, seqs, rows, chunk):
    nb = t_seq // rows
    blk = seqs * rows
    off = row_off // blk
    rmap = lambda n, i: (off + n * nb + i, 0)
    return pl.pallas_call(
        functools.partial(_gdn_kernel, seqs=seqs, rows=rows, chunk=chunk),
        grid=(n_seq // seqs, nb),
        in_specs=[pl.BlockSpec((blk, GDN_QKV), rmap),
                  pl.BlockSpec((blk, 128), rmap),
                  pl.BlockSpec((blk, 512), rmap),
                  pl.BlockSpec((GDN_CONV, GDN_QKV), lambda n, i: (0, 0)),
                  pl.BlockSpec((8, 128), lambda n, i: (0, 0)),
                  pl.BlockSpec((1, GDN_DK), lambda n, i: (0, 0)),
                  pl.BlockSpec((seqs, 8, GDN_QKV), lambda n, i: (n, 0, 0)),
                  pl.BlockSpec((seqs, GDN_HEADS, GDN_DK, GDN_DK), lambda n, i: (n, 0, 0, 0))],
        out_specs=[pl.BlockSpec((blk, 512), lambda n, i: (n * nb + i, 0)),
                   pl.BlockSpec((seqs, GDN_HEADS, GDN_DK, GDN_DK), lambda n, i: (n, 0, 0, 0))],
        out_shape=[jax.ShapeDtypeStruct((n_seq * t_seq, 512), F32),
                   jax.ShapeDtypeStruct((n_seq, GDN_HEADS, GDN_DK, GDN_DK), F32)],
        scratch_shapes=[pltpu.VMEM((seqs, 8 + rows, GDN_QKV), F32),
                        pltpu.VMEM((seqs, GDN_HEADS, GDN_DK, GDN_DK), F32)],
        compiler_params=_cp("parallel", "arbitrary"),
    )(qkv, small, gate, conv_w, gpar, norm_w, conv_buf8, s0)


def _compress_core(xj, posab_ref, w1_ref, w2_ref, m):
    acc = jnp.zeros((m + 8, 4 * CMP_HIDDEN), F32)
    for jj in range(CMP_STRIDE // 2):
        x2 = jnp.concatenate([xj(2 * jj), xj(2 * jj + 1)], axis=1)
        lhs = jnp.concatenate([x2, posab_ref[jj]], axis=0).astype(BF16)
        acc = acc + _dot(lhs, w1_ref[jj])
    top = acc[0:m, 0:2 * CMP_HIDDEN]
    bot = acc[0:m, 2 * CMP_HIDDEN:]
    c = acc[m:m + 1, 0:2 * CMP_HIDDEN] + acc[m + 1:m + 2, 2 * CMP_HIDDEN:]
    h = top + pltpu.roll(bot, m - 1, 0) + c
    out = _dot(_silu(h).astype(BF16), w2_ref[...])
    row = lax.broadcasted_iota(jnp.int32, (m, 2 * NSA_HD), 0)
    return jnp.where(row < m - 1, out, 0.0)


def _compress_prompt_kernel(kc_ref, vc_ref, pk_ref, w1k_ref, w2k_ref, pv_ref, w1v_ref, w2v_ref, ok_ref, ov_ref, *, m):
    ok_ref[0] = _compress_core(lambda j: kc_ref[pl.ds(j, m, stride=CMP_STRIDE), :], pk_ref, w1k_ref, w2k_ref, m)
    ov_ref[0] = _compress_core(lambda j: vc_ref[pl.ds(j, m, stride=CMP_STRIDE), :], pv_ref, w1v_ref, w2v_ref, m)


def _cmp_weight_specs(nmap):
    return [pl.BlockSpec((CMP_STRIDE // 2, 8, 256), nmap(3)),
            pl.BlockSpec((CMP_STRIDE // 2, 256, 4 * CMP_HIDDEN), nmap(3)),
            pl.BlockSpec((2 * CMP_HIDDEN, 2 * NSA_HD), nmap(2))]


def compress_prompt(nkv, cw, n_seq, t_seq):
    m = t_seq // CMP_STRIDE
    zmap = lambda nd: (lambda n: (0,) * nd)
    out = jax.ShapeDtypeStruct((n_seq, m, 2 * NSA_HD), F32)
    return pl.pallas_call(
        functools.partial(_compress_prompt_kernel, m=m),
        grid=(n_seq,),
        in_specs=[pl.BlockSpec((t_seq, 128), lambda n: (n, 4)), pl.BlockSpec((t_seq, 128), lambda n: (n, 5))]
        + _cmp_weight_specs(zmap) + _cmp_weight_specs(zmap),
        out_specs=[pl.BlockSpec((1, m, 2 * NSA_HD), lambda n: (n, 0, 0))] * 2,
        out_shape=[out, out],
        compiler_params=_cp("parallel"),
    )(nkv, nkv, *cw)


def _compress_sample_kernel(pt_ref, *refs, m):
    del pt_ref
    kpages = refs[0:N_PAGES]
    vpages = refs[N_PAGES:2 * N_PAGES]
    pk_ref, w1k_ref, w2k_ref, pv_ref, w1v_ref, w2v_ref, ok_ref, ov_ref, xk_scr, xv_scr = refs[2 * N_PAGES:]
    for p in range(N_PAGES):
        xk_scr[p * PAGE_SIZE:(p + 1) * PAGE_SIZE, :] = kpages[p][...].T
        xv_scr[p * PAGE_SIZE:(p + 1) * PAGE_SIZE, :] = vpages[p][...].T
    ok_ref[0] = _compress_core(lambda j: xk_scr[pl.ds(j, m, stride=CMP_STRIDE), :], pk_ref, w1k_ref, w2k_ref, m)
    ov_ref[0] = _compress_core(lambda j: xv_scr[pl.ds(j, m, stride=CMP_STRIDE), :], pv_ref, w1v_ref, w2v_ref, m)


def _page_specs(layer, nseq=1, b=0):
    return [pl.BlockSpec((None, None, PAGE_SIZE, 128),
                         functools.partial(lambda n, pt, p: (layer, pt[n * nseq + b, p], 0, 0), p=p))
            for p in range(N_PAGES)]


def compress_sample(page_table, cache_k, cache_v, layer, cw):
    m = PAST_LEN // CMP_STRIDE
    zmap = lambda nd: (lambda n, pt: (0,) * nd)
    out = jax.ShapeDtypeStruct((DEC_BATCH, m, 2 * NSA_HD), F32)
    return pl.pallas_call(
        functools.partial(_compress_sample_kernel, m=m),
        grid_spec=pltpu.PrefetchScalarGridSpec(
            num_scalar_prefetch=1, grid=(DEC_BATCH,),
            in_specs=_page_specs(layer) + _page_specs(layer) + _cmp_weight_specs(zmap) + _cmp_weight_specs(zmap),
            out_specs=[pl.BlockSpec((1, m, 2 * NSA_HD), lambda n, pt: (n, 0, 0))] * 2,
            scratch_shapes=[pltpu.VMEM((PAST_LEN, 2 * NSA_HD), F32), pltpu.VMEM((PAST_LEN, 2 * NSA_HD), F32)]),
        out_shape=[out, out],
        compiler_params=_cp("arbitrary"),
    )(page_table, *([cache_k] * N_PAGES), *([cache_v] * N_PAGES), *cw)


def _masked_softmax(s, mask):
    s = jnp.where(mask, s, NEG)
    m = jnp.max(s, axis=-1, keepdims=True)
    p = jnp.where(mask, jnp.exp(s - m), 0.0)
    l = jnp.sum(p, axis=-1, keepdims=True)
    return p / jnp.where(l > 0.0, l, 1.0)


def _stack_heads(nq, g, tq):
    parts = [nq[:, (g * NSA_REP + r) * NSA_HD:(g * NSA_REP + r + 1) * NSA_HD] for r in range(NSA_REP)]
    return (jnp.concatenate(parts, axis=0) * (NSA_HD ** -0.5)).astype(BF16)


def _rep(mask, tq):
    return jnp.concatenate([mask] * NSA_REP, axis=0)


def _importance_t(p, tq, n_blk):
    n_cmp = p.shape[1]
    psum = p[0:tq] + p[tq:2 * tq] + p[2 * tq:3 * tq] + p[3 * tq:4 * tq]
    sj = lax.broadcasted_iota(jnp.int32, (n_blk, n_cmp), 0) * SEL_BLOCK
    ci = lax.broadcasted_iota(jnp.int32, (n_blk, n_cmp), 1) * CMP_STRIDE
    ov = jnp.clip(jnp.minimum(ci + CMP_LEN, sj + SEL_BLOCK) - jnp.maximum(ci, sj), 0, CMP_LEN).astype(F32) / CMP_LEN
    ov = ov.astype(BF16)
    p1 = psum.astype(BF16).astype(F32)
    p2 = (psum - p1).astype(BF16).astype(F32)
    p3 = psum - p1 - p2
    return _dot_nt(jnp.concatenate([ov, ov, ov], axis=1), jnp.concatenate([p1, p2, p3], axis=1).astype(BF16))


def _select_blocks(imp_ts, qpos_row, tq, n_blk):
    blk = lax.broadcasted_iota(jnp.int32, (n_blk, tq), 0)
    rows = qpos_row if isinstance(qpos_row, (list, tuple)) else [qpos_row] * len(imp_ts)
    works = []
    for imp_t, row in zip(imp_ts, rows):
        cur = row // SEL_BLOCK
        valid = blk <= cur
        forced = valid & ((blk == 0) | (blk >= cur - 1))
        works.append(jnp.where(valid, jnp.where(forced, SEL_FORCED, imp_t), NEG))
    sels = [jnp.zeros((n_blk, tq), F32) for _ in imp_ts]
    for _ in range(SEL_TOPN):
        for j in range(len(works)):
            m = jnp.max(works[j], axis=0, keepdims=True)
            idx = jnp.min(jnp.where(works[j] == m, blk, n_blk), axis=0, keepdims=True)
            pick = blk == idx
            sels[j] = jnp.where(pick, 1.0, sels[j])
            works[j] = jnp.where(pick, -jnp.inf, works[j])
    return [s.T for s in sels]


def _expand_sel(sel, first_blk, n_keys):
    n_blk = sel.shape[1]
    bj = lax.broadcasted_iota(jnp.int32, (n_blk, n_keys), 0)
    kb = lax.broadcasted_iota(jnp.int32, (n_blk, n_keys), 1) // SEL_BLOCK + first_blk
    e = jnp.where(bj == kb, 1.0, 0.0).astype(BF16)
    return _dot(sel.astype(BF16), e) > 0.5


def _gate_mix(o_ref, gates, g, o_cmp, o_slc, o_win, tq):
    for r in range(NSA_REP):
        h = g * NSA_REP + r
        rs = slice(r * tq, (r + 1) * tq)
        c0 = 8 + 3 * h
        o = (gates[:, c0:c0 + 1] * o_cmp[rs] + gates[:, c0 + 1:c0 + 2] * o_slc[rs] + gates[:, c0 + 2:c0 + 3] * o_win[rs])
        o_ref[:, h * NSA_HD:(h + 1) * NSA_HD] = o


def _nsa_prompt_kernel(nq_ref, sm_ref, kv_ref, et_ref, kcmp_ref, vcmp_ref, o_ref):
    tq = Q_BLOCK
    nt = NSA_PROMPT_TILES
    kc = 512
    assert (kc // tq) % nt == 0
    step = pl.program_id(1)
    tiles = range(nt)
    probs = [(t, g) for t in tiles for g in range(NSA_KV)]
    gsl = lambda g: slice(g * NSA_HD, (g + 1) * NSA_HD)
    qpos = [(step * nt + t) * tq + lax.broadcasted_iota(jnp.int32, (tq, 1), 0) for t in tiles]
    qrow = [(step * nt + t) * tq + lax.broadcasted_iota(jnp.int32, (1, tq), 1) for t in tiles]
    nq = [nq_ref[t * tq:(t + 1) * tq, :] for t in tiles]
    q4 = {(t, g): _stack_heads(nq[t], g, tq) for t, g in probs}

    n_cmp = kcmp_ref.shape[1]
    cmp_end = lax.broadcasted_iota(jnp.int32, (tq, n_cmp), 1) * CMP_STRIDE + (CMP_LEN - 1)
    s_c = {(t, g): _dot_nt(q4[t, g], kcmp_ref[0, :, gsl(g)].astype(BF16)) for t, g in probs}
    p_c = {(t, g): _masked_softmax(s_c[t, g], _rep(cmp_end <= qpos[t], tq)) for t, g in probs}
    o_cmp = {(t, g): _dot(p_c[t, g].astype(BF16), vcmp_ref[0, :, gsl(g)].astype(BF16)) for t, g in probs}
    sel_list = _select_blocks([_importance_t(p_c[pg], tq, 128) for pg in probs], [qrow[t] for t, _ in probs], tq, 128)
    qas = [jnp.concatenate([_rep(jnp.where(sel > 0.5, 0.0, -SEL_MASK), tq).astype(BF16), q4[pg]], axis=1)
           for sel, pg in zip(sel_list, probs)]

    def slc_step(c, carry, diagonal):
        start = pl.multiple_of(c * kc, kc)
        et = et_ref[pl.ds(start, kc), :]
        kas = [jnp.concatenate([et, kv_ref[pl.ds(start, kc), gsl(g)]], axis=1) for g in range(NSA_KV)]
        ss = [_dot_nt(qas[k], kas[g]) for k, (t, g) in enumerate(probs)]
        if diagonal:
            kpos = start + lax.broadcasted_iota(jnp.int32, (tq, kc), 1)
            ss = [jnp.where(_rep(kpos <= qpos[t], tq), s, -SEL_MASK) for s, (t, g) in zip(ss, probs)]
        m_new = [jnp.maximum(carry[k][0], jnp.max(ss[k], axis=-1, keepdims=True)) for k in range(len(probs))]
        ps = [jnp.exp(ss[k] - m_new[k]) for k in range(len(probs))]
        out = []
        for k, (t, g) in enumerate(probs):
            m_i, l_i, acc = carry[k]
            alpha = jnp.exp(m_i - m_new[k])
            v = kv_ref[pl.ds(start, kc), 128 + g * NSA_HD:128 + (g + 1) * NSA_HD]
            out.append((m_new[k], alpha * l_i + jnp.sum(ps[k], axis=-1, keepdims=True),
                        alpha * acc + _dot(ps[k].astype(BF16), v)))
        return tuple(out)

    init = tuple((jnp.full((NSA_REP * tq, 1), -3.0e38, F32), jnp.zeros((NSA_REP * tq, 1), F32),
                  jnp.zeros((NSA_REP * tq, NSA_HD), F32)) for _ in probs)
    n_full = (step * nt) // (kc // tq)
    carry = lax.fori_loop(0, n_full, functools.partial(slc_step, diagonal=False), init)
    fin = slc_step(n_full, carry, True)

    nband = WINDOW + tq
    wstart = [pl.multiple_of(jnp.maximum(step * nt + t - WINDOW // tq, 0) * tq, tq) for t in tiles]
    win_ok = []
    for t in tiles:
        d = qpos[t] - (wstart[t] + lax.broadcasted_iota(jnp.int32, (tq, nband), 1))
        win_ok.append(_rep((d >= 0) & (d <= WINDOW), tq))
    s_w = [_dot_nt(q4[t, g], kv_ref[pl.ds(wstart[t], nband), 256 + g * NSA_HD:256 + (g + 1) * NSA_HD]) for t, g in probs]
    p_w = [_masked_softmax(s_w[k], win_ok[t]).astype(BF16) for k, (t, g) in enumerate(probs)]
    o_win = [_dot(p_w[k], kv_ref[pl.ds(wstart[t], nband), 384 + g * NSA_HD:384 + (g + 1) * NSA_HD])
             for k, (t, g) in enumerate(probs)]

    for t in tiles:
        gates = _sigmoid(sm_ref[t * tq:(t + 1) * tq, :])
        for g in range(NSA_KV):
            k = t * NSA_KV + g
            _, l_f, acc = fin[k]
            o_slc = acc / jnp.where(l_f > 0.0, l_f, 1.0)
            _gate_mix(o_ref.at[t * tq:(t + 1) * tq], gates, g, o_cmp[t, g], o_slc, o_win[k], tq)


NSA_PROMPT_TILES = 2


def nsa_prompt(nq, small, kv_bf, kcmp, vcmp, n_seq, t_seq):
    rows = NSA_PROMPT_TILES * Q_BLOCK
    nb = t_seq // rows
    m = kcmp.shape[1]
    key_blk = jnp.arange(t_seq, dtype=jnp.int32)[:, None] // SEL_BLOCK
    et = (key_blk == jnp.arange(128, dtype=jnp.int32)[None, :]).astype(BF16)
    return pl.pallas_call(
        _nsa_prompt_kernel,
        grid=(n_seq, nb),
        in_specs=[pl.BlockSpec((rows, 512), lambda n, i: (n * nb + i, 0)),
                  pl.BlockSpec((rows, 128), lambda n, i: (n * nb + i, 0)),
                  pl.BlockSpec((t_seq, 512), lambda n, i: (n, 0)),
                  pl.BlockSpec((t_seq, 128), lambda n, i: (0, 0)),
                  pl.BlockSpec((1, m, 128), lambda n, i: (n, 0, 0)),
                  pl.BlockSpec((1, m, 128), lambda n, i: (n, 0, 0))],
        out_specs=pl.BlockSpec((rows, 512), lambda n, i: (n * nb + i, 0)),
        out_shape=jax.ShapeDtypeStruct((n_seq * t_seq, 512), F32),
        compiler_params=_cp("parallel", "arbitrary"),
    )(nq, small, kv_bf, et, kcmp, vcmp)


NSA_SAMPLE_SEQS = 2


def _nsa_sample_kernel(pt_ref, *refs):
    del pt_ref
    nseq = NSA_SAMPLE_SEQS
    per = 2 * N_PAGES + 2
    seq_refs = [refs[b * per:(b + 1) * per] for b in range(nseq)]
    nq_ref, sm_ref, kvn_ref, kcmp_ref, vcmp_ref, o_ref = refs[nseq * per:]
    tq = DEC_SEQ
    seqs = range(nseq)
    qpos = PAST_LEN + lax.broadcasted_iota(jnp.int32, (tq, 1), 0)
    qpos_row = PAST_LEN + lax.broadcasted_iota(jnp.int32, (1, tq), 1)
    n_keys = (N_PAGES + 1) * PAGE_SIZE
    n_blk = 40
    nr = NSA_REP * tq
    pad = jnp.zeros((PAGE_SIZE - tq, 128), F32)
    z = jnp.zeros((nr, NSA_HD), BF16)
    both = lambda m: jnp.concatenate([_rep(m, tq)] * NSA_KV, axis=0)
    n_cmp = kcmp_ref.shape[1]
    cmp_ok = both(lax.broadcasted_iota(jnp.int32, (tq, n_cmp), 1) * CMP_STRIDE + (CMP_LEN - 1) <= qpos)
    d = qpos - (PAST_LEN - WINDOW + lax.broadcasted_iota(jnp.int32, (tq, WINDOW + PAGE_SIZE), 1))
    win_ok = both((d >= 0) & (d <= WINDOW))
    causal = lax.broadcasted_iota(jnp.int32, (tq, n_keys), 1) <= qpos

    new, q_bd = [], []
    for b in seqs:
        rows = slice(b * tq, (b + 1) * tq)
        kvn = kvn_ref[rows, :]
        new.append([jnp.concatenate([kvn[:, c * 128:(c + 1) * 128], pad], axis=0).astype(BF16) for c in range(4)])
        nq = nq_ref[rows, :]
        q_bd.append(jnp.concatenate([jnp.concatenate([_stack_heads(nq, 0, tq), z], axis=1),
                                     jnp.concatenate([z, _stack_heads(nq, 1, tq)], axis=1)], axis=0))
    s_c = [_dot_nt(q_bd[b], kcmp_ref[b].astype(BF16)) for b in seqs]
    s = [jnp.concatenate([_dot(q_bd[b], p[...].astype(BF16)) for p in seq_refs[b][:N_PAGES]]
                         + [_dot_nt(q_bd[b], new[b][0])], axis=1) for b in seqs]
    sw = [jnp.concatenate([_dot(q_bd[b], seq_refs[b][2 * N_PAGES][...].astype(BF16)), _dot_nt(q_bd[b], new[b][2])],
                          axis=1) for b in seqs]
    p_c = [_masked_softmax(s_c[b], cmp_ok) for b in seqs]
    o_c = [_dot(p_c[b].astype(BF16), vcmp_ref[b].astype(BF16)) for b in seqs]
    pw = [_masked_softmax(sw[b], win_ok).astype(BF16) for b in seqs]
    o_w = [_dot_nt(pw[b][:, :WINDOW], seq_refs[b][2 * N_PAGES + 1][...].astype(BF16)) + _dot(pw[b][:, WINDOW:], new[b][3])
           for b in seqs]
    sels = _select_blocks([_importance_t(p_c[b][g * nr:(g + 1) * nr], tq, n_blk) for b in seqs for g in range(NSA_KV)],
                          qpos_row, tq, n_blk)
    p = []
    for b in seqs:
        msk = jnp.concatenate([_rep(_expand_sel(sels[b * NSA_KV + g], 0, n_keys) & causal, tq) for g in range(NSA_KV)],
                              axis=0)
        p.append(_masked_softmax(s[b], msk).astype(BF16))
    o_s = [_dot(p[b][:, N_PAGES * PAGE_SIZE:], new[b][1]) for b in seqs]
    for c in range(N_PAGES):
        o_s = [o_s[b] + _dot_nt(p[b][:, c * PAGE_SIZE:(c + 1) * PAGE_SIZE], seq_refs[b][N_PAGES + c][...].astype(BF16))
               for b in seqs]
    for b in seqs:
        gates = _sigmoid(sm_ref[b * tq:(b + 1) * tq, :])
        for g in range(NSA_KV):
            rs, gs = slice(g * nr, (g + 1) * nr), slice(g * NSA_HD, (g + 1) * NSA_HD)
            _gate_mix(o_ref.at[b * tq:(b + 1) * tq], gates, g, o_c[b][rs, gs], o_s[b][rs, gs], o_w[b][rs, gs], tq)


def nsa_sample(page_table, cache_k, cache_v, layer, nq, small, nkv, kcmp, vcmp, win_k, win_v):
    nseq = NSA_SAMPLE_SEQS
    rows = nseq * DEC_SEQ
    off = RP // rows
    rmap = lambda n, pt: (off + n, 0)
    m = kcmp.shape[1]
    specs, args = [], []
    for b in range(nseq):
        specs += _page_specs(layer, nseq, b) + _page_specs(layer, nseq, b)
        args += [cache_k] * N_PAGES + [cache_v] * N_PAGES
        wspec = pl.BlockSpec((None, None, 128, WINDOW), functools.partial(lambda n, pt, b: (layer, n * nseq + b, 0, 0), b=b))
        specs += [wspec, wspec]
        args += [win_k, win_v]
    return pl.pallas_call(
        _nsa_sample_kernel,
        grid_spec=pltpu.PrefetchScalarGridSpec(
            num_scalar_prefetch=1, grid=(DEC_BATCH // nseq,),
            in_specs=specs + [
                pl.BlockSpec((rows, 512), rmap),
                pl.BlockSpec((rows, 128), rmap),
                pl.BlockSpec((rows, 768), rmap),
                pl.BlockSpec((nseq, m, 128), lambda n, pt: (n, 0, 0)),
                pl.BlockSpec((nseq, m, 128), lambda n, pt: (n, 0, 0))],
            out_specs=pl.BlockSpec((rows, 512), lambda n, pt: (n, 0))),
        out_shape=jax.ShapeDtypeStruct((RS, 512), F32),
        compiler_params=_cp("arbitrary"),
    )(page_table, *args, nq, small, nkv, kcmp, vcmp)


S5_TILE = 256
S5_LANES = 2048
S5_BLK = 16


def _gelu_tanh(x):
    return 0.5 * x * (1.0 + jnp.tanh(math.sqrt(2.0 / math.pi) * (x + 0.044715 * (x * x * x))))


def _s5_kernel(u_ref, pm_ref, pt3_ref, bdr_ref, bdi_ref, cr_ref, ci_ref, tab_ref, d_ref, x0r_ref, x0i_ref,
               y_ref, fr_ref, fi_ref, xr_scr, xi_scr, car_scr, *, chained):
    i = pl.program_id(1)
    nb = S5_GROUPS // S5_BLK
    wi = S5_BLK * S5_CH
    ws = S5_BLK * S5_P
    seg = S5_TILE // 8
    u = u_ref[...]
    ub = _dot(pm_ref[...], u.astype(BF16)).astype(BF16)
    for b in range(nb):
        xr_scr[:, b * ws:(b + 1) * ws] = _dot(ub[:, b * wi:(b + 1) * wi], bdr_ref[b])
        xi_scr[:, b * ws:(b + 1) * ws] = _dot(ub[:, b * wi:(b + 1) * wi], bdi_ref[b])

    if chained:
        @pl.when(i == 0)
        def _():
            car_scr[0:1, :] = x0r_ref[0]
            car_scr[1:2, :] = x0i_ref[0]

    def cmad(br, bi, mr, mi, xr, xi):
        return br + mr * xr - mi * xi, bi + mr * xi + mi * xr

    for c in range(S5_STATE // S5_LANES):
        ls = slice(c * S5_LANES, (c + 1) * S5_LANES)
        ar, ai = tab_ref[0, 0:1, ls], tab_ref[1, 0:1, ls]
        if chained:
            xr, xi = jnp.zeros((8, S5_LANES), F32), jnp.zeros((8, S5_LANES), F32)
            for k in range(seg):
                rs = slice(8 * k, 8 * k + 8)
                xr, xi = cmad(xr_scr[rs, ls], xi_scr[rs, ls], ar, ai, xr, xi)
                xr_scr[rs, ls] = xr
                xi_scr[rs, ls] = xi
            yr, yi = xr, xi
            for sh, r0 in ((1, seg), (2, seg + 8), (4, seg + 16)):
                yr, yi = cmad(yr, yi, tab_ref[0, r0:r0 + 8, ls], tab_ref[1, r0:r0 + 8, ls],
                              pltpu.roll(yr, sh, 0), pltpu.roll(yi, sh, 0))
            first = lax.broadcasted_iota(jnp.int32, (8, S5_LANES), 0) == 0
            yr = jnp.where(first, 0.0, pltpu.roll(yr, 1, 0))
            yi = jnp.where(first, 0.0, pltpu.roll(yi, 1, 0))
            sr, si = cmad(yr, yi, tab_ref[0, seg + 24:seg + 32, ls], tab_ref[1, seg + 24:seg + 32, ls],
                          car_scr[0:1, ls], car_scr[1:2, ls])
            for k in range(seg):
                rs = slice(8 * k, 8 * k + 8)
                xr, xi = cmad(xr_scr[rs, ls], xi_scr[rs, ls], tab_ref[0, k:k + 1, ls], tab_ref[1, k:k + 1, ls], sr, si)
                xr_scr[rs, ls] = xr
                xi_scr[rs, ls] = xi
            car_scr[0:1, ls] = xr[7:8, :]
            car_scr[1:2, ls] = xi[7:8, :]
        else:
            for q in range(S5_TILE // 64):
                xr, xi = x0r_ref[0, 8 * q:8 * q + 8, ls], x0i_ref[0, 8 * q:8 * q + 8, ls]
                for t in range(8):
                    rs = slice(8 * (8 * q + t), 8 * (8 * q + t) + 8)
                    xr, xi = cmad(xr_scr[rs, ls], xi_scr[rs, ls], ar, ai, xr, xi)
                    xr_scr[rs, ls] = xr
                    xi_scr[rs, ls] = xi
                fr_ref[0, 8 * q:8 * q + 8, ls] = xr
                fi_ref[0, 8 * q:8 * q + 8, ls] = xi

    ys = []
    for b in range(nb):
        ys.append(_dot(xr_scr[:, b * ws:(b + 1) * ws].astype(BF16), cr_ref[b])
                  - _dot(xi_scr[:, b * ws:(b + 1) * ws].astype(BF16), ci_ref[b]))
    yp = jnp.concatenate(ys, axis=1)
    h1 = yp.astype(BF16).astype(F32)
    h2 = (yp - h1).astype(BF16).astype(F32)
    ylin = _dot(pt3_ref[...], jnp.concatenate([h1, h2, yp - h1 - h2], axis=0).astype(BF16))
    y_ref[...] = _gelu_tanh(ylin + d_ref[...] * u).astype(BF16)

    if chained:
        @pl.when(i == pl.num_programs(1) - 1)
        def _():
            fr_ref[0] = car_scr[0:1, :]
            fi_ref[0] = car_scr[1:2, :]


def s5_scan(u, sp, x0r, x0i, row_off, n_seq, t_seq, chained):
    if chained:
        grid = (n_seq, t_seq // S5_TILE)
        nb = grid[1]
        smap = lambda n, i: (n, 0, 0)
    else:
        grid = (1, n_seq * t_seq // S5_TILE)
        nb = grid[1]
        smap = lambda n, i: (i, 0, 0)
    off = row_off // S5_TILE
    rmap = lambda n, i: (off + n * nb + i, 0)
    sblk = (1,) + x0r.shape[1:]
    const = lambda nd: (lambda n, i: (0,) * nd)
    rho = jnp.arange(S5_TILE)
    if chained:
        src = (S5_TILE // 8) * (rho % 8) + rho // 8
    else:
        src = 64 * (rho // 64) + 8 * (rho % 8) + (rho // 8) % 8
    pm = (src[:, None] == jnp.arange(S5_TILE)[None, :]).astype(BF16)
    pt3 = jnp.tile(pm.T, (1, 3))
    in_specs = [pl.BlockSpec((S5_TILE, D_MODEL), rmap),
                pl.BlockSpec((S5_TILE, S5_TILE), const(2)), pl.BlockSpec((S5_TILE, 3 * S5_TILE), const(2)),
                pl.BlockSpec(sp["bdr"].shape, const(3)), pl.BlockSpec(sp["bdi"].shape, const(3)),
                pl.BlockSpec(sp["cr"].shape, const(3)), pl.BlockSpec(sp["ci"].shape, const(3)),
                pl.BlockSpec(sp["tab"].shape, const(3)), pl.BlockSpec((1, D_MODEL), const(2)),
                pl.BlockSpec(sblk, smap), pl.BlockSpec(sblk, smap)]
    args = [u, pm, pt3, sp["bdr"], sp["bdi"], sp["cr"], sp["ci"], sp["tab"], sp["d"], x0r, x0i]
    return pl.pallas_call(
        functools.partial(_s5_kernel, chained=chained),
        grid=grid,
        in_specs=in_specs,
        out_specs=[pl.BlockSpec((S5_TILE, D_MODEL), lambda n, i: (n * nb + i, 0)),
                   pl.BlockSpec(sblk, smap), pl.BlockSpec(sblk, smap)],
        out_shape=[jax.ShapeDtypeStruct((n_seq * t_seq, D_MODEL), BF16), jax.ShapeDtypeStruct(x0r.shape, F32),
                   jax.ShapeDtypeStruct(x0r.shape, F32)],
        scratch_shapes=[pltpu.VMEM((S5_TILE, S5_STATE), F32), pltpu.VMEM((S5_TILE, S5_STATE), F32),
                        pltpu.VMEM((8, S5_STATE), F32)],
        compiler_params=_cp("arbitrary", "arbitrary"),
    )(*args)


def _prep_ab_in(w):
    qkv, b, a, gate, nq, nkv, ngate = jnp.split(w, (1536, 1540, 1544, 2056, 2568, 3336), axis=-1)
    kc, vc, ks, vs, kw, vw = jnp.split(nkv, 6, axis=-1)
    small = jnp.concatenate([b, a, ngate, jnp.zeros(w.shape[:-1] + (96,), w.dtype)], axis=-1)
    return jnp.concatenate([qkv, ks, vs, kw, vw, kc, vc, gate, nq, small], axis=-1).astype(BF16)


AB_SPLITS = (1536, 768, 512, 512, 128)


def _prep_cmp(pos, w1, w2):
    top = w1[:CMP_STRIDE * NSA_HD].reshape(CMP_STRIDE, NSA_HD, CMP_HIDDEN)
    bot = w1[CMP_STRIDE * NSA_HD:].reshape(CMP_STRIDE, NSA_HD, CMP_HIDDEN)
    w1bd = jnp.zeros((CMP_STRIDE, 2 * NSA_HD, 4 * CMP_HIDDEN), F32)
    w2bd = jnp.zeros((2 * CMP_HIDDEN, 2 * NSA_HD), F32)
    for g in range(NSA_KV):
        rs = slice(g * NSA_HD, (g + 1) * NSA_HD)
        w1bd = w1bd.at[:, rs, g * CMP_HIDDEN:(g + 1) * CMP_HIDDEN].set(top)
        w1bd = w1bd.at[:, rs, (2 + g) * CMP_HIDDEN:(3 + g) * CMP_HIDDEN].set(bot)
        w2bd = w2bd.at[g * CMP_HIDDEN:(g + 1) * CMP_HIDDEN, rs].set(w2)
    posab = jnp.zeros((CMP_STRIDE, 8, 2 * NSA_HD), F32)
    posab = posab.at[:, 0, :].set(jnp.tile(pos[:CMP_STRIDE], (1, NSA_KV)))
    posab = posab.at[:, 1, :].set(jnp.tile(pos[CMP_STRIDE:], (1, NSA_KV)))
    half = CMP_STRIDE // 2
    posab = posab.reshape(half, 2, 8, 2 * NSA_HD).transpose(0, 2, 1, 3).reshape(half, 8, 4 * NSA_HD)
    w1bd = w1bd.reshape(half, 4 * NSA_HD, 4 * CMP_HIDDEN)
    return posab, w1bd.astype(BF16), w2bd.astype(BF16)


def _cmul(ar, ai, br, bi):
    return ar * br - ai * bi, ar * bi + ai * br


def _prep_s5(a_re, a_im, b_re, b_im, c_re, c_im, d, log_dt):
    dt = jnp.exp(log_dt)[:, None]
    lr = jnp.minimum(a_re, S5_MAX_RE)
    li = a_im
    mag = jnp.exp(lr * dt)
    ar = mag * jnp.cos(li * dt)
    ai = mag * jnp.sin(li * dt)
    den = lr * lr + li * li
    fr = ((ar - 1.0) * lr + ai * li) / den
    fi = (ai * lr - (ar - 1.0) * li) / den
    bbr = fr[..., None] * b_re - fi[..., None] * b_im
    bbi = fr[..., None] * b_im + fi[..., None] * b_re
    nb = S5_GROUPS // S5_BLK
    eye = jnp.eye(S5_BLK, dtype=F32)

    def bd_in(m):
        m4 = jnp.swapaxes(m, 1, 2).reshape(nb, S5_BLK, S5_CH, S5_P)
        return jnp.einsum('bgcp,gh->bgchp', m4, eye).reshape(nb, S5_BLK * S5_CH, S5_BLK * S5_P).astype(BF16)

    def bd_out(m):
        m4 = jnp.swapaxes(m, 1, 2).reshape(nb, S5_BLK, S5_P, S5_CH)
        return jnp.einsum('bgpc,gh->bgphc', m4, eye).reshape(nb, S5_BLK * S5_P, S5_BLK * S5_CH).astype(BF16)

    seg = S5_TILE // 8
    a1 = (ar.reshape(1, S5_STATE), ai.reshape(1, S5_STATE))
    pw = [a1]
    for _ in range(seg - 1):
        pw.append(_cmul(*pw[-1], *a1))
    row = jnp.arange(8)[:, None]
    s1 = pw[-1]
    s2 = _cmul(*s1, *s1)
    s4 = _cmul(*s2, *s2)
    sp = [(jnp.ones_like(a1[0]), jnp.zeros_like(a1[1]))]
    for _ in range(7):
        sp.append(_cmul(*sp[-1], *s1))
    parts = []
    for j in range(2):
        parts.append(jnp.concatenate([p[j] for p in pw]
                                     + [jnp.where(row >= sh, s[j], 0.0) for sh, s in ((1, s1), (2, s2), (4, s4))]
                                     + [p[j] for p in sp], axis=0))
    return {"bdr": bd_in(bbr), "bdi": bd_in(bbi), "cr": bd_out(c_re), "ci": bd_out(c_im),
            "tab": jnp.stack(parts), "d": d.reshape(1, D_MODEL)}


def _heads(a, n, t):
    return a.reshape(n, t, NSA_KV, NSA_HD)


def kernel(x_prompt, x_sample, mem_prompt, cache_mem_k, cache_mem_v, state_gdn, state_gdn_conv, cache_cmp_k, cache_cmp_v, cache_slc_k, cache_slc_v, cache_win_k, cache_win_v, state_s5_re, state_s5_im, page_table, norm_ffn1, w_ffn1_gate, w_ffn1_up, w_ffn1_down, norm_mix, norm_xq, norm_mem, w_xq, w_xk, w_xv, w_xo, norm_ffn2, w_ffn2_gate, w_ffn2_up, w_ffn2_down, norm_final, w_in_ab, w_out_ab, gdn_conv, gdn_a_log, gdn_dt_bias, gdn_norm, cmp_pos_k, cmp_w1_k, cmp_w2_k, cmp_pos_v, cmp_w1_v, cmp_w2_v, w_in_c, s5_a_re, s5_a_im, s5_b_re, s5_b_im, s5_c_re, s5_c_im, s5_d, s5_log_dt, w_glu, w_out_c):
    bf = lambda w: w.astype(BF16)
    n_ab = w_in_ab.shape[0]
    n_pool = cache_cmp_k.shape[1]
    x = jnp.concatenate([x_prompt.reshape(RP, D_MODEL), x_sample.reshape(RS, D_MODEL)], axis=0)

    memkv = mem_kv_all(mem_prompt.reshape(BATCH * N_MEM, D_MODEL), norm_mem, bf(jnp.concatenate([w_xk, w_xv], axis=-1)))
    mem_k_prompt = memkv[:, :, :D_MODEL].reshape(DEPTH, BATCH, N_MEM, X_HEADS, X_HD)
    mem_v_prompt = memkv[:, :, D_MODEL:].reshape(DEPTH, BATCH, N_MEM, X_HEADS, X_HD)

    fmaj = lambda c: jnp.transpose(c, (0, 1, 3, 4, 2)).reshape(c.shape[0], c.shape[1], NSA_KV * NSA_HD, c.shape[2])
    cck, ccv = fmaj(cache_cmp_k), fmaj(cache_cmp_v)
    csk, csv, cwk, cwv = fmaj(cache_slc_k), fmaj(cache_slc_v), fmaj(cache_win_k), fmaj(cache_win_v)
    ab_p, ab_s, c_p, c_s = [], [], [], []
    for l in range(DEPTH):
        i = l // 2
        x = glu_mlp(x, norm_ffn1[l], bf(w_ffn1_gate[l]), bf(w_ffn1_up[l]), bf(w_ffn1_down[l]), None, 1024, 256, True, True, 0.5)
        if l % 2 == 0:
            qkv, nkv, gate, nq, small = norm_matmul(x, norm_mix[l], _prep_ab_in(w_in_ab[i]), AB_SPLITS, 512)
            gpar = jnp.zeros((8, 128), F32).at[0, 4:8].set(gdn_a_log[i]).at[1, 4:8].set(gdn_dt_bias[i])
            nw = gdn_norm[i].reshape(1, GDN_DK)
            cb_s = jnp.zeros((DEC_BATCH, 8, GDN_QKV), F32).at[:, 5:8].set(state_gdn_conv[i])
            oa_p, st_p = gdn(qkv, small, gate, gdn_conv[i], gpar, nw, jnp.zeros((BATCH, 8, GDN_QKV), F32),
                             jnp.zeros((BATCH, GDN_HEADS, GDN_DK, GDN_DK), F32), 0, BATCH, SEQ, 1, 256, GDN_CHUNK)
            oa_s, st_s = gdn(qkv, small, gate, gdn_conv[i], gpar, nw, cb_s, state_gdn[i], RP, DEC_BATCH, DEC_SEQ,
                             8, DEC_SEQ, math.gcd(DEC_SEQ, GDN_CHUNK))
            cw = _prep_cmp(cmp_pos_k[i], cmp_w1_k[i], cmp_w2_k[i]) + _prep_cmp(cmp_pos_v[i], cmp_w1_v[i], cmp_w2_v[i])
            kcmp_p, vcmp_p = compress_prompt(nkv, cw, BATCH, SEQ)
            ob_p = nsa_prompt(nq, small, bf(nkv[:RP, :512]), kcmp_p, vcmp_p, BATCH, SEQ)
            kcmp_s, vcmp_s = compress_sample(page_table, cck, ccv, i, cw)
            ob_s = nsa_sample(page_table, csk, csv, i, nq, small, nkv, kcmp_s, vcmp_s, cwk, cwv)
            w_out = bf(w_out_ab[i])
            x = matmul_residual([(oa_p, oa_s, w_out[:512]), (ob_p, ob_s, w_out[512:])], x, 512)
            conv_p = jnp.stack([qkv[(n + 1) * SEQ - 3:(n + 1) * SEQ] for n in range(BATCH)])
            qkv_s = qkv[RP:].reshape(DEC_BATCH, DEC_SEQ, GDN_QKV)
            col = lambda a, c: a[:, c * 128:(c + 1) * 128]
            nkv_p, nkv_s = nkv[:RP], nkv[RP:]
            hp = lambda c: _heads(col(nkv_p, c), BATCH, SEQ)
            hs = lambda c: _heads(col(nkv_s, c), DEC_BATCH, DEC_SEQ)
            ab_p.append((conv_p, st_p, hp(4), hp(5), hp(0), hp(1), hp(2)[:, SEQ - WINDOW:], hp(3)[:, SEQ - WINDOW:]))
            ab_s.append((qkv_s[:, DEC_SEQ - 3:], st_s, hs(4), hs(5), hs(0), hs(1),
                         jnp.concatenate([cache_win_k[i][:, DEC_SEQ:], hs(2)], axis=1),
                         jnp.concatenate([cache_win_v[i][:, DEC_SEQ:], hs(3)], axis=1)))
        else:
            (u,) = norm_matmul(x, norm_mix[l], bf(w_in_c[i]), (D_MODEL,), 512)
            sp = _prep_s5(s5_a_re[i], s5_a_im[i], s5_b_re[i], s5_b_im[i], s5_c_re[i], s5_c_im[i], s5_d[i], s5_log_dt[i])
            z0 = jnp.zeros((BATCH, 1, S5_STATE), F32)
            y_p, fr_p, fi_p = s5_scan(u, sp, z0, z0, 0, BATCH, SEQ, True)
            per_tile = S5_TILE // DEC_SEQ
            x0r = state_s5_re[i].reshape(RS // S5_TILE, per_tile, S5_STATE)
            x0i = state_s5_im[i].reshape(RS // S5_TILE, per_tile, S5_STATE)
            y_s, fr_s, fi_s = s5_scan(u, sp, x0r, x0i, RP, DEC_BATCH, DEC_SEQ, False)
            wg = bf(w_glu[i])
            x = glu_mlp(y_p, norm_mix[l], wg[:, :D_MODEL], wg[:, D_MODEL:], bf(w_out_c[i]), x, 1024, 256, False, False, 1.0,
                        src_s=y_s)
            c_p.append((fr_p.reshape(BATCH, S5_GROUPS, S5_P), fi_p.reshape(BATCH, S5_GROUPS, S5_P)))
            c_s.append((fr_s.reshape(DEC_BATCH, S5_GROUPS, S5_P), fi_s.reshape(DEC_BATCH, S5_GROUPS, S5_P)))
        wq, wo = bf(w_xq[l]), bf(w_xo[l])
        (q_s,) = norm_matmul(x, norm_xq[l], wq, (D_MODEL,), 512, row_off=RP, rows=RS)
        o_s = cross_attention_cached(q_s, cache_mem_k, cache_mem_v, l, 0)
        x = cross_attention_block(x, norm_xq[l], wq, wo, memkv, l, 512)
        x = matmul_residual_rows(o_s, wo, x, RP, 512)
        x = glu_mlp(x, norm_ffn2[l], bf(w_ffn2_gate[l]), bf(w_ffn2_up[l]), bf(w_ffn2_down[l]), None, 1024, 256, True, True, 0.5)

    y = rmsnorm_rows(x, norm_final, 512)
    st = lambda grp, j: jnp.stack([t[j] for t in grp])
    return (y[:RP].reshape(BATCH, SEQ, D_MODEL), y[RP:].reshape(DEC_BATCH, DEC_SEQ, D_MODEL),
            mem_k_prompt, mem_v_prompt,
            st(ab_p, 1), st(ab_s, 1), st(ab_p, 0), st(ab_s, 0),
            st(ab_p, 2), st(ab_p, 3), st(ab_p, 4), st(ab_p, 5),
            st(ab_s, 2), st(ab_s, 3), st(ab_s, 4), st(ab_s, 5),
            st(ab_p, 6), st(ab_p, 7), st(ab_s, 6), st(ab_s, 7),
            st(c_p, 0), st(c_p, 1), st(c_s, 0), st(c_s, 1))
```

```python
import functools
import math

import jax
import jax.numpy as jnp
from jax import lax
from jax.experimental import pallas as pl
from jax.experimental.pallas import tpu as pltpu

F32 = jnp.float32
BF16 = jnp.bfloat16
HIGHEST = lax.Precision.HIGHEST

D_MODEL = 1024
BATCH = 2
SEQ = 8192
DEPTH = 4
DEC_BATCH = 128
DEC_SEQ = 8
PAST_LEN = 2048
PAGE_SIZE = 128
N_PAGES = PAST_LEN // PAGE_SIZE
RP = BATCH * SEQ
RS = DEC_BATCH * DEC_SEQ
ROWS = RP + RS

GDN_HEADS = 4
GDN_DK = 128
GDN_QKV = 1536
GDN_CONV = 4
GDN_CHUNK = 64
NSA_HEADS = 8
NSA_KV = 2
NSA_HD = 64
NSA_REP = 4
CMP_STRIDE = 16
CMP_LEN = 32
CMP_HIDDEN = 128
SEL_BLOCK = 64
SEL_TOPN = 16
WINDOW = 512
Q_BLOCK = 128
SEL_FORCED = 1.0e4
NEG = -1.0e30
SEL_MASK = 2.0 ** 100
S5_CH = 16
S5_GROUPS = 64
S5_P = 64
S5_STATE = S5_GROUPS * S5_P
S5_MAX_RE = -1.0e-4
N_MEM = 256
X_HEADS = 4
X_HD = 256
D_FF = 2816
EPS = 1.0e-6

VMEM_LIMIT = 56 * 1024 * 1024


def _cp(*sem):
    return pltpu.CompilerParams(dimension_semantics=sem, vmem_limit_bytes=VMEM_LIMIT)


def _dot(a, b, precision=None):
    return jnp.dot(a, b, preferred_element_type=F32, precision=precision)


def _dot_nt(a, b, precision=None):
    return lax.dot_general(a, b, (((1,), (1,)), ((), ())), preferred_element_type=F32, precision=precision)


def _sigmoid(x):
    return 1.0 / (1.0 + jnp.exp(-x))


def _silu(x):
    return x * _sigmoid(x)


def _rms(x, g):
    return x * lax.rsqrt(jnp.mean(x * x, axis=-1, keepdims=True) + EPS) * g


def _norm_matmul_kernel(x_ref, g_ref, w_ref, *o_refs, norm, splits):
    x = x_ref[...]
    if norm:
        x = _rms(x, g_ref[...])
    h = x.astype(BF16)
    off = 0
    for o_ref, wd in zip(o_refs, splits):
        o_ref[...] = _dot(h, w_ref[:, off:off + wd])
        off += wd


def norm_matmul(x, g, w, splits, tm, norm=True, row_off=0, rows=None):
    k = x.shape[1]
    rows = x.shape[0] if rows is None else rows
    n = w.shape[1]
    off = row_off // tm
    assert sum(splits) == n and rows % tm == 0 and row_off % tm == 0
    outs = pl.pallas_call(
        functools.partial(_norm_matmul_kernel, norm=norm, splits=tuple(splits)),
        grid=(rows // tm,),
        in_specs=[pl.BlockSpec((tm, k), lambda i: (off + i, 0)),
                  pl.BlockSpec((1, k), lambda i: (0, 0)),
                  pl.BlockSpec((k, n), lambda i: (0, 0))],
        out_specs=[pl.BlockSpec((tm, wd), lambda i: (i, 0)) for wd in splits],
        out_shape=[jax.ShapeDtypeStruct((rows, wd), F32) for wd in splits],
        compiler_params=_cp("parallel"),
    )(x, g.reshape(1, k), w)
    return outs


def _mem_kv_kernel(x_ref, g_ref, w_ref, o_ref):
    h = _rms(x_ref[...], g_ref[0]).astype(BF16)
    o_ref[0] = _dot(h, w_ref[0])


def mem_kv_all(mem2d, g, w):
    m, k = mem2d.shape
    nl, _, n = w.shape
    return pl.pallas_call(
        _mem_kv_kernel,
        grid=(nl,),
        in_specs=[pl.BlockSpec((m, k), lambda l: (0, 0)),
                  pl.BlockSpec((1, 1, k), lambda l: (l, 0, 0)),
                  pl.BlockSpec((1, k, n), lambda l: (l, 0, 0))],
        out_specs=pl.BlockSpec((1, m, n), lambda l: (l, 0, 0)),
        out_shape=jax.ShapeDtypeStruct((nl, m, n), F32),
        compiler_params=_cp("parallel"),
    )(mem2d, g.reshape(nl, 1, k), w)


def _matmul_res_kernel(*refs, n_terms, n_p):
    r_ref, o_ref = refs[3 * n_terms], refs[3 * n_terms + 1]
    i = pl.program_id(0)

    def run(sel):
        acc = r_ref[...]
        for t in range(n_terms):
            acc = acc + _dot(refs[3 * t + sel][...].astype(BF16), refs[3 * t + 2][...])
        o_ref[...] = acc

    @pl.when(i < n_p)
    def _():
        run(0)

    @pl.when(i >= n_p)
    def _():
        run(1)


def matmul_residual(terms, res, tm):
    rows, n = res.shape
    n_p = RP // tm
    args, specs = [], []
    for a_p, a_s, w in terms:
        k = w.shape[0]
        args += [a_p, a_s, w]
        specs += [pl.BlockSpec((tm, k), lambda i: (jnp.minimum(i, n_p - 1), 0)),
                  pl.BlockSpec((tm, k), lambda i: (jnp.maximum(i - n_p, 0), 0)),
                  pl.BlockSpec((k, n), lambda i: (0, 0))]
    return pl.pallas_call(
        functools.partial(_matmul_res_kernel, n_terms=len(terms), n_p=n_p),
        grid=(rows // tm,),
        in_specs=specs + [pl.BlockSpec((tm, n), lambda i: (i, 0))],
        out_specs=pl.BlockSpec((tm, n), lambda i: (i, 0)),
        out_shape=jax.ShapeDtypeStruct((rows, n), F32),
        compiler_params=_cp("arbitrary"),
    )(*args, res)


def _glu_mlp_kernel(*refs, norm, swiglu, scale, tf, own_res, n_p):
    if own_res:
        src_ref, g_ref, wa_ref, wb_ref, wd_ref, o_ref = refs
        res_ref = src_ref
    elif n_p is None:
        src_ref, g_ref, wa_ref, wb_ref, wd_ref, res_ref, o_ref = refs
    else:
        src_ref, srcs_ref, g_ref, wa_ref, wb_ref, wd_ref, res_ref, o_ref, h_scr = refs
    if n_p is None:
        x = src_ref[...]
        if norm:
            x = _rms(x, g_ref[...])
        h = x.astype(BF16)
    else:
        @pl.when(pl.program_id(0) < n_p)
        def _():
            h_scr[...] = src_ref[...].astype(BF16)

        @pl.when(pl.program_id(0) >= n_p)
        def _():
            h_scr[...] = srcs_ref[...].astype(BF16)

        h = h_scr[...]
    acc = None
    for j in range(wa_ref.shape[1] // tf):
        cs = slice(j * tf, (j + 1) * tf)
        a = _dot(h, wa_ref[:, cs])
        b = _dot(h, wb_ref[:, cs])
        s = _silu(a) * b if swiglu else a * _sigmoid(b)
        d = _dot(s.astype(BF16), wd_ref[cs, :])
        acc = d if acc is None else acc + d
    o_ref[...] = res_ref[...] + scale * acc


def glu_mlp(src, g, wa, wb, wd, res, tm, tf, norm, swiglu, scale, src_s=None):
    k = src.shape[1]
    ff = wa.shape[1]
    n = wd.shape[1]
    rows = src.shape[0] if res is None else res.shape[0]
    resident = lambda shape: pl.BlockSpec(shape, lambda i: (0, 0), pipeline_mode=pl.Buffered(1))
    weights = [pl.BlockSpec((1, k), lambda i: (0, 0)), resident((k, ff)), resident((k, ff)), resident((ff, n))]
    n_p, scratch = None, []
    if src_s is None:
        in_specs = [pl.BlockSpec((tm, k), lambda i: (i, 0))] + weights
        args = [src, g.reshape(1, k), wa, wb, wd]
    else:
        assert not norm and res is not None
        n_p = src.shape[0] // tm
        in_specs = [pl.BlockSpec((tm, k), lambda i: (jnp.minimum(i, n_p - 1), 0)),
                    pl.BlockSpec((tm, k), lambda i: (jnp.maximum(i - n_p, 0), 0))] + weights
        args = [src, src_s, g.reshape(1, k), wa, wb, wd]
        scratch = [pltpu.VMEM((tm, k), BF16)]
    if res is not None:
        in_specs.append(pl.BlockSpec((tm, n), lambda i: (i, 0)))
        args.append(res)
    return pl.pallas_call(
        functools.partial(_glu_mlp_kernel, norm=norm, swiglu=swiglu, scale=scale, tf=tf, own_res=res is None, n_p=n_p),
        grid=(rows // tm,),
        in_specs=in_specs,
        out_specs=pl.BlockSpec((tm, n), lambda i: (i, 0)),
        out_shape=jax.ShapeDtypeStruct((rows, n), F32),
        scratch_shapes=scratch,
        compiler_params=_cp("arbitrary" if src_s is not None else "parallel"),
    )(*args)


def _rmsnorm_kernel(x_ref, g_ref, o_ref):
    o_ref[...] = _rms(x_ref[...], g_ref[...])


def rmsnorm_rows(x, g, tm):
    rows, k = x.shape
    return pl.pallas_call(
        _rmsnorm_kernel,
        grid=(rows // tm,),
        in_specs=[pl.BlockSpec((tm, k), lambda i: (i, 0)), pl.BlockSpec((1, k), lambda i: (0, 0))],
        out_specs=pl.BlockSpec((tm, k), lambda i: (i, 0)),
        out_shape=jax.ShapeDtypeStruct((rows, k), F32),
        compiler_params=_cp("parallel"),
    )(x, g.reshape(1, k))


def _xattn_block_kernel(x_ref, g_ref, wq_ref, wo_ref, k_ref, v_ref, o_ref):
    x = x_ref[...]
    q = _dot(_rms(x, g_ref[...]).astype(BF16), wq_ref[...])
    heads = [(q[:, h * X_HD:(h + 1) * X_HD] * (X_HD ** -0.5)).astype(BF16) for h in range(X_HEADS)]
    ss = [_dot_nt(heads[h], k_ref[:, h * X_HD:(h + 1) * X_HD].astype(BF16)) for h in range(X_HEADS)]
    ps = [jnp.exp(s - jnp.max(s, axis=-1, keepdims=True)) for s in ss]
    ps = [(p / jnp.sum(p, axis=-1, keepdims=True)).astype(BF16) for p in ps]
    o = jnp.concatenate([_dot(ps[h], v_ref[:, h * X_HD:(h + 1) * X_HD].astype(BF16)) for h in range(X_HEADS)], axis=1)
    o_ref[...] = x + _dot(o.astype(BF16), wo_ref[...])


def cross_attention_block(x, g, wq, wo, memkv, layer, tq):
    nb = SEQ // tq
    resident = lambda shape: pl.BlockSpec(shape, lambda n, i: (0, 0), pipeline_mode=pl.Buffered(1))
    return pl.pallas_call(
        _xattn_block_kernel,
        grid=(BATCH, nb),
        in_specs=[pl.BlockSpec((tq, D_MODEL), lambda n, i: (n * nb + i, 0)),
                  pl.BlockSpec((1, D_MODEL), lambda n, i: (0, 0)),
                  resident((D_MODEL, D_MODEL)), resident((D_MODEL, D_MODEL)),
                  pl.BlockSpec((None, N_MEM, D_MODEL), lambda n, i: (layer, n, 0)),
                  pl.BlockSpec((None, N_MEM, D_MODEL), lambda n, i: (layer, n, 1))],
        out_specs=pl.BlockSpec((tq, D_MODEL), lambda n, i: (n * nb + i, 0)),
        out_shape=jax.ShapeDtypeStruct(x.shape, F32),
        input_output_aliases={0: 0},
        compiler_params=_cp("parallel", "parallel"),
    )(x, g.reshape(1, D_MODEL), wq, wo, memkv, memkv)


def _rows_matmul_res_kernel(a_ref, w_ref, x_ref, o_ref):
    o_ref[...] = x_ref[...] + _dot(a_ref[...].astype(BF16), w_ref[...])


def matmul_residual_rows(a, w, x, row_off, tm):
    off = row_off // tm
    k, n = w.shape
    return pl.pallas_call(
        _rows_matmul_res_kernel,
        grid=(a.shape[0] // tm,),
        in_specs=[pl.BlockSpec((tm, k), lambda i: (i, 0)),
                  pl.BlockSpec((k, n), lambda i: (0, 0)),
                  pl.BlockSpec((tm, n), lambda i: (off + i, 0))],
        out_specs=pl.BlockSpec((tm, n), lambda i: (off + i, 0)),
        out_shape=jax.ShapeDtypeStruct(x.shape, F32),
        input_output_aliases={2: 0},
        compiler_params=_cp("parallel"),
    )(a, w, x)


def _xattn_cache_kernel(q_ref, k_hbm, v_hbm, o_ref, kbuf, vbuf, sem, *, layer):
    n = pl.program_id(0)
    slot = n % 2

    def copies(seq, s):
        cs = []
        for h in range(X_HEADS):
            cs.append(pltpu.make_async_copy(k_hbm.at[layer, seq, :, h, :], kbuf.at[s, h], sem.at[s, h]))
            cs.append(pltpu.make_async_copy(v_hbm.at[layer, seq, :, h, :], vbuf.at[s, h], sem.at[s, X_HEADS + h]))
        return cs

    @pl.when(n == 0)
    def _():
        for c in copies(0, 0):
            c.start()

    @pl.when(n + 1 < pl.num_programs(0))
    def _():
        for c in copies(n + 1, 1 - slot):
            c.start()

    for c in copies(n, slot):
        c.wait()
    q = q_ref[...]
    hs = range(X_HEADS)
    ss = [_dot_nt((q[:, h * X_HD:(h + 1) * X_HD] * (X_HD ** -0.5)).astype(BF16), kbuf[slot, h].astype(BF16)) for h in hs]
    ps = [jnp.exp(s - jnp.max(s, axis=-1, keepdims=True)) for s in ss]
    ps = [(p / jnp.sum(p, axis=-1, keepdims=True)).astype(BF16) for p in ps]
    for h in hs:
        o_ref[:, h * X_HD:(h + 1) * X_HD] = _dot(ps[h], vbuf[slot, h].astype(BF16))


def cross_attention_cached(q_all, cache_k, cache_v, layer, row_off):
    off = row_off // DEC_SEQ
    return pl.pallas_call(
        functools.partial(_xattn_cache_kernel, layer=layer),
        grid=(DEC_BATCH,),
        in_specs=[pl.BlockSpec((DEC_SEQ, D_MODEL), lambda n: (off + n, 0)),
                  pl.BlockSpec(memory_space=pl.ANY), pl.BlockSpec(memory_space=pl.ANY)],
        out_specs=pl.BlockSpec((DEC_SEQ, D_MODEL), lambda n: (n, 0)),
        out_shape=jax.ShapeDtypeStruct((DEC_BATCH * DEC_SEQ, D_MODEL), F32),
        scratch_shapes=[pltpu.VMEM((2, X_HEADS, N_MEM, X_HD), F32), pltpu.VMEM((2, X_HEADS, N_MEM, X_HD), F32),
                        pltpu.SemaphoreType.DMA((2, 2 * X_HEADS))],
        compiler_params=_cp("arbitrary"),
    )(q_all, cache_k, cache_v)


def _softplus(x):
    return jnp.maximum(x, 0.0) + jnp.log(1.0 + jnp.exp(-jnp.abs(x)))


def _split3(x, axis):
    hi = x.astype(BF16).astype(F32)
    return jnp.concatenate([hi, hi, x - hi], axis=axis).astype(BF16)


def _split3r(x, axis):
    hi = x.astype(BF16).astype(F32)
    return jnp.concatenate([hi, x - hi, hi], axis=axis).astype(BF16)


def _dot3(a, b):
    return _dot(_split3(a, 1), _split3r(b, 0))


def _cumsum_rows(tri3, g):
    g1 = g.astype(BF16).astype(F32)
    g2 = (g - g1).astype(BF16).astype(F32)
    g3 = g - g1 - g2
    return _dot(tri3, jnp.concatenate([g1, g2, g3], axis=0).astype(BF16))


def _gdn_kernel(qkv_ref, sm_ref, gate_ref, cw_ref, gp_ref, nw_ref, cb_ref, s0_ref, o_ref, sout_ref,
                xbuf, s_scr, *, seqs, rows, chunk):
    i = pl.program_id(1)
    n_chunks = rows // chunk

    @pl.when(i == 0)
    def _():
        xbuf[:, 0:8, :] = cb_ref[...]
        s_scr[...] = s0_ref[...]

    ri = lax.broadcasted_iota(jnp.int32, (chunk, chunk), 0)
    ci = lax.broadcasted_iota(jnp.int32, (chunk, chunk), 1)
    incl = ri >= ci
    strict = ri > ci
    tri = jnp.where(incl, 1.0, 0.0).astype(BF16)
    tri3 = jnp.concatenate([tri, tri, tri], axis=1)
    eye = jnp.where(ri == ci, 1.0, 0.0)
    levels = []
    w = 1
    while w < chunk:
        levels.append(((ri // (2 * w)) == (ci // (2 * w))) & ((ri % (2 * w)) >= w) & ((ci % (2 * w)) < w))
        w *= 2
    nw = nw_ref[...]
    cw = [cw_ref[j:j + 1, :] for j in range(GDN_CONV)]
    a_neg = -jnp.exp(gp_ref[0:1, :])
    dtb = gp_ref[1:2, :]

    probs = []
    for b in range(seqs):
        tok = slice(b * rows, (b + 1) * rows)
        xbuf[b, 8:8 + rows, :] = qkv_ref[tok, :]
        conv = xbuf[b, 5:5 + rows, :] * cw[0]
        for j in range(1, GDN_CONV):
            conv = conv + xbuf[b, 5 + j:5 + j + rows, :] * cw[j]
        tail = xbuf[b, 8 + rows - 3:8 + rows, :]
        xbuf[b, 5:8, :] = tail
        qkv = _silu(conv)
        sm = sm_ref[tok, :]
        beta_all = _sigmoid(sm)
        g_all = a_neg * _softplus(sm + dtb)
        for c in range(n_chunks):
            rs = slice(c * chunk, (c + 1) * chunk)
            gcum = _cumsum_rows(tri3, g_all[rs])
            gcum_t = gcum.T
            for h in range(GDN_HEADS):
                q = qkv[rs, h * GDN_DK:(h + 1) * GDN_DK]
                k = qkv[rs, 512 + h * GDN_DK:512 + (h + 1) * GDN_DK]
                v = qkv[rs, 1024 + h * GDN_DK:1024 + (h + 1) * GDN_DK]
                q = q * lax.rsqrt(jnp.sum(q * q, axis=-1, keepdims=True) + EPS) * (GDN_DK ** -0.5)
                k = k * lax.rsqrt(jnp.sum(k * k, axis=-1, keepdims=True) + EPS)
                beta = beta_all[rs, h:h + 1]
                gc_col = gcum[:, 4 + h:5 + h]
                gc_row = gcum_t[4 + h:5 + h, :]
                g_last = gcum[chunk - 1:chunk, 4 + h:5 + h]
                decay = jnp.where(incl, jnp.exp(jnp.minimum(gc_col - gc_row, 0.0)), 0.0)
                eg = jnp.exp(gc_col)
                kb = k * beta
                k3r = _split3r(k, 1)
                probs.append(dict(
                    b=b, c=c, h=h, decay=decay,
                    kk=_dot_nt(_split3(kb, 1), k3r), qk=_dot_nt(_split3(q, 1), k3r),
                    rhs=jnp.concatenate([v * beta, kb * eg], axis=1), qd=q * eg,
                    kd_t=(k * jnp.exp(g_last - gc_col)).T, gl=jnp.exp(g_last), minv=eye))
    for p in probs:
        p["lmat"] = jnp.where(strict, p["kk"] * p["decay"], 0.0)
        p["qk"] = jnp.where(incl, p["qk"] * p["decay"], 0.0)
    for p in probs:
        p["minv"] = eye - jnp.where(levels[0], p["lmat"], 0.0)
        p["lh"] = p["lmat"].astype(BF16).astype(F32)
        p["ll"] = p["lmat"] - p["lh"]
    for off_blk in levels[1:]:
        for p in probs:
            mh = p["minv"].astype(BF16).astype(F32)
            ml = p["minv"] - mh
            p["mr3"] = jnp.concatenate([mh, ml, mh], axis=0).astype(BF16)
            ch, cl = jnp.where(off_blk, p["lh"], 0.0), jnp.where(off_blk, p["ll"], 0.0)
            p["t"] = _dot(jnp.concatenate([mh, mh, ml], axis=1).astype(BF16),
                          jnp.concatenate([ch, cl, ch], axis=0).astype(BF16))
        for p in probs:
            p["minv"] = p["minv"] - _dot(_split3(p["t"], 1), p["mr3"])
    for p in probs:
        p["uw"] = _dot3(p["minv"], p["rhs"])
    state = {(b, h): s_scr[b, h] for b in range(seqs) for h in range(GDN_HEADS)}
    for c in range(n_chunks):
        cur = [p for p in probs if p["c"] == c]
        for p in cur:
            p["s3r"] = _split3r(state[(p["b"], p["h"])], 0)
            p["v_new"] = p["uw"][:, :GDN_DK] - _dot(_split3(p["uw"][:, GDN_DK:], 1), p["s3r"])
        for p in cur:
            p["o"] = _dot(_split3(p["qd"], 1), p["s3r"]) + _dot3(p["qk"], p["v_new"])
            state[(p["b"], p["h"])] = state[(p["b"], p["h"])] * p["gl"] + _dot3(p["kd_t"], p["v_new"])
    for p in probs:
        b, c, h = p["b"], p["c"], p["h"]
        r0 = b * rows + c * chunk
        o = _rms(p["o"], nw) * _silu(gate_ref[r0:r0 + chunk, h * GDN_DK:(h + 1) * GDN_DK])
        o_ref[r0:r0 + chunk, h * GDN_DK:(h + 1) * GDN_DK] = o
    for (b, h), s in state.items():
        s_scr[b, h] = s

    @pl.when(i == pl.num_programs(1) - 1)
    def _():
        sout_ref[...] = s_scr[...]


def gdn(qkv, small, gate, conv_w, gpar, norm_w, conv_buf8, s0, row_off, n_seq, t_seq, seqs, rows, chunk):
    nb = t_seq // rows
    blk = seqs * rows
    off = row_off // blk
    rmap = lambda n, i: (off + n * nb + i, 0)
    return pl.pallas_call(
        functools.partial(_gdn_kernel, seqs=seqs, rows=rows, chunk=chunk),
        grid=(n_seq // seqs, nb),
        in_specs=[pl.BlockSpec((blk, GDN_QKV), rmap),
                  pl.BlockSpec((blk, 128), rmap),
                  pl.BlockSpec((blk, 512), rmap),
                  pl.BlockSpec((GDN_CONV, GDN_QKV), lambda n, i: (0, 0)),
                  pl.BlockSpec((8, 128), lambda n, i: (0, 0)),
                  pl.BlockSpec((1, GDN_DK), lambda n, i: (0, 0)),
                  pl.BlockSpec((seqs, 8, GDN_QKV), lambda n, i: (n, 0, 0)),
                  pl.BlockSpec((seqs, GDN_HEADS, GDN_DK, GDN_DK), lambda n, i: (n, 0, 0, 0))],
        out_specs=[pl.BlockSpec((blk, 512), lambda n, i: (n * nb + i, 0)),
                   pl.BlockSpec((seqs, GDN_HEADS, GDN_DK, GDN_DK), lambda n, i: (n, 0, 0, 0))],
        out_shape=[jax.ShapeDtypeStruct((n_seq * t_seq, 512), F32),
                   jax.ShapeDtypeStruct((n_seq, GDN_HEADS, GDN_DK, GDN_DK), F32)],
        scratch_shapes=[pltpu.VMEM((seqs, 8 + rows, GDN_QKV), F32),
                        pltpu.VMEM((seqs, GDN_HEADS, GDN_DK, GDN_DK), F32)],
        compiler_params=_cp("parallel", "arbitrary"),
    )(qkv, small, gate, conv_w, gpar, norm_w, conv_buf8, s0)


def _compress_core(xj, posab_ref, w1_ref, w2_ref, m):
    acc = jnp.zeros((m + 8, 4 * CMP_HIDDEN), F32)
    for jj in range(CMP_STRIDE // 2):
        x2 = jnp.concatenate([xj(2 * jj), xj(2 * jj + 1)], axis=1)
        lhs = jnp.concatenate([x2, posab_ref[jj]], axis=0).astype(BF16)
        acc = acc + _dot(lhs, w1_ref[jj])
    top = acc[0:m, 0:2 * CMP_HIDDEN]
    bot = acc[0:m, 2 * CMP_HIDDEN:]
    c = acc[m:m + 1, 0:2 * CMP_HIDDEN] + acc[m + 1:m + 2, 2 * CMP_HIDDEN:]
    h = top + pltpu.roll(bot, m - 1, 0) + c
    out = _dot(_silu(h).astype(BF16), w2_ref[...])
    row = lax.broadcasted_iota(jnp.int32, (m, 2 * NSA_HD), 0)
    return jnp.where(row < m - 1, out, 0.0)


def _compress_prompt_kernel(kc_ref, vc_ref, pk_ref, w1k_ref, w2k_ref, pv_ref, w1v_ref, w2v_ref, ok_ref, ov_ref, *, m):
    ok_ref[0] = _compress_core(lambda j: kc_ref[pl.ds(j, m, stride=CMP_STRIDE), :], pk_ref, w1k_ref, w2k_ref, m)
    ov_ref[0] = _compress_core(lambda j: vc_ref[pl.ds(j, m, stride=CMP_STRIDE), :], pv_ref, w1v_ref, w2v_ref, m)


def _cmp_weight_specs(nmap):
    return [pl.BlockSpec((CMP_STRIDE // 2, 8, 256), nmap(3)),
            pl.BlockSpec((CMP_STRIDE // 2, 256, 4 * CMP_HIDDEN), nmap(3)),
            pl.BlockSpec((2 * CMP_HIDDEN, 2 * NSA_HD), nmap(2))]


def compress_prompt(nkv, cw, n_seq, t_seq):
    m = t_seq // CMP_STRIDE
    zmap = lambda nd: (lambda n: (0,) * nd)
    out = jax.ShapeDtypeStruct((n_seq, m, 2 * NSA_HD), F32)
    return pl.pallas_call(
        functools.partial(_compress_prompt_kernel, m=m),
        grid=(n_seq,),
        in_specs=[pl.BlockSpec((t_seq, 128), lambda n: (n, 4)), pl.BlockSpec((t_seq, 128), lambda n: (n, 5))]
        + _cmp_weight_specs(zmap) + _cmp_weight_specs(zmap),
        out_specs=[pl.BlockSpec((1, m, 2 * NSA_HD), lambda n: (n, 0, 0))] * 2,
        out_shape=[out, out],
        compiler_params=_cp("parallel"),
    )(nkv, nkv, *cw)


def _compress_sample_kernel(pt_ref, *refs, m):
    del pt_ref
    kpages = refs[0:N_PAGES]
    vpages = refs[N_PAGES:2 * N_PAGES]
    pk_ref, w1k_ref, w2k_ref, pv_ref, w1v_ref, w2v_ref, ok_ref, ov_ref, xk_scr, xv_scr = refs[2 * N_PAGES:]
    for p in range(N_PAGES):
        xk_scr[p * PAGE_SIZE:(p + 1) * PAGE_SIZE, :] = kpages[p][...].T
        xv_scr[p * PAGE_SIZE:(p + 1) * PAGE_SIZE, :] = vpages[p][...].T
    ok_ref[0] = _compress_core(lambda j: xk_scr[pl.ds(j, m, stride=CMP_STRIDE), :], pk_ref, w1k_ref, w2k_ref, m)
    ov_ref[0] = _compress_core(lambda j: xv_scr[pl.ds(j, m, stride=CMP_STRIDE), :], pv_ref, w1v_ref, w2v_ref, m)


def _page_specs(layer, nseq=1, b=0):
    return [pl.BlockSpec((None, None, PAGE_SIZE, 128),
                         functools.partial(lambda n, pt, p: (layer, pt[n * nseq + b, p], 0, 0), p=p))
            for p in range(N_PAGES)]


def compress_sample(page_table, cache_k, cache_v, layer, cw):
    m = PAST_LEN // CMP_STRIDE
    zmap = lambda nd: (lambda n, pt: (0,) * nd)
    out = jax.ShapeDtypeStruct((DEC_BATCH, m, 2 * NSA_HD), F32)
    return pl.pallas_call(
        functools.partial(_compress_sample_kernel, m=m),
        grid_spec=pltpu.PrefetchScalarGridSpec(
            num_scalar_prefetch=1, grid=(DEC_BATCH,),
            in_specs=_page_specs(layer) + _page_specs(layer) + _cmp_weight_specs(zmap) + _cmp_weight_specs(zmap),
            out_specs=[pl.BlockSpec((1, m, 2 * NSA_HD), lambda n, pt: (n, 0, 0))] * 2,
            scratch_shapes=[pltpu.VMEM((PAST_LEN, 2 * NSA_HD), F32), pltpu.VMEM((PAST_LEN, 2 * NSA_HD), F32)]),
        out_shape=[out, out],
        compiler_params=_cp("arbitrary"),
    )(page_table, *([cache_k] * N_PAGES), *([cache_v] * N_PAGES), *cw)


def _masked_softmax(s, mask):
    s = jnp.where(mask, s, NEG)
    m = jnp.max(s, axis=-1, keepdims=True)
    p = jnp.where(mask, jnp.exp(s - m), 0.0)
    l = jnp.sum(p, axis=-1, keepdims=True)
    return p / jnp.where(l > 0.0, l, 1.0)


def _stack_heads(nq, g, tq):
    parts = [nq[:, (g * NSA_REP + r) * NSA_HD:(g * NSA_REP + r + 1) * NSA_HD] for r in range(NSA_REP)]
    return (jnp.concatenate(parts, axis=0) * (NSA_HD ** -0.5)).astype(BF16)


def _rep(mask, tq):
    return jnp.concatenate([mask] * NSA_REP, axis=0)


def _importance_t(p, tq, n_blk):
    n_cmp = p.shape[1]
    psum = p[0:tq] + p[tq:2 * tq] + p[2 * tq:3 * tq] + p[3 * tq:4 * tq]
    sj = lax.broadcasted_iota(jnp.int32, (n_blk, n_cmp), 0) * SEL_BLOCK
    ci = lax.broadcasted_iota(jnp.int32, (n_blk, n_cmp), 1) * CMP_STRIDE
    ov = jnp.clip(jnp.minimum(ci + CMP_LEN, sj + SEL_BLOCK) - jnp.maximum(ci, sj), 0, CMP_LEN).astype(F32) / CMP_LEN
    ov = ov.astype(BF16)
    p1 = psum.astype(BF16).astype(F32)
    p2 = (psum - p1).astype(BF16).astype(F32)
    p3 = psum - p1 - p2
    return _dot_nt(jnp.concatenate([ov, ov, ov], axis=1), jnp.concatenate([p1, p2, p3], axis=1).astype(BF16))


def _select_blocks(imp_ts, qpos_row, tq, n_blk):
    blk = lax.broadcasted_iota(jnp.int32, (n_blk, tq), 0)
    rows = qpos_row if isinstance(qpos_row, (list, tuple)) else [qpos_row] * len(imp_ts)
    works = []
    for imp_t, row in zip(imp_ts, rows):
        cur = row // SEL_BLOCK
        valid = blk <= cur
        forced = valid & ((blk == 0) | (blk >= cur - 1))
        works.append(jnp.where(valid, jnp.where(forced, SEL_FORCED, imp_t), NEG))
    sels = [jnp.zeros((n_blk, tq), F32) for _ in imp_ts]
    for _ in range(SEL_TOPN):
        for j in range(len(works)):
            m = jnp.max(works[j], axis=0, keepdims=True)
            idx = jnp.min(jnp.where(works[j] == m, blk, n_blk), axis=0, keepdims=True)
            pick = blk == idx
            sels[j] = jnp.where(pick, 1.0, sels[j])
            works[j] = jnp.where(pick, -jnp.inf, works[j])
    return [s.T for s in sels]


def _expand_sel(sel, first_blk, n_keys):
    n_blk = sel.shape[1]
    bj = lax.broadcasted_iota(jnp.int32, (n_blk, n_keys), 0)
    kb = lax.broadcasted_iota(jnp.int32, (n_blk, n_keys), 1) // SEL_BLOCK + first_blk
    e = jnp.where(bj == kb, 1.0, 0.0).astype(BF16)
    return _dot(sel.astype(BF16), e) > 0.5


def _gate_mix(o_ref, gates, g, o_cmp, o_slc, o_win, tq):
    for r in range(NSA_REP):
        h = g * NSA_REP + r
        rs = slice(r * tq, (r + 1) * tq)
        c0 = 8 + 3 * h
        o = (gates[:, c0:c0 + 1] * o_cmp[rs] + gates[:, c0 + 1:c0 + 2] * o_slc[rs] + gates[:, c0 + 2:c0 + 3] * o_win[rs])
        o_ref[:, h * NSA_HD:(h + 1) * NSA_HD] = o


def _nsa_prompt_kernel(nq_ref, sm_ref, kv_ref, et_ref, kcmp_ref, vcmp_ref, o_ref):
    tq = Q_BLOCK
    nt = NSA_PROMPT_TILES
    kc = 512
    assert (kc // tq) % nt == 0
    step = pl.program_id(1)
    tiles = range(nt)
    probs = [(t, g) for t in tiles for g in range(NSA_KV)]
    gsl = lambda g: slice(g * NSA_HD, (g + 1) * NSA_HD)
    qpos = [(step * nt + t) * tq + lax.broadcasted_iota(jnp.int32, (tq, 1), 0) for t in tiles]
    qrow = [(step * nt + t) * tq + lax.broadcasted_iota(jnp.int32, (1, tq), 1) for t in tiles]
    nq = [nq_ref[t * tq:(t + 1) * tq, :] for t in tiles]
    q4 = {(t, g): _stack_heads(nq[t], g, tq) for t, g in probs}

    n_cmp = kcmp_ref.shape[1]
    cmp_end = lax.broadcasted_iota(jnp.int32, (tq, n_cmp), 1) * CMP_STRIDE + (CMP_LEN - 1)
    s_c = {(t, g): _dot_nt(q4[t, g], kcmp_ref[0, :, gsl(g)].astype(BF16)) for t, g in probs}
    p_c = {(t, g): _masked_softmax(s_c[t, g], _rep(cmp_end <= qpos[t], tq)) for t, g in probs}
    o_cmp = {(t, g): _dot(p_c[t, g].astype(BF16), vcmp_ref[0, :, gsl(g)].astype(BF16)) for t, g in probs}
    sel_list = _select_blocks([_importance_t(p_c[pg], tq, 128) for pg in probs], [qrow[t] for t, _ in probs], tq, 128)
    qas = [jnp.concatenate([_rep(jnp.where(sel > 0.5, 0.0, -SEL_MASK), tq).astype(BF16), q4[pg]], axis=1)
           for sel, pg in zip(sel_list, probs)]

    def slc_step(c, carry, diagonal):
        start = pl.multiple_of(c * kc, kc)
        et = et_ref[pl.ds(start, kc), :]
        kas = [jnp.concatenate([et, kv_ref[pl.ds(start, kc), gsl(g)]], axis=1) for g in range(NSA_KV)]
        ss = [_dot_nt(qas[k], kas[g]) for k, (t, g) in enumerate(probs)]
        if diagonal:
            kpos = start + lax.broadcasted_iota(jnp.int32, (tq, kc), 1)
            ss = [jnp.where(_rep(kpos <= qpos[t], tq), s, -SEL_MASK) for s, (t, g) in zip(ss, probs)]
        m_new = [jnp.maximum(carry[k][0], jnp.max(ss[k], axis=-1, keepdims=True)) for k in range(len(probs))]
        ps = [jnp.exp(ss[k] - m_new[k]) for k in range(len(probs))]
        out = []
        for k, (t, g) in enumerate(probs):
            m_i, l_i, acc = carry[k]
            alpha = jnp.exp(m_i - m_new[k])
            v = kv_ref[pl.ds(start, kc), 128 + g * NSA_HD:128 + (g + 1) * NSA_HD]
            out.append((m_new[k], alpha * l_i + jnp.sum(ps[k], axis=-1, keepdims=True),
                        alpha * acc + _dot(ps[k].astype(BF16), v)))
        return tuple(out)

    init = tuple((jnp.full((NSA_REP * tq, 1), -3.0e38, F32), jnp.zeros((NSA_REP * tq, 1), F32),
                  jnp.zeros((NSA_REP * tq, NSA_HD), F32)) for _ in probs)
    n_full = (step * nt) // (kc // tq)
    carry = lax.fori_loop(0, n_full, functools.partial(slc_step, diagonal=False), init)
    fin = slc_step(n_full, carry, True)

    nband = WINDOW + tq
    wstart = [pl.multiple_of(jnp.maximum(step * nt + t - WINDOW // tq, 0) * tq, tq) for t in tiles]
    win_ok = []
    for t in tiles:
        d = qpos[t] - (wstart[t] + lax.broadcasted_iota(jnp.int32, (tq, nband), 1))
        win_ok.append(_rep((d >= 0) & (d <= WINDOW), tq))
    s_w = [_dot_nt(q4[t, g], kv_ref[pl.ds(wstart[t], nband), 256 + g * NSA_HD:256 + (g + 1) * NSA_HD]) for t, g in probs]
    p_w = [_masked_softmax(s_w[k], win_ok[t]).astype(BF16) for k, (t, g) in enumerate(probs)]
    o_win = [_dot(p_w[k], kv_ref[pl.ds(wstart[t], nband), 384 + g * NSA_HD:384 + (g + 1) * NSA_HD])
             for k, (t, g) in enumerate(probs)]

    for t in tiles:
        gates = _sigmoid(sm_ref[t * tq:(t + 1) * tq, :])
        for g in range(NSA_KV):
            k = t * NSA_KV + g
            _, l_f, acc = fin[k]
            o_slc = acc / jnp.where(l_f > 0.0, l_f, 1.0)
            _gate_mix(o_ref.at[t * tq:(t + 1) * tq], gates, g, o_cmp[t, g], o_slc, o_win[k], tq)


NSA_PROMPT_TILES = 2


def nsa_prompt(nq, small, kv_bf, kcmp, vcmp, n_seq, t_seq):
    rows = NSA_PROMPT_TILES * Q_BLOCK
    nb = t_seq // rows
    m = kcmp.shape[1]
    key_blk = jnp.arange(t_seq, dtype=jnp.int32)[:, None] // SEL_BLOCK
    et = (key_blk == jnp.arange(128, dtype=jnp.int32)[None, :]).astype(BF16)
    return pl.pallas_call(
        _nsa_prompt_kernel,
        grid=(n_seq, nb),
        in_specs=[pl.BlockSpec((rows, 512), lambda n, i: (n * nb + i, 0)),
                  pl.BlockSpec((rows, 128), lambda n, i: (n * nb + i, 0)),
                  pl.BlockSpec((t_seq, 512), lambda n, i: (n, 0)),
                  pl.BlockSpec((t_seq, 128), lambda n, i: (0, 0)),
                  pl.BlockSpec((1, m, 128), lambda n, i: (n, 0, 0)),
                  pl.BlockSpec((1, m, 128), lambda n, i: (n, 0, 0))],
        out_specs=pl.BlockSpec((rows, 512), lambda n, i: (n * nb + i, 0)),
        out_shape=jax.ShapeDtypeStruct((n_seq * t_seq, 512), F32),
        compiler_params=_cp("parallel", "arbitrary"),
    )(nq, small, kv_bf, et, kcmp, vcmp)


NSA_SAMPLE_SEQS = 4


def _nsa_sample_kernel(pt_ref, *refs):
    del pt_ref
    nseq = NSA_SAMPLE_SEQS
    per = 2 * N_PAGES + 2
    seq_refs = [refs[b * per:(b + 1) * per] for b in range(nseq)]
    nq_ref, sm_ref, kvn_ref, kcmp_ref, vcmp_ref, o_ref = refs[nseq * per:]
    tq = DEC_SEQ
    seqs = range(nseq)
    qpos = PAST_LEN + lax.broadcasted_iota(jnp.int32, (tq, 1), 0)
    qpos_row = PAST_LEN + lax.broadcasted_iota(jnp.int32, (1, tq), 1)
    n_keys = (N_PAGES + 1) * PAGE_SIZE
    n_blk = 40
    nr = NSA_REP * tq
    pad = jnp.zeros((PAGE_SIZE - tq, 128), F32)
    z = jnp.zeros((nr, NSA_HD), BF16)
    both = lambda m: jnp.concatenate([_rep(m, tq)] * NSA_KV, axis=0)
    n_cmp = kcmp_ref.shape[1]
    cmp_ok = both(lax.broadcasted_iota(jnp.int32, (tq, n_cmp), 1) * CMP_STRIDE + (CMP_LEN - 1) <= qpos)
    d = qpos - (PAST_LEN - WINDOW + lax.broadcasted_iota(jnp.int32, (tq, WINDOW + PAGE_SIZE), 1))
    win_ok = both((d >= 0) & (d <= WINDOW))
    causal = lax.broadcasted_iota(jnp.int32, (tq, n_keys), 1) <= qpos

    new, q_bd = [], []
    for b in seqs:
        rows = slice(b * tq, (b + 1) * tq)
        kvn = kvn_ref[rows, :]
        new.append([jnp.concatenate([kvn[:, c * 128:(c + 1) * 128], pad], axis=0).astype(BF16) for c in range(4)])
        nq = nq_ref[rows, :]
        q_bd.append(jnp.concatenate([jnp.concatenate([_stack_heads(nq, 0, tq), z], axis=1),
                                     jnp.concatenate([z, _stack_heads(nq, 1, tq)], axis=1)], axis=0))
    s_c = [_dot_nt(q_bd[b], kcmp_ref[b].astype(BF16)) for b in seqs]
    s = [jnp.concatenate([_dot(q_bd[b], p[...].astype(BF16)) for p in seq_refs[b][:N_PAGES]]
                         + [_dot_nt(q_bd[b], new[b][0])], axis=1) for b in seqs]
    sw = [jnp.concatenate([_dot(q_bd[b], seq_refs[b][2 * N_PAGES][...].astype(BF16)), _dot_nt(q_bd[b], new[b][2])],
                          axis=1) for b in seqs]
    p_c = [_masked_softmax(s_c[b], cmp_ok) for b in seqs]
    o_c = [_dot(p_c[b].astype(BF16), vcmp_ref[b].astype(BF16)) for b in seqs]
    pw = [_masked_softmax(sw[b], win_ok).astype(BF16) for b in seqs]
    o_w = [_dot_nt(pw[b][:, :WINDOW], seq_refs[b][2 * N_PAGES + 1][...].astype(BF16)) + _dot(pw[b][:, WINDOW:], new[b][3])
           for b in seqs]
    sels = _select_blocks([_importance_t(p_c[b][g * nr:(g + 1) * nr], tq, n_blk) for b in seqs for g in range(NSA_KV)],
                          qpos_row, tq, n_blk)
    p = []
    for b in seqs:
        msk = jnp.concatenate([_rep(_expand_sel(sels[b * NSA_KV + g], 0, n_keys) & causal, tq) for g in range(NSA_KV)],
                              axis=0)
        p.append(_masked_softmax(s[b], msk).astype(BF16))
    o_s = [_dot(p[b][:, N_PAGES * PAGE_SIZE:], new[b][1]) for b in seqs]
    for c in range(N_PAGES):
        o_s = [o_s[b] + _dot_nt(p[b][:, c * PAGE_SIZE:(c + 1) * PAGE_SIZE], seq_refs[b][N_PAGES + c][...].astype(BF16))
               for b in seqs]
    for b in seqs:
        gates = _sigmoid(sm_ref[b * tq:(b + 1) * tq, :])
        for g in range(NSA_KV):
            rs, gs = slice(g * nr, (g + 1) * nr), slice(g * NSA_HD, (g + 1) * NSA_HD)
            _gate_mix(o_ref.at[b * tq:(b + 1) * tq], gates, g, o_c[b][rs, gs], o_s[b][rs, gs], o_w[b][rs, gs], tq)


def nsa_sample(page_table, cache_k, cache_v, layer, nq, small, nkv, kcmp, vcmp, win_k, win_v):
    nseq = NSA_SAMPLE_SEQS
    rows = nseq * DEC_SEQ
    off = RP // rows
    rmap = lambda n, pt: (off + n, 0)
    m = kcmp.shape[1]
    specs, args = [], []
    for b in range(nseq):
        specs += _page_specs(layer, nseq, b) + _page_specs(layer, nseq, b)
        args += [cache_k] * N_PAGES + [cache_v] * N_PAGES
        wspec = pl.BlockSpec((None, None, 128, WINDOW), functools.partial(lambda n, pt, b: (layer, n * nseq + b, 0, 0), b=b))
        specs += [wspec, wspec]
        args += [win_k, win_v]
    return pl.pallas_call(
        _nsa_sample_kernel,
        grid_spec=pltpu.PrefetchScalarGridSpec(
            num_scalar_prefetch=1, grid=(DEC_BATCH // nseq,),
            in_specs=specs + [
                pl.BlockSpec((rows, 512), rmap),
                pl.BlockSpec((rows, 128), rmap),
                pl.BlockSpec((rows, 768), rmap),
                pl.BlockSpec((nseq, m, 128), lambda n, pt: (n, 0, 0)),
                pl.BlockSpec((nseq, m, 128), lambda n, pt: (n, 0, 0))],
            out_specs=pl.BlockSpec((rows, 512), lambda n, pt: (n, 0))),
        out_shape=jax.ShapeDtypeStruct((RS, 512), F32),
        compiler_params=_cp("arbitrary"),
    )(page_table, *args, nq, small, nkv, kcmp, vcmp)


S5_TILE = 256
S5_LANES = 2048
S5_BLK = 16


def _gelu_tanh(x):
    return 0.5 * x * (1.0 + jnp.tanh(math.sqrt(2.0 / math.pi) * (x + 0.044715 * (x * x * x))))


def _s5_kernel(u_ref, pm_ref, pt3_ref, bdr_ref, bdi_ref, cr_ref, ci_ref, tab_ref, d_ref, x0r_ref, x0i_ref,
               y_ref, fr_ref, fi_ref, xr_scr, xi_scr, car_scr, *, chained):
    i = pl.program_id(1)
    nb = S5_GROUPS // S5_BLK
    wi = S5_BLK * S5_CH
    ws = S5_BLK * S5_P
    seg = S5_TILE // 8
    u = u_ref[...]
    ub = _dot(pm_ref[...], u.astype(BF16)).astype(BF16)
    for b in range(nb):
        xr_scr[:, b * ws:(b + 1) * ws] = _dot(ub[:, b * wi:(b + 1) * wi], bdr_ref[b])
        xi_scr[:, b * ws:(b + 1) * ws] = _dot(ub[:, b * wi:(b + 1) * wi], bdi_ref[b])

    if chained:
        @pl.when(i == 0)
        def _():
            car_scr[0:1, :] = x0r_ref[0]
            car_scr[1:2, :] = x0i_ref[0]

    def cmad(br, bi, mr, mi, xr, xi):
        return br + mr * xr - mi * xi, bi + mr * xi + mi * xr

    for c in range(S5_STATE // S5_LANES):
        ls = slice(c * S5_LANES, (c + 1) * S5_LANES)
        ar, ai = tab_ref[0, 0:1, ls], tab_ref[1, 0:1, ls]
        if chained:
            xr, xi = jnp.zeros((8, S5_LANES), F32), jnp.zeros((8, S5_LANES), F32)
            for k in range(seg):
                rs = slice(8 * k, 8 * k + 8)
                xr, xi = cmad(xr_scr[rs, ls], xi_scr[rs, ls], ar, ai, xr, xi)
                xr_scr[rs, ls] = xr
                xi_scr[rs, ls] = xi
            yr, yi = xr, xi
            for sh, r0 in ((1, seg), (2, seg + 8), (4, seg + 16)):
                yr, yi = cmad(yr, yi, tab_ref[0, r0:r0 + 8, ls], tab_ref[1, r0:r0 + 8, ls],
                              pltpu.roll(yr, sh, 0), pltpu.roll(yi, sh, 0))
            first = lax.broadcasted_iota(jnp.int32, (8, S5_LANES), 0) == 0
            yr = jnp.where(first, 0.0, pltpu.roll(yr, 1, 0))
            yi = jnp.where(first, 0.0, pltpu.roll(yi, 1, 0))
            sr, si = cmad(yr, yi, tab_ref[0, seg + 24:seg + 32, ls], tab_ref[1, seg + 24:seg + 32, ls],
                          car_scr[0:1, ls], car_scr[1:2, ls])
            for k in range(seg):
                rs = slice(8 * k, 8 * k + 8)
                xr, xi = cmad(xr_scr[rs, ls], xi_scr[rs, ls], tab_ref[0, k:k + 1, ls], tab_ref[1, k:k + 1, ls], sr, si)
                xr_scr[rs, ls] = xr
                xi_scr[rs, ls] = xi
            car_scr[0:1, ls] = xr[7:8, :]
            car_scr[1:2, ls] = xi[7:8, :]
        else:
            for q in range(S5_TILE // 64):
                xr, xi = x0r_ref[0, 8 * q:8 * q + 8, ls], x0i_ref[0, 8 * q:8 * q + 8, ls]
                for t in range(8):
                    rs = slice(8 * (8 * q + t), 8 * (8 * q + t) + 8)
                    xr, xi = cmad(xr_scr[rs, ls], xi_scr[rs, ls], ar, ai, xr, xi)
                    xr_scr[rs, ls] = xr
                    xi_scr[rs, ls] = xi
                fr_ref[0, 8 * q:8 * q + 8, ls] = xr
                fi_ref[0, 8 * q:8 * q + 8, ls] = xi

    ys = []
    for b in range(nb):
        ys.append(_dot(xr_scr[:, b * ws:(b + 1) * ws].astype(BF16), cr_ref[b])
                  - _dot(xi_scr[:, b * ws:(b + 1) * ws].astype(BF16), ci_ref[b]))
    yp = jnp.concatenate(ys, axis=1)
    h1 = yp.astype(BF16).astype(F32)
    h2 = (yp - h1).astype(BF16).astype(F32)
    ylin = _dot(pt3_ref[...], jnp.concatenate([h1, h2, yp - h1 - h2], axis=0).astype(BF16))
    y_ref[...] = _gelu_tanh(ylin + d_ref[...] * u).astype(BF16)

    if chained:
        @pl.when(i == pl.num_programs(1) - 1)
        def _():
            fr_ref[0] = car_scr[0:1, :]
            fi_ref[0] = car_scr[1:2, :]


def s5_scan(u, sp, x0r, x0i, row_off, n_seq, t_seq, chained):
    if chained:
        grid = (n_seq, t_seq // S5_TILE)
        nb = grid[1]
        smap = lambda n, i: (n, 0, 0)
    else:
        grid = (1, n_seq * t_seq // S5_TILE)
        nb = grid[1]
        smap = lambda n, i: (i, 0, 0)
    off = row_off // S5_TILE
    rmap = lambda n, i: (off + n * nb + i, 0)
    sblk = (1,) + x0r.shape[1:]
    const = lambda nd: (lambda n, i: (0,) * nd)
    rho = jnp.arange(S5_TILE)
    if chained:
        src = (S5_TILE // 8) * (rho % 8) + rho // 8
    else:
        src = 64 * (rho // 64) + 8 * (rho % 8) + (rho // 8) % 8
    pm = (src[:, None] == jnp.arange(S5_TILE)[None, :]).astype(BF16)
    pt3 = jnp.tile(pm.T, (1, 3))
    in_specs = [pl.BlockSpec((S5_TILE, D_MODEL), rmap),
                pl.BlockSpec((S5_TILE, S5_TILE), const(2)), pl.BlockSpec((S5_TILE, 3 * S5_TILE), const(2)),
                pl.BlockSpec(sp["bdr"].shape, const(3)), pl.BlockSpec(sp["bdi"].shape, const(3)),
                pl.BlockSpec(sp["cr"].shape, const(3)), pl.BlockSpec(sp["ci"].shape, const(3)),
                pl.BlockSpec(sp["tab"].shape, const(3)), pl.BlockSpec((1, D_MODEL), const(2)),
                pl.BlockSpec(sblk, smap), pl.BlockSpec(sblk, smap)]
    args = [u, pm, pt3, sp["bdr"], sp["bdi"], sp["cr"], sp["ci"], sp["tab"], sp["d"], x0r, x0i]
    return pl.pallas_call(
        functools.partial(_s5_kernel, chained=chained),
        grid=grid,
        in_specs=in_specs,
        out_specs=[pl.BlockSpec((S5_TILE, D_MODEL), lambda n, i: (n * nb + i, 0)),
                   pl.BlockSpec(sblk, smap), pl.BlockSpec(sblk, smap)],
        out_shape=[jax.ShapeDtypeStruct((n_seq * t_seq, D_MODEL), BF16), jax.ShapeDtypeStruct(x0r.shape, F32),
                   jax.ShapeDtypeStruct(x0r.shape, F32)],
        scratch_shapes=[pltpu.VMEM((S5_TILE, S5_STATE), F32), pltpu.VMEM((S5_TILE, S5_STATE), F32),
                        pltpu.VMEM((8, S5_STATE), F32)],
        compiler_params=_cp("arbitrary", "arbitrary"),
    )(*args)


def _prep_ab_in(w):
    qkv, b, a, gate, nq, nkv, ngate = jnp.split(w, (1536, 1540, 1544, 2056, 2568, 3336), axis=-1)
    kc, vc, ks, vs, kw, vw = jnp.split(nkv, 6, axis=-1)
    small = jnp.concatenate([b, a, ngate, jnp.zeros(w.shape[:-1] + (96,), w.dtype)], axis=-1)
    return jnp.concatenate([qkv, ks, vs, kw, vw, kc, vc, gate, nq, small], axis=-1).astype(BF16)


AB_SPLITS = (1536, 768, 512, 512, 128)


def _prep_cmp(pos, w1, w2):
    top = w1[:CMP_STRIDE * NSA_HD].reshape(CMP_STRIDE, NSA_HD, CMP_HIDDEN)
    bot = w1[CMP_STRIDE * NSA_HD:].reshape(CMP_STRIDE, NSA_HD, CMP_HIDDEN)
    w1bd = jnp.zeros((CMP_STRIDE, 2 * NSA_HD, 4 * CMP_HIDDEN), F32)
    w2bd = jnp.zeros((2 * CMP_HIDDEN, 2 * NSA_HD), F32)
    for g in range(NSA_KV):
        rs = slice(g * NSA_HD, (g + 1) * NSA_HD)
        w1bd = w1bd.at[:, rs, g * CMP_HIDDEN:(g + 1) * CMP_HIDDEN].set(top)
        w1bd = w1bd.at[:, rs, (2 + g) * CMP_HIDDEN:(3 + g) * CMP_HIDDEN].set(bot)
        w2bd = w2bd.at[g * CMP_HIDDEN:(g + 1) * CMP_HIDDEN, rs].set(w2)
    posab = jnp.zeros((CMP_STRIDE, 8, 2 * NSA_HD), F32)
    posab = posab.at[:, 0, :].set(jnp.tile(pos[:CMP_STRIDE], (1, NSA_KV)))
    posab = posab.at[:, 1, :].set(jnp.tile(pos[CMP_STRIDE:], (1, NSA_KV)))
    half = CMP_STRIDE // 2
    posab = posab.reshape(half, 2, 8, 2 * NSA_HD).transpose(0, 2, 1, 3).reshape(half, 8, 4 * NSA_HD)
    w1bd = w1bd.reshape(half, 4 * NSA_HD, 4 * CMP_HIDDEN)
    return posab, w1bd.astype(BF16), w2bd.astype(BF16)


def _cmul(ar, ai, br, bi):
    return ar * br - ai * bi, ar * bi + ai * br


def _prep_s5(a_re, a_im, b_re, b_im, c_re, c_im, d, log_dt):
    dt = jnp.exp(log_dt)[:, None]
    lr = jnp.minimum(a_re, S5_MAX_RE)
    li = a_im
    mag = jnp.exp(lr * dt)
    ar = mag * jnp.cos(li * dt)
    ai = mag * jnp.sin(li * dt)
    den = lr * lr + li * li
    fr = ((ar - 1.0) * lr + ai * li) / den
    fi = (ai * lr - (ar - 1.0) * li) / den
    bbr = fr[..., None] * b_re - fi[..., None] * b_im
    bbi = fr[..., None] * b_im + fi[..., None] * b_re
    nb = S5_GROUPS // S5_BLK
    eye = jnp.eye(S5_BLK, dtype=F32)

    def bd_in(m):
        m4 = jnp.swapaxes(m, 1, 2).reshape(nb, S5_BLK, S5_CH, S5_P)
        return jnp.einsum('bgcp,gh->bgchp', m4, eye).reshape(nb, S5_BLK * S5_CH, S5_BLK * S5_P).astype(BF16)

    def bd_out(m):
        m4 = jnp.swapaxes(m, 1, 2).reshape(nb, S5_BLK, S5_P, S5_CH)
        return jnp.einsum('bgpc,gh->bgphc', m4, eye).reshape(nb, S5_BLK * S5_P, S5_BLK * S5_CH).astype(BF16)

    seg = S5_TILE // 8
    a1 = (ar.reshape(1, S5_STATE), ai.reshape(1, S5_STATE))
    pw = [a1]
    for _ in range(seg - 1):
        pw.append(_cmul(*pw[-1], *a1))
    row = jnp.arange(8)[:, None]
    s1 = pw[-1]
    s2 = _cmul(*s1, *s1)
    s4 = _cmul(*s2, *s2)
    sp = [(jnp.ones_like(a1[0]), jnp.zeros_like(a1[1]))]
    for _ in range(7):
        sp.append(_cmul(*sp[-1], *s1))
    parts = []
    for j in range(2):
        parts.append(jnp.concatenate([p[j] for p in pw]
                                     + [jnp.where(row >= sh, s[j], 0.0) for sh, s in ((1, s1), (2, s2), (4, s4))]
                                     + [p[j] for p in sp], axis=0))
    return {"bdr": bd_in(bbr), "bdi": bd_in(bbi), "cr": bd_out(c_re), "ci": bd_out(c_im),
            "tab": jnp.stack(parts), "d": d.reshape(1, D_MODEL)}


def _heads(a, n, t):
    return a.reshape(n, t, NSA_KV, NSA_HD)


def kernel(x_prompt, x_sample, mem_prompt, cache_mem_k, cache_mem_v, state_gdn, state_gdn_conv, cache_cmp_k, cache_cmp_v, cache_slc_k, cache_slc_v, cache_win_k, cache_win_v, state_s5_re, state_s5_im, page_table, norm_ffn1, w_ffn1_gate, w_ffn1_up, w_ffn1_down, norm_mix, norm_xq, norm_mem, w_xq, w_xk, w_xv, w_xo, norm_ffn2, w_ffn2_gate, w_ffn2_up, w_ffn2_down, norm_final, w_in_ab, w_out_ab, gdn_conv, gdn_a_log, gdn_dt_bias, gdn_norm, cmp_pos_k, cmp_w1_k, cmp_w2_k, cmp_pos_v, cmp_w1_v, cmp_w2_v, w_in_c, s5_a_re, s5_a_im, s5_b_re, s5_b_im, s5_c_re, s5_c_im, s5_d, s5_log_dt, w_glu, w_out_c):
    bf = lambda w: w.astype(BF16)
    n_ab = w_in_ab.shape[0]
    n_pool = cache_cmp_k.shape[1]
    x = jnp.concatenate([x_prompt.reshape(RP, D_MODEL), x_sample.reshape(RS, D_MODEL)], axis=0)

    memkv = mem_kv_all(mem_prompt.reshape(BATCH * N_MEM, D_MODEL), norm_mem, bf(jnp.concatenate([w_xk, w_xv], axis=-1)))
    mem_k_prompt = memkv[:, :, :D_MODEL].reshape(DEPTH, BATCH, N_MEM, X_HEADS, X_HD)
    mem_v_prompt = memkv[:, :, D_MODEL:].reshape(DEPTH, BATCH, N_MEM, X_HEADS, X_HD)

    fmaj = lambda c: jnp.transpose(c, (0, 1, 3, 4, 2)).reshape(c.shape[0], c.shape[1], NSA_KV * NSA_HD, c.shape[2])
    cck, ccv = fmaj(cache_cmp_k), fmaj(cache_cmp_v)
    csk, csv, cwk, cwv = fmaj(cache_slc_k), fmaj(cache_slc_v), fmaj(cache_win_k), fmaj(cache_win_v)
    ab_p, ab_s, c_p, c_s = [], [], [], []
    for l in range(DEPTH):
        i = l // 2
        x = glu_mlp(x, norm_ffn1[l], bf(w_ffn1_gate[l]), bf(w_ffn1_up[l]), bf(w_ffn1_down[l]), None, 1024, 256, True, True, 0.5)
        if l % 2 == 0:
            qkv, nkv, gate, nq, small = norm_matmul(x, norm_mix[l], _prep_ab_in(w_in_ab[i]), AB_SPLITS, 512)
            gpar = jnp.zeros((8, 128), F32).at[0, 4:8].set(gdn_a_log[i]).at[1, 4:8].set(gdn_dt_bias[i])
            nw = gdn_norm[i].reshape(1, GDN_DK)
            cb_s = jnp.zeros((DEC_BATCH, 8, GDN_QKV), F32).at[:, 5:8].set(state_gdn_conv[i])
            oa_p, st_p = gdn(qkv, small, gate, gdn_conv[i], gpar, nw, jnp.zeros((BATCH, 8, GDN_QKV), F32),
                             jnp.zeros((BATCH, GDN_HEADS, GDN_DK, GDN_DK), F32), 0, BATCH, SEQ, 1, 256, GDN_CHUNK)
            oa_s, st_s = gdn(qkv, small, gate, gdn_conv[i], gpar, nw, cb_s, state_gdn[i], RP, DEC_BATCH, DEC_SEQ,
                             8, DEC_SEQ, math.gcd(DEC_SEQ, GDN_CHUNK))
            cw = _prep_cmp(cmp_pos_k[i], cmp_w1_k[i], cmp_w2_k[i]) + _prep_cmp(cmp_pos_v[i], cmp_w1_v[i], cmp_w2_v[i])
            kcmp_p, vcmp_p = compress_prompt(nkv, cw, BATCH, SEQ)
            ob_p = nsa_prompt(nq, small, bf(nkv[:RP, :512]), kcmp_p, vcmp_p, BATCH, SEQ)
            kcmp_s, vcmp_s = compress_sample(page_table, cck, ccv, i, cw)
            ob_s = nsa_sample(page_table, csk, csv, i, nq, small, nkv, kcmp_s, vcmp_s, cwk, cwv)
            w_out = bf(w_out_ab[i])
            x = matmul_residual([(oa_p, oa_s, w_out[:512]), (ob_p, ob_s, w_out[512:])], x, 512)
            conv_p = jnp.stack([qkv[(n + 1) * SEQ - 3:(n + 1) * SEQ] for n in range(BATCH)])
            qkv_s = qkv[RP:].reshape(DEC_BATCH, DEC_SEQ, GDN_QKV)
            col = lambda a, c: a[:, c * 128:(c + 1) * 128]
            nkv_p, nkv_s = nkv[:RP], nkv[RP:]
            hp = lambda c: _heads(col(nkv_p, c), BATCH, SEQ)
            hs = lambda c: _heads(col(nkv_s, c), DEC_BATCH, DEC_SEQ)
            ab_p.append((conv_p, st_p, hp(4), hp(5), hp(0), hp(1), hp(2)[:, SEQ - WINDOW:], hp(3)[:, SEQ - WINDOW:]))
            ab_s.append((qkv_s[:, DEC_SEQ - 3:], st_s, hs(4), hs(5), hs(0), hs(1),
                         jnp.concatenate([cache_win_k[i][:, DEC_SEQ:], hs(2)], axis=1),
                         jnp.concatenate([cache_win_v[i][:, DEC_SEQ:], hs(3)], axis=1)))
        else:
            (u,) = norm_matmul(x, norm_mix[l], bf(w_in_c[i]), (D_MODEL,), 512)
            sp = _prep_s5(s5_a_re[i], s5_a_im[i], s5_b_re[i], s5_b_im[i], s5_c_re[i], s5_c_im[i], s5_d[i], s5_log_dt[i])
            z0 = jnp.zeros((BATCH, 1, S5_STATE), F32)
            y_p, fr_p, fi_p = s5_scan(u, sp, z0, z0, 0, BATCH, SEQ, True)
            per_tile = S5_TILE // DEC_SEQ
            x0r = state_s5_re[i].reshape(RS // S5_TILE, per_tile, S5_STATE)
            x0i = state_s5_im[i].reshape(RS // S5_TILE, per_tile, S5_STATE)
            y_s, fr_s, fi_s = s5_scan(u, sp, x0r, x0i, RP, DEC_BATCH, DEC_SEQ, False)
            wg = bf(w_glu[i])
            x = glu_mlp(y_p, norm_mix[l], wg[:, :D_MODEL], wg[:, D_MODEL:], bf(w_out_c[i]), x, 1024, 256, False, False, 1.0,
                        src_s=y_s)
            c_p.append((fr_p.reshape(BATCH, S5_GROUPS, S5_P), fi_p.reshape(BATCH, S5_GROUPS, S5_P)))
            c_s.append((fr_s.reshape(DEC_BATCH, S5_GROUPS, S5_P), fi_s.reshape(DEC_BATCH, S5_GROUPS, S5_P)))
        wq, wo = bf(w_xq[l]), bf(w_xo[l])
        (q_s,) = norm_matmul(x, norm_xq[l], wq, (D_MODEL,), 512, row_off=RP, rows=RS)
        o_s = cross_attention_cached(q_s, cache_mem_k, cache_mem_v, l, 0)
        x = cross_attention_block(x, norm_xq[l], wq, wo, memkv, l, 512)
        x = matmul_residual_rows(o_s, wo, x, RP, 512)
        x = glu_mlp(x, norm_ffn2[l], bf(w_ffn2_gate[l]), bf(w_ffn2_up[l]), bf(w_ffn2_down[l]), None, 1024, 256, True, True, 0.5)

    y = rmsnorm_rows(x, norm_final, 512)
    st = lambda grp, j: jnp.stack([t[j] for t in grp])
    return (y[:RP].reshape(BATCH, SEQ, D_MODEL), y[RP:].reshape(DEC_BATCH, DEC_SEQ, D_MODEL),
            mem_k_prompt, mem_v_prompt,
            st(ab_p, 1), st(ab_s, 1), st(ab_p, 0), st(ab_s, 0),
            st(ab_p, 2), st(ab_p, 3), st(ab_p, 4), st(ab_p, 5),
            st(ab_s, 2), st(ab_s, 3), st(ab_s, 4), st(ab_s, 5),
            st(ab_p, 6), st(ab_p, 7), st(ab_s, 6), st(ab_s, 7),
            st(c_p, 0), st(c_p, 1), st(c_s, 0), st(c_s, 1))
```

```python
import functools
import math

import jax
import jax.numpy as jnp
from jax import lax
from jax.experimental import pallas as pl
from jax.experimental.pallas import tpu as pltpu

F32 = jnp.float32
BF16 = jnp.bfloat16
HIGHEST = lax.Precision.HIGHEST

D_MODEL = 1024
BATCH = 2
SEQ = 8192
DEPTH = 4
DEC_BATCH = 128
DEC_SEQ = 8
PAST_LEN = 2048
PAGE_SIZE = 128
N_PAGES = PAST_LEN // PAGE_SIZE
RP = BATCH * SEQ
RS = DEC_BATCH * DEC_SEQ
ROWS = RP + RS

GDN_HEADS = 4
GDN_DK = 128
GDN_QKV = 1536
GDN_CONV = 4
GDN_CHUNK = 64
NSA_HEADS = 8
NSA_KV = 2
NSA_HD = 64
NSA_REP = 4
CMP_STRIDE = 16
CMP_LEN = 32
CMP_HIDDEN = 128
SEL_BLOCK = 64
SEL_TOPN = 16
WINDOW = 512
Q_BLOCK = 128
SEL_FORCED = 1.0e4
NEG = -1.0e30
SEL_MASK = 2.0 ** 100
S5_CH = 16
S5_GROUPS = 64
S5_P = 64
S5_STATE = S5_GROUPS * S5_P
S5_MAX_RE = -1.0e-4
N_MEM = 256
X_HEADS = 4
X_HD = 256
D_FF = 2816
EPS = 1.0e-6

VMEM_LIMIT = 56 * 1024 * 1024


def _cp(*sem):
    return pltpu.CompilerParams(dimension_semantics=sem, vmem_limit_bytes=VMEM_LIMIT)


def _dot(a, b, precision=None):
    return jnp.dot(a, b, preferred_element_type=F32, precision=precision)


def _dot_nt(a, b, precision=None):
    return lax.dot_general(a, b, (((1,), (1,)), ((), ())), preferred_element_type=F32, precision=precision)


def _sigmoid(x):
    return 1.0 / (1.0 + jnp.exp(-x))


def _silu(x):
    return x * _sigmoid(x)


def _rms(x, g):
    return x * lax.rsqrt(jnp.mean(x * x, axis=-1, keepdims=True) + EPS) * g


def _norm_matmul_kernel(x_ref, g_ref, w_ref, *o_refs, norm, splits):
    x = x_ref[...]
    if norm:
        x = _rms(x, g_ref[...])
    h = x.astype(BF16)
    off = 0
    for o_ref, wd in zip(o_refs, splits):
        o_ref[...] = _dot(h, w_ref[:, off:off + wd])
        off += wd


def norm_matmul(x, g, w, splits, tm, norm=True, row_off=0, rows=None):
    k = x.shape[1]
    rows = x.shape[0] if rows is None else rows
    n = w.shape[1]
    off = row_off // tm
    assert sum(splits) == n and rows % tm == 0 and row_off % tm == 0
    outs = pl.pallas_call(
        functools.partial(_norm_matmul_kernel, norm=norm, splits=tuple(splits)),
        grid=(rows // tm,),
        in_specs=[pl.BlockSpec((tm, k), lambda i: (off + i, 0)),
                  pl.BlockSpec((1, k), lambda i: (0, 0)),
                  pl.BlockSpec((k, n), lambda i: (0, 0))],
        out_specs=[pl.BlockSpec((tm, wd), lambda i: (i, 0)) for wd in splits],
        out_shape=[jax.ShapeDtypeStruct((rows, wd), F32) for wd in splits],
        compiler_params=_cp("parallel"),
    )(x, g.reshape(1, k), w)
    return outs


def _mem_kv_kernel(x_ref, g_ref, w_ref, o_ref):
    h = _rms(x_ref[...], g_ref[0]).astype(BF16)
    o_ref[0] = _dot(h, w_ref[0])


def mem_kv_all(mem2d, g, w):
    m, k = mem2d.shape
    nl, _, n = w.shape
    return pl.pallas_call(
        _mem_kv_kernel,
        grid=(nl,),
        in_specs=[pl.BlockSpec((m, k), lambda l: (0, 0)),
                  pl.BlockSpec((1, 1, k), lambda l: (l, 0, 0)),
                  pl.BlockSpec((1, k, n), lambda l: (l, 0, 0))],
        out_specs=pl.BlockSpec((1, m, n), lambda l: (l, 0, 0)),
        out_shape=jax.ShapeDtypeStruct((nl, m, n), F32),
        compiler_params=_cp("parallel"),
    )(mem2d, g.reshape(nl, 1, k), w)


def _matmul_res_kernel(*refs, n_terms, n_p):
    r_ref, o_ref = refs[3 * n_terms], refs[3 * n_terms + 1]
    i = pl.program_id(0)

    def run(sel):
        acc = r_ref[...]
        for t in range(n_terms):
            acc = acc + _dot(refs[3 * t + sel][...].astype(BF16), refs[3 * t + 2][...])
        o_ref[...] = acc

    @pl.when(i < n_p)
    def _():
        run(0)

    @pl.when(i >= n_p)
    def _():
        run(1)


def matmul_residual(terms, res, tm):
    rows, n = res.shape
    n_p = RP // tm
    args, specs = [], []
    for a_p, a_s, w in terms:
        k = w.shape[0]
        args += [a_p, a_s, w]
        specs += [pl.BlockSpec((tm, k), lambda i: (jnp.minimum(i, n_p - 1), 0)),
                  pl.BlockSpec((tm, k), lambda i: (jnp.maximum(i - n_p, 0), 0)),
                  pl.BlockSpec((k, n), lambda i: (0, 0))]
    return pl.pallas_call(
        functools.partial(_matmul_res_kernel, n_terms=len(terms), n_p=n_p),
        grid=(rows // tm,),
        in_specs=specs + [pl.BlockSpec((tm, n), lambda i: (i, 0))],
        out_specs=pl.BlockSpec((tm, n), lambda i: (i, 0)),
        out_shape=jax.ShapeDtypeStruct((rows, n), F32),
        compiler_params=_cp("arbitrary"),
    )(*args, res)


def _glu_mlp_kernel(*refs, norm, swiglu, scale, tf, own_res, n_p):
    if own_res:
        src_ref, g_ref, wa_ref, wb_ref, wd_ref, o_ref = refs
        res_ref = src_ref
    elif n_p is None:
        src_ref, g_ref, wa_ref, wb_ref, wd_ref, res_ref, o_ref = refs
    else:
        src_ref, srcs_ref, g_ref, wa_ref, wb_ref, wd_ref, res_ref, o_ref, h_scr = refs
    if n_p is None:
        x = src_ref[...]
        if norm:
            x = _rms(x, g_ref[...])
        h = x.astype(BF16)
    else:
        @pl.when(pl.program_id(0) < n_p)
        def _():
            h_scr[...] = src_ref[...].astype(BF16)

        @pl.when(pl.program_id(0) >= n_p)
        def _():
            h_scr[...] = srcs_ref[...].astype(BF16)

        h = h_scr[...]
    acc = None
    for j in range(wa_ref.shape[1] // tf):
        cs = slice(j * tf, (j + 1) * tf)
        a = _dot(h, wa_ref[:, cs])
        b = _dot(h, wb_ref[:, cs])
        s = _silu(a) * b if swiglu else a * _sigmoid(b)
        d = _dot(s.astype(BF16), wd_ref[cs, :])
        acc = d if acc is None else acc + d
    o_ref[...] = res_ref[...] + scale * acc


def glu_mlp(src, g, wa, wb, wd, res, tm, tf, norm, swiglu, scale, src_s=None):
    k = src.shape[1]
    ff = wa.shape[1]
    n = wd.shape[1]
    rows = src.shape[0] if res is None else res.shape[0]
    resident = lambda shape: pl.BlockSpec(shape, lambda i: (0, 0), pipeline_mode=pl.Buffered(1))
    weights = [pl.BlockSpec((1, k), lambda i: (0, 0)), resident((k, ff)), resident((k, ff)), resident((ff, n))]
    n_p, scratch = None, []
    if src_s is None:
        in_specs = [pl.BlockSpec((tm, k), lambda i: (i, 0))] + weights
        args = [src, g.reshape(1, k), wa, wb, wd]
    else:
        assert not norm and res is not None
        n_p = src.shape[0] // tm
        in_specs = [pl.BlockSpec((tm, k), lambda i: (jnp.minimum(i, n_p - 1), 0)),
                    pl.BlockSpec((tm, k), lambda i: (jnp.maximum(i - n_p, 0), 0))] + weights
        args = [src, src_s, g.reshape(1, k), wa, wb, wd]
        scratch = [pltpu.VMEM((tm, k), BF16)]
    if res is not None:
        in_specs.append(pl.BlockSpec((tm, n), lambda i: (i, 0)))
        args.append(res)
    return pl.pallas_call(
        functools.partial(_glu_mlp_kernel, norm=norm, swiglu=swiglu, scale=scale, tf=tf, own_res=res is None, n_p=n_p),
        grid=(rows // tm,),
        in_specs=in_specs,
        out_specs=pl.BlockSpec((tm, n), lambda i: (i, 0)),
        out_shape=jax.ShapeDtypeStruct((rows, n), F32),
        scratch_shapes=scratch,
        compiler_params=_cp("arbitrary" if src_s is not None else "parallel"),
    )(*args)


def _rmsnorm_kernel(x_ref, g_ref, o_ref):
    o_ref[...] = _rms(x_ref[...], g_ref[...])


def rmsnorm_rows(x, g, tm):
    rows, k = x.shape
    return pl.pallas_call(
        _rmsnorm_kernel,
        grid=(rows // tm,),
        in_specs=[pl.BlockSpec((tm, k), lambda i: (i, 0)), pl.BlockSpec((1, k), lambda i: (0, 0))],
        out_specs=pl.BlockSpec((tm, k), lambda i: (i, 0)),
        out_shape=jax.ShapeDtypeStruct((rows, k), F32),
        compiler_params=_cp("parallel"),
    )(x, g.reshape(1, k))


def _xattn_block_kernel(x_ref, g_ref, wq_ref, wo_ref, k_ref, v_ref, o_ref):
    x = x_ref[...]
    q = _dot(_rms(x, g_ref[...]).astype(BF16), wq_ref[...])
    heads = [(q[:, h * X_HD:(h + 1) * X_HD] * (X_HD ** -0.5)).astype(BF16) for h in range(X_HEADS)]
    ss = [_dot_nt(heads[h], k_ref[:, h * X_HD:(h + 1) * X_HD].astype(BF16)) for h in range(X_HEADS)]
    ps = [jnp.exp(s - jnp.max(s, axis=-1, keepdims=True)) for s in ss]
    ps = [(p / jnp.sum(p, axis=-1, keepdims=True)).astype(BF16) for p in ps]
    o = jnp.concatenate([_dot(ps[h], v_ref[:, h * X_HD:(h + 1) * X_HD].astype(BF16)) for h in range(X_HEADS)], axis=1)
    o_ref[...] = x + _dot(o.astype(BF16), wo_ref[...])


def cross_attention_block(x, g, wq, wo, memkv, layer, tq):
    nb = SEQ // tq
    resident = lambda shape: pl.BlockSpec(shape, lambda n, i: (0, 0), pipeline_mode=pl.Buffered(1))
    return pl.pallas_call(
        _xattn_block_kernel,
        grid=(BATCH, nb),
        in_specs=[pl.BlockSpec((tq, D_MODEL), lambda n, i: (n * nb + i, 0)),
                  pl.BlockSpec((1, D_MODEL), lambda n, i: (0, 0)),
                  resident((D_MODEL, D_MODEL)), resident((D_MODEL, D_MODEL)),
                  pl.BlockSpec((None, N_MEM, D_MODEL), lambda n, i: (layer, n, 0)),
                  pl.BlockSpec((None, N_MEM, D_MODEL), lambda n, i: (layer, n, 1))],
        out_specs=pl.BlockSpec((tq, D_MODEL), lambda n, i: (n * nb + i, 0)),
        out_shape=jax.ShapeDtypeStruct(x.shape, F32),
        input_output_aliases={0: 0},
        compiler_params=_cp("parallel", "parallel"),
    )(x, g.reshape(1, D_MODEL), wq, wo, memkv, memkv)


def _rows_matmul_res_kernel(a_ref, w_ref, x_ref, o_ref):
    o_ref[...] = x_ref[...] + _dot(a_ref[...].astype(BF16), w_ref[...])


def matmul_residual_rows(a, w, x, row_off, tm):
    off = row_off // tm
    k, n = w.shape
    return pl.pallas_call(
        _rows_matmul_res_kernel,
        grid=(a.shape[0] // tm,),
        in_specs=[pl.BlockSpec((tm, k), lambda i: (i, 0)),
                  pl.BlockSpec((k, n), lambda i: (0, 0)),
                  pl.BlockSpec((tm, n), lambda i: (off + i, 0))],
        out_specs=pl.BlockSpec((tm, n), lambda i: (off + i, 0)),
        out_shape=jax.ShapeDtypeStruct(x.shape, F32),
        input_output_aliases={2: 0},
        compiler_params=_cp("parallel"),
    )(a, w, x)


def _xattn_cache_kernel(q_ref, k_hbm, v_hbm, o_ref, kbuf, vbuf, sem, *, layer):
    n = pl.program_id(0)
    slot = n % 2

    def copies(seq, s):
        cs = []
        for h in range(X_HEADS):
            cs.append(pltpu.make_async_copy(k_hbm.at[layer, seq, :, h, :], kbuf.at[s, h], sem.at[s, h]))
            cs.append(pltpu.make_async_copy(v_hbm.at[layer, seq, :, h, :], vbuf.at[s, h], sem.at[s, X_HEADS + h]))
        return cs

    @pl.when(n == 0)
    def _():
        for c in copies(0, 0):
            c.start()

    @pl.when(n + 1 < pl.num_programs(0))
    def _():
        for c in copies(n + 1, 1 - slot):
            c.start()

    for c in copies(n, slot):
        c.wait()
    q = q_ref[...]
    hs = range(X_HEADS)
    ss = [_dot_nt((q[:, h * X_HD:(h + 1) * X_HD] * (X_HD ** -0.5)).astype(BF16), kbuf[slot, h].astype(BF16)) for h in hs]
    ps = [jnp.exp(s - jnp.max(s, axis=-1, keepdims=True)) for s in ss]
    ps = [(p / jnp.sum(p, axis=-1, keepdims=True)).astype(BF16) for p in ps]
    for h in hs:
        o_ref[:, h * X_HD:(h + 1) * X_HD] = _dot(ps[h], vbuf[slot, h].astype(BF16))


def cross_attention_cached(q_all, cache_k, cache_v, layer, row_off):
    off = row_off // DEC_SEQ
    return pl.pallas_call(
        functools.partial(_xattn_cache_kernel, layer=layer),
        grid=(DEC_BATCH,),
        in_specs=[pl.BlockSpec((DEC_SEQ, D_MODEL), lambda n: (off + n, 0)),
                  pl.BlockSpec(memory_space=pl.ANY), pl.BlockSpec(memory_space=pl.ANY)],
        out_specs=pl.BlockSpec((DEC_SEQ, D_MODEL), lambda n: (n, 0)),
        out_shape=jax.ShapeDtypeStruct((DEC_BATCH * DEC_SEQ, D_MODEL), F32),
        scratch_shapes=[pltpu.VMEM((2, X_HEADS, N_MEM, X_HD), F32), pltpu.VMEM((2, X_HEADS, N_MEM, X_HD), F32),
                        pltpu.SemaphoreType.DMA((2, 2 * X_HEADS))],
        compiler_params=_cp("arbitrary"),
    )(q_all, cache_k, cache_v)


def _softplus(x):
    return jnp.maximum(x, 0.0) + jnp.log(1.0 + jnp.exp(-jnp.abs(x)))


def _split3(x, axis):
    hi = x.astype(BF16).astype(F32)
    return jnp.concatenate([hi, hi, x - hi], axis=axis).astype(BF16)


def _split3r(x, axis):
    hi = x.astype(BF16).astype(F32)
    return jnp.concatenate([hi, x - hi, hi], axis=axis).astype(BF16)


def _dot3(a, b):
    return _dot(_split3(a, 1), _split3r(b, 0))


def _cumsum_rows(tri3, g):
    g1 = g.astype(BF16).astype(F32)
    g2 = (g - g1).astype(BF16).astype(F32)
    g3 = g - g1 - g2
    return _dot(tri3, jnp.concatenate([g1, g2, g3], axis=0).astype(BF16))


def _gdn_kernel(qkv_ref, sm_ref, gate_ref, cw_ref, gp_ref, nw_ref, cb_ref, s0_ref, o_ref, sout_ref,
                xbuf, s_scr, *, seqs, rows, chunk):
    i = pl.program_id(1)
    n_chunks = rows // chunk

    @pl.when(i == 0)
    def _():
        xbuf[:, 0:8, :] = cb_ref[...]
        s_scr[...] = s0_ref[...]

    ri = lax.broadcasted_iota(jnp.int32, (chunk, chunk), 0)
    ci = lax.broadcasted_iota(jnp.int32, (chunk, chunk), 1)
    incl = ri >= ci
    strict = ri > ci
    tri = jnp.where(incl, 1.0, 0.0).astype(BF16)
    tri3 = jnp.concatenate([tri, tri, tri], axis=1)
    eye = jnp.where(ri == ci, 1.0, 0.0)
    levels = []
    w = 1
    while w < chunk:
        levels.append(((ri // (2 * w)) == (ci // (2 * w))) & ((ri % (2 * w)) >= w) & ((ci % (2 * w)) < w))
        w *= 2
    nw = nw_ref[...]
    cw = [cw_ref[j:j + 1, :] for j in range(GDN_CONV)]
    a_neg = -jnp.exp(gp_ref[0:1, :])
    dtb = gp_ref[1:2, :]

    probs = []
    for b in range(seqs):
        tok = slice(b * rows, (b + 1) * rows)
        xbuf[b, 8:8 + rows, :] = qkv_ref[tok, :]
        conv = xbuf[b, 5:5 + rows, :] * cw[0]
        for j in range(1, GDN_CONV):
            conv = conv + xbuf[b, 5 + j:5 + j + rows, :] * cw[j]
        tail = xbuf[b, 8 + rows - 3:8 + rows, :]
        xbuf[b, 5:8, :] = tail
        qkv = _silu(conv)
        sm = sm_ref[tok, :]
        beta_all = _sigmoid(sm)
        g_all = a_neg * _softplus(sm + dtb)
        for c in range(n_chunks):
            rs = slice(c * chunk, (c + 1) * chunk)
            gcum = _cumsum_rows(tri3, g_all[rs])
            gcum_t = gcum.T
            for h in range(GDN_HEADS):
                q = qkv[rs, h * GDN_DK:(h + 1) * GDN_DK]
                k = qkv[rs, 512 + h * GDN_DK:512 + (h + 1) * GDN_DK]
                v = qkv[rs, 1024 + h * GDN_DK:1024 + (h + 1) * GDN_DK]
                q = q * lax.rsqrt(jnp.sum(q * q, axis=-1, keepdims=True) + EPS) * (GDN_DK ** -0.5)
                k = k * lax.rsqrt(jnp.sum(k * k, axis=-1, keepdims=True) + EPS)
                beta = beta_all[rs, h:h + 1]
                gc_col = gcum[:, 4 + h:5 + h]
                gc_row = gcum_t[4 + h:5 + h, :]
                g_last = gcum[chunk - 1:chunk, 4 + h:5 + h]
                decay = jnp.where(incl, jnp.exp(jnp.minimum(gc_col - gc_row, 0.0)), 0.0)
                eg = jnp.exp(gc_col)
                kb = k * beta
                k3r = _split3r(k, 1)
                probs.append(dict(
                    b=b, c=c, h=h, decay=decay,
                    kk=_dot_nt(_split3(kb, 1), k3r), qk=_dot_nt(_split3(q, 1), k3r),
                    rhs=jnp.concatenate([v * beta, kb * eg], axis=1), qd=q * eg,
                    kd_t=(k * jnp.exp(g_last - gc_col)).T, gl=jnp.exp(g_last), minv=eye))
    for p in probs:
        p["lmat"] = jnp.where(strict, p["kk"] * p["decay"], 0.0)
        p["qk"] = jnp.where(incl, p["qk"] * p["decay"], 0.0)
    for p in probs:
        p["minv"] = eye - jnp.where(levels[0], p["lmat"], 0.0)
        p["lh"] = p["lmat"].astype(BF16).astype(F32)
        p["ll"] = p["lmat"] - p["lh"]
    for off_blk in levels[1:]:
        for p in probs:
            mh = p["minv"].astype(BF16).astype(F32)
            ml = p["minv"] - mh
            p["mr3"] = jnp.concatenate([mh, ml, mh], axis=0).astype(BF16)
            ch, cl = jnp.where(off_blk, p["lh"], 0.0), jnp.where(off_blk, p["ll"], 0.0)
            p["t"] = _dot(jnp.concatenate([mh, mh, ml], axis=1).astype(BF16),
                          jnp.concatenate([ch, cl, ch], axis=0).astype(BF16))
        for p in probs:
            p["minv"] = p["minv"] - _dot(_split3(p["t"], 1), p["mr3"])
    for p in probs:
        p["uw"] = _dot3(p["minv"], p["rhs"])
    state = {(b, h): s_scr[b, h] for b in range(seqs) for h in range(GDN_HEADS)}
    for c in range(n_chunks):
        cur = [p for p in probs if p["c"] == c]
        for p in cur:
            p["s3r"] = _split3r(state[(p["b"], p["h"])], 0)
            p["v_new"] = p["uw"][:, :GDN_DK] - _dot(_split3(p["uw"][:, GDN_DK:], 1), p["s3r"])
        for p in cur:
            p["o"] = _dot(_split3(p["qd"], 1), p["s3r"]) + _dot3(p["qk"], p["v_new"])
            state[(p["b"], p["h"])] = state[(p["b"], p["h"])] * p["gl"] + _dot3(p["kd_t"], p["v_new"])
    for p in probs:
        b, c, h = p["b"], p["c"], p["h"]
        r0 = b * rows + c * chunk
        o = _rms(p["o"], nw) * _silu(gate_ref[r0:r0 + chunk, h * GDN_DK:(h + 1) * GDN_DK])
        o_ref[r0:r0 + chunk, h * GDN_DK:(h + 1) * GDN_DK] = o
    for (b, h), s in state.items():
        s_scr[b, h] = s

    @pl.when(i == pl.num_programs(1) - 1)
    def _():
        sout_ref[...] = s_scr[...]


def gdn(qkv, small, gate, conv_w, gpar, norm_w, conv_buf8, s0, row_off, n_seq, t_seq, seqs, rows, chunk):
    nb = t_seq // rows
    blk = seqs * rows
    off = row_off // blk
    rmap = lambda n, i: (off + n * nb + i, 0)
    return pl.pallas_call(
        functools.partial(_gdn_kernel, seqs=seqs, rows=rows, chunk=chunk),
        grid=(n_seq // seqs, nb),
        in_specs=[pl.BlockSpec((blk, GDN_QKV), rmap),
                  pl.BlockSpec((blk, 128), rmap),
                  pl.BlockSpec((blk, 512), rmap),
                  pl.BlockSpec((GDN_CONV, GDN_QKV), lambda n, i: (0, 0)),
                  pl.BlockSpec((8, 128), lambda n, i: (0, 0)),
                  pl.BlockSpec((1, GDN_DK), lambda n, i: (0, 0)),
                  pl.BlockSpec((seqs, 8, GDN_QKV), lambda n, i: (n, 0, 0)),
                  pl.BlockSpec((seqs, GDN_HEADS, GDN_DK, GDN_DK), lambda n, i: (n, 0, 0, 0))],
        out_specs=[pl.BlockSpec((blk, 512), lambda n, i: (n * nb + i, 0)),
                   pl.BlockSpec((seqs, GDN_HEADS, GDN_DK, GDN_DK), lambda n, i: (n, 0, 0, 0))],
        out_shape=[jax.ShapeDtypeStruct((n_seq * t_seq, 512), F32),
                   jax.ShapeDtypeStruct((n_seq, GDN_HEADS, GDN_DK, GDN_DK), F32)],
        scratch_shapes=[pltpu.VMEM((seqs, 8 + rows, GDN_QKV), F32),
                        pltpu.VMEM((seqs, GDN_HEADS, GDN_DK, GDN_DK), F32)],
        compiler_params=_cp("parallel", "arbitrary"),
    )(qkv, small, gate, conv_w, gpar, norm_w, conv_buf8, s0)


def _compress_many(problems, m):
    accs = [jnp.zeros((m + 8, 4 * CMP_HIDDEN), F32) for _ in problems]
    for jj in range(CMP_STRIDE // 2):
        for k, (xj, posab_ref, w1_ref, _) in enumerate(problems):
            x2 = jnp.concatenate([xj(2 * jj), xj(2 * jj + 1)], axis=1)
            lhs = jnp.concatenate([x2, posab_ref[jj]], axis=0).astype(BF16)
            accs[k] = accs[k] + _dot(lhs, w1_ref[jj])
    row = lax.broadcasted_iota(jnp.int32, (m, 2 * NSA_HD), 0)
    hs = []
    for acc in accs:
        top = acc[0:m, 0:2 * CMP_HIDDEN]
        bot = acc[0:m, 2 * CMP_HIDDEN:]
        c = acc[m:m + 1, 0:2 * CMP_HIDDEN] + acc[m + 1:m + 2, 2 * CMP_HIDDEN:]
        hs.append(_silu(top + pltpu.roll(bot, m - 1, 0) + c).astype(BF16))
    return [jnp.where(row < m - 1, _dot(h, p[3][...]), 0.0) for h, p in zip(hs, problems)]


def _compress_prompt_kernel(kc_ref, vc_ref, pk_ref, w1k_ref, w2k_ref, pv_ref, w1v_ref, w2v_ref, ok_ref, ov_ref, *, m):
    ok_ref[0], ov_ref[0] = _compress_many(
        [(lambda j: kc_ref[pl.ds(j, m, stride=CMP_STRIDE), :], pk_ref, w1k_ref, w2k_ref),
         (lambda j: vc_ref[pl.ds(j, m, stride=CMP_STRIDE), :], pv_ref, w1v_ref, w2v_ref)], m)


def _cmp_weight_specs(nmap):
    return [pl.BlockSpec((CMP_STRIDE // 2, 8, 256), nmap(3)),
            pl.BlockSpec((CMP_STRIDE // 2, 256, 4 * CMP_HIDDEN), nmap(3)),
            pl.BlockSpec((2 * CMP_HIDDEN, 2 * NSA_HD), nmap(2))]


def compress_prompt(nkv, cw, n_seq, t_seq):
    m = t_seq // CMP_STRIDE
    zmap = lambda nd: (lambda n: (0,) * nd)
    out = jax.ShapeDtypeStruct((n_seq, m, 2 * NSA_HD), F32)
    return pl.pallas_call(
        functools.partial(_compress_prompt_kernel, m=m),
        grid=(n_seq,),
        in_specs=[pl.BlockSpec((t_seq, 128), lambda n: (n, 4)), pl.BlockSpec((t_seq, 128), lambda n: (n, 5))]
        + _cmp_weight_specs(zmap) + _cmp_weight_specs(zmap),
        out_specs=[pl.BlockSpec((1, m, 2 * NSA_HD), lambda n: (n, 0, 0))] * 2,
        out_shape=[out, out],
        compiler_params=_cp("parallel"),
    )(nkv, nkv, *cw)


def _compress_sample_kernel(pt_ref, *refs, m):
    del pt_ref
    nseq = CMP_SAMPLE_SEQS
    pages = refs[:2 * N_PAGES * nseq]
    pk_ref, w1k_ref, w2k_ref, pv_ref, w1v_ref, w2v_ref, ok_ref, ov_ref, x_scr = refs[2 * N_PAGES * nseq:]
    for p in range(N_PAGES):
        for s in range(2 * nseq):
            x_scr[s, p * PAGE_SIZE:(p + 1) * PAGE_SIZE, :] = pages[s * N_PAGES + p][...].T
    weights = ((pk_ref, w1k_ref, w2k_ref), (pv_ref, w1v_ref, w2v_ref))
    outs = _compress_many([(functools.partial(lambda j, s: x_scr[s, pl.ds(j, m, stride=CMP_STRIDE), :], s=s),) + weights[s % 2]
                           for s in range(2 * nseq)], m)
    for b in range(nseq):
        ok_ref[b] = outs[2 * b]
        ov_ref[b] = outs[2 * b + 1]


def _page_specs(layer, nseq=1, b=0):
    return [pl.BlockSpec((None, None, PAGE_SIZE, 128),
                         functools.partial(lambda n, pt, p: (layer, pt[n * nseq + b, p], 0, 0), p=p))
            for p in range(N_PAGES)]


def compress_sample(page_table, cache_k, cache_v, layer, cw):
    m = PAST_LEN // CMP_STRIDE
    nseq = CMP_SAMPLE_SEQS
    zmap = lambda nd: (lambda n, pt: (0,) * nd)
    out = jax.ShapeDtypeStruct((DEC_BATCH, m, 2 * NSA_HD), F32)
    specs, args = [], []
    for b in range(nseq):
        specs += _page_specs(layer, nseq, b) + _page_specs(layer, nseq, b)
        args += [cache_k] * N_PAGES + [cache_v] * N_PAGES
    return pl.pallas_call(
        functools.partial(_compress_sample_kernel, m=m),
        grid_spec=pltpu.PrefetchScalarGridSpec(
            num_scalar_prefetch=1, grid=(DEC_BATCH // nseq,),
            in_specs=specs + _cmp_weight_specs(zmap) + _cmp_weight_specs(zmap),
            out_specs=[pl.BlockSpec((nseq, m, 2 * NSA_HD), lambda n, pt: (n, 0, 0))] * 2,
            scratch_shapes=[pltpu.VMEM((2 * nseq, PAST_LEN, 2 * NSA_HD), F32)]),
        out_shape=[out, out],
        compiler_params=_cp("arbitrary"),
    )(page_table, *args, *cw)


CMP_SAMPLE_SEQS = 2


def _masked_softmax(s, mask):
    s = jnp.where(mask, s, NEG)
    m = jnp.max(s, axis=-1, keepdims=True)
    p = jnp.where(mask, jnp.exp(s - m), 0.0)
    l = jnp.sum(p, axis=-1, keepdims=True)
    return p / jnp.where(l > 0.0, l, 1.0)


def _stack_heads(nq, g, tq):
    parts = [nq[:, (g * NSA_REP + r) * NSA_HD:(g * NSA_REP + r + 1) * NSA_HD] for r in range(NSA_REP)]
    return (jnp.concatenate(parts, axis=0) * (NSA_HD ** -0.5)).astype(BF16)


def _rep(mask, tq):
    return jnp.concatenate([mask] * NSA_REP, axis=0)


def _importance_t(p, tq, n_blk):
    n_cmp = p.shape[1]
    psum = p[0:tq] + p[tq:2 * tq] + p[2 * tq:3 * tq] + p[3 * tq:4 * tq]
    sj = lax.broadcasted_iota(jnp.int32, (n_blk, n_cmp), 0) * SEL_BLOCK
    ci = lax.broadcasted_iota(jnp.int32, (n_blk, n_cmp), 1) * CMP_STRIDE
    ov = jnp.clip(jnp.minimum(ci + CMP_LEN, sj + SEL_BLOCK) - jnp.maximum(ci, sj), 0, CMP_LEN).astype(F32) / CMP_LEN
    ov = ov.astype(BF16)
    p1 = psum.astype(BF16).astype(F32)
    p2 = (psum - p1).astype(BF16).astype(F32)
    p3 = psum - p1 - p2
    return _dot_nt(jnp.concatenate([ov, ov, ov], axis=1), jnp.concatenate([p1, p2, p3], axis=1).astype(BF16))


def _select_blocks(imp_ts, qpos_row, tq, n_blk):
    blk = lax.broadcasted_iota(jnp.int32, (n_blk, tq), 0)
    rows = qpos_row if isinstance(qpos_row, (list, tuple)) else [qpos_row] * len(imp_ts)
    works = []
    for imp_t, row in zip(imp_ts, rows):
        cur = row // SEL_BLOCK
        valid = blk <= cur
        forced = valid & ((blk == 0) | (blk >= cur - 1))
        works.append(jnp.where(valid, jnp.where(forced, SEL_FORCED, imp_t), NEG))
    sels = [jnp.zeros((n_blk, tq), F32) for _ in imp_ts]
    for _ in range(SEL_TOPN):
        for j in range(len(works)):
            m = jnp.max(works[j], axis=0, keepdims=True)
            idx = jnp.min(jnp.where(works[j] == m, blk, n_blk), axis=0, keepdims=True)
            pick = blk == idx
            sels[j] = jnp.where(pick, 1.0, sels[j])
            works[j] = jnp.where(pick, -jnp.inf, works[j])
    return [s.T for s in sels]


def _expand_sel(sel, first_blk, n_keys):
    n_blk = sel.shape[1]
    bj = lax.broadcasted_iota(jnp.int32, (n_blk, n_keys), 0)
    kb = lax.broadcasted_iota(jnp.int32, (n_blk, n_keys), 1) // SEL_BLOCK + first_blk
    e = jnp.where(bj == kb, 1.0, 0.0).astype(BF16)
    return _dot(sel.astype(BF16), e) > 0.5


def _gate_mix(o_ref, gates, g, o_cmp, o_slc, o_win, tq):
    for r in range(NSA_REP):
        h = g * NSA_REP + r
        rs = slice(r * tq, (r + 1) * tq)
        c0 = 8 + 3 * h
        o = (gates[:, c0:c0 + 1] * o_cmp[rs] + gates[:, c0 + 1:c0 + 2] * o_slc[rs] + gates[:, c0 + 2:c0 + 3] * o_win[rs])
        o_ref[:, h * NSA_HD:(h + 1) * NSA_HD] = o


def _nsa_prompt_kernel(nq_ref, sm_ref, kv_ref, et_ref, kcmp_ref, vcmp_ref, o_ref):
    tq = Q_BLOCK
    nt = NSA_PROMPT_TILES
    kc = 512
    assert (kc // tq) % nt == 0
    step = pl.program_id(1)
    tiles = range(nt)
    probs = [(t, g) for t in tiles for g in range(NSA_KV)]
    gsl = lambda g: slice(g * NSA_HD, (g + 1) * NSA_HD)
    qpos = [(step * nt + t) * tq + lax.broadcasted_iota(jnp.int32, (tq, 1), 0) for t in tiles]
    qrow = [(step * nt + t) * tq + lax.broadcasted_iota(jnp.int32, (1, tq), 1) for t in tiles]
    nq = [nq_ref[t * tq:(t + 1) * tq, :] for t in tiles]
    q4 = {(t, g): _stack_heads(nq[t], g, tq) for t, g in probs}

    n_cmp = kcmp_ref.shape[1]
    cmp_end = lax.broadcasted_iota(jnp.int32, (tq, n_cmp), 1) * CMP_STRIDE + (CMP_LEN - 1)
    s_c = {(t, g): _dot_nt(q4[t, g], kcmp_ref[0, :, gsl(g)].astype(BF16)) for t, g in probs}
    p_c = {(t, g): _masked_softmax(s_c[t, g], _rep(cmp_end <= qpos[t], tq)) for t, g in probs}
    o_cmp = {(t, g): _dot(p_c[t, g].astype(BF16), vcmp_ref[0, :, gsl(g)].astype(BF16)) for t, g in probs}
    sel_list = _select_blocks([_importance_t(p_c[pg], tq, 128) for pg in probs], [qrow[t] for t, _ in probs], tq, 128)
    qas = [jnp.concatenate([_rep(jnp.where(sel > 0.5, 0.0, -SEL_MASK), tq).astype(BF16), q4[pg]], axis=1)
           for sel, pg in zip(sel_list, probs)]

    def slc_step(c, carry, diagonal):
        start = pl.multiple_of(c * kc, kc)
        et = et_ref[pl.ds(start, kc), :]
        kas = [jnp.concatenate([et, kv_ref[pl.ds(start, kc), gsl(g)]], axis=1) for g in range(NSA_KV)]
        ss = [_dot_nt(qas[k], kas[g]) for k, (t, g) in enumerate(probs)]
        if diagonal:
            kpos = start + lax.broadcasted_iota(jnp.int32, (tq, kc), 1)
            ss = [jnp.where(_rep(kpos <= qpos[t], tq), s, -SEL_MASK) for s, (t, g) in zip(ss, probs)]
        m_new = [jnp.maximum(carry[k][0], jnp.max(ss[k], axis=-1, keepdims=True)) for k in range(len(probs))]
        ps = [jnp.exp(ss[k] - m_new[k]) for k in range(len(probs))]
        out = []
        for k, (t, g) in enumerate(probs):
            m_i, l_i, acc = carry[k]
            alpha = jnp.exp(m_i - m_new[k])
            v = kv_ref[pl.ds(start, kc), 128 + g * NSA_HD:128 + (g + 1) * NSA_HD]
            out.append((m_new[k], alpha * l_i + jnp.sum(ps[k], axis=-1, keepdims=True),
                        alpha * acc + _dot(ps[k].astype(BF16), v)))
        return tuple(out)

    init = tuple((jnp.full((NSA_REP * tq, 1), -3.0e38, F32), jnp.zeros((NSA_REP * tq, 1), F32),
                  jnp.zeros((NSA_REP * tq, NSA_HD), F32)) for _ in probs)
    n_full = (step * nt) // (kc // tq)
    carry = lax.fori_loop(0, n_full, functools.partial(slc_step, diagonal=False), init)
    fin = slc_step(n_full, carry, True)

    nband = WINDOW + tq
    wstart = [pl.multiple_of(jnp.maximum(step * nt + t - WINDOW // tq, 0) * tq, tq) for t in tiles]
    win_ok = []
    for t in tiles:
        d = qpos[t] - (wstart[t] + lax.broadcasted_iota(jnp.int32, (tq, nband), 1))
        win_ok.append(_rep((d >= 0) & (d <= WINDOW), tq))
    s_w = [_dot_nt(q4[t, g], kv_ref[pl.ds(wstart[t], nband), 256 + g * NSA_HD:256 + (g + 1) * NSA_HD]) for t, g in probs]
    p_w = [_masked_softmax(s_w[k], win_ok[t]).astype(BF16) for k, (t, g) in enumerate(probs)]
    o_win = [_dot(p_w[k], kv_ref[pl.ds(wstart[t], nband), 384 + g * NSA_HD:384 + (g + 1) * NSA_HD])
             for k, (t, g) in enumerate(probs)]

    for t in tiles:
        gates = _sigmoid(sm_ref[t * tq:(t + 1) * tq, :])
        for g in range(NSA_KV):
            k = t * NSA_KV + g
            _, l_f, acc = fin[k]
            o_slc = acc / jnp.where(l_f > 0.0, l_f, 1.0)
            _gate_mix(o_ref.at[t * tq:(t + 1) * tq], gates, g, o_cmp[t, g], o_slc, o_win[k], tq)


NSA_PROMPT_TILES = 2


def nsa_prompt(nq, small, kv_bf, kcmp, vcmp, n_seq, t_seq):
    rows = NSA_PROMPT_TILES * Q_BLOCK
    nb = t_seq // rows
    m = kcmp.shape[1]
    key_blk = jnp.arange(t_seq, dtype=jnp.int32)[:, None] // SEL_BLOCK
    et = (key_blk == jnp.arange(128, dtype=jnp.int32)[None, :]).astype(BF16)
    return pl.pallas_call(
        _nsa_prompt_kernel,
        grid=(n_seq, nb),
        in_specs=[pl.BlockSpec((rows, 512), lambda n, i: (n * nb + i, 0)),
                  pl.BlockSpec((rows, 128), lambda n, i: (n * nb + i, 0)),
                  pl.BlockSpec((t_seq, 512), lambda n, i: (n, 0)),
                  pl.BlockSpec((t_seq, 128), lambda n, i: (0, 0)),
                  pl.BlockSpec((1, m, 128), lambda n, i: (n, 0, 0)),
                  pl.BlockSpec((1, m, 128), lambda n, i: (n, 0, 0))],
        out_specs=pl.BlockSpec((rows, 512), lambda n, i: (n * nb + i, 0)),
        out_shape=jax.ShapeDtypeStruct((n_seq * t_seq, 512), F32),
        compiler_params=_cp("parallel", "arbitrary"),
    )(nq, small, kv_bf, et, kcmp, vcmp)


NSA_SAMPLE_SEQS = 4


def _nsa_sample_kernel(pt_ref, *refs):
    del pt_ref
    nseq = NSA_SAMPLE_SEQS
    per = 2 * N_PAGES + 2
    seq_refs = [refs[b * per:(b + 1) * per] for b in range(nseq)]
    nq_ref, sm_ref, kvn_ref, kcmp_ref, vcmp_ref, o_ref = refs[nseq * per:]
    tq = DEC_SEQ
    seqs = range(nseq)
    qpos = PAST_LEN + lax.broadcasted_iota(jnp.int32, (tq, 1), 0)
    qpos_row = PAST_LEN + lax.broadcasted_iota(jnp.int32, (1, tq), 1)
    n_keys = (N_PAGES + 1) * PAGE_SIZE
    n_blk = 40
    nr = NSA_REP * tq
    pad = jnp.zeros((PAGE_SIZE - tq, 128), F32)
    z = jnp.zeros((nr, NSA_HD), BF16)
    both = lambda m: jnp.concatenate([_rep(m, tq)] * NSA_KV, axis=0)
    n_cmp = kcmp_ref.shape[1]
    cmp_ok = both(lax.broadcasted_iota(jnp.int32, (tq, n_cmp), 1) * CMP_STRIDE + (CMP_LEN - 1) <= qpos)
    d = qpos - (PAST_LEN - WINDOW + lax.broadcasted_iota(jnp.int32, (tq, WINDOW + PAGE_SIZE), 1))
    win_ok = both((d >= 0) & (d <= WINDOW))
    causal = lax.broadcasted_iota(jnp.int32, (tq, n_keys), 1) <= qpos

    new, q_bd = [], []
    for b in seqs:
        rows = slice(b * tq, (b + 1) * tq)
        kvn = kvn_ref[rows, :]
        new.append([jnp.concatenate([kvn[:, c * 128:(c + 1) * 128], pad], axis=0).astype(BF16) for c in range(4)])
        nq = nq_ref[rows, :]
        q_bd.append(jnp.concatenate([jnp.concatenate([_stack_heads(nq, 0, tq), z], axis=1),
                                     jnp.concatenate([z, _stack_heads(nq, 1, tq)], axis=1)], axis=0))
    s_c = [_dot_nt(q_bd[b], kcmp_ref[b].astype(BF16)) for b in seqs]
    s = [jnp.concatenate([_dot(q_bd[b], p[...].astype(BF16)) for p in seq_refs[b][:N_PAGES]]
                         + [_dot_nt(q_bd[b], new[b][0])], axis=1) for b in seqs]
    sw = [jnp.concatenate([_dot(q_bd[b], seq_refs[b][2 * N_PAGES][...].astype(BF16)), _dot_nt(q_bd[b], new[b][2])],
                          axis=1) for b in seqs]
    p_c = [_masked_softmax(s_c[b], cmp_ok) for b in seqs]
    o_c = [_dot(p_c[b].astype(BF16), vcmp_ref[b].astype(BF16)) for b in seqs]
    pw = [_masked_softmax(sw[b], win_ok).astype(BF16) for b in seqs]
    o_w = [_dot_nt(pw[b][:, :WINDOW], seq_refs[b][2 * N_PAGES + 1][...].astype(BF16)) + _dot(pw[b][:, WINDOW:], new[b][3])
           for b in seqs]
    sels = _select_blocks([_importance_t(p_c[b][g * nr:(g + 1) * nr], tq, n_blk) for b in seqs for g in range(NSA_KV)],
                          qpos_row, tq, n_blk)
    p = []
    for b in seqs:
        msk = jnp.concatenate([_rep(_expand_sel(sels[b * NSA_KV + g], 0, n_keys) & causal, tq) for g in range(NSA_KV)],
                              axis=0)
        p.append(_masked_softmax(s[b], msk).astype(BF16))
    o_s = [_dot(p[b][:, N_PAGES * PAGE_SIZE:], new[b][1]) for b in seqs]
    for c in range(N_PAGES):
        o_s = [o_s[b] + _dot_nt(p[b][:, c * PAGE_SIZE:(c + 1) * PAGE_SIZE], seq_refs[b][N_PAGES + c][...].astype(BF16))
               for b in seqs]
    for b in seqs:
        gates = _sigmoid(sm_ref[b * tq:(b + 1) * tq, :])
        for g in range(NSA_KV):
            rs, gs = slice(g * nr, (g + 1) * nr), slice(g * NSA_HD, (g + 1) * NSA_HD)
            _gate_mix(o_ref.at[b * tq:(b + 1) * tq], gates, g, o_c[b][rs, gs], o_s[b][rs, gs], o_w[b][rs, gs], tq)


def nsa_sample(page_table, cache_k, cache_v, layer, nq, small, nkv, kcmp, vcmp, win_k, win_v):
    nseq = NSA_SAMPLE_SEQS
    rows = nseq * DEC_SEQ
    off = RP // rows
    rmap = lambda n, pt: (off + n, 0)
    m = kcmp.shape[1]
    specs, args = [], []
    for b in range(nseq):
        specs += _page_specs(layer, nseq, b) + _page_specs(layer, nseq, b)
        args += [cache_k] * N_PAGES + [cache_v] * N_PAGES
        wspec = pl.BlockSpec((None, None, 128, WINDOW), functools.partial(lambda n, pt, b: (layer, n * nseq + b, 0, 0), b=b))
        specs += [wspec, wspec]
        args += [win_k, win_v]
    return pl.pallas_call(
        _nsa_sample_kernel,
        grid_spec=pltpu.PrefetchScalarGridSpec(
            num_scalar_prefetch=1, grid=(DEC_BATCH // nseq,),
            in_specs=specs + [
                pl.BlockSpec((rows, 512), rmap),
                pl.BlockSpec((rows, 128), rmap),
                pl.BlockSpec((rows, 768), rmap),
                pl.BlockSpec((nseq, m, 128), lambda n, pt: (n, 0, 0)),
                pl.BlockSpec((nseq, m, 128), lambda n, pt: (n, 0, 0))],
            out_specs=pl.BlockSpec((rows, 512), lambda n, pt: (n, 0))),
        out_shape=jax.ShapeDtypeStruct((RS, 512), F32),
        compiler_params=_cp("arbitrary"),
    )(page_table, *args, nq, small, nkv, kcmp, vcmp)


S5_TILE = 256
S5_LANES = 2048
S5_BLK = 16


def _gelu_tanh(x):
    return 0.5 * x * (1.0 + jnp.tanh(math.sqrt(2.0 / math.pi) * (x + 0.044715 * (x * x * x))))


def _s5_kernel(u_ref, pm_ref, pt3_ref, bdr_ref, bdi_ref, cr_ref, ci_ref, tab_ref, d_ref, x0r_ref, x0i_ref,
               y_ref, fr_ref, fi_ref, xr_scr, xi_scr, car_scr, *, chained):
    i = pl.program_id(1)
    nb = S5_GROUPS // S5_BLK
    wi = S5_BLK * S5_CH
    ws = S5_BLK * S5_P
    seg = S5_TILE // 8
    u = u_ref[...]
    ub = _dot(pm_ref[...], u.astype(BF16)).astype(BF16)
    for b in range(nb):
        xr_scr[:, b * ws:(b + 1) * ws] = _dot(ub[:, b * wi:(b + 1) * wi], bdr_ref[b])
        xi_scr[:, b * ws:(b + 1) * ws] = _dot(ub[:, b * wi:(b + 1) * wi], bdi_ref[b])

    if chained:
        @pl.when(i == 0)
        def _():
            car_scr[0:1, :] = x0r_ref[0]
            car_scr[1:2, :] = x0i_ref[0]

    def cmad(br, bi, mr, mi, xr, xi):
        return br + mr * xr - mi * xi, bi + mr * xi + mi * xr

    for c in range(S5_STATE // S5_LANES):
        ls = slice(c * S5_LANES, (c + 1) * S5_LANES)
        ar, ai = tab_ref[0, 0:1, ls], tab_ref[1, 0:1, ls]
        if chained:
            xr, xi = jnp.zeros((8, S5_LANES), F32), jnp.zeros((8, S5_LANES), F32)
            for k in range(seg):
                rs = slice(8 * k, 8 * k + 8)
                xr, xi = cmad(xr_scr[rs, ls], xi_scr[rs, ls], ar, ai, xr, xi)
                xr_scr[rs, ls] = xr
                xi_scr[rs, ls] = xi
            yr, yi = xr, xi
            for sh, r0 in ((1, seg), (2, seg + 8), (4, seg + 16)):
                yr, yi = cmad(yr, yi, tab_ref[0, r0:r0 + 8, ls], tab_ref[1, r0:r0 + 8, ls],
                              pltpu.roll(yr, sh, 0), pltpu.roll(yi, sh, 0))
            first = lax.broadcasted_iota(jnp.int32, (8, S5_LANES), 0) == 0
            yr = jnp.where(first, 0.0, pltpu.roll(yr, 1, 0))
            yi = jnp.where(first, 0.0, pltpu.roll(yi, 1, 0))
            sr, si = cmad(yr, yi, tab_ref[0, seg + 24:seg + 32, ls], tab_ref[1, seg + 24:seg + 32, ls],
                          car_scr[0:1, ls], car_scr[1:2, ls])
            for k in range(seg):
                rs = slice(8 * k, 8 * k + 8)
                xr, xi = cmad(xr_scr[rs, ls], xi_scr[rs, ls], tab_ref[0, k:k + 1, ls], tab_ref[1, k:k + 1, ls], sr, si)
                xr_scr[rs, ls] = xr
                xi_scr[rs, ls] = xi
            car_scr[0:1, ls] = xr[7:8, :]
            car_scr[1:2, ls] = xi[7:8, :]
        else:
            for q in range(S5_TILE // 64):
                xr, xi = x0r_ref[0, 8 * q:8 * q + 8, ls], x0i_ref[0, 8 * q:8 * q + 8, ls]
                for t in range(8):
                    rs = slice(8 * (8 * q + t), 8 * (8 * q + t) + 8)
                    xr, xi = cmad(xr_scr[rs, ls], xi_scr[rs, ls], ar, ai, xr, xi)
                    xr_scr[rs, ls] = xr
                    xi_scr[rs, ls] = xi
                fr_ref[0, 8 * q:8 * q + 8, ls] = xr
                fi_ref[0, 8 * q:8 * q + 8, ls] = xi

    ys = []
    for b in range(nb):
        ys.append(_dot(xr_scr[:, b * ws:(b + 1) * ws].astype(BF16), cr_ref[b])
                  - _dot(xi_scr[:, b * ws:(b + 1) * ws].astype(BF16), ci_ref[b]))
    yp = jnp.concatenate(ys, axis=1)
    h1 = yp.astype(BF16).astype(F32)
    h2 = (yp - h1).astype(BF16).astype(F32)
    ylin = _dot(pt3_ref[...], jnp.concatenate([h1, h2, yp - h1 - h2], axis=0).astype(BF16))
    y_ref[...] = _gelu_tanh(ylin + d_ref[...] * u).astype(BF16)

    if chained:
        @pl.when(i == pl.num_programs(1) - 1)
        def _():
            fr_ref[0] = car_scr[0:1, :]
            fi_ref[0] = car_scr[1:2, :]


def s5_scan(u, sp, x0r, x0i, row_off, n_seq, t_seq, chained):
    if chained:
        grid = (n_seq, t_seq // S5_TILE)
        nb = grid[1]
        smap = lambda n, i: (n, 0, 0)
    else:
        grid = (1, n_seq * t_seq // S5_TILE)
        nb = grid[1]
        smap = lambda n, i: (i, 0, 0)
    off = row_off // S5_TILE
    rmap = lambda n, i: (off + n * nb + i, 0)
    sblk = (1,) + x0r.shape[1:]
    const = lambda nd: (lambda n, i: (0,) * nd)
    rho = jnp.arange(S5_TILE)
    if chained:
        src = (S5_TILE // 8) * (rho % 8) + rho // 8
    else:
        src = 64 * (rho // 64) + 8 * (rho % 8) + (rho // 8) % 8
    pm = (src[:, None] == jnp.arange(S5_TILE)[None, :]).astype(BF16)
    pt3 = jnp.tile(pm.T, (1, 3))
    in_specs = [pl.BlockSpec((S5_TILE, D_MODEL), rmap),
                pl.BlockSpec((S5_TILE, S5_TILE), const(2)), pl.BlockSpec((S5_TILE, 3 * S5_TILE), const(2)),
                pl.BlockSpec(sp["bdr"].shape, const(3)), pl.BlockSpec(sp["bdi"].shape, const(3)),
                pl.BlockSpec(sp["cr"].shape, const(3)), pl.BlockSpec(sp["ci"].shape, const(3)),
                pl.BlockSpec(sp["tab"].shape, const(3)), pl.BlockSpec((1, D_MODEL), const(2)),
                pl.BlockSpec(sblk, smap), pl.BlockSpec(sblk, smap)]
    args = [u, pm, pt3, sp["bdr"], sp["bdi"], sp["cr"], sp["ci"], sp["tab"], sp["d"], x0r, x0i]
    return pl.pallas_call(
        functools.partial(_s5_kernel, chained=chained),
        grid=grid,
        in_specs=in_specs,
        out_specs=[pl.BlockSpec((S5_TILE, D_MODEL), lambda n, i: (n * nb + i, 0)),
                   pl.BlockSpec(sblk, smap), pl.BlockSpec(sblk, smap)],
        out_shape=[jax.ShapeDtypeStruct((n_seq * t_seq, D_MODEL), BF16), jax.ShapeDtypeStruct(x0r.shape, F32),
                   jax.ShapeDtypeStruct(x0r.shape, F32)],
        scratch_shapes=[pltpu.VMEM((S5_TILE, S5_STATE), F32), pltpu.VMEM((S5_TILE, S5_STATE), F32),
                        pltpu.VMEM((8, S5_STATE), F32)],
        compiler_params=_cp("arbitrary", "arbitrary"),
    )(*args)


def _prep_ab_in(w):
    qkv, b, a, gate, nq, nkv, ngate = jnp.split(w, (1536, 1540, 1544, 2056, 2568, 3336), axis=-1)
    kc, vc, ks, vs, kw, vw = jnp.split(nkv, 6, axis=-1)
    small = jnp.concatenate([b, a, ngate, jnp.zeros(w.shape[:-1] + (96,), w.dtype)], axis=-1)
    return jnp.concatenate([qkv, ks, vs, kw, vw, kc, vc, gate, nq, small], axis=-1).astype(BF16)


AB_SPLITS = (1536, 768, 512, 512, 128)


def _prep_cmp(pos, w1, w2):
    top = w1[:CMP_STRIDE * NSA_HD].reshape(CMP_STRIDE, NSA_HD, CMP_HIDDEN)
    bot = w1[CMP_STRIDE * NSA_HD:].reshape(CMP_STRIDE, NSA_HD, CMP_HIDDEN)
    w1bd = jnp.zeros((CMP_STRIDE, 2 * NSA_HD, 4 * CMP_HIDDEN), F32)
    w2bd = jnp.zeros((2 * CMP_HIDDEN, 2 * NSA_HD), F32)
    for g in range(NSA_KV):
        rs = slice(g * NSA_HD, (g + 1) * NSA_HD)
        w1bd = w1bd.at[:, rs, g * CMP_HIDDEN:(g + 1) * CMP_HIDDEN].set(top)
        w1bd = w1bd.at[:, rs, (2 + g) * CMP_HIDDEN:(3 + g) * CMP_HIDDEN].set(bot)
        w2bd = w2bd.at[g * CMP_HIDDEN:(g + 1) * CMP_HIDDEN, rs].set(w2)
    posab = jnp.zeros((CMP_STRIDE, 8, 2 * NSA_HD), F32)
    posab = posab.at[:, 0, :].set(jnp.tile(pos[:CMP_STRIDE], (1, NSA_KV)))
    posab = posab.at[:, 1, :].set(jnp.tile(pos[CMP_STRIDE:], (1, NSA_KV)))
    half = CMP_STRIDE // 2
    posab = posab.reshape(half, 2, 8, 2 * NSA_HD).transpose(0, 2, 1, 3).reshape(half, 8, 4 * NSA_HD)
    w1bd = w1bd.reshape(half, 4 * NSA_HD, 4 * CMP_HIDDEN)
    return posab, w1bd.astype(BF16), w2bd.astype(BF16)


def _cmul(ar, ai, br, bi):
    return ar * br - ai * bi, ar * bi + ai * br


def _prep_s5(a_re, a_im, b_re, b_im, c_re, c_im, d, log_dt):
    dt = jnp.exp(log_dt)[:, None]
    lr = jnp.minimum(a_re, S5_MAX_RE)
    li = a_im
    mag = jnp.exp(lr * dt)
    ar = mag * jnp.cos(li * dt)
    ai = mag * jnp.sin(li * dt)
    den = lr * lr + li * li
    fr = ((ar - 1.0) * lr + ai * li) / den
    fi = (ai * lr - (ar - 1.0) * li) / den
    bbr = fr[..., None] * b_re - fi[..., None] * b_im
    bbi = fr[..., None] * b_im + fi[..., None] * b_re
    nb = S5_GROUPS // S5_BLK
    eye = jnp.eye(S5_BLK, dtype=F32)

    def bd_in(m):
        m4 = jnp.swapaxes(m, 1, 2).reshape(nb, S5_BLK, S5_CH, S5_P)
        return jnp.einsum('bgcp,gh->bgchp', m4, eye).reshape(nb, S5_BLK * S5_CH, S5_BLK * S5_P).astype(BF16)

    def bd_out(m):
        m4 = jnp.swapaxes(m, 1, 2).reshape(nb, S5_BLK, S5_P, S5_CH)
        return jnp.einsum('bgpc,gh->bgphc', m4, eye).reshape(nb, S5_BLK * S5_P, S5_BLK * S5_CH).astype(BF16)

    seg = S5_TILE // 8
    a1 = (ar.reshape(1, S5_STATE), ai.reshape(1, S5_STATE))
    pw = [a1]
    for _ in range(seg - 1):
        pw.append(_cmul(*pw[-1], *a1))
    row = jnp.arange(8)[:, None]
    s1 = pw[-1]
    s2 = _cmul(*s1, *s1)
    s4 = _cmul(*s2, *s2)
    sp = [(jnp.ones_like(a1[0]), jnp.zeros_like(a1[1]))]
    for _ in range(7):
        sp.append(_cmul(*sp[-1], *s1))
    parts = []
    for j in range(2):
        parts.append(jnp.concatenate([p[j] for p in pw]
                                     + [jnp.where(row >= sh, s[j], 0.0) for sh, s in ((1, s1), (2, s2), (4, s4))]
                                     + [p[j] for p in sp], axis=0))
    return {"bdr": bd_in(bbr), "bdi": bd_in(bbi), "cr": bd_out(c_re), "ci": bd_out(c_im),
            "tab": jnp.stack(parts), "d": d.reshape(1, D_MODEL)}


def _heads(a, n, t):
    return a.reshape(n, t, NSA_KV, NSA_HD)


def kernel(x_prompt, x_sample, mem_prompt, cache_mem_k, cache_mem_v, state_gdn, state_gdn_conv, cache_cmp_k, cache_cmp_v, cache_slc_k, cache_slc_v, cache_win_k, cache_win_v, state_s5_re, state_s5_im, page_table, norm_ffn1, w_ffn1_gate, w_ffn1_up, w_ffn1_down, norm_mix, norm_xq, norm_mem, w_xq, w_xk, w_xv, w_xo, norm_ffn2, w_ffn2_gate, w_ffn2_up, w_ffn2_down, norm_final, w_in_ab, w_out_ab, gdn_conv, gdn_a_log, gdn_dt_bias, gdn_norm, cmp_pos_k, cmp_w1_k, cmp_w2_k, cmp_pos_v, cmp_w1_v, cmp_w2_v, w_in_c, s5_a_re, s5_a_im, s5_b_re, s5_b_im, s5_c_re, s5_c_im, s5_d, s5_log_dt, w_glu, w_out_c):
    bf = lambda w: w.astype(BF16)
    n_ab = w_in_ab.shape[0]
    n_pool = cache_cmp_k.shape[1]
    x = jnp.concatenate([x_prompt.reshape(RP, D_MODEL), x_sample.reshape(RS, D_MODEL)], axis=0)

    memkv = mem_kv_all(mem_prompt.reshape(BATCH * N_MEM, D_MODEL), norm_mem, bf(jnp.concatenate([w_xk, w_xv], axis=-1)))
    mem_k_prompt = memkv[:, :, :D_MODEL].reshape(DEPTH, BATCH, N_MEM, X_HEADS, X_HD)
    mem_v_prompt = memkv[:, :, D_MODEL:].reshape(DEPTH, BATCH, N_MEM, X_HEADS, X_HD)

    fmaj = lambda c: jnp.transpose(c, (0, 1, 3, 4, 2)).reshape(c.shape[0], c.shape[1], NSA_KV * NSA_HD, c.shape[2])
    cck, ccv = fmaj(cache_cmp_k), fmaj(cache_cmp_v)
    csk, csv, cwk, cwv = fmaj(cache_slc_k), fmaj(cache_slc_v), fmaj(cache_win_k), fmaj(cache_win_v)
    ab_p, ab_s, c_p, c_s = [], [], [], []
    for l in range(DEPTH):
        i = l // 2
        x = glu_mlp(x, norm_ffn1[l], bf(w_ffn1_gate[l]), bf(w_ffn1_up[l]), bf(w_ffn1_down[l]), None, 1024, 256, True, True, 0.5)
        if l % 2 == 0:
            qkv, nkv, gate, nq, small = norm_matmul(x, norm_mix[l], _prep_ab_in(w_in_ab[i]), AB_SPLITS, 512)
            gpar = jnp.zeros((8, 128), F32).at[0, 4:8].set(gdn_a_log[i]).at[1, 4:8].set(gdn_dt_bias[i])
            nw = gdn_norm[i].reshape(1, GDN_DK)
            cb_s = jnp.zeros((DEC_BATCH, 8, GDN_QKV), F32).at[:, 5:8].set(state_gdn_conv[i])
            oa_p, st_p = gdn(qkv, small, gate, gdn_conv[i], gpar, nw, jnp.zeros((BATCH, 8, GDN_QKV), F32),
                             jnp.zeros((BATCH, GDN_HEADS, GDN_DK, GDN_DK), F32), 0, BATCH, SEQ, 1, 256, GDN_CHUNK)
            oa_s, st_s = gdn(qkv, small, gate, gdn_conv[i], gpar, nw, cb_s, state_gdn[i], RP, DEC_BATCH, DEC_SEQ,
                             8, DEC_SEQ, math.gcd(DEC_SEQ, GDN_CHUNK))
            cw = _prep_cmp(cmp_pos_k[i], cmp_w1_k[i], cmp_w2_k[i]) + _prep_cmp(cmp_pos_v[i], cmp_w1_v[i], cmp_w2_v[i])
            kcmp_p, vcmp_p = compress_prompt(nkv, cw, BATCH, SEQ)
            ob_p = nsa_prompt(nq, small, bf(nkv[:RP, :512]), kcmp_p, vcmp_p, BATCH, SEQ)
            kcmp_s, vcmp_s = compress_sample(page_table, cck, ccv, i, cw)
            ob_s = nsa_sample(page_table, csk, csv, i, nq, small, nkv, kcmp_s, vcmp_s, cwk, cwv)
            w_out = bf(w_out_ab[i])
            x = matmul_residual([(oa_p, oa_s, w_out[:512]), (ob_p, ob_s, w_out[512:])], x, 512)
            conv_p = jnp.stack([qkv[(n + 1) * SEQ - 3:(n + 1) * SEQ] for n in range(BATCH)])
            qkv_s = qkv[RP:].reshape(DEC_BATCH, DEC_SEQ, GDN_QKV)
            col = lambda a, c: a[:, c * 128:(c + 1) * 128]
            nkv_p, nkv_s = nkv[:RP], nkv[RP:]
            hp = lambda c: _heads(col(nkv_p, c), BATCH, SEQ)
            hs = lambda c: _heads(col(nkv_s, c), DEC_BATCH, DEC_SEQ)
            ab_p.append((conv_p, st_p, hp(4), hp(5), hp(0), hp(1), hp(2)[:, SEQ - WINDOW:], hp(3)[:, SEQ - WINDOW:]))
            ab_s.append((qkv_s[:, DEC_SEQ - 3:], st_s, hs(4), hs(5), hs(0), hs(1),
                         jnp.concatenate([cache_win_k[i][:, DEC_SEQ:], hs(2)], axis=1),
                         jnp.concatenate([cache_win_v[i][:, DEC_SEQ:], hs(3)], axis=1)))
        else:
            (u,) = norm_matmul(x, norm_mix[l], bf(w_in_c[i]), (D_MODEL,), 512)
            sp = _prep_s5(s5_a_re[i], s5_a_im[i], s5_b_re[i], s5_b_im[i], s5_c_re[i], s5_c_im[i], s5_d[i], s5_log_dt[i])
            z0 = jnp.zeros((BATCH, 1, S5_STATE), F32)
            y_p, fr_p, fi_p = s5_scan(u, sp, z0, z0, 0, BATCH, SEQ, True)
            per_tile = S5_TILE // DEC_SEQ
            x0r = state_s5_re[i].reshape(RS // S5_TILE, per_tile, S5_STATE)
            x0i = state_s5_im[i].reshape(RS // S5_TILE, per_tile, S5_STATE)
            y_s, fr_s, fi_s = s5_scan(u, sp, x0r, x0i, RP, DEC_BATCH, DEC_SEQ, False)
            wg = bf(w_glu[i])
            x = glu_mlp(y_p, norm_mix[l], wg[:, :D_MODEL], wg[:, D_MODEL:], bf(w_out_c[i]), x, 1024, 256, False, False, 1.0,
                        src_s=y_s)
            c_p.append((fr_p.reshape(BATCH, S5_GROUPS, S5_P), fi_p.reshape(BATCH, S5_GROUPS, S5_P)))
            c_s.append((fr_s.reshape(DEC_BATCH, S5_GROUPS, S5_P), fi_s.reshape(DEC_BATCH, S5_GROUPS, S5_P)))
        wq, wo = bf(w_xq[l]), bf(w_xo[l])
        (q_s,) = norm_matmul(x, norm_xq[l], wq, (D_MODEL,), 512, row_off=RP, rows=RS)
        o_s = cross_attention_cached(q_s, cache_mem_k, cache_mem_v, l, 0)
        x = cross_attention_block(x, norm_xq[l], wq, wo, memkv, l, 512)
        x = matmul_residual_rows(o_s, wo, x, RP, 512)
        x = glu_mlp(x, norm_ffn2[l], bf(w_ffn2_gate[l]), bf(w_ffn2_up[l]), bf(w_ffn2_down[l]), None, 1024, 256, True, True, 0.5)

    y = rmsnorm_rows(x, norm_final, 512)
    st = lambda grp, j: jnp.stack([t[j] for t in grp])
    return (y[:RP].reshape(BATCH, SEQ, D_MODEL), y[RP:].reshape(DEC_BATCH, DEC_SEQ, D_MODEL),
            mem_k_prompt, mem_v_prompt,
            st(ab_p, 1), st(ab_s, 1), st(ab_p, 0), st(ab_s, 0),
            st(ab_p, 2), st(ab_p, 3), st(ab_p, 4), st(ab_p, 5),
            st(ab_s, 2), st(ab_s, 3), st(ab_s, 4), st(ab_s, 5),
            st(ab_p, 6), st(ab_p, 7), st(ab_s, 6), st(ab_s, 7),
            st(c_p, 0), st(c_p, 1), st(c_s, 0), st(c_s, 1))
```
